```python
import math
import jax, jax.numpy as jnp
from jax import lax
import numpy as np

D_MODEL = 1024
BATCH = 2
SEQ = 8192
DEPTH = 2

GRID_W = 64
Q_BLOCK = 128
NORM_EPS = 1e-6
ROPE_THETA = 10000.0

HY_D = 256
HY_ORDER = 2
HY_EMB = 33
HY_BANDS = (HY_EMB - 1) // 2
HY_FILTER_HIDDEN = 64
HY_DECAY_TARGET = 1e-2
HY_FAST_DECAY_PCT = 0.3
HY_SLOW_DECAY_PCT = 1.5

GQA_HEADS = 8
GQA_KV_HEADS = 2
GQA_HEAD_DIM = 64

MLA_HEADS = 4
MLA_Q_RANK = 256
MLA_KV_RANK = 128
MLA_NOPE_DIM = 64
MLA_ROPE_DIM = 32
MLA_V_DIM = 64

D_MIX = HY_D + GQA_HEADS * GQA_HEAD_DIM + MLA_HEADS * MLA_V_DIM

D_FF = 2816

IN_SPLITS = ((HY_ORDER + 1) * HY_D, GQA_HEADS * GQA_HEAD_DIM, GQA_KV_HEADS * GQA_HEAD_DIM,
             GQA_KV_HEADS * GQA_HEAD_DIM, MLA_Q_RANK, MLA_KV_RANK, MLA_ROPE_DIM)
D_IN = 1952

kernel_name = 'hybrid_hyena_gqa_mla_encoder'

F32 = jnp.float32


def rms_norm(x, g):
    xf = x.astype(F32)
    y = xf * lax.rsqrt(jnp.mean(xf * xf, axis=-1, keepdims=True) + NORM_EPS)
    return (y * g.astype(F32)).astype(x.dtype)


def dwconv3(x, w, b):
    xp = jnp.pad(x, ((0, 0), (1, 1), (0, 0)))
    return xp[:, :-2] * w[0] + xp[:, 1:-1] * w[1] + xp[:, 2:] * w[2] + b


def axial_rope_tables(row_idx, col_idx, rot_dim):
    n_axis = rot_dim // 4
    inv = ROPE_THETA ** (-jnp.arange(n_axis, dtype=F32) / n_axis)
    ang = jnp.concatenate([row_idx[:, None].astype(F32) * inv,
                           col_idx[:, None].astype(F32) * inv], axis=-1)
    return jnp.cos(ang), jnp.sin(ang)


def apply_rope(x, cos, sin):
    x1, x2 = jnp.split(x.astype(F32), 2, axis=-1)
    return jnp.concatenate([x1 * cos - x2 * sin, x1 * sin + x2 * cos], axis=-1).astype(x.dtype)


def hyena_position_features(L):
    t = jnp.linspace(0.0, 1.0, L, dtype=F32)[:, None]
    w = 2.0 * math.pi * jnp.arange(L, dtype=F32)[:, None] / L
    f = jnp.linspace(1e-4, HY_BANDS - 1, HY_BANDS, dtype=F32)[None, :]
    z = jnp.concatenate([t, jnp.cos(f * w), -jnp.sin(f * w)], axis=-1)
    max_decay = math.log(HY_DECAY_TARGET) / HY_FAST_DECAY_PCT
    min_decay = math.log(HY_DECAY_TARGET) / HY_SLOW_DECAY_PCT
    deltas = jnp.linspace(min_decay, max_decay, HY_D, dtype=F32)
    window = jnp.exp(-t * jnp.abs(deltas)[None, :])
    return z, window


def hyena_filters(z, window, w1, b1, f1, w2, b2, f2, w3):
    h = jnp.sin(f1.astype(F32) * (z @ w1.astype(F32) + b1.astype(F32)))
    h = jnp.sin(f2.astype(F32) * (h @ w2.astype(F32) + b2.astype(F32)))
    h = (h @ w3.astype(F32)).reshape(-1, HY_ORDER, 2, HY_D)
    return h * window[:, None, None, :]


def bidir_fftconv(u, h_fwd, h_bwd, skip):
    L = u.shape[1]
    k = jnp.concatenate([h_fwd, h_bwd[::-1]], axis=0)
    uf = jnp.fft.rfft(u.astype(F32), n=2 * L, axis=1)
    kf = jnp.fft.rfft(k, axis=0)
    y = jnp.fft.irfft(uf * kf[None], n=2 * L, axis=1)[:, :L]
    return (y + u.astype(F32) * skip.astype(F32)).astype(u.dtype)


def hyena_mixer(u, conv_w, conv_b, filters, skip):
    uc = dwconv3(u, conv_w, conv_b)
    v, x1, x2 = jnp.split(uc, 3, axis=-1)
    z = v
    for i, gate in enumerate((x1, x2)):
        z = gate * bidir_fftconv(z, filters[:, i, 0], filters[:, i, 1], skip[i])
    return z


def gqa_attention(q, k, v, scale):
    B, L, Hq, d = q.shape
    Hkv = k.shape[2]
    G = Hq // Hkv
    nb = L // Q_BLOCK
    qb = q.reshape(B, nb, Q_BLOCK, Hkv, G, d).transpose(1, 0, 3, 4, 2, 5)

    def one_block(qi):
        s = jnp.einsum('bkgqd,bskd->bkgqs', qi, k, preferred_element_type=F32) * scale
        p = jax.nn.softmax(s, axis=-1).astype(v.dtype)
        return jnp.einsum('bkgqs,bskd->bkgqd', p, v)

    o = lax.map(one_block, qb)
    return o.transpose(1, 0, 4, 2, 3, 5).reshape(B, L, Hq * d)


def mla_attention(q_nope, q_pe, k_nope, k_pe, v, scale):
    B, L, H, dn = q_nope.shape
    dr = q_pe.shape[-1]
    dv = v.shape[-1]
    nb = L // Q_BLOCK
    qn = q_nope.reshape(B, nb, Q_BLOCK, H, dn).transpose(1, 0, 3, 2, 4)
    qr = q_pe.reshape(B, nb, Q_BLOCK, H, dr).transpose(1, 0, 3, 2, 4)

    def one_block(args):
        qn_i, qr_i = args
        s = (jnp.einsum('bhqd,bshd->bhqs', qn_i, k_nope, preferred_element_type=F32)
             + jnp.einsum('bhqr,bsr->bhqs', qr_i, k_pe, preferred_element_type=F32)) * scale
        p = jax.nn.softmax(s, axis=-1).astype(v.dtype)
        return jnp.einsum('bhqs,bshd->bhqd', p, v)

    o = lax.map(one_block, (qn, qr))
    return o.transpose(1, 0, 3, 2, 4).reshape(B, L, H * dv)


def setup_inputs(seed: int = 0) -> dict:
    key = jax.random.key(seed)
    ks = jax.random.split(key, 30)

    def nrm(k, shape, scale):
        return jax.random.normal(k, shape, F32) * scale

    def gain(k, n):
        return 1.0 + 0.05 * jax.random.normal(k, (DEPTH, n), F32)

    hy_cols = (HY_ORDER + 1) * HY_D
    return {
        'x': nrm(ks[0], (BATCH, SEQ, D_MODEL), 1.0),
        'mix_pre_norm': gain(ks[1], D_MODEL),
        'w_in': nrm(ks[2], (DEPTH, D_MODEL, D_IN), D_MODEL ** -0.5),
        'hy_conv_w': nrm(ks[3], (DEPTH, 3, hy_cols), 3 ** -0.5),
        'hy_conv_b': nrm(ks[4], (DEPTH, hy_cols), 0.01),
        'hy_filt_w1': nrm(ks[5], (DEPTH, HY_EMB, HY_FILTER_HIDDEN), HY_EMB ** -0.5),
        'hy_filt_b1': nrm(ks[6], (DEPTH, HY_FILTER_HIDDEN), 0.1),
        'hy_filt_freq1': gain(ks[7], HY_FILTER_HIDDEN),
        'hy_filt_w2': nrm(ks[8], (DEPTH, HY_FILTER_HIDDEN, HY_FILTER_HIDDEN), HY_FILTER_HIDDEN ** -0.5),
        'hy_filt_b2': nrm(ks[9], (DEPTH, HY_FILTER_HIDDEN), 0.1),
        'hy_filt_freq2': gain(ks[10], HY_FILTER_HIDDEN),
        'hy_filt_w3': nrm(ks[11], (DEPTH, HY_FILTER_HIDDEN, HY_ORDER * 2 * HY_D), HY_FILTER_HIDDEN ** -0.5),
        'hy_skip': nrm(ks[12], (DEPTH, HY_ORDER, HY_D), 0.5),
        'gqa_q_norm': gain(ks[13], GQA_HEAD_DIM),
        'gqa_k_norm': gain(ks[14], GQA_HEAD_DIM),
        'mla_q_a_norm': gain(ks[15], MLA_Q_RANK),
        'mla_w_uq': nrm(ks[16], (DEPTH, MLA_Q_RANK, MLA_HEADS * (MLA_NOPE_DIM + MLA_ROPE_DIM)), MLA_Q_RANK ** -0.5),
        'mla_kv_a_norm': gain(ks[17], MLA_KV_RANK),
        'mla_w_ukv': nrm(ks[18], (DEPTH, MLA_KV_RANK, MLA_HEADS * (MLA_NOPE_DIM + MLA_V_DIM)), MLA_KV_RANK ** -0.5),
        'hy_out_norm': gain(ks[19], HY_D),
        'gqa_out_norm': gain(ks[20], GQA_HEADS * GQA_HEAD_DIM),
        'mla_out_norm': gain(ks[21], MLA_HEADS * MLA_V_DIM),
        'w_out': nrm(ks[22], (DEPTH, D_MIX, D_MODEL), D_MIX ** -0.5),
        'mix_post_norm': gain(ks[23], D_MODEL),
        'ffn_pre_norm': gain(ks[24], D_MODEL),
        'w_up': nrm(ks[25], (DEPTH, D_MODEL, 2 * D_FF), D_MODEL ** -0.5),
        'ffn_conv_w': nrm(ks[26], (DEPTH, 3, 2 * D_FF), 3 ** -0.5),
        'ffn_conv_b': nrm(ks[27], (DEPTH, 2 * D_FF), 0.01),
        'w_down': nrm(ks[28], (DEPTH, D_FF, D_MODEL), D_FF ** -0.5),
        'ffn_post_norm': gain(ks[29], D_MODEL),
    }


def reference(x, mix_pre_norm, w_in, hy_conv_w, hy_conv_b, hy_filt_w1, hy_filt_b1, hy_filt_freq1,
              hy_filt_w2, hy_filt_b2, hy_filt_freq2, hy_filt_w3, hy_skip, gqa_q_norm, gqa_k_norm,
              mla_q_a_norm, mla_w_uq, mla_kv_a_norm, mla_w_ukv, hy_out_norm, gqa_out_norm,
              mla_out_norm, w_out, mix_post_norm, ffn_pre_norm, w_up, ffn_conv_w, ffn_conv_b,
              w_down, ffn_post_norm):
    B, L, _ = x.shape
    rows = L // GRID_W
    row_idx = jnp.broadcast_to(jnp.arange(rows)[:, None], (rows, GRID_W)).reshape(L)
    col_idx = jnp.broadcast_to(jnp.arange(GRID_W)[None, :], (rows, GRID_W)).reshape(L)

    cos_g, sin_g = axial_rope_tables(row_idx, col_idx, GQA_HEAD_DIM)
    cos_m, sin_m = axial_rope_tables(row_idx, col_idx, MLA_ROPE_DIM)
    z_pos, window = hyena_position_features(L)
    offsets = [int(o) for o in np.cumsum(IN_SPLITS)[:-1]]

    for l in range(DEPTH):
        h = rms_norm(x, mix_pre_norm[l])
        proj = h @ w_in[l]
        hy_in, gq, gk, gv, mq, mkv, mkr = jnp.split(proj, offsets, axis=-1)

        filters = hyena_filters(z_pos, window, hy_filt_w1[l], hy_filt_b1[l], hy_filt_freq1[l],
                                hy_filt_w2[l], hy_filt_b2[l], hy_filt_freq2[l], hy_filt_w3[l])
        y_hy = hyena_mixer(hy_in, hy_conv_w[l], hy_conv_b[l], filters, hy_skip[l])

        q = rms_norm(gq.reshape(B, L, GQA_HEADS, GQA_HEAD_DIM), gqa_q_norm[l])
        k = rms_norm(gk.reshape(B, L, GQA_KV_HEADS, GQA_HEAD_DIM), gqa_k_norm[l])
        q = apply_rope(q, cos_g[:, None, :], sin_g[:, None, :])
        k = apply_rope(k, cos_g[:, None, :], sin_g[:, None, :])
        v = gv.reshape(B, L, GQA_KV_HEADS, GQA_HEAD_DIM)
        y_gqa = gqa_attention(q, k, v, GQA_HEAD_DIM ** -0.5)

        cq = rms_norm(mq, mla_q_a_norm[l])
        qm = (cq @ mla_w_uq[l]).reshape(B, L, MLA_HEADS, MLA_NOPE_DIM + MLA_ROPE_DIM)
        q_nope, q_pe = jnp.split(qm, [MLA_NOPE_DIM], axis=-1)
        q_pe = apply_rope(q_pe, cos_m[:, None, :], sin_m[:, None, :])
        ckv = rms_norm(mkv, mla_kv_a_norm[l])
        kv = (ckv @ mla_w_ukv[l]).reshape(B, L, MLA_HEADS, MLA_NOPE_DIM + MLA_V_DIM)
        k_nope, v_m = jnp.split(kv, [MLA_NOPE_DIM], axis=-1)
        k_pe = apply_rope(mkr, cos_m, sin_m)
        y_mla = mla_attention(q_nope, q_pe, k_nope, k_pe, v_m,
                              (MLA_NOPE_DIM + MLA_ROPE_DIM) ** -0.5)

        groups = jnp.concatenate([rms_norm(y_hy, hy_out_norm[l]),
                                  rms_norm(y_gqa, gqa_out_norm[l]),
                                  rms_norm(y_mla, mla_out_norm[l])], axis=-1)
        x = x + rms_norm(groups @ w_out[l], mix_post_norm[l])

        h = rms_norm(x, ffn_pre_norm[l])
        up = dwconv3(h @ w_up[l], ffn_conv_w[l], ffn_conv_b[l])
        gate, val = jnp.split(up, 2, axis=-1)
        f = (jax.nn.gelu(gate, approximate=True) * val) @ w_down[l]
        x = x + rms_norm(f, ffn_post_norm[l])
    return x
```

```python
import functools
import math

import numpy as np
import jax
import jax.numpy as jnp
from jax import lax
from jax.experimental import pallas as pl
from jax.experimental.pallas import tpu as pltpu

F32 = jnp.float32
BF16 = jnp.bfloat16

NORM_EPS = 1e-6
ROPE_THETA = 10000.0
GRID_W = 64
LOG2E = math.log2(math.e)

D_MODEL = 1024
HY_D = 256
HY_EMB = 33
HY_BANDS = 16
HY_DECAY_TARGET = 1e-2
HY_FAST_DECAY_PCT = 0.3
HY_SLOW_DECAY_PCT = 1.5
GQA_HEADS = 8
GQA_KV_HEADS = 2
GQA_HEAD_DIM = 64
MLA_HEADS = 4
MLA_Q_RANK = 256
MLA_KV_RANK = 128
MLA_NOPE_DIM = 64
MLA_ROPE_DIM = 32
MLA_V_DIM = 64
D_FF = 2816

LANES = 128
FFT_N1 = 128
FFT_N2 = 128

TM_IN = 512
TQ_GQA = 128
TQ_MLA = 512
TM_OUT = 512
TM_FFN = 1024
TF_FFN = 256
TL_HCONV = 1024
VMEM_LIMIT = 56 * 1024 * 1024


def _cparams(sem):
    return pltpu.CompilerParams(dimension_semantics=sem, vmem_limit_bytes=VMEM_LIMIT)


def _rms(x, g):
    return x * lax.rsqrt(jnp.mean(x * x, axis=-1, keepdims=True) + NORM_EPS) * g


def _rope_lanes(y, c, s_up, s_dn, half):
    return y * c + pltpu.roll(y, LANES - half, 1) * s_up + pltpu.roll(y, half, 1) * s_dn


def _kin_kernel(x_ref, gpre_ref, win_ref, gqn_ref, gkn_ref, cg_ref, sug_ref, sdg_ref,
                mqn_ref, wuq_ref, cmq_ref, sumq_ref, sdmq_ref,
                mkvn_ref, wukvk_ref, wukvv_ref, cmk_ref, sumk_ref, sdmk_ref, vones_ref,
                hy_ref, qg_ref, kg_ref, vg_ref, qm_ref, km_ref, vm_ref):
    x = x_ref[0]
    h = _rms(x, gpre_ref[...]).astype(BF16)

    def proj(lo, hi):
        return jnp.dot(h, win_ref[:, lo:hi], preferred_element_type=F32)

    hy_ref[0] = proj(0, 768)

    cg, sug, sdg = cg_ref[...], sug_ref[...], sdg_ref[...]

    def head(xc, gain):
        ms = jnp.sum(xc * xc, axis=-1, keepdims=True) * (1.0 / GQA_HEAD_DIM)
        y = xc * lax.rsqrt(ms + NORM_EPS) * gain
        return _rope_lanes(y, cg, sug, sdg, GQA_HEAD_DIM // 2)

    o = 768
    gq = proj(o, o + GQA_HEADS * LANES)
    for j in range(GQA_HEADS):
        qg_ref[0, :, j * LANES:(j + 1) * LANES] = head(
            gq[:, j * LANES:(j + 1) * LANES], gqn_ref[...]).astype(BF16)
    o += GQA_HEADS * LANES
    gk = proj(o, o + GQA_KV_HEADS * LANES)
    for j in range(GQA_KV_HEADS):
        kj = head(gk[:, j * LANES:(j + 1) * LANES], gkn_ref[...])
        kg_ref[0, j, 0] = kj.T.astype(BF16)
    o += GQA_KV_HEADS * LANES
    vg_ref[0, :, :LANES] = proj(o, o + LANES).astype(BF16)
    vg_ref[0, :, LANES:] = jnp.ones((x.shape[0], LANES), BF16)
    o += LANES

    cq = _rms(proj(o, o + MLA_Q_RANK), mqn_ref[...]).astype(BF16)
    o += MLA_Q_RANK
    qm = jnp.dot(cq, wuq_ref[...], preferred_element_type=F32)
    cmq, sumq, sdmq = cmq_ref[...], sumq_ref[...], sdmq_ref[...]
    for j in range(MLA_HEADS):
        qm_ref[0, :, j * LANES:(j + 1) * LANES] = _rope_lanes(
            qm[:, j * LANES:(j + 1) * LANES], cmq, sumq, sdmq, MLA_ROPE_DIM // 2).astype(BF16)

    ckv = _rms(proj(o, o + MLA_KV_RANK), mkvn_ref[...]).astype(BF16)
    o += MLA_KV_RANK
    kpe = _rope_lanes(proj(o, o + LANES), cmk_ref[...], sumk_ref[...], sdmk_ref[...],
                      MLA_ROPE_DIM // 2)
    kn = jnp.dot(ckv, wukvk_ref[...], preferred_element_type=F32)
    for j in range(MLA_HEADS):
        km_ref[0, j, 0] = (kn[:, j * LANES:(j + 1) * LANES] + kpe).T.astype(BF16)
    vm = jnp.dot(ckv, wukvv_ref[...], preferred_element_type=F32) + vones_ref[...]
    vm_ref[0] = vm.astype(BF16)


def _kin_call(x, gpre, win_p, gqn, gkn, tg, mqn, wuq_p, tmq, mkvn, wukvk_p, wukvv_p, tmk, vones):
    B, L, D = x.shape
    tm = TM_IN
    nt = L // tm
    ncol = win_p.shape[1]

    def full(a):
        return pl.BlockSpec(a.shape, lambda b, i: (0,) * a.ndim)

    def rows(w):
        return pl.BlockSpec((tm, w), lambda b, i: (i, 0))

    in_specs = [pl.BlockSpec((1, tm, D), lambda b, i: (b, i, 0)), full(gpre), full(win_p),
                full(gqn), full(gkn), rows(LANES), rows(LANES), rows(LANES),
                full(mqn), full(wuq_p), rows(LANES), rows(LANES), rows(LANES),
                full(mkvn), full(wukvk_p), full(wukvv_p), rows(LANES), rows(LANES), rows(LANES),
                full(vones)]
    out_shape = [
        jax.ShapeDtypeStruct((B, L, 768), F32),
        jax.ShapeDtypeStruct((B, L, GQA_HEADS * LANES), BF16),
        jax.ShapeDtypeStruct((B, GQA_KV_HEADS, nt, LANES, tm), BF16),
        jax.ShapeDtypeStruct((B, L, 2 * LANES), BF16),
        jax.ShapeDtypeStruct((B, L, MLA_HEADS * LANES), BF16),
        jax.ShapeDtypeStruct((B, MLA_HEADS, nt, LANES, tm), BF16),
        jax.ShapeDtypeStruct((B, L, 4 * LANES), BF16),
    ]
    out_specs = [
        pl.BlockSpec((1, tm, 768), lambda b, i: (b, i, 0)),
        pl.BlockSpec((1, tm, GQA_HEADS * LANES), lambda b, i: (b, i, 0)),
        pl.BlockSpec((1, GQA_KV_HEADS, 1, LANES, tm), lambda b, i: (b, 0, i, 0, 0)),
        pl.BlockSpec((1, tm, 2 * LANES), lambda b, i: (b, i, 0)),
        pl.BlockSpec((1, tm, MLA_HEADS * LANES), lambda b, i: (b, i, 0)),
        pl.BlockSpec((1, MLA_HEADS, 1, LANES, tm), lambda b, i: (b, 0, i, 0, 0)),
        pl.BlockSpec((1, tm, 4 * LANES), lambda b, i: (b, i, 0)),
    ]
    return pl.pallas_call(
        _kin_kernel, grid=(B, nt), in_specs=in_specs, out_specs=out_specs, out_shape=out_shape,
        compiler_params=_cparams(("parallel", "parallel")), name="in_proj",
    )(x, gpre, win_p, gqn, gkn, *tg, mqn, wuq_p, *tmq, mkvn, wukvk_p, wukvv_p, *tmk, vones)


def _attn_kernel(q_ref, k_ref, v_ref, o_ref, *, n_stack, tq, n_chunks, tk):
    sel = pl.program_id(1) % 2
    q = jnp.concatenate([q_ref[0, :, j * LANES:(j + 1) * LANES] for j in range(n_stack)], axis=0)
    rows = n_stack * tq

    def body(c, carry):
        m, acc = carry
        s = jnp.dot(q, k_ref[0, 0, c], preferred_element_type=F32)
        m_new = jnp.maximum(m, jnp.max(s, axis=-1, keepdims=True))
        alpha = jnp.exp2(m - m_new)
        p = jnp.exp2(s - m_new).astype(BF16)
        vc = v_ref[0, pl.ds(pl.multiple_of(c * tk, tk), tk), :]
        return m_new, alpha * acc + jnp.dot(p, vc, preferred_element_type=F32)

    m0 = jnp.full((rows, 1), -jnp.inf, F32)
    acc0 = jnp.zeros((rows, 2 * LANES), F32)
    _, acc = lax.fori_loop(0, n_chunks, body, (m0, acc0))
    o = acc[:, :LANES] / acc[:, LANES:]
    o = jnp.where(sel == 0, o, pltpu.roll(o, LANES // 2, 1))
    lane = lax.broadcasted_iota(jnp.int32, o.shape, 1)
    o = jnp.where(lane < LANES // 2, o, 0.0).astype(BF16)
    for j in range(n_stack):
        o_ref[0, :, j * LANES:(j + 1) * LANES] = o[j * tq:(j + 1) * tq]


def _attn_call(q, kt, v, *, n_stack, tq, v_per_pair, name):
    B, L, hq = q.shape
    G = kt.shape[1]
    n_chunks, tk = kt.shape[2], kt.shape[4]
    if v_per_pair:
        v_map = lambda b, g, i: (b, 0, g // 2)
    else:
        v_map = lambda b, g, i: (b, 0, 0)
    kern = functools.partial(_attn_kernel, n_stack=n_stack, tq=tq, n_chunks=n_chunks, tk=tk)
    return pl.pallas_call(
        kern, grid=(B, G, L // tq),
        in_specs=[pl.BlockSpec((1, tq, n_stack * LANES), lambda b, g, i: (b, i, g)),
                  pl.BlockSpec((1, 1, n_chunks, LANES, tk), lambda b, g, i: (b, g, 0, 0, 0)),
                  pl.BlockSpec((1, L, 2 * LANES), v_map)],
        out_specs=pl.BlockSpec((1, tq, n_stack * LANES), lambda b, g, i: (b, i, g)),
        out_shape=jax.ShapeDtypeStruct((B, L, hq), BF16),
        compiler_params=_cparams(("parallel", "parallel", "arbitrary")), name=name,
    )(q, kt, v)


def _hconv_kernel(x_ref, xp_ref, xn_ref, w_ref, b_ref, v_ref, x1_ref, x2_ref):
    i = pl.program_id(1)
    x = x_ref[0]
    tl = x.shape[0]
    prev = jnp.where(i > 0, xp_ref[0][7:8, :], 0.0)
    nxt = jnp.where(i < pl.num_programs(1) - 1, xn_ref[0][0:1, :], 0.0)
    r = lax.broadcasted_iota(jnp.int32, x.shape, 0)
    xm = jnp.where(r == 0, prev, pltpu.roll(x, 1, 0))
    xp = jnp.where(r == tl - 1, nxt, pltpu.roll(x, tl - 1, 0))
    uc = xm * w_ref[0:1, :] + x * w_ref[1:2, :] + xp * w_ref[2:3, :] + b_ref[...]
    v_ref[0] = uc[:, :HY_D]
    x1_ref[0] = uc[:, HY_D:2 * HY_D]
    x2_ref[0] = uc[:, 2 * HY_D:]


def _hconv_call(hy_in, w, b):
    B, L, C = hy_in.shape
    tl = TL_HCONV
    nb = tl // 8
    last = L // 8 - 1
    out = jax.ShapeDtypeStruct((B, L, HY_D), F32)
    ospec = pl.BlockSpec((1, tl, HY_D), lambda b_, i: (b_, i, 0))
    return pl.pallas_call(
        _hconv_kernel, grid=(B, L // tl),
        in_specs=[pl.BlockSpec((1, tl, C), lambda b_, i: (b_, i, 0)),
                  pl.BlockSpec((1, 8, C), lambda b_, i: (b_, jnp.maximum(i * nb - 1, 0), 0)),
                  pl.BlockSpec((1, 8, C), lambda b_, i: (b_, jnp.minimum((i + 1) * nb, last), 0)),
                  pl.BlockSpec((3, C), lambda b_, i: (0, 0)),
                  pl.BlockSpec((1, C), lambda b_, i: (0, 0))],
        out_specs=[ospec, ospec, ospec], out_shape=[out, out, out],
        compiler_params=_cparams(("parallel", "parallel")), name="hy_conv3",
    )(hy_in, hy_in, hy_in, w, b)


def _filt_kernel(z_ref, w1_ref, b1_ref, f1_ref, w2_ref, b2_ref, f2_ref, w3_ref, win_ref, o_ref):
    hp = lax.Precision.HIGHEST
    h = jnp.sin(f1_ref[...] * (jnp.dot(z_ref[...], w1_ref[...], precision=hp,
                                       preferred_element_type=F32) + b1_ref[...]))
    h = jnp.sin(f2_ref[...] * (jnp.dot(h, w2_ref[...], precision=hp,
                                       preferred_element_type=F32) + b2_ref[...]))
    k = jnp.dot(h, w3_ref[...], precision=hp, preferred_element_type=F32)
    n1 = lax.broadcasted_iota(jnp.int32, (k.shape[0], 2 * HY_D), 0) % FFT_N1
    kk = jnp.where(n1 >= FFT_N1 // 2, k[:, 2 * HY_D:], k[:, :2 * HY_D])
    win = win_ref[...]
    o_ref[...] = kk * jnp.concatenate([win, win], axis=-1)


def _filt_call(z_perm, w1, b1, f1, w2, b2, f2, w3, win_perm):
    n = z_perm.shape[0]
    tp = 1024

    def full(a):
        return pl.BlockSpec(a.shape, lambda i: (0,) * a.ndim)

    return pl.pallas_call(
        _filt_kernel, grid=(n // tp,),
        in_specs=[pl.BlockSpec((tp, z_perm.shape[1]), lambda i: (i, 0)), full(w1), full(b1), full(f1),
                  full(w2), full(b2), full(f2), full(w3),
                  pl.BlockSpec((tp, HY_D), lambda i: (i, 0))],
        out_specs=pl.BlockSpec((tp, 2 * HY_D), lambda i: (i, 0)),
        out_shape=jax.ShapeDtypeStruct((n, 2 * HY_D), F32),
        compiler_params=_cparams(("parallel",)), name="hy_filter",
    )(z_perm, w1, b1, f1, w2, b2, f2, w3, win_perm)


def _bm_kernel(g_ref, x_ref, o_ref, *, to, shared):
    for t in range(to):
        g = g_ref[0] if shared else g_ref[t]
        o_ref[t] = jnp.dot(g, x_ref[t].astype(BF16), preferred_element_type=F32)


def _bm_call(g, x, name, to=8):
    O, K, N = x.shape
    M = g.shape[1]
    shared = g.shape[0] == 1
    gspec = (pl.BlockSpec((1, M, K), lambda i: (0, 0, 0)) if shared
             else pl.BlockSpec((to, M, K), lambda i: (i, 0, 0)))
    return pl.pallas_call(
        functools.partial(_bm_kernel, to=to, shared=shared), grid=(O // to,),
        in_specs=[gspec, pl.BlockSpec((to, K, N), lambda i: (i, 0, 0))],
        out_specs=pl.BlockSpec((to, M, N), lambda i: (i, 0, 0)),
        out_shape=jax.ShapeDtypeStruct((O, M, N), F32),
        compiler_params=_cparams(("parallel",)), name=name,
    )(g, x)


def _convb_kernel(mb_ref, gc_ref, x_ref, kf_ref, o_ref, *, to):
    h = FFT_N2
    for t in range(to):
        xs = jnp.dot(mb_ref[...], x_ref[t].astype(BF16), preferred_element_type=F32)
        xr, xi = xs[:h], xs[h:]
        kr, ki = kf_ref[t, :h, :], kf_ref[t, h:, :]
        ys = jnp.concatenate([xr * kr - xi * ki, xr * ki + xi * kr], axis=0).astype(BF16)
        o_ref[t] = jnp.dot(gc_ref[t], ys, preferred_element_type=F32)


def _convb_call(mb, gc, xt, kf, order, to=4):
    O, K, N = xt.shape
    return pl.pallas_call(
        functools.partial(_convb_kernel, to=to), grid=(O // to,),
        in_specs=[pl.BlockSpec(mb.shape, lambda i: (0, 0)),
                  pl.BlockSpec((to, K, K), lambda i: (i, 0, 0)),
                  pl.BlockSpec((to, K, N), lambda i: (i, 0, 0)),
                  pl.BlockSpec((to, K, N), lambda i: (i, 0, order))],
        out_specs=pl.BlockSpec((to, K, N), lambda i: (i, 0, 0)),
        out_shape=jax.ShapeDtypeStruct((O, K, N), F32),
        compiler_params=_cparams(("parallel",)), name="hy_spec_mul",
    )(mb, gc, xt, kf)


def _gate_kernel(y_ref, u_ref, g_ref, s_ref, o_ref):
    o_ref[...] = g_ref[...] * (y_ref[...] + u_ref[...] * s_ref[...])


def _gate_call(y, u, gate, skip):
    B, L, C = y.shape
    tl = 2048
    spec = pl.BlockSpec((1, tl, C), lambda b, i: (b, i, 0))
    return pl.pallas_call(
        _gate_kernel, grid=(B, L // tl),
        in_specs=[spec, spec, spec, pl.BlockSpec((1, 1, C), lambda b, i: (0, 0, 0))],
        out_specs=spec, out_shape=jax.ShapeDtypeStruct((B, L, C), F32),
        compiler_params=_cparams(("parallel", "parallel")), name="hy_gate",
    )(y, u, gate, skip.reshape(1, 1, C))


def _swap_outer(a):
    O, P2, N = a.shape
    P = P2 // 2
    return a.reshape(O, 2, P, N).transpose(2, 1, 0, 3).reshape(P, 2 * O, N)


def _dft_tables():
    n = FFT_N1 * FFT_N2
    k = np.arange(FFT_N1)
    f = np.exp(-2j * np.pi * np.outer(k, k) / FFT_N1)
    t = np.exp(-2j * np.pi * np.outer(k, k) / n)
    return f, t, n


def _dft_matrices():
    f, t, n = _dft_tables()
    fr, fi = jnp.asarray(f.real, F32), jnp.asarray(f.imag, F32)
    tr, ti = jnp.asarray(t.real, F32), jnp.asarray(t.imag, F32)
    half = FFT_N1 // 2
    er = fr[None] * tr[:, :, None] - fi[None] * ti[:, :, None]
    ei = fr[None] * ti[:, :, None] + fi[None] * tr[:, :, None]
    ga = jnp.concatenate([jnp.concatenate([er[:, :, :half], -ei[:, :, :half]], axis=2),
                          jnp.concatenate([ei[:, :, :half], er[:, :, :half]], axis=2)], axis=1)
    gaf = jnp.concatenate([er, ei], axis=1)
    mb = jnp.concatenate([jnp.concatenate([fr, -fi], axis=1),
                          jnp.concatenate([fi, fr], axis=1)], axis=0)
    tct = jnp.transpose(tr)[:, :, None]
    tst = -jnp.transpose(ti)[:, :, None]
    gr = tct * fr[None] - tst * (-fi[None])
    gi = tct * (-fi[None]) + tst * fr[None]
    gc = jnp.concatenate([jnp.concatenate([gr, -gi], axis=2),
                          jnp.concatenate([gi, gr], axis=2)], axis=1)
    hr, hi = fr[:half] / n, -fi[:half] / n
    md = jnp.concatenate([jnp.concatenate([hr, -hi], axis=1),
                          jnp.concatenate([hi, hr], axis=1)], axis=0)
    return (ga.astype(BF16), gaf.astype(BF16), mb.astype(BF16), gc.astype(BF16),
            md.astype(BF16)[None])


def _hyena_positions(L):
    t = jnp.linspace(0.0, 1.0, L, dtype=F32)[:, None]
    w = 2.0 * math.pi * jnp.arange(L, dtype=F32)[:, None] / L
    f = jnp.linspace(1e-4, HY_BANDS - 1, HY_BANDS, dtype=F32)[None, :]
    z = jnp.concatenate([t, jnp.cos(f * w), -jnp.sin(f * w)], axis=-1)
    max_decay = math.log(HY_DECAY_TARGET) / HY_FAST_DECAY_PCT
    min_decay = math.log(HY_DECAY_TARGET) / HY_SLOW_DECAY_PCT
    deltas = jnp.linspace(min_decay, max_decay, HY_D, dtype=F32)
    window = jnp.exp(-t * jnp.abs(deltas)[None, :])

    def perm(a):
        ext = jnp.concatenate([a, a[::-1]], axis=0)
        return ext.reshape(FFT_N1, FFT_N2, -1).transpose(1, 0, 2).reshape(2 * L, -1)

    z_perm = jnp.pad(perm(z), ((0, 0), (0, 64 - HY_EMB)))
    return z_perm, perm(window)


def _hyena_layer(hy_in, conv_w, conv_b, fw, skip, z_perm, win_perm, mats):
    B, L, _ = hy_in.shape
    ga, gaf, mb, gc, md = mats
    v, x1, x2 = _hconv_call(hy_in, conv_w, conv_b[None])
    kperm = _filt_call(z_perm, *fw, win_perm)
    ka = _bm_call(gaf, kperm.reshape(FFT_N2, FFT_N1, 2 * HY_D), "hy_fft_filt_a")
    kf = _bm_call(mb[None], _swap_outer(ka), "hy_fft_filt_b")

    def pack(u):
        return (u.reshape(B, FFT_N1 // 2, FFT_N2, HY_D).transpose(2, 0, 1, 3)
                .reshape(FFT_N2, FFT_N1, HY_D))

    def unpack(yt):
        return (yt.reshape(FFT_N2, B, FFT_N1 // 2, HY_D).transpose(1, 2, 0, 3)
                .reshape(B, L, HY_D))

    z = v
    for i, gate in enumerate((x1, x2)):
        a = _bm_call(ga, pack(z), "hy_fft_a")
        c = _convb_call(mb, gc, _swap_outer(a), kf, i)
        y = unpack(_bm_call(md, _swap_outer(c), "hy_fft_d"))
        z = _gate_call(y, z, gate, skip[i])
    return z


def _kout_kernel(x_ref, yh_ref, yg_ref, ym_ref, gh_ref, gg_ref, gm_ref, w_ref, gpost_ref, gffn_ref,
                 xo_ref, h_ref):
    a = _rms(yh_ref[0], gh_ref[...]).astype(BF16)
    yg = yg_ref[0].astype(F32)
    ms = jnp.sum(yg * yg, axis=-1, keepdims=True) * (1.0 / (GQA_HEADS * GQA_HEAD_DIM))
    b = (yg * lax.rsqrt(ms + NORM_EPS) * gg_ref[...]).astype(BF16)
    ym = ym_ref[0].astype(F32)
    ms = jnp.sum(ym * ym, axis=-1, keepdims=True) * (1.0 / (MLA_HEADS * MLA_V_DIM))
    c = (ym * lax.rsqrt(ms + NORM_EPS) * gm_ref[...]).astype(BF16)
    o1 = HY_D
    o2 = o1 + GQA_HEADS * LANES
    y = (jnp.dot(a, w_ref[:o1, :], preferred_element_type=F32)
         + jnp.dot(b, w_ref[o1:o2, :], preferred_element_type=F32)
         + jnp.dot(c, w_ref[o2:, :], preferred_element_type=F32))
    xo = x_ref[0] + _rms(y, gpost_ref[...])
    xo_ref[0] = xo
    h_ref[0] = _rms(xo, gffn_ref[...]).astype(BF16)


def _kout_call(x, yh, yg, ym, gh, gg, gm, w_p, gpost, gffn):
    B, L, D = x.shape
    tm = TM_OUT

    def rows(a):
        return pl.BlockSpec((1, tm, a.shape[2]), lambda b, i: (b, i, 0))

    def full(a):
        return pl.BlockSpec(a.shape, lambda b, i: (0,) * a.ndim)

    return pl.pallas_call(
        _kout_kernel, grid=(B, L // tm),
        in_specs=[rows(x), rows(yh), rows(yg), rows(ym), full(gh), full(gg), full(gm), full(w_p),
                  full(gpost), full(gffn)],
        out_specs=[rows(x), rows(x)],
        out_shape=[jax.ShapeDtypeStruct((B, L, D), F32), jax.ShapeDtypeStruct((B, L, D), BF16)],
        compiler_params=_cparams(("parallel", "parallel")), name="out_proj",
    )(x, yh, yg, ym, gh, gg, gm, w_p, gpost, gffn)


HALO = 16


def _ffn_kernel(h_ref, hp_ref, hn_ref, x_ref, wg_ref, wu_ref, cw_ref, cb_ref, wd_ref, gpost_ref,
                o_ref, acc_ref):
    i = pl.program_id(1)
    j = pl.program_id(2)
    tm = h_ref.shape[1]
    prev = jnp.where(i > 0, hp_ref[0], jnp.zeros_like(hp_ref[0]))
    nxt = jnp.where(i < pl.num_programs(1) - 1, hn_ref[0], jnp.zeros_like(hn_ref[0]))
    he = jnp.concatenate([prev, h_ref[0], nxt], axis=0)
    ext = tm + 2 * HALO

    def conv(w_ref, col):
        up = jnp.dot(he, w_ref[...], preferred_element_type=F32)
        um = pltpu.roll(up, 1, 0)[HALO:HALO + tm]
        upl = pltpu.roll(up, ext - 1, 0)[HALO:HALO + tm]
        return (um * cw_ref[0, 0:1, col] + up[HALO:HALO + tm] * cw_ref[0, 1:2, col]
                + upl * cw_ref[0, 2:3, col] + cb_ref[0, :, col])

    tf = wg_ref.shape[1]
    g = conv(wg_ref, slice(0, tf))
    u = conv(wu_ref, slice(tf, 2 * tf))
    gelu = 0.5 * g * (1.0 + jnp.tanh(math.sqrt(2.0 / math.pi) * (g + 0.044715 * (g * g * g))))
    part = jnp.dot((gelu * u).astype(BF16), wd_ref[...], preferred_element_type=F32)

    @pl.when(j == 0)
    def _():
        acc_ref[...] = part

    @pl.when(j > 0)
    def _():
        acc_ref[...] += part

    @pl.when(j == pl.num_programs(2) - 1)
    def _():
        o_ref[0] = x_ref[0] + _rms(acc_ref[...], gpost_ref[...])


def _ffn_call(h, x, w_up, cw, cb, w_down, gpost):
    B, L, D = x.shape
    tm, tf = TM_FFN, TF_FFN
    nj = D_FF // tf
    nb = tm // HALO
    last = L // HALO - 1
    cw_c = jnp.concatenate([cw[:, :D_FF].reshape(3, nj, tf), cw[:, D_FF:].reshape(3, nj, tf)],
                           axis=2).transpose(1, 0, 2)
    cb_c = jnp.concatenate([cb[:D_FF].reshape(nj, 1, tf), cb[D_FF:].reshape(nj, 1, tf)], axis=2)
    return pl.pallas_call(
        _ffn_kernel, grid=(B, L // tm, nj),
        in_specs=[pl.BlockSpec((1, tm, D), lambda b, i, j: (b, i, 0)),
                  pl.BlockSpec((1, HALO, D), lambda b, i, j: (b, jnp.maximum(i * nb - 1, 0), 0)),
                  pl.BlockSpec((1, HALO, D), lambda b, i, j: (b, jnp.minimum((i + 1) * nb, last), 0)),
                  pl.BlockSpec((1, tm, D), lambda b, i, j: (b, i, 0)),
                  pl.BlockSpec((D, tf), lambda b, i, j: (0, j)),
                  pl.BlockSpec((D, tf), lambda b, i, j: (0, nj + j)),
                  pl.BlockSpec((1, 3, 2 * tf), lambda b, i, j: (j, 0, 0)),
                  pl.BlockSpec((1, 1, 2 * tf), lambda b, i, j: (j, 0, 0)),
                  pl.BlockSpec((tf, D), lambda b, i, j: (j, 0)),
                  pl.BlockSpec((1, D), lambda b, i, j: (0, 0))],
        out_specs=pl.BlockSpec((1, tm, D), lambda b, i, j: (b, i, 0)),
        out_shape=jax.ShapeDtypeStruct((B, L, D), F32),
        scratch_shapes=[pltpu.VMEM((tm, D), F32)],
        compiler_params=_cparams(("parallel", "parallel", "arbitrary")), name="conv_ffn",
    )(h, h, h, x, w_up, w_up, cw_c, cb_c, w_down, gpost)


def _axial_tables(L, rot_dim):
    rows = L // GRID_W
    row_idx = jnp.broadcast_to(jnp.arange(rows)[:, None], (rows, GRID_W)).reshape(L)
    col_idx = jnp.broadcast_to(jnp.arange(GRID_W)[None, :], (rows, GRID_W)).reshape(L)
    n_axis = rot_dim // 4
    inv = ROPE_THETA ** (-jnp.arange(n_axis, dtype=F32) / n_axis)
    ang = jnp.concatenate([row_idx[:, None].astype(F32) * inv,
                           col_idx[:, None].astype(F32) * inv], axis=-1)
    return jnp.cos(ang), jnp.sin(ang)


def _rope_tables(L):
    def lanes(parts):
        used = sum(p.shape[1] for p in parts)
        return jnp.concatenate(parts + [jnp.zeros((L, LANES - used), F32)], axis=1)

    cg, sg = _axial_tables(L, GQA_HEAD_DIM)
    zg = jnp.zeros_like(sg)
    tg = (lanes([cg, cg]), lanes([-sg, zg]), lanes([zg, sg]))
    cm, sm = _axial_tables(L, MLA_ROPE_DIM)
    zm = jnp.zeros_like(sm)
    nope0 = jnp.zeros((L, MLA_NOPE_DIM), F32)
    nope1 = jnp.ones((L, MLA_NOPE_DIM), F32)
    tmk = (lanes([nope0, cm, cm]), lanes([nope0, -sm, zm]), lanes([nope0, zm, sm]))
    sc = (MLA_NOPE_DIM + MLA_ROPE_DIM) ** -0.5 * LOG2E
    tmq = (lanes([nope1, cm, cm]) * sc, tmk[1] * sc, tmk[2] * sc)
    return tg, tmq, tmk


def _pad_heads(w, n_heads, width):
    k = w.shape[0]
    return jnp.pad(w.reshape(k, n_heads, width), ((0, 0), (0, 0), (0, LANES - width))).reshape(
        k, n_heads * LANES)


def _pad_head_rows(w, n_heads, width):
    n = w.shape[1]
    return jnp.pad(w.reshape(n_heads, width, n), ((0, 0), (0, LANES - width), (0, 0))).reshape(
        n_heads * LANES, n)


def _pad_vec(g, n_heads, width):
    return jnp.pad(g.reshape(n_heads, width), ((0, 0), (0, LANES - width))).reshape(1, n_heads * LANES)


def kernel(x, mix_pre_norm, w_in, hy_conv_w, hy_conv_b, hy_filt_w1, hy_filt_b1, hy_filt_freq1,
           hy_filt_w2, hy_filt_b2, hy_filt_freq2, hy_filt_w3, hy_skip, gqa_q_norm, gqa_k_norm,
           mla_q_a_norm, mla_w_uq, mla_kv_a_norm, mla_w_ukv, hy_out_norm, gqa_out_norm,
           mla_out_norm, w_out, mix_post_norm, ffn_pre_norm, w_up, ffn_conv_w, ffn_conv_b,
           w_down, ffn_post_norm):
    B, L, D = x.shape
    assert B == 2 and 2 * L == FFT_N1 * FFT_N2 and D == D_MODEL
    depth = w_in.shape[0]
    tg, tmq, tmk = _rope_tables(L)
    z_perm, win_perm = _hyena_positions(L)
    mats = _dft_matrices()
    vones = jnp.tile(jnp.concatenate([jnp.zeros((1, LANES), F32), jnp.ones((1, LANES), F32)], axis=1),
                     (1, 2))
    hid = LANES - hy_filt_w1.shape[2]

    for l in range(depth):
        wl = w_in[l]
        o = 0
        parts = []
        for n, fn in ((768, None), (512, lambda w: _pad_heads(w, GQA_HEADS, GQA_HEAD_DIM)),
                      (128, lambda w: _pad_heads(w, GQA_KV_HEADS, GQA_HEAD_DIM)), (128, None),
                      (MLA_Q_RANK, None), (MLA_KV_RANK, None),
                      (MLA_ROPE_DIM, lambda w: jnp.pad(w, ((0, 0), (MLA_NOPE_DIM, LANES - MLA_NOPE_DIM - MLA_ROPE_DIM))))):
            w = wl[:, o:o + n]
            parts.append(w if fn is None else fn(w))
            o += n
        win_p = jnp.concatenate(parts, axis=1).astype(BF16)
        gqn = _pad_vec(gqa_q_norm[l] * (GQA_HEAD_DIM ** -0.5 * LOG2E), 1, GQA_HEAD_DIM)
        gkn = _pad_vec(gqa_k_norm[l], 1, GQA_HEAD_DIM)
        wuq_p = _pad_heads(mla_w_uq[l], MLA_HEADS, MLA_NOPE_DIM + MLA_ROPE_DIM).astype(BF16)
        wukv = mla_w_ukv[l].reshape(MLA_KV_RANK, MLA_HEADS, MLA_NOPE_DIM + MLA_V_DIM)
        wukvk_p = _pad_heads(wukv[:, :, :MLA_NOPE_DIM].reshape(MLA_KV_RANK, -1), MLA_HEADS,
                             MLA_NOPE_DIM).astype(BF16)
        wv = wukv[:, :, MLA_NOPE_DIM:].reshape(MLA_KV_RANK, 2, 2 * MLA_V_DIM)
        wukvv_p = jnp.pad(wv, ((0, 0), (0, 0), (0, LANES))).reshape(MLA_KV_RANK, 4 * LANES).astype(BF16)

        hy_in, qg, kg, vg, qm, km, vm = _kin_call(
            x, mix_pre_norm[l][None], win_p, gqn, gkn, tg, mla_q_a_norm[l][None], wuq_p, tmq,
            mla_kv_a_norm[l][None], wukvk_p, wukvv_p, tmk, vones)

        w3 = hy_filt_w3[l].reshape(-1, 2, 2, HY_D).transpose(0, 2, 1, 3).reshape(-1, 4 * HY_D)
        fw = (jnp.pad(hy_filt_w1[l], ((0, 64 - HY_EMB), (0, hid))),
              jnp.pad(hy_filt_b1[l], (0, hid))[None], jnp.pad(hy_filt_freq1[l], (0, hid))[None],
              jnp.pad(hy_filt_w2[l], ((0, hid), (0, hid))),
              jnp.pad(hy_filt_b2[l], (0, hid))[None], jnp.pad(hy_filt_freq2[l], (0, hid))[None],
              jnp.pad(w3, ((0, hid), (0, 0))))
        y_hy = _hyena_layer(hy_in, hy_conv_w[l], hy_conv_b[l], fw, hy_skip[l], z_perm, win_perm, mats)

        y_gqa = _attn_call(qg, kg, vg, n_stack=GQA_HEADS // GQA_KV_HEADS, tq=TQ_GQA,
                           v_per_pair=False, name="attn_gqa")
        y_mla = _attn_call(qm, km, vm, n_stack=1, tq=TQ_MLA, v_per_pair=True, name="attn_mla")

        wo = w_out[l]
        o1, o2 = HY_D, HY_D + GQA_HEADS * GQA_HEAD_DIM
        wo_p = jnp.concatenate([wo[:o1], _pad_head_rows(wo[o1:o2], GQA_HEADS, GQA_HEAD_DIM),
                                _pad_head_rows(wo[o2:], MLA_HEADS, MLA_V_DIM)], axis=0).astype(BF16)
        x, h2 = _kout_call(x, y_hy, y_gqa, y_mla, hy_out_norm[l][None],
                           _pad_vec(gqa_out_norm[l], GQA_HEADS, GQA_HEAD_DIM),
                           _pad_vec(mla_out_norm[l], MLA_HEADS, MLA_V_DIM), wo_p,
                           mix_post_norm[l][None], ffn_pre_norm[l][None])
        x = _ffn_call(h2, x, w_up[l].astype(BF16), ffn_conv_w[l], ffn_conv_b[l],
                      w_down[l].astype(BF16), ffn_post_norm[l][None])
    return x
```

```python
import functools
import math

import numpy as np
import jax
import jax.numpy as jnp
from jax import lax
from jax.experimental import pallas as pl
from jax.experimental.pallas import tpu as pltpu

F32 = jnp.float32
BF16 = jnp.bfloat16

NORM_EPS = 1e-6
ROPE_THETA = 10000.0
GRID_W = 64
LOG2E = math.log2(math.e)

D_MODEL = 1024
HY_D = 256
HY_EMB = 33
HY_BANDS = 16
HY_DECAY_TARGET = 1e-2
HY_FAST_DECAY_PCT = 0.3
HY_SLOW_DECAY_PCT = 1.5
GQA_HEADS = 8
GQA_KV_HEADS = 2
GQA_HEAD_DIM = 64
MLA_HEADS = 4
MLA_Q_RANK = 256
MLA_KV_RANK = 128
MLA_NOPE_DIM = 64
MLA_ROPE_DIM = 32
MLA_V_DIM = 64
D_FF = 2816

LANES = 128
FFT_N1 = 128
FFT_N2 = 128

TM_IN = 512
TQ_GQA = 128
TQ_MLA = 512
TM_OUT = 512
TM_FFN = 1024
TF_FFN = 256
TL_HCONV = 1024
VMEM_LIMIT = 56 * 1024 * 1024


def _cparams(sem):
    return pltpu.CompilerParams(dimension_semantics=sem, vmem_limit_bytes=VMEM_LIMIT)


def _rms(x, g):
    return x * lax.rsqrt(jnp.mean(x * x, axis=-1, keepdims=True) + NORM_EPS) * g


def _rope_lanes(y, c, s_up, s_dn, half):
    return y * c + pltpu.roll(y, LANES - half, 1) * s_up + pltpu.roll(y, half, 1) * s_dn


def _kin_kernel(x_ref, gpre_ref, win_ref, gqn_ref, gkn_ref, cg_ref, sug_ref, sdg_ref,
                mqn_ref, wuq_ref, cmq_ref, sumq_ref, sdmq_ref,
                mkvn_ref, wukvk_ref, wukvv_ref, cmk_ref, sumk_ref, sdmk_ref, vones_ref,
                hy_ref, qg_ref, kg_ref, vg_ref, qm_ref, km_ref, vm_ref):
    x = x_ref[0]
    h = _rms(x, gpre_ref[...]).astype(BF16)

    def proj(lo, hi):
        return jnp.dot(h, win_ref[:, lo:hi], preferred_element_type=F32)

    hy_ref[0] = proj(0, 768)

    cg, sug, sdg = cg_ref[...], sug_ref[...], sdg_ref[...]

    def head(xc, gain):
        ms = jnp.sum(xc * xc, axis=-1, keepdims=True) * (1.0 / GQA_HEAD_DIM)
        y = xc * lax.rsqrt(ms + NORM_EPS) * gain
        return _rope_lanes(y, cg, sug, sdg, GQA_HEAD_DIM // 2)

    o = 768
    gq = proj(o, o + GQA_HEADS * LANES)
    for j in range(GQA_HEADS):
        qg_ref[0, :, j * LANES:(j + 1) * LANES] = head(
            gq[:, j * LANES:(j + 1) * LANES], gqn_ref[...]).astype(BF16)
    o += GQA_HEADS * LANES
    gk = proj(o, o + GQA_KV_HEADS * LANES)
    for j in range(GQA_KV_HEADS):
        kj = head(gk[:, j * LANES:(j + 1) * LANES], gkn_ref[...])
        kg_ref[0, j, 0] = kj.T.astype(BF16)
    o += GQA_KV_HEADS * LANES
    vg_ref[0, :, :LANES] = proj(o, o + LANES).astype(BF16)
    vg_ref[0, :, LANES:] = jnp.ones((x.shape[0], LANES), BF16)
    o += LANES

    cq = _rms(proj(o, o + MLA_Q_RANK), mqn_ref[...]).astype(BF16)
    o += MLA_Q_RANK
    qm = jnp.dot(cq, wuq_ref[...], preferred_element_type=F32)
    cmq, sumq, sdmq = cmq_ref[...], sumq_ref[...], sdmq_ref[...]
    for j in range(MLA_HEADS):
        qm_ref[0, :, j * LANES:(j + 1) * LANES] = _rope_lanes(
            qm[:, j * LANES:(j + 1) * LANES], cmq, sumq, sdmq, MLA_ROPE_DIM // 2).astype(BF16)

    ckv = _rms(proj(o, o + MLA_KV_RANK), mkvn_ref[...]).astype(BF16)
    o += MLA_KV_RANK
    kpe = _rope_lanes(proj(o, o + LANES), cmk_ref[...], sumk_ref[...], sdmk_ref[...],
                      MLA_ROPE_DIM // 2)
    kn = jnp.dot(ckv, wukvk_ref[...], preferred_element_type=F32)
    for j in range(MLA_HEADS):
        km_ref[0, j, 0] = (kn[:, j * LANES:(j + 1) * LANES] + kpe).T.astype(BF16)
    vm = jnp.dot(ckv, wukvv_ref[...], preferred_element_type=F32) + vones_ref[...]
    vm_ref[0] = vm.astype(BF16)


def _kin_call(x, gpre, win_p, gqn, gkn, tg, mqn, wuq_p, tmq, mkvn, wukvk_p, wukvv_p, tmk, vones):
    B, L, D = x.shape
    tm = TM_IN
    nt = L // tm
    ncol = win_p.shape[1]

    def full(a):
        return pl.BlockSpec(a.shape, lambda b, i: (0,) * a.ndim)

    def rows(w):
        return pl.BlockSpec((tm, w), lambda b, i: (i, 0))

    in_specs = [pl.BlockSpec((1, tm, D), lambda b, i: (b, i, 0)), full(gpre), full(win_p),
                full(gqn), full(gkn), rows(LANES), rows(LANES), rows(LANES),
                full(mqn), full(wuq_p), rows(LANES), rows(LANES), rows(LANES),
                full(mkvn), full(wukvk_p), full(wukvv_p), rows(LANES), rows(LANES), rows(LANES),
                full(vones)]
    out_shape = [
        jax.ShapeDtypeStruct((B, L, 768), F32),
        jax.ShapeDtypeStruct((B, L, GQA_HEADS * LANES), BF16),
        jax.ShapeDtypeStruct((B, GQA_KV_HEADS, nt, LANES, tm), BF16),
        jax.ShapeDtypeStruct((B, L, 2 * LANES), BF16),
        jax.ShapeDtypeStruct((B, L, MLA_HEADS * LANES), BF16),
        jax.ShapeDtypeStruct((B, MLA_HEADS, nt, LANES, tm), BF16),
        jax.ShapeDtypeStruct((B, L, 4 * LANES), BF16),
    ]
    out_specs = [
        pl.BlockSpec((1, tm, 768), lambda b, i: (b, i, 0)),
        pl.BlockSpec((1, tm, GQA_HEADS * LANES), lambda b, i: (b, i, 0)),
        pl.BlockSpec((1, GQA_KV_HEADS, 1, LANES, tm), lambda b, i: (b, 0, i, 0, 0)),
        pl.BlockSpec((1, tm, 2 * LANES), lambda b, i: (b, i, 0)),
        pl.BlockSpec((1, tm, MLA_HEADS * LANES), lambda b, i: (b, i, 0)),
        pl.BlockSpec((1, MLA_HEADS, 1, LANES, tm), lambda b, i: (b, 0, i, 0, 0)),
        pl.BlockSpec((1, tm, 4 * LANES), lambda b, i: (b, i, 0)),
    ]
    return pl.pallas_call(
        _kin_kernel, grid=(B, nt), in_specs=in_specs, out_specs=out_specs, out_shape=out_shape,
        compiler_params=_cparams(("parallel", "parallel")), name="in_proj",
    )(x, gpre, win_p, gqn, gkn, *tg, mqn, wuq_p, *tmq, mkvn, wukvk_p, wukvv_p, *tmk, vones)


def _attn_kernel(q_ref, k_ref, v_ref, o_ref, sa_ref, sb_ref, ma_ref, mb_ref, *,
                 n_stack, tq, n_chunks, tk):
    i = pl.program_id(2)
    sel = pl.program_id(1) % 2
    rows = n_stack * tq
    nblk = tk // LANES

    @pl.when(i == 0)
    def _():
        sb_ref[...] = jnp.zeros(sb_ref.shape, F32)
        mb_ref[...] = jnp.zeros(mb_ref.shape, F32)

    def step(sw_ref, mw_ref, sr_ref, mr_ref):
        q = jnp.concatenate([q_ref[0, :, j * LANES:(j + 1) * LANES] for j in range(n_stack)],
                            axis=0)
        mx = jnp.broadcast_to(jnp.max(mr_ref[...], axis=-1, keepdims=True), (rows, LANES))
        m = jnp.full((rows, LANES), -jnp.inf, F32)
        acc = jnp.zeros((rows, 2 * LANES), F32)
        for c in range(n_chunks):
            s = jnp.dot(q, k_ref[0, 0, c], preferred_element_type=F32)
            sw_ref[c] = s
            for j in range(nblk):
                m = jnp.maximum(m, s[:, j * LANES:(j + 1) * LANES])
            sp = sr_ref[c]
            p = jnp.concatenate(
                [jnp.exp2(sp[:, j * LANES:(j + 1) * LANES] - mx) for j in range(nblk)],
                axis=1).astype(BF16)
            acc = acc + jnp.dot(p, v_ref[0, c * tk:(c + 1) * tk, :], preferred_element_type=F32)
        mw_ref[...] = m
        o = acc[:, :LANES] / acc[:, LANES:]
        o = jnp.where(sel == 0, o, pltpu.roll(o, LANES // 2, 1))
        lane = lax.broadcasted_iota(jnp.int32, o.shape, 1)
        o = jnp.where(lane < LANES // 2, o, 0.0).astype(BF16)
        for j in range(n_stack):
            o_ref[0, :, j * LANES:(j + 1) * LANES] = o[j * tq:(j + 1) * tq]

    @pl.when(i % 2 == 0)
    def _():
        step(sa_ref, ma_ref, sb_ref, mb_ref)

    @pl.when(i % 2 == 1)
    def _():
        step(sb_ref, mb_ref, sa_ref, ma_ref)


def _attn_call(q, kt, v, *, n_stack, tq, v_per_pair, name):
    B, L, hq = q.shape
    G = kt.shape[1]
    n_chunks, tk = kt.shape[2], kt.shape[4]
    nq = L // tq
    rows = n_stack * tq
    if v_per_pair:
        v_map = lambda b, g, i: (b, 0, g // 2)
    else:
        v_map = lambda b, g, i: (b, 0, 0)
    kern = functools.partial(_attn_kernel, n_stack=n_stack, tq=tq, n_chunks=n_chunks, tk=tk)
    return pl.pallas_call(
        kern, grid=(B, G, nq + 1),
        in_specs=[pl.BlockSpec((1, tq, n_stack * LANES),
                               lambda b, g, i: (b, jnp.minimum(i, nq - 1), g)),
                  pl.BlockSpec((1, 1, n_chunks, LANES, tk), lambda b, g, i: (b, g, 0, 0, 0),
                               pipeline_mode=pl.Buffered(1)),
                  pl.BlockSpec((1, L, 2 * LANES), v_map, pipeline_mode=pl.Buffered(1))],
        out_specs=pl.BlockSpec((1, tq, n_stack * LANES),
                               lambda b, g, i: (b, jnp.maximum(i - 1, 0), g)),
        out_shape=jax.ShapeDtypeStruct((B, L, hq), BF16),
        scratch_shapes=[pltpu.VMEM((n_chunks, rows, tk), F32), pltpu.VMEM((n_chunks, rows, tk), F32),
                        pltpu.VMEM((rows, LANES), F32), pltpu.VMEM((rows, LANES), F32)],
        compiler_params=_cparams(("parallel", "parallel", "arbitrary")), name=name,
    )(q, kt, v)


def _hconv_kernel(x_ref, xp_ref, xn_ref, w_ref, b_ref, v_ref, x1_ref, x2_ref):
    i = pl.program_id(1)
    x = x_ref[0]
    tl = x.shape[0]
    prev = jnp.where(i > 0, xp_ref[0][7:8, :], 0.0)
    nxt = jnp.where(i < pl.num_programs(1) - 1, xn_ref[0][0:1, :], 0.0)
    r = lax.broadcasted_iota(jnp.int32, x.shape, 0)
    xm = jnp.where(r == 0, prev, pltpu.roll(x, 1, 0))
    xp = jnp.where(r == tl - 1, nxt, pltpu.roll(x, tl - 1, 0))
    uc = xm * w_ref[0:1, :] + x * w_ref[1:2, :] + xp * w_ref[2:3, :] + b_ref[...]
    v_ref[0] = uc[:, :HY_D]
    x1_ref[0] = uc[:, HY_D:2 * HY_D]
    x2_ref[0] = uc[:, 2 * HY_D:]


def _hconv_call(hy_in, w, b):
    B, L, C = hy_in.shape
    tl = TL_HCONV
    nb = tl // 8
    last = L // 8 - 1
    out = jax.ShapeDtypeStruct((B, L, HY_D), F32)
    ospec = pl.BlockSpec((1, tl, HY_D), lambda b_, i: (b_, i, 0))
    return pl.pallas_call(
        _hconv_kernel, grid=(B, L // tl),
        in_specs=[pl.BlockSpec((1, tl, C), lambda b_, i: (b_, i, 0)),
                  pl.BlockSpec((1, 8, C), lambda b_, i: (b_, jnp.maximum(i * nb - 1, 0), 0)),
                  pl.BlockSpec((1, 8, C), lambda b_, i: (b_, jnp.minimum((i + 1) * nb, last), 0)),
                  pl.BlockSpec((3, C), lambda b_, i: (0, 0)),
                  pl.BlockSpec((1, C), lambda b_, i: (0, 0))],
        out_specs=[ospec, ospec, ospec], out_shape=[out, out, out],
        compiler_params=_cparams(("parallel", "parallel")), name="hy_conv3",
    )(hy_in, hy_in, hy_in, w, b)


def _filt_kernel(z_ref, w1_ref, b1_ref, f1_ref, w2_ref, b2_ref, f2_ref, w3_ref, win_ref, o_ref):
    hp = lax.Precision.HIGHEST
    h = jnp.sin(f1_ref[...] * (jnp.dot(z_ref[...], w1_ref[...], precision=hp,
                                       preferred_element_type=F32) + b1_ref[...]))
    h = jnp.sin(f2_ref[...] * (jnp.dot(h, w2_ref[...], precision=hp,
                                       preferred_element_type=F32) + b2_ref[...]))
    k = jnp.dot(h, w3_ref[...], precision=hp, preferred_element_type=F32)
    n1 = lax.broadcasted_iota(jnp.int32, (k.shape[0], 2 * HY_D), 0) % FFT_N1
    kk = jnp.where(n1 >= FFT_N1 // 2, k[:, 2 * HY_D:], k[:, :2 * HY_D])
    win = win_ref[...]
    o_ref[...] = kk * jnp.concatenate([win, win], axis=-1)


def _filt_call(z_perm, w1, b1, f1, w2, b2, f2, w3, win_perm):
    n = z_perm.shape[0]
    tp = 1024

    def full(a):
        return pl.BlockSpec(a.shape, lambda i: (0,) * a.ndim)

    return pl.pallas_call(
        _filt_kernel, grid=(n // tp,),
        in_specs=[pl.BlockSpec((tp, z_perm.shape[1]), lambda i: (i, 0)), full(w1), full(b1), full(f1),
                  full(w2), full(b2), full(f2), full(w3),
                  pl.BlockSpec((tp, HY_D), lambda i: (i, 0))],
        out_specs=pl.BlockSpec((tp, 2 * HY_D), lambda i: (i, 0)),
        out_shape=jax.ShapeDtypeStruct((n, 2 * HY_D), F32),
        compiler_params=_cparams(("parallel",)), name="hy_filter",
    )(z_perm, w1, b1, f1, w2, b2, f2, w3, win_perm)


def _bm_kernel(g_ref, x_ref, o_ref, *, to, shared):
    for t in range(to):
        g = g_ref[0] if shared else g_ref[t]
        o_ref[t] = jnp.dot(g, x_ref[t].astype(BF16), preferred_element_type=F32)


def _bm_call(g, x, name, to=8):
    O, K, N = x.shape
    M = g.shape[1]
    shared = g.shape[0] == 1
    gspec = (pl.BlockSpec((1, M, K), lambda i: (0, 0, 0)) if shared
             else pl.BlockSpec((to, M, K), lambda i: (i, 0, 0)))
    return pl.pallas_call(
        functools.partial(_bm_kernel, to=to, shared=shared), grid=(O // to,),
        in_specs=[gspec, pl.BlockSpec((to, K, N), lambda i: (i, 0, 0))],
        out_specs=pl.BlockSpec((to, M, N), lambda i: (i, 0, 0)),
        out_shape=jax.ShapeDtypeStruct((O, M, N), F32),
        compiler_params=_cparams(("parallel",)), name=name,
    )(g, x)


def _convb_kernel(mb_ref, gc_ref, x_ref, kf_ref, o_ref, *, to):
    h = FFT_N2
    for t in range(to):
        xs = jnp.dot(mb_ref[...], x_ref[t].astype(BF16), preferred_element_type=F32)
        xr, xi = xs[:h], xs[h:]
        kr, ki = kf_ref[t, :h, :], kf_ref[t, h:, :]
        ys = jnp.concatenate([xr * kr - xi * ki, xr * ki + xi * kr], axis=0).astype(BF16)
        o_ref[t] = jnp.dot(gc_ref[t], ys, preferred_element_type=F32)


def _convb_call(mb, gc, xt, kf, order, to=4):
    O, K, N = xt.shape
    return pl.pallas_call(
        functools.partial(_convb_kernel, to=to), grid=(O // to,),
        in_specs=[pl.BlockSpec(mb.shape, lambda i: (0, 0)),
                  pl.BlockSpec((to, K, K), lambda i: (i, 0, 0)),
                  pl.BlockSpec((to, K, N), lambda i: (i, 0, 0)),
                  pl.BlockSpec((to, K, N), lambda i: (i, 0, order))],
        out_specs=pl.BlockSpec((to, K, N), lambda i: (i, 0, 0)),
        out_shape=jax.ShapeDtypeStruct((O, K, N), F32),
        compiler_params=_cparams(("parallel",)), name="hy_spec_mul",
    )(mb, gc, xt, kf)


def _gate_kernel(y_ref, u_ref, g_ref, s_ref, o_ref):
    o_ref[...] = g_ref[...] * (y_ref[...] + u_ref[...] * s_ref[...])


def _gate_call(y, u, gate, skip):
    B, L, C = y.shape
    tl = 2048
    spec = pl.BlockSpec((1, tl, C), lambda b, i: (b, i, 0))
    return pl.pallas_call(
        _gate_kernel, grid=(B, L // tl),
        in_specs=[spec, spec, spec, pl.BlockSpec((1, 1, C), lambda b, i: (0, 0, 0))],
        out_specs=spec, out_shape=jax.ShapeDtypeStruct((B, L, C), F32),
        compiler_params=_cparams(("parallel", "parallel")), name="hy_gate",
    )(y, u, gate, skip.reshape(1, 1, C))


def _swap_outer(a):
    O, P2, N = a.shape
    P = P2 // 2
    return a.reshape(O, 2, P, N).transpose(2, 1, 0, 3).reshape(P, 2 * O, N)


def _dft_tables():
    n = FFT_N1 * FFT_N2
    k = np.arange(FFT_N1)
    f = np.exp(-2j * np.pi * np.outer(k, k) / FFT_N1)
    t = np.exp(-2j * np.pi * np.outer(k, k) / n)
    return f, t, n


def _dft_matrices():
    f, t, n = _dft_tables()
    fr, fi = jnp.asarray(f.real, F32), jnp.asarray(f.imag, F32)
    tr, ti = jnp.asarray(t.real, F32), jnp.asarray(t.imag, F32)
    half = FFT_N1 // 2
    er = fr[None] * tr[:, :, None] - fi[None] * ti[:, :, None]
    ei = fr[None] * ti[:, :, None] + fi[None] * tr[:, :, None]
    ga = jnp.concatenate([jnp.concatenate([er[:, :, :half], -ei[:, :, :half]], axis=2),
                          jnp.concatenate([ei[:, :, :half], er[:, :, :half]], axis=2)], axis=1)
    gaf = jnp.concatenate([er, ei], axis=1)
    mb = jnp.concatenate([jnp.concatenate([fr, -fi], axis=1),
                          jnp.concatenate([fi, fr], axis=1)], axis=0)
    tct = jnp.transpose(tr)[:, :, None]
    tst = -jnp.transpose(ti)[:, :, None]
    gr = tct * fr[None] - tst * (-fi[None])
    gi = tct * (-fi[None]) + tst * fr[None]
    gc = jnp.concatenate([jnp.concatenate([gr, -gi], axis=2),
                          jnp.concatenate([gi, gr], axis=2)], axis=1)
    hr, hi = fr[:half] / n, -fi[:half] / n
    md = jnp.concatenate([jnp.concatenate([hr, -hi], axis=1),
                          jnp.concatenate([hi, hr], axis=1)], axis=0)
    return (ga.astype(BF16), gaf.astype(BF16), mb.astype(BF16), gc.astype(BF16),
            md.astype(BF16)[None])


def _hyena_positions(L):
    p = FFT_N2 * jnp.arange(FFT_N1)[None, :] + jnp.arange(FFT_N2)[:, None]
    pos = jnp.where(p < L, p, 2 * L - 1 - p).reshape(2 * L, 1).astype(F32)
    t = pos / (L - 1)
    w = 2.0 * math.pi * pos / L
    f = jnp.linspace(1e-4, HY_BANDS - 1, HY_BANDS, dtype=F32)[None, :]
    z = jnp.concatenate([t, jnp.cos(f * w), -jnp.sin(f * w),
                         jnp.zeros((2 * L, 64 - HY_EMB), F32)], axis=-1)
    max_decay = math.log(HY_DECAY_TARGET) / HY_FAST_DECAY_PCT
    min_decay = math.log(HY_DECAY_TARGET) / HY_SLOW_DECAY_PCT
    deltas = jnp.linspace(min_decay, max_decay, HY_D, dtype=F32)
    return z, jnp.exp(-t * jnp.abs(deltas)[None, :])


def _hyena_layer(hy_in, conv_w, conv_b, fw, skip, z_perm, win_perm, mats):
    B, L, _ = hy_in.shape
    ga, gaf, mb, gc, md = mats
    v, x1, x2 = _hconv_call(hy_in, conv_w, conv_b[None])
    kperm = _filt_call(z_perm, *fw, win_perm)
    ka = _bm_call(gaf, kperm.reshape(FFT_N2, FFT_N1, 2 * HY_D), "hy_fft_filt_a")
    kf = _bm_call(mb[None], _swap_outer(ka), "hy_fft_filt_b")

    def pack(u):
        return (u.reshape(B, FFT_N1 // 2, FFT_N2, HY_D).transpose(2, 0, 1, 3)
                .reshape(FFT_N2, FFT_N1, HY_D))

    def unpack(yt):
        return (yt.reshape(FFT_N2, B, FFT_N1 // 2, HY_D).transpose(1, 2, 0, 3)
                .reshape(B, L, HY_D))

    z = v
    for i, gate in enumerate((x1, x2)):
        a = _bm_call(ga, pack(z), "hy_fft_a")
        c = _convb_call(mb, gc, _swap_outer(a), kf, i)
        y = unpack(_bm_call(md, _swap_outer(c), "hy_fft_d"))
        z = _gate_call(y, z, gate, skip[i])
    return z


def _kout_kernel(x_ref, yh_ref, yg_ref, ym_ref, gh_ref, gg_ref, gm_ref, w_ref, gpost_ref, gffn_ref,
                 xo_ref, h_ref):
    a = _rms(yh_ref[0], gh_ref[...]).astype(BF16)
    yg = yg_ref[0].astype(F32)
    ms = jnp.sum(yg * yg, axis=-1, keepdims=True) * (1.0 / (GQA_HEADS * GQA_HEAD_DIM))
    b = (yg * lax.rsqrt(ms + NORM_EPS) * gg_ref[...]).astype(BF16)
    ym = ym_ref[0].astype(F32)
    ms = jnp.sum(ym * ym, axis=-1, keepdims=True) * (1.0 / (MLA_HEADS * MLA_V_DIM))
    c = (ym * lax.rsqrt(ms + NORM_EPS) * gm_ref[...]).astype(BF16)
    o1 = HY_D
    o2 = o1 + GQA_HEADS * LANES
    y = (jnp.dot(a, w_ref[:o1, :], preferred_element_type=F32)
         + jnp.dot(b, w_ref[o1:o2, :], preferred_element_type=F32)
         + jnp.dot(c, w_ref[o2:, :], preferred_element_type=F32))
    xo = x_ref[0] + _rms(y, gpost_ref[...])
    xo_ref[0] = xo
    h_ref[0] = _rms(xo, gffn_ref[...]).astype(BF16)


def _kout_call(x, yh, yg, ym, gh, gg, gm, w_p, gpost, gffn):
    B, L, D = x.shape
    tm = TM_OUT

    def rows(a):
        return pl.BlockSpec((1, tm, a.shape[2]), lambda b, i: (b, i, 0))

    def full(a):
        return pl.BlockSpec(a.shape, lambda b, i: (0,) * a.ndim)

    return pl.pallas_call(
        _kout_kernel, grid=(B, L // tm),
        in_specs=[rows(x), rows(yh), rows(yg), rows(ym), full(gh), full(gg), full(gm), full(w_p),
                  full(gpost), full(gffn)],
        out_specs=[rows(x), rows(x)],
        out_shape=[jax.ShapeDtypeStruct((B, L, D), F32), jax.ShapeDtypeStruct((B, L, D), BF16)],
        compiler_params=_cparams(("parallel", "parallel")), name="out_proj",
    )(x, yh, yg, ym, gh, gg, gm, w_p, gpost, gffn)


HALO = 16


def _ffn_kernel(h_ref, hp_ref, hn_ref, x_ref, wg_ref, wu_ref, cw_ref, cb_ref, wd_ref, gpost_ref,
                o_ref, acc_ref):
    i = pl.program_id(1)
    j = pl.program_id(2)
    tm = h_ref.shape[1]
    prev = jnp.where(i > 0, hp_ref[0], jnp.zeros_like(hp_ref[0]))
    nxt = jnp.where(i < pl.num_programs(1) - 1, hn_ref[0], jnp.zeros_like(hn_ref[0]))
    he = jnp.concatenate([prev, h_ref[0], nxt], axis=0)
    ext = tm + 2 * HALO

    def conv(w_ref, col):
        up = jnp.dot(he, w_ref[...], preferred_element_type=F32)
        um = pltpu.roll(up, 1, 0)[HALO:HALO + tm]
        upl = pltpu.roll(up, ext - 1, 0)[HALO:HALO + tm]
        return (um * cw_ref[0, 0:1, col] + up[HALO:HALO + tm] * cw_ref[0, 1:2, col]
                + upl * cw_ref[0, 2:3, col] + cb_ref[0, :, col])

    tf = wg_ref.shape[1]
    g = conv(wg_ref, slice(0, tf))
    u = conv(wu_ref, slice(tf, 2 * tf))
    gelu = 0.5 * g * (1.0 + jnp.tanh(math.sqrt(2.0 / math.pi) * (g + 0.044715 * (g * g * g))))
    part = jnp.dot((gelu * u).astype(BF16), wd_ref[...], preferred_element_type=F32)

    @pl.when(j == 0)
    def _():
        acc_ref[...] = part

    @pl.when(j > 0)
    def _():
        acc_ref[...] += part

    @pl.when(j == pl.num_programs(2) - 1)
    def _():
        o_ref[0] = x_ref[0] + _rms(acc_ref[...], gpost_ref[...])


def _ffn_call(h, x, w_up, cw, cb, w_down, gpost):
    B, L, D = x.shape
    tm, tf = TM_FFN, TF_FFN
    nj = D_FF // tf
    nb = tm // HALO
    last = L // HALO - 1
    cw_c = jnp.concatenate([cw[:, :D_FF].reshape(3, nj, tf), cw[:, D_FF:].reshape(3, nj, tf)],
                           axis=2).transpose(1, 0, 2)
    cb_c = jnp.concatenate([cb[:D_FF].reshape(nj, 1, tf), cb[D_FF:].reshape(nj, 1, tf)], axis=2)
    return pl.pallas_call(
        _ffn_kernel, grid=(B, L // tm, nj),
        in_specs=[pl.BlockSpec((1, tm, D), lambda b, i, j: (b, i, 0)),
                  pl.BlockSpec((1, HALO, D), lambda b, i, j: (b, jnp.maximum(i * nb - 1, 0), 0)),
                  pl.BlockSpec((1, HALO, D), lambda b, i, j: (b, jnp.minimum((i + 1) * nb, last), 0)),
                  pl.BlockSpec((1, tm, D), lambda b, i, j: (b, i, 0)),
                  pl.BlockSpec((D, tf), lambda b, i, j: (0, j)),
                  pl.BlockSpec((D, tf), lambda b, i, j: (0, nj + j)),
                  pl.BlockSpec((1, 3, 2 * tf), lambda b, i, j: (j, 0, 0)),
                  pl.BlockSpec((1, 1, 2 * tf), lambda b, i, j: (j, 0, 0)),
                  pl.BlockSpec((tf, D), lambda b, i, j: (j, 0)),
                  pl.BlockSpec((1, D), lambda b, i, j: (0, 0))],
        out_specs=pl.BlockSpec((1, tm, D), lambda b, i, j: (b, i, 0)),
        out_shape=jax.ShapeDtypeStruct((B, L, D), F32),
        scratch_shapes=[pltpu.VMEM((tm, D), F32)],
        compiler_params=_cparams(("parallel", "parallel", "arbitrary")), name="conv_ffn",
    )(h, h, h, x, w_up, w_up, cw_c, cb_c, w_down, gpost)


def _axial_tables(L, rot_dim):
    rows = L // GRID_W
    row_idx = jnp.broadcast_to(jnp.arange(rows)[:, None], (rows, GRID_W)).reshape(L)
    col_idx = jnp.broadcast_to(jnp.arange(GRID_W)[None, :], (rows, GRID_W)).reshape(L)
    n_axis = rot_dim // 4
    inv = ROPE_THETA ** (-jnp.arange(n_axis, dtype=F32) / n_axis)
    ang = jnp.concatenate([row_idx[:, None].astype(F32) * inv,
                           col_idx[:, None].astype(F32) * inv], axis=-1)
    return jnp.cos(ang), jnp.sin(ang)


def _rope_tables(L):
    def lanes(parts):
        used = sum(p.shape[1] for p in parts)
        return jnp.concatenate(parts + [jnp.zeros((L, LANES - used), F32)], axis=1)

    cg, sg = _axial_tables(L, GQA_HEAD_DIM)
    zg = jnp.zeros_like(sg)
    tg = (lanes([cg, cg]), lanes([-sg, zg]), lanes([zg, sg]))
    cm, sm = _axial_tables(L, MLA_ROPE_DIM)
    zm = jnp.zeros_like(sm)
    nope0 = jnp.zeros((L, MLA_NOPE_DIM), F32)
    nope1 = jnp.ones((L, MLA_NOPE_DIM), F32)
    tmk = (lanes([nope0, cm, cm]), lanes([nope0, -sm, zm]), lanes([nope0, zm, sm]))
    sc = (MLA_NOPE_DIM + MLA_ROPE_DIM) ** -0.5 * LOG2E
    tmq = (lanes([nope1, cm, cm]) * sc, tmk[1] * sc, tmk[2] * sc)
    return tg, tmq, tmk


def _pad_heads(w, n_heads, width):
    k = w.shape[0]
    return jnp.pad(w.reshape(k, n_heads, width), ((0, 0), (0, 0), (0, LANES - width))).reshape(
        k, n_heads * LANES)


def _pad_head_rows(w, n_heads, width):
    n = w.shape[1]
    return jnp.pad(w.reshape(n_heads, width, n), ((0, 0), (0, LANES - width), (0, 0))).reshape(
        n_heads * LANES, n)


def _pad_vec(g, n_heads, width):
    return jnp.pad(g.reshape(n_heads, width), ((0, 0), (0, LANES - width))).reshape(1, n_heads * LANES)


def kernel(x, mix_pre_norm, w_in, hy_conv_w, hy_conv_b, hy_filt_w1, hy_filt_b1, hy_filt_freq1,
           hy_filt_w2, hy_filt_b2, hy_filt_freq2, hy_filt_w3, hy_skip, gqa_q_norm, gqa_k_norm,
           mla_q_a_norm, mla_w_uq, mla_kv_a_norm, mla_w_ukv, hy_out_norm, gqa_out_norm,
           mla_out_norm, w_out, mix_post_norm, ffn_pre_norm, w_up, ffn_conv_w, ffn_conv_b,
           w_down, ffn_post_norm):
    B, L, D = x.shape
    assert B == 2 and 2 * L == FFT_N1 * FFT_N2 and D == D_MODEL
    depth = w_in.shape[0]
    tg, tmq, tmk = _rope_tables(L)
    z_perm, win_perm = _hyena_positions(L)
    mats = _dft_matrices()
    vones = jnp.tile(jnp.concatenate([jnp.zeros((1, LANES), F32), jnp.ones((1, LANES), F32)], axis=1),
                     (1, 2))
    hid = LANES - hy_filt_w1.shape[2]

    for l in range(depth):
        wl = w_in[l]
        o = 0
        parts = []
        for n, fn in ((768, None), (512, lambda w: _pad_heads(w, GQA_HEADS, GQA_HEAD_DIM)),
                      (128, lambda w: _pad_heads(w, GQA_KV_HEADS, GQA_HEAD_DIM)), (128, None),
                      (MLA_Q_RANK, None), (MLA_KV_RANK, None),
                      (MLA_ROPE_DIM, lambda w: jnp.pad(w, ((0, 0), (MLA_NOPE_DIM, LANES - MLA_NOPE_DIM - MLA_ROPE_DIM))))):
            w = wl[:, o:o + n]
            parts.append(w if fn is None else fn(w))
            o += n
        win_p = jnp.concatenate(parts, axis=1).astype(BF16)
        gqn = _pad_vec(gqa_q_norm[l] * (GQA_HEAD_DIM ** -0.5 * LOG2E), 1, GQA_HEAD_DIM)
        gkn = _pad_vec(gqa_k_norm[l], 1, GQA_HEAD_DIM)
        wuq_p = _pad_heads(mla_w_uq[l], MLA_HEADS, MLA_NOPE_DIM + MLA_ROPE_DIM).astype(BF16)
        wukv = mla_w_ukv[l].reshape(MLA_KV_RANK, MLA_HEADS, MLA_NOPE_DIM + MLA_V_DIM)
        wukvk_p = _pad_heads(wukv[:, :, :MLA_NOPE_DIM].reshape(MLA_KV_RANK, -1), MLA_HEADS,
                             MLA_NOPE_DIM).astype(BF16)
        wv = wukv[:, :, MLA_NOPE_DIM:].reshape(MLA_KV_RANK, 2, 2 * MLA_V_DIM)
        wukvv_p = jnp.pad(wv, ((0, 0), (0, 0), (0, LANES))).reshape(MLA_KV_RANK, 4 * LANES).astype(BF16)

        hy_in, qg, kg, vg, qm, km, vm = _kin_call(
            x, mix_pre_norm[l][None], win_p, gqn, gkn, tg, mla_q_a_norm[l][None], wuq_p, tmq,
            mla_kv_a_norm[l][None], wukvk_p, wukvv_p, tmk, vones)

        w3 = hy_filt_w3[l].reshape(-1, 2, 2, HY_D).transpose(0, 2, 1, 3).reshape(-1, 4 * HY_D)
        fw = (jnp.pad(hy_filt_w1[l], ((0, 64 - HY_EMB), (0, hid))),
              jnp.pad(hy_filt_b1[l], (0, hid))[None], jnp.pad(hy_filt_freq1[l], (0, hid))[None],
              jnp.pad(hy_filt_w2[l], ((0, hid), (0, hid))),
              jnp.pad(hy_filt_b2[l], (0, hid))[None], jnp.pad(hy_filt_freq2[l], (0, hid))[None],
              jnp.pad(w3, ((0, hid), (0, 0))))
        y_hy = _hyena_layer(hy_in, hy_conv_w[l], hy_conv_b[l], fw, hy_skip[l], z_perm, win_perm, mats)

        y_gqa = _attn_call(qg, kg, vg, n_stack=GQA_HEADS // GQA_KV_HEADS, tq=TQ_GQA,
                           v_per_pair=False, name="attn_gqa")
        y_mla = _attn_call(qm, km, vm, n_stack=1, tq=TQ_MLA, v_per_pair=True, name="attn_mla")

        wo = w_out[l]
        o1, o2 = HY_D, HY_D + GQA_HEADS * GQA_HEAD_DIM
        wo_p = jnp.concatenate([wo[:o1], _pad_head_rows(wo[o1:o2], GQA_HEADS, GQA_HEAD_DIM),
                                _pad_head_rows(wo[o2:], MLA_HEADS, MLA_V_DIM)], axis=0).astype(BF16)
        x, h2 = _kout_call(x, y_hy, y_gqa, y_mla, hy_out_norm[l][None],
                           _pad_vec(gqa_out_norm[l], GQA_HEADS, GQA_HEAD_DIM),
                           _pad_vec(mla_out_norm[l], MLA_HEADS, MLA_V_DIM), wo_p,
                           mix_post_norm[l][None], ffn_pre_norm[l][None])
        x = _ffn_call(h2, x, w_up[l].astype(BF16), ffn_conv_w[l], ffn_conv_b[l],
                      w_down[l].astype(BF16), ffn_post_norm[l][None])
    return x
```

```python
import functools
import math

import numpy as np
import jax
import jax.numpy as jnp
from jax import lax
from jax.experimental import pallas as pl
from jax.experimental.pallas import tpu as pltpu

F32 = jnp.float32
BF16 = jnp.bfloat16

NORM_EPS = 1e-6
ROPE_THETA = 10000.0
GRID_W = 64
LOG2E = math.log2(math.e)

D_MODEL = 1024
HY_D = 256
HY_EMB = 33
HY_BANDS = 16
HY_DECAY_TARGET = 1e-2
HY_FAST_DECAY_PCT = 0.3
HY_SLOW_DECAY_PCT = 1.5
GQA_HEADS = 8
GQA_KV_HEADS = 2
GQA_HEAD_DIM = 64
MLA_HEADS = 4
MLA_Q_RANK = 256
MLA_KV_RANK = 128
MLA_NOPE_DIM = 64
MLA_ROPE_DIM = 32
MLA_V_DIM = 64
D_FF = 2816

LANES = 128
FFT_N1 = 128
FFT_N2 = 128

TM_IN = 512
TQ_GQA = 128
TQ_MLA = 512
TM_OUT = 512
TM_FFN = 512
TF_FFN = 256
TL_HCONV = 1024
VMEM_LIMIT = 56 * 1024 * 1024


def _cparams(sem):
    return pltpu.CompilerParams(dimension_semantics=sem, vmem_limit_bytes=VMEM_LIMIT)


def _rms(x, g):
    return x * lax.rsqrt(jnp.mean(x * x, axis=-1, keepdims=True) + NORM_EPS) * g


def _rope_lanes(y, c, s_up, s_dn, half):
    return y * c + pltpu.roll(y, LANES - half, 1) * s_up + pltpu.roll(y, half, 1) * s_dn


def _kin_kernel(x_ref, gpre_ref, win_ref, gqn_ref, gkn_ref, cg_ref, sug_ref, sdg_ref,
                mqn_ref, wuq_ref, cmq_ref, sumq_ref, sdmq_ref,
                mkvn_ref, wukvk_ref, wukvv_ref, cmk_ref, sumk_ref, sdmk_ref, vones_ref,
                hy_ref, qg_ref, kg_ref, vg_ref, qm_ref, km_ref, vm_ref):
    x = x_ref[0]
    h = _rms(x, gpre_ref[...]).astype(BF16)

    def proj(lo, hi):
        return jnp.dot(h, win_ref[:, lo:hi], preferred_element_type=F32)

    hy_ref[0] = proj(0, 768)

    cg, sug, sdg = cg_ref[...], sug_ref[...], sdg_ref[...]

    def head(xc, gain):
        ms = jnp.sum(xc * xc, axis=-1, keepdims=True) * (1.0 / GQA_HEAD_DIM)
        y = xc * lax.rsqrt(ms + NORM_EPS) * gain
        return _rope_lanes(y, cg, sug, sdg, GQA_HEAD_DIM // 2)

    o = 768
    gq = proj(o, o + GQA_HEADS * LANES)
    for j in range(GQA_HEADS):
        qg_ref[0, :, j * LANES:(j + 1) * LANES] = head(
            gq[:, j * LANES:(j + 1) * LANES], gqn_ref[...]).astype(BF16)
    o += GQA_HEADS * LANES
    gk = proj(o, o + GQA_KV_HEADS * LANES)
    for j in range(GQA_KV_HEADS):
        kj = head(gk[:, j * LANES:(j + 1) * LANES], gkn_ref[...])
        kg_ref[0, j, 0] = kj.T.astype(BF16)
    o += GQA_KV_HEADS * LANES
    vg_ref[0, :, :LANES] = proj(o, o + LANES).astype(BF16)
    vg_ref[0, :, LANES:] = jnp.ones((x.shape[0], LANES), BF16)
    o += LANES

    cq = _rms(proj(o, o + MLA_Q_RANK), mqn_ref[...]).astype(BF16)
    o += MLA_Q_RANK
    qm = jnp.dot(cq, wuq_ref[...], preferred_element_type=F32)
    cmq, sumq, sdmq = cmq_ref[...], sumq_ref[...], sdmq_ref[...]
    for j in range(MLA_HEADS):
        qm_ref[0, :, j * LANES:(j + 1) * LANES] = _rope_lanes(
            qm[:, j * LANES:(j + 1) * LANES], cmq, sumq, sdmq, MLA_ROPE_DIM // 2).astype(BF16)

    ckv = _rms(proj(o, o + MLA_KV_RANK), mkvn_ref[...]).astype(BF16)
    o += MLA_KV_RANK
    kpe = _rope_lanes(proj(o, o + LANES), cmk_ref[...], sumk_ref[...], sdmk_ref[...],
                      MLA_ROPE_DIM // 2)
    kn = jnp.dot(ckv, wukvk_ref[...], preferred_element_type=F32)
    for j in range(MLA_HEADS):
        km_ref[0, j, 0] = (kn[:, j * LANES:(j + 1) * LANES] + kpe).T.astype(BF16)
    vm = jnp.dot(ckv, wukvv_ref[...], preferred_element_type=F32) + vones_ref[...]
    vm_ref[0] = vm.astype(BF16)


def _kin_call(x, gpre, win_p, gqn, gkn, tg, mqn, wuq_p, tmq, mkvn, wukvk_p, wukvv_p, tmk, vones):
    B, L, D = x.shape
    tm = TM_IN
    nt = L // tm
    ncol = win_p.shape[1]

    def full(a):
        return pl.BlockSpec(a.shape, lambda b, i: (0,) * a.ndim)

    def rows(w):
        return pl.BlockSpec((tm, w), lambda b, i: (i, 0))

    in_specs = [pl.BlockSpec((1, tm, D), lambda b, i: (b, i, 0)), full(gpre), full(win_p),
                full(gqn), full(gkn), rows(LANES), rows(LANES), rows(LANES),
                full(mqn), full(wuq_p), rows(LANES), rows(LANES), rows(LANES),
                full(mkvn), full(wukvk_p), full(wukvv_p), rows(LANES), rows(LANES), rows(LANES),
                full(vones)]
    out_shape = [
        jax.ShapeDtypeStruct((B, L, 768), F32),
        jax.ShapeDtypeStruct((B, L, GQA_HEADS * LANES), BF16),
        jax.ShapeDtypeStruct((B, GQA_KV_HEADS, nt, LANES, tm), BF16),
        jax.ShapeDtypeStruct((B, L, 2 * LANES), BF16),
        jax.ShapeDtypeStruct((B, L, MLA_HEADS * LANES), BF16),
        jax.ShapeDtypeStruct((B, MLA_HEADS, nt, LANES, tm), BF16),
        jax.ShapeDtypeStruct((B, L, 4 * LANES), BF16),
    ]
    out_specs = [
        pl.BlockSpec((1, tm, 768), lambda b, i: (b, i, 0)),
        pl.BlockSpec((1, tm, GQA_HEADS * LANES), lambda b, i: (b, i, 0)),
        pl.BlockSpec((1, GQA_KV_HEADS, 1, LANES, tm), lambda b, i: (b, 0, i, 0, 0)),
        pl.BlockSpec((1, tm, 2 * LANES), lambda b, i: (b, i, 0)),
        pl.BlockSpec((1, tm, MLA_HEADS * LANES), lambda b, i: (b, i, 0)),
        pl.BlockSpec((1, MLA_HEADS, 1, LANES, tm), lambda b, i: (b, 0, i, 0, 0)),
        pl.BlockSpec((1, tm, 4 * LANES), lambda b, i: (b, i, 0)),
    ]
    return pl.pallas_call(
        _kin_kernel, grid=(B, nt), in_specs=in_specs, out_specs=out_specs, out_shape=out_shape,
        compiler_params=_cparams(("parallel", "parallel")), name="in_proj",
    )(x, gpre, win_p, gqn, gkn, *tg, mqn, wuq_p, *tmq, mkvn, wukvk_p, wukvv_p, *tmk, vones)


def _attn_kernel(q_ref, k_ref, v_ref, o_ref, sa_ref, sb_ref, ma_ref, mb_ref, *,
                 n_stack, tq, n_chunks, tk):
    i = pl.program_id(2)
    sel = pl.program_id(1) % 2
    rows = n_stack * tq
    nblk = tk // LANES

    @pl.when(i == 0)
    def _():
        sb_ref[...] = jnp.zeros(sb_ref.shape, F32)
        mb_ref[...] = jnp.zeros(mb_ref.shape, F32)

    def step(sw_ref, mw_ref, sr_ref, mr_ref):
        q = jnp.concatenate([q_ref[0, :, j * LANES:(j + 1) * LANES] for j in range(n_stack)],
                            axis=0)
        mx = jnp.broadcast_to(jnp.max(mr_ref[...], axis=-1, keepdims=True), (rows, LANES))
        m = jnp.full((rows, LANES), -jnp.inf, F32)
        acc = jnp.zeros((rows, 2 * LANES), F32)
        for c in range(n_chunks):
            s = jnp.dot(q, k_ref[0, 0, c], preferred_element_type=F32)
            sw_ref[c] = s
            for j in range(nblk):
                m = jnp.maximum(m, s[:, j * LANES:(j + 1) * LANES])
            sp = sr_ref[c]
            p = jnp.concatenate(
                [jnp.exp2(sp[:, j * LANES:(j + 1) * LANES] - mx) for j in range(nblk)],
                axis=1).astype(BF16)
            acc = acc + jnp.dot(p, v_ref[0, c * tk:(c + 1) * tk, :], preferred_element_type=F32)
        mw_ref[...] = m
        o = acc[:, :LANES] / acc[:, LANES:]
        o = jnp.where(sel == 0, o, pltpu.roll(o, LANES // 2, 1))
        lane = lax.broadcasted_iota(jnp.int32, o.shape, 1)
        o = jnp.where(lane < LANES // 2, o, 0.0).astype(BF16)
        for j in range(n_stack):
            o_ref[0, :, j * LANES:(j + 1) * LANES] = o[j * tq:(j + 1) * tq]

    @pl.when(i % 2 == 0)
    def _():
        step(sa_ref, ma_ref, sb_ref, mb_ref)

    @pl.when(i % 2 == 1)
    def _():
        step(sb_ref, mb_ref, sa_ref, ma_ref)


def _attn_call(q, kt, v, *, n_stack, tq, v_per_pair, name):
    B, L, hq = q.shape
    G = kt.shape[1]
    n_chunks, tk = kt.shape[2], kt.shape[4]
    nq = L // tq
    rows = n_stack * tq
    if v_per_pair:
        v_map = lambda b, g, i: (b, 0, g // 2)
    else:
        v_map = lambda b, g, i: (b, 0, 0)
    kern = functools.partial(_attn_kernel, n_stack=n_stack, tq=tq, n_chunks=n_chunks, tk=tk)
    return pl.pallas_call(
        kern, grid=(B, G, nq + 1),
        in_specs=[pl.BlockSpec((1, tq, n_stack * LANES),
                               lambda b, g, i: (b, jnp.minimum(i, nq - 1), g)),
                  pl.BlockSpec((1, 1, n_chunks, LANES, tk), lambda b, g, i: (b, g, 0, 0, 0),
                               pipeline_mode=pl.Buffered(1)),
                  pl.BlockSpec((1, L, 2 * LANES), v_map, pipeline_mode=pl.Buffered(1))],
        out_specs=pl.BlockSpec((1, tq, n_stack * LANES),
                               lambda b, g, i: (b, jnp.maximum(i - 1, 0), g)),
        out_shape=jax.ShapeDtypeStruct((B, L, hq), BF16),
        scratch_shapes=[pltpu.VMEM((n_chunks, rows, tk), F32), pltpu.VMEM((n_chunks, rows, tk), F32),
                        pltpu.VMEM((rows, LANES), F32), pltpu.VMEM((rows, LANES), F32)],
        compiler_params=_cparams(("parallel", "parallel", "arbitrary")), name=name,
    )(q, kt, v)


def _hconv_kernel(x_ref, xp_ref, xn_ref, w_ref, b_ref, v_ref, x1_ref, x2_ref):
    i = pl.program_id(1)
    x = x_ref[0]
    tl = x.shape[0]
    prev = jnp.where(i > 0, xp_ref[0][7:8, :], 0.0)
    nxt = jnp.where(i < pl.num_programs(1) - 1, xn_ref[0][0:1, :], 0.0)
    r = lax.broadcasted_iota(jnp.int32, x.shape, 0)
    xm = jnp.where(r == 0, prev, pltpu.roll(x, 1, 0))
    xp = jnp.where(r == tl - 1, nxt, pltpu.roll(x, tl - 1, 0))
    uc = xm * w_ref[0:1, :] + x * w_ref[1:2, :] + xp * w_ref[2:3, :] + b_ref[...]
    for r in range(tl // FFT_N2):
        blk = uc[r * FFT_N2:(r + 1) * FFT_N2]
        v_ref[:, r, :] = blk[:, :HY_D]
        x1_ref[:, r, :] = blk[:, HY_D:2 * HY_D]
        x2_ref[:, r, :] = blk[:, 2 * HY_D:]


def _hconv_call(hy_in, w, b):
    B, L, C = hy_in.shape
    tl = TL_HCONV
    nb = tl // 8
    last = L // 8 - 1
    nt = L // tl
    rows = tl // FFT_N2
    out = jax.ShapeDtypeStruct((FFT_N2, B * L // FFT_N2, HY_D), F32)
    ospec = pl.BlockSpec((FFT_N2, rows, HY_D), lambda b_, i: (0, b_ * nt + i, 0))
    return pl.pallas_call(
        _hconv_kernel, grid=(B, L // tl),
        in_specs=[pl.BlockSpec((1, tl, C), lambda b_, i: (b_, i, 0)),
                  pl.BlockSpec((1, 8, C), lambda b_, i: (b_, jnp.maximum(i * nb - 1, 0), 0)),
                  pl.BlockSpec((1, 8, C), lambda b_, i: (b_, jnp.minimum((i + 1) * nb, last), 0)),
                  pl.BlockSpec((3, C), lambda b_, i: (0, 0)),
                  pl.BlockSpec((1, C), lambda b_, i: (0, 0))],
        out_specs=[ospec, ospec, ospec], out_shape=[out, out, out],
        compiler_params=_cparams(("parallel", "parallel")), name="hy_conv3",
    )(hy_in, hy_in, hy_in, w, b)


def _filt_kernel(z_ref, w1_ref, b1_ref, f1_ref, w2_ref, b2_ref, f2_ref, w3_ref, win_ref, o_ref):
    hp = lax.Precision.HIGHEST
    h = jnp.sin(f1_ref[...] * (jnp.dot(z_ref[...], w1_ref[...], precision=hp,
                                       preferred_element_type=F32) + b1_ref[...]))
    h = jnp.sin(f2_ref[...] * (jnp.dot(h, w2_ref[...], precision=hp,
                                       preferred_element_type=F32) + b2_ref[...]))
    k = jnp.dot(h, w3_ref[...], precision=hp, preferred_element_type=F32)
    n1 = lax.broadcasted_iota(jnp.int32, (k.shape[0], 2 * HY_D), 0) % FFT_N1
    kk = jnp.where(n1 >= FFT_N1 // 2, k[:, 2 * HY_D:], k[:, :2 * HY_D])
    win = win_ref[...]
    o_ref[...] = kk * jnp.concatenate([win, win], axis=-1)


def _filt_call(z_perm, w1, b1, f1, w2, b2, f2, w3, win_perm):
    n = z_perm.shape[0]
    tp = 1024

    def full(a):
        return pl.BlockSpec(a.shape, lambda i: (0,) * a.ndim)

    return pl.pallas_call(
        _filt_kernel, grid=(n // tp,),
        in_specs=[pl.BlockSpec((tp, z_perm.shape[1]), lambda i: (i, 0)), full(w1), full(b1), full(f1),
                  full(w2), full(b2), full(f2), full(w3),
                  pl.BlockSpec((tp, HY_D), lambda i: (i, 0))],
        out_specs=pl.BlockSpec((tp, 2 * HY_D), lambda i: (i, 0)),
        out_shape=jax.ShapeDtypeStruct((n, 2 * HY_D), F32),
        compiler_params=_cparams(("parallel",)), name="hy_filter",
    )(z_perm, w1, b1, f1, w2, b2, f2, w3, win_perm)


def _bm_kernel(g_ref, x_ref, o_ref, *, to):
    for t in range(to):
        o_ref[t] = jnp.dot(g_ref[t], x_ref[t].astype(BF16), preferred_element_type=F32)


def _bm_call(g, x, name, to=8):
    O, K, N = x.shape
    M = g.shape[1]
    return pl.pallas_call(
        functools.partial(_bm_kernel, to=to), grid=(O // to,),
        in_specs=[pl.BlockSpec((to, M, K), lambda i: (i, 0, 0)),
                  pl.BlockSpec((to, K, N), lambda i: (i, 0, 0))],
        out_specs=pl.BlockSpec((to, M, N), lambda i: (i, 0, 0)),
        out_shape=jax.ShapeDtypeStruct((O, M, N), F32),
        compiler_params=_cparams(("parallel",)), name=name,
    )(g, x)


def _gather_ri(x_ref, j):
    return jnp.concatenate([x_ref[:, 0, j, :], x_ref[:, 1, j, :]], axis=0).astype(BF16)


def _gather_spec(a, to):
    return pl.BlockSpec((a.shape[0], 2, to, a.shape[3]), lambda i: (0, 0, i, 0))


def _filtb_kernel(mb_ref, x_ref, o_ref, *, to):
    for j in range(to):
        o_ref[j] = jnp.dot(mb_ref[...], _gather_ri(x_ref, j), preferred_element_type=F32)


def _filtb_call(mb, ka, to=8):
    O, _, P, N = ka.shape
    return pl.pallas_call(
        functools.partial(_filtb_kernel, to=to), grid=(P // to,),
        in_specs=[pl.BlockSpec(mb.shape, lambda i: (0, 0)), _gather_spec(ka, to)],
        out_specs=pl.BlockSpec((to, 2 * O, N), lambda i: (i, 0, 0)),
        out_shape=jax.ShapeDtypeStruct((P, 2 * O, N), F32),
        compiler_params=_cparams(("parallel",)), name="hy_fft_filt_b",
    )(mb, ka)


def _convb_kernel(mb_ref, gc_ref, x_ref, kf_ref, o_ref, *, to):
    h = FFT_N2
    for j in range(to):
        xs = jnp.dot(mb_ref[...], _gather_ri(x_ref, j), preferred_element_type=F32)
        xr, xi = xs[:h], xs[h:]
        kr, ki = kf_ref[j, :h, :], kf_ref[j, h:, :]
        ys = jnp.concatenate([xr * kr - xi * ki, xr * ki + xi * kr], axis=0).astype(BF16)
        o_ref[j] = jnp.dot(gc_ref[j], ys, preferred_element_type=F32)


def _convb_call(mb, gc, a, kf, order, to=8):
    O, _, P, N = a.shape
    return pl.pallas_call(
        functools.partial(_convb_kernel, to=to), grid=(P // to,),
        in_specs=[pl.BlockSpec(mb.shape, lambda i: (0, 0)),
                  pl.BlockSpec((to, 2 * O, 2 * O), lambda i: (i, 0, 0)),
                  _gather_spec(a, to),
                  pl.BlockSpec((to, 2 * O, N), lambda i: (i, 0, order))],
        out_specs=pl.BlockSpec((to, 2 * O, N), lambda i: (i, 0, 0)),
        out_shape=jax.ShapeDtypeStruct((P, 2 * O, N), F32),
        compiler_params=_cparams(("parallel",)), name="hy_spec_mul",
    )(mb, gc, a, kf)


def _convd_kernel(md_ref, c_ref, g_ref, u_ref, s_ref, o_ref, *, to, natural):
    for j in range(to):
        y = jnp.dot(md_ref[...], _gather_ri(c_ref, j), preferred_element_type=F32)
        z = g_ref[j] * (y + u_ref[j] * s_ref[...])
        if natural:
            o_ref[:, j, :] = z
        else:
            o_ref[j] = z


def _convd_call(md, c, gate, u, skip, natural, to=8):
    O, _, P, N = c.shape
    R = md.shape[0]
    tspec = pl.BlockSpec((to, R, N), lambda i: (i, 0, 0))
    if natural:
        ospec = pl.BlockSpec((R, to, N), lambda i: (0, i, 0))
        oshape = jax.ShapeDtypeStruct((R, P, N), F32)
    else:
        ospec, oshape = tspec, jax.ShapeDtypeStruct((P, R, N), F32)
    return pl.pallas_call(
        functools.partial(_convd_kernel, to=to, natural=natural), grid=(P // to,),
        in_specs=[pl.BlockSpec(md.shape, lambda i: (0, 0)), _gather_spec(c, to), tspec, tspec,
                  pl.BlockSpec((1, N), lambda i: (0, 0))],
        out_specs=ospec, out_shape=oshape,
        compiler_params=_cparams(("parallel",)), name="hy_fft_d",
    )(md, c, gate, u, skip.reshape(1, N))


def _dft_tables():
    n = FFT_N1 * FFT_N2
    k = np.arange(FFT_N1)
    f = np.exp(-2j * np.pi * np.outer(k, k) / FFT_N1)
    t = np.exp(-2j * np.pi * np.outer(k, k) / n)
    return f, t, n


def _dft_matrices():
    f, t, n = _dft_tables()
    fr, fi = jnp.asarray(f.real, F32), jnp.asarray(f.imag, F32)
    tr, ti = jnp.asarray(t.real, F32), jnp.asarray(t.imag, F32)
    half = FFT_N1 // 2
    er = fr[None] * tr[:, :, None] - fi[None] * ti[:, :, None]
    ei = fr[None] * ti[:, :, None] + fi[None] * tr[:, :, None]
    ga = jnp.concatenate([jnp.concatenate([er[:, :, :half], -ei[:, :, :half]], axis=2),
                          jnp.concatenate([ei[:, :, :half], er[:, :, :half]], axis=2)], axis=1)
    gaf = jnp.concatenate([er, ei], axis=1)
    mb = jnp.concatenate([jnp.concatenate([fr, -fi], axis=1),
                          jnp.concatenate([fi, fr], axis=1)], axis=0)
    tct = jnp.transpose(tr)[:, :, None]
    tst = -jnp.transpose(ti)[:, :, None]
    gr = tct * fr[None] - tst * (-fi[None])
    gi = tct * (-fi[None]) + tst * fr[None]
    gc = jnp.concatenate([jnp.concatenate([gr, -gi], axis=2),
                          jnp.concatenate([gi, gr], axis=2)], axis=1)
    hr, hi = fr[:half] / n, -fi[:half] / n
    md = jnp.concatenate([jnp.concatenate([hr, -hi], axis=1),
                          jnp.concatenate([hi, hr], axis=1)], axis=0)
    return (ga.astype(BF16), gaf.astype(BF16), mb.astype(BF16), gc.astype(BF16), md.astype(BF16))


def _hyena_positions(L):
    p = FFT_N2 * jnp.arange(FFT_N1)[None, :] + jnp.arange(FFT_N2)[:, None]
    pos = jnp.where(p < L, p, 2 * L - 1 - p).reshape(2 * L, 1).astype(F32)
    t = pos / (L - 1)
    w = 2.0 * math.pi * pos / L
    f = jnp.linspace(1e-4, HY_BANDS - 1, HY_BANDS, dtype=F32)[None, :]
    z = jnp.concatenate([t, jnp.cos(f * w), -jnp.sin(f * w),
                         jnp.zeros((2 * L, 64 - HY_EMB), F32)], axis=-1)
    max_decay = math.log(HY_DECAY_TARGET) / HY_FAST_DECAY_PCT
    min_decay = math.log(HY_DECAY_TARGET) / HY_SLOW_DECAY_PCT
    deltas = jnp.linspace(min_decay, max_decay, HY_D, dtype=F32)
    return z, jnp.exp(-t * jnp.abs(deltas)[None, :])


def _hyena_layer(hy_in, conv_w, conv_b, fw, skip, z_perm, win_perm, mats):
    B, L, _ = hy_in.shape
    ga, gaf, mb, gc, md = mats
    v, x1, x2 = _hconv_call(hy_in, conv_w, conv_b[None])
    kperm = _filt_call(z_perm, *fw, win_perm)
    ka = _bm_call(gaf, kperm.reshape(FFT_N2, FFT_N1, 2 * HY_D), "hy_fft_filt_a")
    kf = _filtb_call(mb, ka.reshape(FFT_N2, 2, FFT_N1, 2 * HY_D))

    z = v
    for i, gate in enumerate((x1, x2)):
        a = _bm_call(ga, z, "hy_fft_a")
        c = _convb_call(mb, gc, a.reshape(FFT_N2, 2, FFT_N1, HY_D), kf, i)
        z = _convd_call(md, c.reshape(FFT_N1, 2, FFT_N2, HY_D), gate, z, skip[i],
                        natural=(i == 1))
    return z.reshape(B, L, HY_D)


def _kout_kernel(x_ref, yh_ref, yg_ref, ym_ref, gh_ref, gg_ref, gm_ref, w_ref, gpost_ref, gffn_ref,
                 xo_ref, h_ref):
    a = _rms(yh_ref[0], gh_ref[...]).astype(BF16)
    yg = yg_ref[0].astype(F32)
    ms = jnp.sum(yg * yg, axis=-1, keepdims=True) * (1.0 / (GQA_HEADS * GQA_HEAD_DIM))
    b = (yg * lax.rsqrt(ms + NORM_EPS) * gg_ref[...]).astype(BF16)
    ym = ym_ref[0].astype(F32)
    ms = jnp.sum(ym * ym, axis=-1, keepdims=True) * (1.0 / (MLA_HEADS * MLA_V_DIM))
    c = (ym * lax.rsqrt(ms + NORM_EPS) * gm_ref[...]).astype(BF16)
    o1 = HY_D
    o2 = o1 + GQA_HEADS * LANES
    y = (jnp.dot(a, w_ref[:o1, :], preferred_element_type=F32)
         + jnp.dot(b, w_ref[o1:o2, :], preferred_element_type=F32)
         + jnp.dot(c, w_ref[o2:, :], preferred_element_type=F32))
    xo = x_ref[0] + _rms(y, gpost_ref[...])
    xo_ref[0] = xo
    h_ref[0] = _rms(xo, gffn_ref[...]).astype(BF16)


def _kout_call(x, yh, yg, ym, gh, gg, gm, w_p, gpost, gffn):
    B, L, D = x.shape
    tm = TM_OUT

    def rows(a):
        return pl.BlockSpec((1, tm, a.shape[2]), lambda b, i: (b, i, 0))

    def full(a):
        return pl.BlockSpec(a.shape, lambda b, i: (0,) * a.ndim)

    return pl.pallas_call(
        _kout_kernel, grid=(B, L // tm),
        in_specs=[rows(x), rows(yh), rows(yg), rows(ym), full(gh), full(gg), full(gm), full(w_p),
                  full(gpost), full(gffn)],
        out_specs=[rows(x), rows(x)],
        out_shape=[jax.ShapeDtypeStruct((B, L, D), F32), jax.ShapeDtypeStruct((B, L, D), BF16)],
        compiler_params=_cparams(("parallel", "parallel")), name="out_proj",
    )(x, yh, yg, ym, gh, gg, gm, w_p, gpost, gffn)


HALO = 16


def _ffn_kernel(h_ref, hp_ref, hn_ref, x_ref, wup_ref, cw_ref, cb_ref, wd_ref, gpost_ref,
                o_ref, act_ref):
    i = pl.program_id(1)
    tm = h_ref.shape[1]
    prev = jnp.where(i > 0, hp_ref[0], jnp.zeros_like(hp_ref[0]))
    nxt = jnp.where(i < pl.num_programs(1) - 1, hn_ref[0], jnp.zeros_like(hn_ref[0]))
    he = jnp.concatenate([prev, h_ref[0], nxt], axis=0)
    ext = tm + 2 * HALO
    tf = TF_FFN

    def conv(c0):
        up = jnp.dot(he, wup_ref[:, c0:c0 + tf], preferred_element_type=F32)
        um = pltpu.roll(up, 1, 0)[HALO:HALO + tm]
        upl = pltpu.roll(up, ext - 1, 0)[HALO:HALO + tm]
        return (um * cw_ref[0:1, c0:c0 + tf] + up[HALO:HALO + tm] * cw_ref[1:2, c0:c0 + tf]
                + upl * cw_ref[2:3, c0:c0 + tf] + cb_ref[:, c0:c0 + tf])

    for j in range(D_FF // tf):
        g = conv(j * tf)
        u = conv(D_FF + j * tf)
        gelu = 0.5 * g * (1.0 + jnp.tanh(math.sqrt(2.0 / math.pi) * (g + 0.044715 * (g * g * g))))
        act_ref[:, j * tf:(j + 1) * tf] = (gelu * u).astype(BF16)
    f = jnp.dot(act_ref[...], wd_ref[...], preferred_element_type=F32)
    o_ref[0] = x_ref[0] + _rms(f, gpost_ref[...])


def _ffn_call(h, x, w_up, cw, cb, w_down, gpost):
    B, L, D = x.shape
    tm = TM_FFN
    nb = tm // HALO
    last = L // HALO - 1

    def resident(a):
        return pl.BlockSpec(a.shape, lambda b, i: (0,) * a.ndim, pipeline_mode=pl.Buffered(1))

    cb = cb[None]
    return pl.pallas_call(
        _ffn_kernel, grid=(B, L // tm),
        in_specs=[pl.BlockSpec((1, tm, D), lambda b, i: (b, i, 0)),
                  pl.BlockSpec((1, HALO, D), lambda b, i: (b, jnp.maximum(i * nb - 1, 0), 0)),
                  pl.BlockSpec((1, HALO, D), lambda b, i: (b, jnp.minimum((i + 1) * nb, last), 0)),
                  pl.BlockSpec((1, tm, D), lambda b, i: (b, i, 0)),
                  resident(w_up), resident(cw), resident(cb), resident(w_down), resident(gpost)],
        out_specs=pl.BlockSpec((1, tm, D), lambda b, i: (b, i, 0)),
        out_shape=jax.ShapeDtypeStruct((B, L, D), F32),
        scratch_shapes=[pltpu.VMEM((tm, D_FF), BF16)],
        compiler_params=_cparams(("parallel", "parallel")), name="conv_ffn",
    )(h, h, h, x, w_up, cw, cb, w_down, gpost)


def _axial_tables(L, rot_dim):
    rows = L // GRID_W
    row_idx = jnp.broadcast_to(jnp.arange(rows)[:, None], (rows, GRID_W)).reshape(L)
    col_idx = jnp.broadcast_to(jnp.arange(GRID_W)[None, :], (rows, GRID_W)).reshape(L)
    n_axis = rot_dim // 4
    inv = ROPE_THETA ** (-jnp.arange(n_axis, dtype=F32) / n_axis)
    ang = jnp.concatenate([row_idx[:, None].astype(F32) * inv,
                           col_idx[:, None].astype(F32) * inv], axis=-1)
    return jnp.cos(ang), jnp.sin(ang)


def _rope_tables(L):
    def lanes(parts):
        used = sum(p.shape[1] for p in parts)
        return jnp.concatenate(parts + [jnp.zeros((L, LANES - used), F32)], axis=1)

    cg, sg = _axial_tables(L, GQA_HEAD_DIM)
    zg = jnp.zeros_like(sg)
    tg = (lanes([cg, cg]), lanes([-sg, zg]), lanes([zg, sg]))
    cm, sm = _axial_tables(L, MLA_ROPE_DIM)
    zm = jnp.zeros_like(sm)
    nope0 = jnp.zeros((L, MLA_NOPE_DIM), F32)
    nope1 = jnp.ones((L, MLA_NOPE_DIM), F32)
    tmk = (lanes([nope0, cm, cm]), lanes([nope0, -sm, zm]), lanes([nope0, zm, sm]))
    sc = (MLA_NOPE_DIM + MLA_ROPE_DIM) ** -0.5 * LOG2E
    tmq = (lanes([nope1, cm, cm]) * sc, tmk[1] * sc, tmk[2] * sc)
    return tg, tmq, tmk


def _pad_heads(w, n_heads, width):
    k = w.shape[0]
    return jnp.pad(w.reshape(k, n_heads, width), ((0, 0), (0, 0), (0, LANES - width))).reshape(
        k, n_heads * LANES)


def _pad_head_rows(w, n_heads, width):
    n = w.shape[1]
    return jnp.pad(w.reshape(n_heads, width, n), ((0, 0), (0, LANES - width), (0, 0))).reshape(
        n_heads * LANES, n)


def _pad_vec(g, n_heads, width):
    return jnp.pad(g.reshape(n_heads, width), ((0, 0), (0, LANES - width))).reshape(1, n_heads * LANES)


def kernel(x, mix_pre_norm, w_in, hy_conv_w, hy_conv_b, hy_filt_w1, hy_filt_b1, hy_filt_freq1,
           hy_filt_w2, hy_filt_b2, hy_filt_freq2, hy_filt_w3, hy_skip, gqa_q_norm, gqa_k_norm,
           mla_q_a_norm, mla_w_uq, mla_kv_a_norm, mla_w_ukv, hy_out_norm, gqa_out_norm,
           mla_out_norm, w_out, mix_post_norm, ffn_pre_norm, w_up, ffn_conv_w, ffn_conv_b,
           w_down, ffn_post_norm):
    B, L, D = x.shape
    assert B == 2 and 2 * L == FFT_N1 * FFT_N2 and D == D_MODEL
    depth = w_in.shape[0]
    tg, tmq, tmk = _rope_tables(L)
    z_perm, win_perm = _hyena_positions(L)
    mats = _dft_matrices()
    vones = jnp.tile(jnp.concatenate([jnp.zeros((1, LANES), F32), jnp.ones((1, LANES), F32)], axis=1),
                     (1, 2))
    hid = LANES - hy_filt_w1.shape[2]

    for l in range(depth):
        wl = w_in[l]
        o = 0
        parts = []
        for n, fn in ((768, None), (512, lambda w: _pad_heads(w, GQA_HEADS, GQA_HEAD_DIM)),
                      (128, lambda w: _pad_heads(w, GQA_KV_HEADS, GQA_HEAD_DIM)), (128, None),
                      (MLA_Q_RANK, None), (MLA_KV_RANK, None),
                      (MLA_ROPE_DIM, lambda w: jnp.pad(w, ((0, 0), (MLA_NOPE_DIM, LANES - MLA_NOPE_DIM - MLA_ROPE_DIM))))):
            w = wl[:, o:o + n]
            parts.append(w if fn is None else fn(w))
            o += n
        win_p = jnp.concatenate(parts, axis=1).astype(BF16)
        gqn = _pad_vec(gqa_q_norm[l] * (GQA_HEAD_DIM ** -0.5 * LOG2E), 1, GQA_HEAD_DIM)
        gkn = _pad_vec(gqa_k_norm[l], 1, GQA_HEAD_DIM)
        wuq_p = _pad_heads(mla_w_uq[l], MLA_HEADS, MLA_NOPE_DIM + MLA_ROPE_DIM).astype(BF16)
        wukv = mla_w_ukv[l].reshape(MLA_KV_RANK, MLA_HEADS, MLA_NOPE_DIM + MLA_V_DIM)
        wukvk_p = _pad_heads(wukv[:, :, :MLA_NOPE_DIM].reshape(MLA_KV_RANK, -1), MLA_HEADS,
                             MLA_NOPE_DIM).astype(BF16)
        wv = wukv[:, :, MLA_NOPE_DIM:].reshape(MLA_KV_RANK, 2, 2 * MLA_V_DIM)
        wukvv_p = jnp.pad(wv, ((0, 0), (0, 0), (0, LANES))).reshape(MLA_KV_RANK, 4 * LANES).astype(BF16)

        hy_in, qg, kg, vg, qm, km, vm = _kin_call(
            x, mix_pre_norm[l][None], win_p, gqn, gkn, tg, mla_q_a_norm[l][None], wuq_p, tmq,
            mla_kv_a_norm[l][None], wukvk_p, wukvv_p, tmk, vones)

        w3 = hy_filt_w3[l].reshape(-1, 2, 2, HY_D).transpose(0, 2, 1, 3).reshape(-1, 4 * HY_D)
        fw = (jnp.pad(hy_filt_w1[l], ((0, 64 - HY_EMB), (0, hid))),
              jnp.pad(hy_filt_b1[l], (0, hid))[None], jnp.pad(hy_filt_freq1[l], (0, hid))[None],
              jnp.pad(hy_filt_w2[l], ((0, hid), (0, hid))),
              jnp.pad(hy_filt_b2[l], (0, hid))[None], jnp.pad(hy_filt_freq2[l], (0, hid))[None],
              jnp.pad(w3, ((0, hid), (0, 0))))
        y_hy = _hyena_layer(hy_in, hy_conv_w[l], hy_conv_b[l], fw, hy_skip[l], z_perm, win_perm, mats)

        y_gqa = _attn_call(qg, kg, vg, n_stack=GQA_HEADS // GQA_KV_HEADS, tq=TQ_GQA,
                           v_per_pair=False, name="attn_gqa")
        y_mla = _attn_call(qm, km, vm, n_stack=1, tq=TQ_MLA, v_per_pair=True, name="attn_mla")

        wo = w_out[l]
        o1, o2 = HY_D, HY_D + GQA_HEADS * GQA_HEAD_DIM
        wo_p = jnp.concatenate([wo[:o1], _pad_head_rows(wo[o1:o2], GQA_HEADS, GQA_HEAD_DIM),
                                _pad_head_rows(wo[o2:], MLA_HEADS, MLA_V_DIM)], axis=0).astype(BF16)
        x, h2 = _kout_call(x, y_hy, y_gqa, y_mla, hy_out_norm[l][None],
                           _pad_vec(gqa_out_norm[l], GQA_HEADS, GQA_HEAD_DIM),
                           _pad_vec(mla_out_norm[l], MLA_HEADS, MLA_V_DIM), wo_p,
                           mix_post_norm[l][None], ffn_pre_norm[l][None])
        x = _ffn_call(h2, x, w_up[l].astype(BF16), ffn_conv_w[l], ffn_conv_b[l],
                      w_down[l].astype(BF16), ffn_post_norm[l][None])
    return x
```

```python
import functools
import math

import numpy as np
import jax
import jax.numpy as jnp
from jax import lax
from jax.experimental import pallas as pl
from jax.experimental.pallas import tpu as pltpu

F32 = jnp.float32
BF16 = jnp.bfloat16

NORM_EPS = 1e-6
ROPE_THETA = 10000.0
GRID_W = 64
LOG2E = math.log2(math.e)

D_MODEL = 1024
HY_D = 256
HY_EMB = 33
HY_BANDS = 16
HY_DECAY_TARGET = 1e-2
HY_FAST_DECAY_PCT = 0.3
HY_SLOW_DECAY_PCT = 1.5
GQA_HEADS = 8
GQA_KV_HEADS = 2
GQA_HEAD_DIM = 64
MLA_HEADS = 4
MLA_Q_RANK = 256
MLA_KV_RANK = 128
MLA_NOPE_DIM = 64
MLA_ROPE_DIM = 32
MLA_V_DIM = 64
D_FF = 2816

LANES = 128
FFT_N1 = 128
FFT_N2 = 128

TM_IN = 512
TQ_GQA = 128
TQ_MLA = 256
TM_OUT = 512
TM_FFN = 512
TF_FFN = 256
TL_HCONV = 1024
VMEM_LIMIT = 56 * 1024 * 1024


def _cparams(sem):
    return pltpu.CompilerParams(dimension_semantics=sem, vmem_limit_bytes=VMEM_LIMIT)


def _rms(x, g):
    return x * lax.rsqrt(jnp.mean(x * x, axis=-1, keepdims=True) + NORM_EPS) * g


def _rope_lanes(y, c, s_up, s_dn, half):
    return y * c + pltpu.roll(y, LANES - half, 1) * s_up + pltpu.roll(y, half, 1) * s_dn


def _kin_kernel(x_ref, gpre_ref, win_ref, gqn_ref, gkn_ref, cg_ref, sug_ref, sdg_ref,
                mqn_ref, wuq_ref, cmq_ref, sumq_ref, sdmq_ref,
                mkvn_ref, wukvk_ref, wukvv_ref, cmk_ref, sumk_ref, sdmk_ref,
                hy_ref, qg_ref, kg_ref, vg_ref, qm_ref, km_ref, vm_ref):
    x = x_ref[0]
    h = _rms(x, gpre_ref[...]).astype(BF16)

    def proj(lo, hi):
        return jnp.dot(h, win_ref[:, lo:hi], preferred_element_type=F32)

    hy_ref[0] = proj(0, 768)

    cg, sug, sdg = cg_ref[...], sug_ref[...], sdg_ref[...]

    def head(xc, gain):
        ms = jnp.sum(xc * xc, axis=-1, keepdims=True) * (1.0 / GQA_HEAD_DIM)
        y = xc * lax.rsqrt(ms + NORM_EPS) * gain
        return _rope_lanes(y, cg, sug, sdg, GQA_HEAD_DIM // 2)

    o = 768
    gq = proj(o, o + GQA_HEADS * LANES)
    for j in range(GQA_HEADS):
        qg_ref[0, :, j * LANES:(j + 1) * LANES] = head(
            gq[:, j * LANES:(j + 1) * LANES], gqn_ref[...]).astype(BF16)
    o += GQA_HEADS * LANES
    gk = proj(o, o + GQA_KV_HEADS * LANES)
    for j in range(GQA_KV_HEADS):
        kj = head(gk[:, j * LANES:(j + 1) * LANES], gkn_ref[...])
        kg_ref[0, j] = kj.astype(BF16)
    o += GQA_KV_HEADS * LANES
    vg_ref[0] = proj(o, o + LANES).T.astype(BF16)
    o += LANES

    cq = _rms(proj(o, o + MLA_Q_RANK), mqn_ref[...]).astype(BF16)
    o += MLA_Q_RANK
    qm = jnp.dot(cq, wuq_ref[...], preferred_element_type=F32)
    cmq, sumq, sdmq = cmq_ref[...], sumq_ref[...], sdmq_ref[...]
    for j in range(MLA_HEADS):
        qm_ref[0, :, j * LANES:(j + 1) * LANES] = _rope_lanes(
            qm[:, j * LANES:(j + 1) * LANES], cmq, sumq, sdmq, MLA_ROPE_DIM // 2).astype(BF16)

    ckv = _rms(proj(o, o + MLA_KV_RANK), mkvn_ref[...]).astype(BF16)
    o += MLA_KV_RANK
    kpe = _rope_lanes(proj(o, o + LANES), cmk_ref[...], sumk_ref[...], sdmk_ref[...],
                      MLA_ROPE_DIM // 2)
    kn = jnp.dot(ckv, wukvk_ref[...], preferred_element_type=F32)
    for j in range(MLA_HEADS):
        km_ref[0, j] = (kn[:, j * LANES:(j + 1) * LANES] + kpe).astype(BF16)
    vm = jnp.dot(ckv, wukvv_ref[...], preferred_element_type=F32)
    for j in range(vm.shape[1] // LANES):
        vm_ref[0, j * LANES:(j + 1) * LANES, :] = vm[:, j * LANES:(j + 1) * LANES].T.astype(BF16)


def _kin_call(x, gpre, win_p, gqn, gkn, tg, mqn, wuq_p, tmq, mkvn, wukvk_p, wukvv, tmk):
    B, L, D = x.shape
    tm = TM_IN
    nt = L // tm
    ncol = win_p.shape[1]

    def full(a):
        return pl.BlockSpec(a.shape, lambda b, i: (0,) * a.ndim)

    def rows(w):
        return pl.BlockSpec((tm, w), lambda b, i: (i, 0))

    in_specs = [pl.BlockSpec((1, tm, D), lambda b, i: (b, i, 0)), full(gpre), full(win_p),
                full(gqn), full(gkn), rows(LANES), rows(LANES), rows(LANES),
                full(mqn), full(wuq_p), rows(LANES), rows(LANES), rows(LANES),
                full(mkvn), full(wukvk_p), full(wukvv), rows(LANES), rows(LANES), rows(LANES)]
    gv_rows = GQA_KV_HEADS * GQA_HEAD_DIM
    mv_rows = MLA_HEADS * MLA_V_DIM
    out_shape = [
        jax.ShapeDtypeStruct((B, L, 768), F32),
        jax.ShapeDtypeStruct((B, L, GQA_HEADS * LANES), BF16),
        jax.ShapeDtypeStruct((B, GQA_KV_HEADS, L, LANES), BF16),
        jax.ShapeDtypeStruct((B, gv_rows, L), BF16),
        jax.ShapeDtypeStruct((B, L, MLA_HEADS * LANES), BF16),
        jax.ShapeDtypeStruct((B, MLA_HEADS, L, LANES), BF16),
        jax.ShapeDtypeStruct((B, mv_rows, L), BF16),
    ]
    out_specs = [
        pl.BlockSpec((1, tm, 768), lambda b, i: (b, i, 0)),
        pl.BlockSpec((1, tm, GQA_HEADS * LANES), lambda b, i: (b, i, 0)),
        pl.BlockSpec((1, GQA_KV_HEADS, tm, LANES), lambda b, i: (b, 0, i, 0)),
        pl.BlockSpec((1, gv_rows, tm), lambda b, i: (b, 0, i)),
        pl.BlockSpec((1, tm, MLA_HEADS * LANES), lambda b, i: (b, i, 0)),
        pl.BlockSpec((1, MLA_HEADS, tm, LANES), lambda b, i: (b, 0, i, 0)),
        pl.BlockSpec((1, mv_rows, tm), lambda b, i: (b, 0, i)),
    ]
    return pl.pallas_call(
        _kin_kernel, grid=(B, nt), in_specs=in_specs, out_specs=out_specs, out_shape=out_shape,
        compiler_params=_cparams(("parallel", "parallel")), name="in_proj",
    )(x, gpre, win_p, gqn, gkn, *tg, mqn, wuq_p, *tmq, mkvn, wukvk_p, wukvv, *tmk)


def _attn_kernel(q_ref, k_ref, vt_ref, o_ref, sa_ref, sb_ref, ma_ref, mb_ref, *,
                 n_kv, n_rep, tq, n_chunks, tk):
    i = pl.program_id(2)
    cols = n_rep * tq
    width = n_kv * cols
    grp = tk // 8

    @pl.when(i == 0)
    def _():
        sb_ref[...] = jnp.zeros(sb_ref.shape, F32)
        mb_ref[...] = jnp.zeros(mb_ref.shape, F32)

    def step(sw_ref, mw_ref, sr_ref, mr_ref):
        qs = [jnp.concatenate([q_ref[0, :, (a * n_rep + j) * LANES:(a * n_rep + j + 1) * LANES]
                               for j in range(n_rep)], axis=0) for a in range(n_kv)]
        mx = jnp.max(mr_ref[...], axis=0, keepdims=True)
        m = jnp.full((8, width), -jnp.inf, F32)
        l = jnp.zeros((8, width), F32)
        accs = [jnp.zeros((vt_ref.shape[2], cols), F32) for _ in range(n_kv)]
        for c in range(n_chunks):
            st = jnp.concatenate(
                [lax.dot_general(k_ref[0, a, c * tk:(c + 1) * tk, :], qs[a],
                                 (((1,), (1,)), ((), ())), preferred_element_type=F32)
                 for a in range(n_kv)], axis=1)
            sw_ref[c] = st
            m = jnp.maximum(m, jnp.max(st.reshape(grp, 8, width), axis=0))
            p = jnp.exp2(sr_ref[c] - mx)
            l = l + jnp.sum(p.reshape(grp, 8, width), axis=0)
            pb = p.astype(BF16)
            for a in range(n_kv):
                accs[a] = accs[a] + jnp.dot(vt_ref[0, a, :, c * tk:(c + 1) * tk],
                                            pb[:, a * cols:(a + 1) * cols],
                                            preferred_element_type=F32)
        mw_ref[...] = m
        ls = jnp.sum(l, axis=0, keepdims=True)
        heads = []
        for a in range(n_kv):
            oa = accs[a] / ls[:, a * cols:(a + 1) * cols]
            heads += [oa[:, j * tq:(j + 1) * tq] for j in range(n_rep)]
        o_ref[0] = jnp.concatenate(heads, axis=0).T.astype(BF16)

    @pl.when(i % 2 == 0)
    def _():
        step(sa_ref, ma_ref, sb_ref, mb_ref)

    @pl.when(i % 2 == 1)
    def _():
        step(sb_ref, mb_ref, sa_ref, ma_ref)


def _attn_call(q, k, vt, *, n_kv, n_rep, tq, name):
    B, L, hq = q.shape
    H = hq // LANES
    hkv, dv = vt.shape[1], vt.shape[2]
    G = hkv // n_kv
    hs = n_kv * n_rep
    tk = TM_IN
    n_chunks = L // tk
    nq = L // tq
    width = hs * tq
    kern = functools.partial(_attn_kernel, n_kv=n_kv, n_rep=n_rep, tq=tq, n_chunks=n_chunks, tk=tk)
    return pl.pallas_call(
        kern, grid=(B, G, nq + 1),
        in_specs=[pl.BlockSpec((1, tq, hs * LANES), lambda b, g, i: (b, jnp.minimum(i, nq - 1), g)),
                  pl.BlockSpec((1, n_kv, L, LANES), lambda b, g, i: (b, g, 0, 0),
                               pipeline_mode=pl.Buffered(1)),
                  pl.BlockSpec((1, n_kv, dv, L), lambda b, g, i: (b, g, 0, 0),
                               pipeline_mode=pl.Buffered(1))],
        out_specs=pl.BlockSpec((1, tq, hs * dv), lambda b, g, i: (b, jnp.maximum(i - 1, 0), g)),
        out_shape=jax.ShapeDtypeStruct((B, L, H * dv), BF16),
        scratch_shapes=[pltpu.VMEM((n_chunks, tk, width), F32), pltpu.VMEM((n_chunks, tk, width), F32),
                        pltpu.VMEM((8, width), F32), pltpu.VMEM((8, width), F32)],
        compiler_params=_cparams(("parallel", "parallel", "arbitrary")), name=name,
    )(q, k, vt)


def _hconv_kernel(x_ref, xp_ref, xn_ref, w_ref, b_ref, v_ref, x1_ref, x2_ref):
    i = pl.program_id(1)
    x = x_ref[0]
    tl = x.shape[0]
    prev = jnp.where(i > 0, xp_ref[0][7:8, :], 0.0)
    nxt = jnp.where(i < pl.num_programs(1) - 1, xn_ref[0][0:1, :], 0.0)
    r = lax.broadcasted_iota(jnp.int32, x.shape, 0)
    xm = jnp.where(r == 0, prev, pltpu.roll(x, 1, 0))
    xp = jnp.where(r == tl - 1, nxt, pltpu.roll(x, tl - 1, 0))
    uc = xm * w_ref[0:1, :] + x * w_ref[1:2, :] + xp * w_ref[2:3, :] + b_ref[...]
    for r in range(tl // FFT_N2):
        blk = uc[r * FFT_N2:(r + 1) * FFT_N2]
        v_ref[:, r, :] = blk[:, :HY_D]
        x1_ref[:, r, :] = blk[:, HY_D:2 * HY_D]
        x2_ref[:, r, :] = blk[:, 2 * HY_D:]


def _hconv_call(hy_in, w, b):
    B, L, C = hy_in.shape
    tl = TL_HCONV
    nb = tl // 8
    last = L // 8 - 1
    nt = L // tl
    rows = tl // FFT_N2
    out = jax.ShapeDtypeStruct((FFT_N2, B * L // FFT_N2, HY_D), F32)
    ospec = pl.BlockSpec((FFT_N2, rows, HY_D), lambda b_, i: (0, b_ * nt + i, 0))
    return pl.pallas_call(
        _hconv_kernel, grid=(B, L // tl),
        in_specs=[pl.BlockSpec((1, tl, C), lambda b_, i: (b_, i, 0)),
                  pl.BlockSpec((1, 8, C), lambda b_, i: (b_, jnp.maximum(i * nb - 1, 0), 0)),
                  pl.BlockSpec((1, 8, C), lambda b_, i: (b_, jnp.minimum((i + 1) * nb, last), 0)),
                  pl.BlockSpec((3, C), lambda b_, i: (0, 0)),
                  pl.BlockSpec((1, C), lambda b_, i: (0, 0))],
        out_specs=[ospec, ospec, ospec], out_shape=[out, out, out],
        compiler_params=_cparams(("parallel", "parallel")), name="hy_conv3",
    )(hy_in, hy_in, hy_in, w, b)


def _filt_kernel(z_ref, w1_ref, b1_ref, f1_ref, w2_ref, b2_ref, f2_ref, w3_ref, win_ref, o_ref):
    hp = lax.Precision.HIGHEST
    h = jnp.sin(f1_ref[...] * (jnp.dot(z_ref[...], w1_ref[...], precision=hp,
                                       preferred_element_type=F32) + b1_ref[...]))
    h = jnp.sin(f2_ref[...] * (jnp.dot(h, w2_ref[...], precision=hp,
                                       preferred_element_type=F32) + b2_ref[...]))
    k = jnp.dot(h, w3_ref[...], precision=hp, preferred_element_type=F32)
    n1 = lax.broadcasted_iota(jnp.int32, (k.shape[0], 2 * HY_D), 0) % FFT_N1
    kk = jnp.where(n1 >= FFT_N1 // 2, k[:, 2 * HY_D:], k[:, :2 * HY_D])
    win = win_ref[...]
    o_ref[...] = kk * jnp.concatenate([win, win], axis=-1)


def _filt_call(z_perm, w1, b1, f1, w2, b2, f2, w3, win_perm):
    n = z_perm.shape[0]
    tp = 1024

    def full(a):
        return pl.BlockSpec(a.shape, lambda i: (0,) * a.ndim)

    return pl.pallas_call(
        _filt_kernel, grid=(n // tp,),
        in_specs=[pl.BlockSpec((tp, z_perm.shape[1]), lambda i: (i, 0)), full(w1), full(b1), full(f1),
                  full(w2), full(b2), full(f2), full(w3),
                  pl.BlockSpec((tp, HY_D), lambda i: (i, 0))],
        out_specs=pl.BlockSpec((tp, 2 * HY_D), lambda i: (i, 0)),
        out_shape=jax.ShapeDtypeStruct((n, 2 * HY_D), F32),
        compiler_params=_cparams(("parallel",)), name="hy_filter",
    )(z_perm, w1, b1, f1, w2, b2, f2, w3, win_perm)


def _bm_kernel(g_ref, x_ref, o_ref, *, to):
    for t in range(to):
        o_ref[t] = jnp.dot(g_ref[t], x_ref[t].astype(BF16), preferred_element_type=F32)


def _bm_call(g, x, name, to=8):
    O, K, N = x.shape
    M = g.shape[1]
    return pl.pallas_call(
        functools.partial(_bm_kernel, to=to), grid=(O // to,),
        in_specs=[pl.BlockSpec((to, M, K), lambda i: (i, 0, 0)),
                  pl.BlockSpec((to, K, N), lambda i: (i, 0, 0))],
        out_specs=pl.BlockSpec((to, M, N), lambda i: (i, 0, 0)),
        out_shape=jax.ShapeDtypeStruct((O, M, N), F32),
        compiler_params=_cparams(("parallel",)), name=name,
    )(g, x)


def _gather_ri(x_ref, j):
    return jnp.concatenate([x_ref[:, 0, j, :], x_ref[:, 1, j, :]], axis=0).astype(BF16)


def _gather_spec(a, to):
    return pl.BlockSpec((a.shape[0], 2, to, a.shape[3]), lambda i: (0, 0, i, 0))


def _filtb_kernel(mb_ref, x_ref, o_ref, *, to):
    for j in range(to):
        o_ref[j] = jnp.dot(mb_ref[...], _gather_ri(x_ref, j), preferred_element_type=F32)


def _filtb_call(mb, ka, to=8):
    O, _, P, N = ka.shape
    return pl.pallas_call(
        functools.partial(_filtb_kernel, to=to), grid=(P // to,),
        in_specs=[pl.BlockSpec(mb.shape, lambda i: (0, 0)), _gather_spec(ka, to)],
        out_specs=pl.BlockSpec((to, 2 * O, N), lambda i: (i, 0, 0)),
        out_shape=jax.ShapeDtypeStruct((P, 2 * O, N), F32),
        compiler_params=_cparams(("parallel",)), name="hy_fft_filt_b",
    )(mb, ka)


def _convb_kernel(mb_ref, gc_ref, x_ref, kf_ref, o_ref, *, to):
    h = FFT_N2
    for j in range(to):
        xs = jnp.dot(mb_ref[...], _gather_ri(x_ref, j), preferred_element_type=F32)
        xr, xi = xs[:h], xs[h:]
        kr, ki = kf_ref[j, :h, :], kf_ref[j, h:, :]
        ys = jnp.concatenate([xr * kr - xi * ki, xr * ki + xi * kr], axis=0).astype(BF16)
        o_ref[j] = jnp.dot(gc_ref[j], ys, preferred_element_type=F32)


def _convb_call(mb, gc, a, kf, order, to=8):
    O, _, P, N = a.shape
    return pl.pallas_call(
        functools.partial(_convb_kernel, to=to), grid=(P // to,),
        in_specs=[pl.BlockSpec(mb.shape, lambda i: (0, 0)),
                  pl.BlockSpec((to, 2 * O, 2 * O), lambda i: (i, 0, 0)),
                  _gather_spec(a, to),
                  pl.BlockSpec((to, 2 * O, N), lambda i: (i, 0, order))],
        out_specs=pl.BlockSpec((to, 2 * O, N), lambda i: (i, 0, 0)),
        out_shape=jax.ShapeDtypeStruct((P, 2 * O, N), F32),
        compiler_params=_cparams(("parallel",)), name="hy_spec_mul",
    )(mb, gc, a, kf)


def _convd_kernel(md_ref, c_ref, g_ref, u_ref, s_ref, o_ref, *, to, natural):
    for j in range(to):
        y = jnp.dot(md_ref[...], _gather_ri(c_ref, j), preferred_element_type=F32)
        z = g_ref[j] * (y + u_ref[j] * s_ref[...])
        if natural:
            o_ref[:, j, :] = z
        else:
            o_ref[j] = z


def _convd_call(md, c, gate, u, skip, natural, to=8):
    O, _, P, N = c.shape
    R = md.shape[0]
    tspec = pl.BlockSpec((to, R, N), lambda i: (i, 0, 0))
    if natural:
        ospec = pl.BlockSpec((R, to, N), lambda i: (0, i, 0))
        oshape = jax.ShapeDtypeStruct((R, P, N), F32)
    else:
        ospec, oshape = tspec, jax.ShapeDtypeStruct((P, R, N), F32)
    return pl.pallas_call(
        functools.partial(_convd_kernel, to=to, natural=natural), grid=(P // to,),
        in_specs=[pl.BlockSpec(md.shape, lambda i: (0, 0)), _gather_spec(c, to), tspec, tspec,
                  pl.BlockSpec((1, N), lambda i: (0, 0))],
        out_specs=ospec, out_shape=oshape,
        compiler_params=_cparams(("parallel",)), name="hy_fft_d",
    )(md, c, gate, u, skip.reshape(1, N))


def _dft_tables():
    n = FFT_N1 * FFT_N2
    k = np.arange(FFT_N1)
    f = np.exp(-2j * np.pi * np.outer(k, k) / FFT_N1)
    t = np.exp(-2j * np.pi * np.outer(k, k) / n)
    return f, t, n


def _dft_matrices():
    f, t, n = _dft_tables()
    fr, fi = jnp.asarray(f.real, F32), jnp.asarray(f.imag, F32)
    tr, ti = jnp.asarray(t.real, F32), jnp.asarray(t.imag, F32)
    half = FFT_N1 // 2
    er = fr[None] * tr[:, :, None] - fi[None] * ti[:, :, None]
    ei = fr[None] * ti[:, :, None] + fi[None] * tr[:, :, None]
    ga = jnp.concatenate([jnp.concatenate([er[:, :, :half], -ei[:, :, :half]], axis=2),
                          jnp.concatenate([ei[:, :, :half], er[:, :, :half]], axis=2)], axis=1)
    gaf = jnp.concatenate([er, ei], axis=1)
    mb = jnp.concatenate([jnp.concatenate([fr, -fi], axis=1),
                          jnp.concatenate([fi, fr], axis=1)], axis=0)
    tct = jnp.transpose(tr)[:, :, None]
    tst = -jnp.transpose(ti)[:, :, None]
    gr = tct * fr[None] - tst * (-fi[None])
    gi = tct * (-fi[None]) + tst * fr[None]
    gc = jnp.concatenate([jnp.concatenate([gr, -gi], axis=2),
                          jnp.concatenate([gi, gr], axis=2)], axis=1)
    hr, hi = fr[:half] / n, -fi[:half] / n
    md = jnp.concatenate([jnp.concatenate([hr, -hi], axis=1),
                          jnp.concatenate([hi, hr], axis=1)], axis=0)
    return (ga.astype(BF16), gaf.astype(BF16), mb.astype(BF16), gc.astype(BF16), md.astype(BF16))


def _hyena_positions(L):
    p = FFT_N2 * np.arange(FFT_N1)[None, :] + np.arange(FFT_N2)[:, None]
    pos = np.where(p < L, p, 2 * L - 1 - p).reshape(2 * L, 1).astype(np.float64)
    t = pos / (L - 1)
    w = 2.0 * math.pi * pos / L
    f = np.linspace(1e-4, HY_BANDS - 1, HY_BANDS)[None, :]
    z = np.concatenate([t, np.cos(f * w), -np.sin(f * w), np.zeros((2 * L, 64 - HY_EMB))], axis=-1)
    max_decay = math.log(HY_DECAY_TARGET) / HY_FAST_DECAY_PCT
    min_decay = math.log(HY_DECAY_TARGET) / HY_SLOW_DECAY_PCT
    deltas = jnp.linspace(min_decay, max_decay, HY_D, dtype=F32)
    window = jnp.exp(-jnp.asarray(t, F32) * jnp.abs(deltas)[None, :])
    return jnp.asarray(z, F32), window


def _hyena_layer(hy_in, conv_w, conv_b, fw, skip, z_perm, win_perm, mats):
    B, L, _ = hy_in.shape
    ga, gaf, mb, gc, md = mats
    v, x1, x2 = _hconv_call(hy_in, conv_w, conv_b[None])
    kperm = _filt_call(z_perm, *fw, win_perm)
    ka = _bm_call(gaf, kperm.reshape(FFT_N2, FFT_N1, 2 * HY_D), "hy_fft_filt_a")
    kf = _filtb_call(mb, ka.reshape(FFT_N2, 2, FFT_N1, 2 * HY_D))

    z = v
    for i, gate in enumerate((x1, x2)):
        a = _bm_call(ga, z, "hy_fft_a")
        c = _convb_call(mb, gc, a.reshape(FFT_N2, 2, FFT_N1, HY_D), kf, i)
        z = _convd_call(md, c.reshape(FFT_N1, 2, FFT_N2, HY_D), gate, z, skip[i],
                        natural=(i == 1))
    return z.reshape(B, L, HY_D)


def _kout_kernel(x_ref, yh_ref, yg_ref, ym_ref, gh_ref, gg_ref, gm_ref, w_ref, gpost_ref, gffn_ref,
                 xo_ref, h_ref):
    a = _rms(yh_ref[0], gh_ref[...]).astype(BF16)
    b = _rms(yg_ref[0].astype(F32), gg_ref[...]).astype(BF16)
    c = _rms(ym_ref[0].astype(F32), gm_ref[...]).astype(BF16)
    o1 = HY_D
    o2 = o1 + GQA_HEADS * GQA_HEAD_DIM
    y = (jnp.dot(a, w_ref[:o1, :], preferred_element_type=F32)
         + jnp.dot(b, w_ref[o1:o2, :], preferred_element_type=F32)
         + jnp.dot(c, w_ref[o2:, :], preferred_element_type=F32))
    xo = x_ref[0] + _rms(y, gpost_ref[...])
    xo_ref[0] = xo
    h_ref[0] = _rms(xo, gffn_ref[...]).astype(BF16)


def _kout_call(x, yh, yg, ym, gh, gg, gm, w_p, gpost, gffn):
    B, L, D = x.shape
    tm = TM_OUT

    def rows(a):
        return pl.BlockSpec((1, tm, a.shape[2]), lambda b, i: (b, i, 0))

    def full(a):
        return pl.BlockSpec(a.shape, lambda b, i: (0,) * a.ndim)

    return pl.pallas_call(
        _kout_kernel, grid=(B, L // tm),
        in_specs=[rows(x), rows(yh), rows(yg), rows(ym), full(gh), full(gg), full(gm), full(w_p),
                  full(gpost), full(gffn)],
        out_specs=[rows(x), rows(x)],
        out_shape=[jax.ShapeDtypeStruct((B, L, D), F32), jax.ShapeDtypeStruct((B, L, D), BF16)],
        compiler_params=_cparams(("parallel", "parallel")), name="out_proj",
    )(x, yh, yg, ym, gh, gg, gm, w_p, gpost, gffn)


HALO = 16


def _ffn_kernel(h_ref, hp_ref, hn_ref, x_ref, wup_ref, cw_ref, cb_ref, wd_ref, gpost_ref,
                o_ref, act_ref):
    i = pl.program_id(1)
    tm = h_ref.shape[1]
    prev = jnp.where(i > 0, hp_ref[0], jnp.zeros_like(hp_ref[0]))
    nxt = jnp.where(i < pl.num_programs(1) - 1, hn_ref[0], jnp.zeros_like(hn_ref[0]))
    he = jnp.concatenate([prev, h_ref[0], nxt], axis=0)
    ext = tm + 2 * HALO
    tf = TF_FFN

    def conv(c0):
        up = jnp.dot(he, wup_ref[:, c0:c0 + tf], preferred_element_type=F32)
        um = pltpu.roll(up, 1, 0)[HALO:HALO + tm]
        upl = pltpu.roll(up, ext - 1, 0)[HALO:HALO + tm]
        return (um * cw_ref[0:1, c0:c0 + tf] + up[HALO:HALO + tm] * cw_ref[1:2, c0:c0 + tf]
                + upl * cw_ref[2:3, c0:c0 + tf] + cb_ref[:, c0:c0 + tf])

    for j in range(D_FF // tf):
        g = conv(j * tf)
        u = conv(D_FF + j * tf)
        gelu = 0.5 * g * (1.0 + jnp.tanh(math.sqrt(2.0 / math.pi) * (g + 0.044715 * (g * g * g))))
        act_ref[:, j * tf:(j + 1) * tf] = (gelu * u).astype(BF16)
    f = jnp.dot(act_ref[...], wd_ref[...], preferred_element_type=F32)
    o_ref[0] = x_ref[0] + _rms(f, gpost_ref[...])


def _ffn_call(h, x, w_up, cw, cb, w_down, gpost):
    B, L, D = x.shape
    tm = TM_FFN
    nb = tm // HALO
    last = L // HALO - 1

    def resident(a):
        return pl.BlockSpec(a.shape, lambda b, i: (0,) * a.ndim, pipeline_mode=pl.Buffered(1))

    cb = cb[None]
    return pl.pallas_call(
        _ffn_kernel, grid=(B, L // tm),
        in_specs=[pl.BlockSpec((1, tm, D), lambda b, i: (b, i, 0)),
                  pl.BlockSpec((1, HALO, D), lambda b, i: (b, jnp.maximum(i * nb - 1, 0), 0)),
                  pl.BlockSpec((1, HALO, D), lambda b, i: (b, jnp.minimum((i + 1) * nb, last), 0)),
                  pl.BlockSpec((1, tm, D), lambda b, i: (b, i, 0)),
                  resident(w_up), resident(cw), resident(cb), resident(w_down), resident(gpost)],
        out_specs=pl.BlockSpec((1, tm, D), lambda b, i: (b, i, 0)),
        out_shape=jax.ShapeDtypeStruct((B, L, D), F32),
        scratch_shapes=[pltpu.VMEM((tm, D_FF), BF16)],
        compiler_params=_cparams(("parallel", "parallel")), name="conv_ffn",
    )(h, h, h, x, w_up, cw, cb, w_down, gpost)


def _axial_tables(L, rot_dim):
    rows = L // GRID_W
    row_idx = jnp.broadcast_to(jnp.arange(rows)[:, None], (rows, GRID_W)).reshape(L)
    col_idx = jnp.broadcast_to(jnp.arange(GRID_W)[None, :], (rows, GRID_W)).reshape(L)
    n_axis = rot_dim // 4
    inv = ROPE_THETA ** (-jnp.arange(n_axis, dtype=F32) / n_axis)
    ang = jnp.concatenate([row_idx[:, None].astype(F32) * inv,
                           col_idx[:, None].astype(F32) * inv], axis=-1)
    return jnp.cos(ang), jnp.sin(ang)


def _rope_tables(L):
    def lanes(parts):
        used = sum(p.shape[1] for p in parts)
        return jnp.concatenate(parts + [jnp.zeros((L, LANES - used), F32)], axis=1)

    cg, sg = _axial_tables(L, GQA_HEAD_DIM)
    zg = jnp.zeros_like(sg)
    tg = (lanes([cg, cg]), lanes([-sg, zg]), lanes([zg, sg]))
    cm, sm = _axial_tables(L, MLA_ROPE_DIM)
    zm = jnp.zeros_like(sm)
    nope0 = jnp.zeros((L, MLA_NOPE_DIM), F32)
    nope1 = jnp.ones((L, MLA_NOPE_DIM), F32)
    tmk = (lanes([nope0, cm, cm]), lanes([nope0, -sm, zm]), lanes([nope0, zm, sm]))
    sc = (MLA_NOPE_DIM + MLA_ROPE_DIM) ** -0.5 * LOG2E
    tmq = (lanes([nope1, cm, cm]) * sc, tmk[1] * sc, tmk[2] * sc)
    return tg, tmq, tmk


def _pad_heads(w, n_heads, width):
    k = w.shape[0]
    return jnp.pad(w.reshape(k, n_heads, width), ((0, 0), (0, 0), (0, LANES - width))).reshape(
        k, n_heads * LANES)


def _pad_vec(g, n_heads, width):
    return jnp.pad(g.reshape(n_heads, width), ((0, 0), (0, LANES - width))).reshape(1, n_heads * LANES)


def kernel(x, mix_pre_norm, w_in, hy_conv_w, hy_conv_b, hy_filt_w1, hy_filt_b1, hy_filt_freq1,
           hy_filt_w2, hy_filt_b2, hy_filt_freq2, hy_filt_w3, hy_skip, gqa_q_norm, gqa_k_norm,
           mla_q_a_norm, mla_w_uq, mla_kv_a_norm, mla_w_ukv, hy_out_norm, gqa_out_norm,
           mla_out_norm, w_out, mix_post_norm, ffn_pre_norm, w_up, ffn_conv_w, ffn_conv_b,
           w_down, ffn_post_norm):
    B, L, D = x.shape
    assert B == 2 and 2 * L == FFT_N1 * FFT_N2 and D == D_MODEL
    depth = w_in.shape[0]
    tg, tmq, tmk = _rope_tables(L)
    z_perm, win_perm = _hyena_positions(L)
    mats = _dft_matrices()
    hid = LANES - hy_filt_w1.shape[2]

    for l in range(depth):
        wl = w_in[l]
        o = 0
        parts = []
        for n, fn in ((768, None), (512, lambda w: _pad_heads(w, GQA_HEADS, GQA_HEAD_DIM)),
                      (128, lambda w: _pad_heads(w, GQA_KV_HEADS, GQA_HEAD_DIM)), (128, None),
                      (MLA_Q_RANK, None), (MLA_KV_RANK, None),
                      (MLA_ROPE_DIM, lambda w: jnp.pad(w, ((0, 0), (MLA_NOPE_DIM, LANES - MLA_NOPE_DIM - MLA_ROPE_DIM))))):
            w = wl[:, o:o + n]
            parts.append(w if fn is None else fn(w))
            o += n
        win_p = jnp.concatenate(parts, axis=1).astype(BF16)
        gqn = _pad_vec(gqa_q_norm[l] * (GQA_HEAD_DIM ** -0.5 * LOG2E), 1, GQA_HEAD_DIM)
        gkn = _pad_vec(gqa_k_norm[l], 1, GQA_HEAD_DIM)
        wuq_p = _pad_heads(mla_w_uq[l], MLA_HEADS, MLA_NOPE_DIM + MLA_ROPE_DIM).astype(BF16)
        wukv = mla_w_ukv[l].reshape(MLA_KV_RANK, MLA_HEADS, MLA_NOPE_DIM + MLA_V_DIM)
        wukvk_p = _pad_heads(wukv[:, :, :MLA_NOPE_DIM].reshape(MLA_KV_RANK, -1), MLA_HEADS,
                             MLA_NOPE_DIM).astype(BF16)
        wukvv = wukv[:, :, MLA_NOPE_DIM:].reshape(MLA_KV_RANK, MLA_HEADS * MLA_V_DIM).astype(BF16)

        hy_in, qg, kg, vg, qm, km, vm = _kin_call(
            x, mix_pre_norm[l][None], win_p, gqn, gkn, tg, mla_q_a_norm[l][None], wuq_p, tmq,
            mla_kv_a_norm[l][None], wukvk_p, wukvv, tmk)

        w3 = hy_filt_w3[l].reshape(-1, 2, 2, HY_D).transpose(0, 2, 1, 3).reshape(-1, 4 * HY_D)
        fw = (jnp.pad(hy_filt_w1[l], ((0, 64 - HY_EMB), (0, hid))),
              jnp.pad(hy_filt_b1[l], (0, hid))[None], jnp.pad(hy_filt_freq1[l], (0, hid))[None],
              jnp.pad(hy_filt_w2[l], ((0, hid), (0, hid))),
              jnp.pad(hy_filt_b2[l], (0, hid))[None], jnp.pad(hy_filt_freq2[l], (0, hid))[None],
              jnp.pad(w3, ((0, hid), (0, 0))))
        y_hy = _hyena_layer(hy_in, hy_conv_w[l], hy_conv_b[l], fw, hy_skip[l], z_perm, win_perm, mats)

        y_gqa = _attn_call(qg, kg, vg.reshape(B, GQA_KV_HEADS, GQA_HEAD_DIM, L), n_kv=1,
                           n_rep=GQA_HEADS // GQA_KV_HEADS, tq=TQ_GQA, name="attn_gqa")
        y_mla = _attn_call(qm, km, vm.reshape(B, MLA_HEADS, MLA_V_DIM, L), n_kv=2, n_rep=1,
                           tq=TQ_MLA, name="attn_mla")

        x, h2 = _kout_call(x, y_hy, y_gqa, y_mla, hy_out_norm[l][None], gqa_out_norm[l][None],
                           mla_out_norm[l][None], w_out[l].astype(BF16),
                           mix_post_norm[l][None], ffn_pre_norm[l][None])
        x = _ffn_call(h2, x, w_up[l].astype(BF16), ffn_conv_w[l], ffn_conv_b[l],
                      w_down[l].astype(BF16), ffn_post_norm[l][None])
    return x
```

```python
import functools
import math

import numpy as np
import jax
import jax.numpy as jnp
from jax import lax
from jax.experimental import pallas as pl
from jax.experimental.pallas import tpu as pltpu

F32 = jnp.float32
BF16 = jnp.bfloat16

NORM_EPS = 1e-6
ROPE_THETA = 10000.0
GRID_W = 64
LOG2E = math.log2(math.e)

D_MODEL = 1024
HY_D = 256
HY_EMB = 33
HY_BANDS = 16
HY_DECAY_TARGET = 1e-2
HY_FAST_DECAY_PCT = 0.3
HY_SLOW_DECAY_PCT = 1.5
GQA_HEADS = 8
GQA_KV_HEADS = 2
GQA_HEAD_DIM = 64
MLA_HEADS = 4
MLA_Q_RANK = 256
MLA_KV_RANK = 128
MLA_NOPE_DIM = 64
MLA_ROPE_DIM = 32
MLA_V_DIM = 64
D_FF = 2816

LANES = 128
FFT_N1 = 128
FFT_N2 = 128

TM_IN = 512
TQ_GQA = 128
TQ_MLA = 256
TM_OUT = 512
TM_FFN = 512
TF_FFN = 256
TL_HCONV = 1024
FILT_HALF = 512
FILT_FEAT = 64
VMEM_LIMIT = 56 * 1024 * 1024


def _cparams(sem):
    return pltpu.CompilerParams(dimension_semantics=sem, vmem_limit_bytes=VMEM_LIMIT)


def _rms(x, g):
    return x * lax.rsqrt(jnp.mean(x * x, axis=-1, keepdims=True) + NORM_EPS) * g


def _kin_kernel(x_ref, gpre_ref, win_ref, gqn_ref, gqs_ref, gkn_ref, gks_ref, cg_ref, sg_ref,
                mqn_ref, wuq_ref, cmq_ref, smq_ref,
                mkvn_ref, wukvk_ref, wukvv_ref, cmk_ref, smk_ref,
                hy_ref, qg_ref, kg_ref, vg_ref, qm_ref, km_ref, vm_ref):
    x = x_ref[0]
    h = _rms(x, gpre_ref[...]).astype(BF16)
    cur = [0]

    def proj(n):
        lo = cur[0]
        cur[0] = lo + n
        return jnp.dot(h, win_ref[:, lo:lo + n], preferred_element_type=F32)

    hy_ref[0] = proj(768)

    cg, sg = cg_ref[...], sg_ref[...]

    def heads(n, gain, gain_sw, store):
        xa, xb = proj(n * LANES), proj(n * LANES)
        ca, sa = cg * gain, sg * gain_sw
        for j in range(n):
            xc, xs = xa[:, j * LANES:(j + 1) * LANES], xb[:, j * LANES:(j + 1) * LANES]
            ms = jnp.sum(xc * xc, axis=-1, keepdims=True) * (1.0 / GQA_HEAD_DIM)
            store(j, ((xc * ca + xs * sa) * lax.rsqrt(ms + NORM_EPS)).astype(BF16))

    def store_q(j, v):
        qg_ref[0, :, j * LANES:(j + 1) * LANES] = v

    def store_k(j, v):
        kg_ref[0, j] = v

    heads(GQA_HEADS, gqn_ref[...], gqs_ref[...], store_q)
    heads(GQA_KV_HEADS, gkn_ref[...], gks_ref[...], store_k)
    vg_ref[0] = proj(LANES).T.astype(BF16)

    cq = _rms(proj(MLA_Q_RANK), mqn_ref[...]).astype(BF16)
    qm = jnp.dot(cq, wuq_ref[...], preferred_element_type=F32)
    cmq, smq = cmq_ref[...], smq_ref[...]
    nq = MLA_HEADS * LANES
    for j in range(MLA_HEADS):
        qm_ref[0, :, j * LANES:(j + 1) * LANES] = (
            qm[:, j * LANES:(j + 1) * LANES] * cmq
            + qm[:, nq + j * LANES:nq + (j + 1) * LANES] * smq).astype(BF16)

    ckv = _rms(proj(MLA_KV_RANK), mkvn_ref[...]).astype(BF16)
    kpe = proj(LANES) * cmk_ref[...] + proj(LANES) * smk_ref[...]
    kn = jnp.dot(ckv, wukvk_ref[...], preferred_element_type=F32)
    for j in range(MLA_HEADS):
        km_ref[0, j] = (kn[:, j * LANES:(j + 1) * LANES] + kpe).astype(BF16)
    vm = jnp.dot(ckv, wukvv_ref[...], preferred_element_type=F32)
    for j in range(vm.shape[1] // LANES):
        vm_ref[0, j * LANES:(j + 1) * LANES, :] = vm[:, j * LANES:(j + 1) * LANES].T.astype(BF16)


def _kin_call(x, gpre, win_p, gq, gk, tg, mqn, wuq_p, tmq, mkvn, wukvk_p, wukvv, tmk):
    B, L, D = x.shape
    tm = TM_IN
    nt = L // tm

    def full(a):
        return pl.BlockSpec(a.shape, lambda b, i: (0,) * a.ndim, pipeline_mode=pl.Buffered(1))

    def rows(w):
        return pl.BlockSpec((tm, w), lambda b, i: (i, 0))

    in_specs = [pl.BlockSpec((1, tm, D), lambda b, i: (b, i, 0)), full(gpre), full(win_p),
                full(gq[0]), full(gq[1]), full(gk[0]), full(gk[1]), rows(LANES), rows(LANES),
                full(mqn), full(wuq_p), rows(LANES), rows(LANES),
                full(mkvn), full(wukvk_p), full(wukvv), rows(LANES), rows(LANES)]
    gv_rows = GQA_KV_HEADS * GQA_HEAD_DIM
    mv_rows = MLA_HEADS * MLA_V_DIM
    out_shape = [
        jax.ShapeDtypeStruct((B, L, 768), F32),
        jax.ShapeDtypeStruct((B, L, GQA_HEADS * LANES), BF16),
        jax.ShapeDtypeStruct((B, GQA_KV_HEADS, L, LANES), BF16),
        jax.ShapeDtypeStruct((B, gv_rows, L), BF16),
        jax.ShapeDtypeStruct((B, L, MLA_HEADS * LANES), BF16),
        jax.ShapeDtypeStruct((B, MLA_HEADS, L, LANES), BF16),
        jax.ShapeDtypeStruct((B, mv_rows, L), BF16),
    ]
    out_specs = [
        pl.BlockSpec((1, tm, 768), lambda b, i: (b, i, 0)),
        pl.BlockSpec((1, tm, GQA_HEADS * LANES), lambda b, i: (b, i, 0)),
        pl.BlockSpec((1, GQA_KV_HEADS, tm, LANES), lambda b, i: (b, 0, i, 0)),
        pl.BlockSpec((1, gv_rows, tm), lambda b, i: (b, 0, i)),
        pl.BlockSpec((1, tm, MLA_HEADS * LANES), lambda b, i: (b, i, 0)),
        pl.BlockSpec((1, MLA_HEADS, tm, LANES), lambda b, i: (b, 0, i, 0)),
        pl.BlockSpec((1, mv_rows, tm), lambda b, i: (b, 0, i)),
    ]
    return pl.pallas_call(
        _kin_kernel, grid=(B, nt), in_specs=in_specs, out_specs=out_specs, out_shape=out_shape,
        compiler_params=_cparams(("parallel", "parallel")), name="in_proj",
    )(x, gpre, win_p, *gq, *gk, *tg, mqn, wuq_p, *tmq, mkvn, wukvk_p, wukvv, *tmk)


def _attn_kernel(q_ref, k_ref, vt_ref, o_ref, sa_ref, sb_ref, ma_ref, mb_ref, *,
                 n_kv, n_rep, tq, n_chunks, tk):
    i = pl.program_id(2)
    cols = n_rep * tq
    width = n_kv * cols
    grp = tk // 8

    @pl.when(i == 0)
    def _():
        sb_ref[...] = jnp.zeros(sb_ref.shape, F32)
        mb_ref[...] = jnp.zeros(mb_ref.shape, F32)

    def step(sw_ref, mw_ref, sr_ref, mr_ref):
        qs = [jnp.concatenate([q_ref[0, :, (a * n_rep + j) * LANES:(a * n_rep + j + 1) * LANES]
                               for j in range(n_rep)], axis=0) for a in range(n_kv)]
        mx = jnp.max(mr_ref[...], axis=0, keepdims=True)
        m = jnp.full((8, width), -jnp.inf, F32)
        l = jnp.zeros((8, width), F32)
        accs = [jnp.zeros((vt_ref.shape[2], cols), F32) for _ in range(n_kv)]
        for c in range(n_chunks):
            st = jnp.concatenate(
                [lax.dot_general(k_ref[0, a, c * tk:(c + 1) * tk, :], qs[a],
                                 (((1,), (1,)), ((), ())), preferred_element_type=F32)
                 for a in range(n_kv)], axis=1)
            sw_ref[c] = st
            m = jnp.maximum(m, jnp.max(st.reshape(grp, 8, width), axis=0))
            p = jnp.exp2(sr_ref[c] - mx)
            l = l + jnp.sum(p.reshape(grp, 8, width), axis=0)
            pb = p.astype(BF16)
            for a in range(n_kv):
                accs[a] = accs[a] + jnp.dot(vt_ref[0, a, :, c * tk:(c + 1) * tk],
                                            pb[:, a * cols:(a + 1) * cols],
                                            preferred_element_type=F32)
        mw_ref[...] = m
        ls = jnp.sum(l, axis=0, keepdims=True)
        heads = []
        for a in range(n_kv):
            oa = accs[a] / ls[:, a * cols:(a + 1) * cols]
            heads += [oa[:, j * tq:(j + 1) * tq] for j in range(n_rep)]
        o_ref[0] = jnp.concatenate(heads, axis=0).T.astype(BF16)

    @pl.when(i % 2 == 0)
    def _():
        step(sa_ref, ma_ref, sb_ref, mb_ref)

    @pl.when(i % 2 == 1)
    def _():
        step(sb_ref, mb_ref, sa_ref, ma_ref)


def _attn_call(q, k, vt, *, n_kv, n_rep, tq, name):
    B, L, hq = q.shape
    H = hq // LANES
    hkv, dv = vt.shape[1], vt.shape[2]
    G = hkv // n_kv
    hs = n_kv * n_rep
    tk = TM_IN
    n_chunks = L // tk
    nq = L // tq
    width = hs * tq
    kern = functools.partial(_attn_kernel, n_kv=n_kv, n_rep=n_rep, tq=tq, n_chunks=n_chunks, tk=tk)
    return pl.pallas_call(
        kern, grid=(B, G, nq + 1),
        in_specs=[pl.BlockSpec((1, tq, hs * LANES), lambda b, g, i: (b, jnp.minimum(i, nq - 1), g)),
                  pl.BlockSpec((1, n_kv, L, LANES), lambda b, g, i: (b, g, 0, 0),
                               pipeline_mode=pl.Buffered(1)),
                  pl.BlockSpec((1, n_kv, dv, L), lambda b, g, i: (b, g, 0, 0),
                               pipeline_mode=pl.Buffered(1))],
        out_specs=pl.BlockSpec((1, tq, hs * dv), lambda b, g, i: (b, jnp.maximum(i - 1, 0), g)),
        out_shape=jax.ShapeDtypeStruct((B, L, H * dv), BF16),
        scratch_shapes=[pltpu.VMEM((n_chunks, tk, width), F32), pltpu.VMEM((n_chunks, tk, width), F32),
                        pltpu.VMEM((8, width), F32), pltpu.VMEM((8, width), F32)],
        compiler_params=_cparams(("parallel", "parallel", "arbitrary")), name=name,
    )(q, k, vt)


def _hconv_kernel(x_ref, xp_ref, xn_ref, w_ref, b_ref, v_ref, x1_ref, x2_ref):
    i = pl.program_id(1)
    x = x_ref[0]
    tl = x.shape[0]
    prev = jnp.where(i > 0, xp_ref[0][7:8, :], 0.0)
    nxt = jnp.where(i < pl.num_programs(1) - 1, xn_ref[0][0:1, :], 0.0)
    r = lax.broadcasted_iota(jnp.int32, x.shape, 0)
    xm = jnp.where(r == 0, prev, pltpu.roll(x, 1, 0))
    xp = jnp.where(r == tl - 1, nxt, pltpu.roll(x, tl - 1, 0))
    uc = xm * w_ref[0:1, :] + x * w_ref[1:2, :] + xp * w_ref[2:3, :] + b_ref[...]
    for r in range(tl // FFT_N2):
        blk = uc[r * FFT_N2:(r + 1) * FFT_N2]
        v_ref[:, r, :] = blk[:, :HY_D]
        x1_ref[:, r, :] = blk[:, HY_D:2 * HY_D]
        x2_ref[:, r, :] = blk[:, 2 * HY_D:]


def _hconv_call(hy_in, w, b):
    B, L, C = hy_in.shape
    tl = TL_HCONV
    nb = tl // 8
    last = L // 8 - 1
    nt = L // tl
    rows = tl // FFT_N2
    out = jax.ShapeDtypeStruct((FFT_N2, B * L // FFT_N2, HY_D), F32)
    ospec = pl.BlockSpec((FFT_N2, rows, HY_D), lambda b_, i: (0, b_ * nt + i, 0))
    return pl.pallas_call(
        _hconv_kernel, grid=(B, L // tl),
        in_specs=[pl.BlockSpec((1, tl, C), lambda b_, i: (b_, i, 0)),
                  pl.BlockSpec((1, 8, C), lambda b_, i: (b_, jnp.maximum(i * nb - 1, 0), 0)),
                  pl.BlockSpec((1, 8, C), lambda b_, i: (b_, jnp.minimum((i + 1) * nb, last), 0)),
                  pl.BlockSpec((3, C), lambda b_, i: (0, 0)),
                  pl.BlockSpec((1, C), lambda b_, i: (0, 0))],
        out_specs=[ospec, ospec, ospec], out_shape=[out, out, out],
        compiler_params=_cparams(("parallel", "parallel")), name="hy_conv3",
    )(hy_in, hy_in, hy_in, w, b)


def _split_bf16(a):
    hi = a.astype(BF16)
    return hi, (a - hi.astype(F32)).astype(BF16)


def _dot3(a, w_hi, w_lo):
    a_hi, a_lo = _split_bf16(a)
    dot = functools.partial(jnp.dot, preferred_element_type=F32)
    return dot(a_hi, w_hi) + dot(a_lo, w_hi) + dot(a_hi, w_lo)


def _filt_kernel(z_ref, w1h_ref, w1l_ref, b1_ref, f1_ref, w2h_ref, w2l_ref, b2_ref, f2_ref,
                 w3h_ref, w3l_ref, win_ref, o_ref):
    h = jnp.sin(f1_ref[...] * (_dot3(z_ref[...], w1h_ref[...], w1l_ref[...]) + b1_ref[...]))
    h = jnp.sin(f2_ref[...] * (_dot3(h, w2h_ref[...], w2l_ref[...]) + b2_ref[...]))
    k = _dot3(h, w3h_ref[...], w3l_ref[...])
    half = k.shape[0]
    wcols = 4 * HY_D
    n1 = lax.broadcasted_iota(jnp.int32, (half, 2 * HY_D), 0) % FFT_N1
    for s in range(2):
        ks = k[:, s * wcols:(s + 1) * wcols]
        kk = jnp.where(n1 >= FFT_N1 // 2, ks[:, 2 * HY_D:], ks[:, :2 * HY_D])
        win = win_ref[s * half:(s + 1) * half, :]
        o_ref[s * half:(s + 1) * half, :] = kk * jnp.concatenate([win, win], axis=-1)


def _filt_call(z_pack, fw, win_perm):
    tp = 2 * FILT_HALF
    n = win_perm.shape[0]

    def full(a):
        return pl.BlockSpec(a.shape, lambda i: (0,) * a.ndim)

    return pl.pallas_call(
        _filt_kernel, grid=(n // tp,),
        in_specs=[pl.BlockSpec((FILT_HALF, z_pack.shape[1]), lambda i: (i, 0))]
        + [full(a) for a in fw] + [pl.BlockSpec((tp, HY_D), lambda i: (i, 0))],
        out_specs=pl.BlockSpec((tp, 2 * HY_D), lambda i: (i, 0)),
        out_shape=jax.ShapeDtypeStruct((n, 2 * HY_D), F32),
        compiler_params=_cparams(("parallel",)), name="hy_filter",
    )(z_pack, *fw, win_perm)


def _bm_kernel(g_ref, x_ref, o_ref, *, to):
    for t in range(to):
        o_ref[t] = jnp.dot(g_ref[t], x_ref[t].astype(BF16), preferred_element_type=F32)


def _bm_call(g, x, name, to=8):
    O, K, N = x.shape
    M = g.shape[1]
    return pl.pallas_call(
        functools.partial(_bm_kernel, to=to), grid=(O // to,),
        in_specs=[pl.BlockSpec((to, M, K), lambda i: (i, 0, 0)),
                  pl.BlockSpec((to, K, N), lambda i: (i, 0, 0))],
        out_specs=pl.BlockSpec((to, M, N), lambda i: (i, 0, 0)),
        out_shape=jax.ShapeDtypeStruct((O, M, N), F32),
        compiler_params=_cparams(("parallel",)), name=name,
    )(g, x)


def _gather_ri(x_ref, j):
    return jnp.concatenate([x_ref[:, 0, j, :], x_ref[:, 1, j, :]], axis=0).astype(BF16)


def _gather_spec(a, to):
    return pl.BlockSpec((a.shape[0], 2, to, a.shape[3]), lambda i: (0, 0, i, 0))


def _filtb_kernel(mb_ref, x_ref, o_ref, *, to):
    for j in range(to):
        o_ref[j] = jnp.dot(mb_ref[...], _gather_ri(x_ref, j), preferred_element_type=F32)


def _filtb_call(mb, ka, to=8):
    O, _, P, N = ka.shape
    return pl.pallas_call(
        functools.partial(_filtb_kernel, to=to), grid=(P // to,),
        in_specs=[pl.BlockSpec(mb.shape, lambda i: (0, 0)), _gather_spec(ka, to)],
        out_specs=pl.BlockSpec((to, 2 * O, N), lambda i: (i, 0, 0)),
        out_shape=jax.ShapeDtypeStruct((P, 2 * O, N), F32),
        compiler_params=_cparams(("parallel",)), name="hy_fft_filt_b",
    )(mb, ka)


def _convb_kernel(mb_ref, gc_ref, x_ref, kf_ref, o_ref, *, to):
    h = FFT_N2
    for j in range(to):
        xs = jnp.dot(mb_ref[...], _gather_ri(x_ref, j), preferred_element_type=F32)
        xr, xi = xs[:h], xs[h:]
        kr, ki = kf_ref[j, :h, :], kf_ref[j, h:, :]
        ys = jnp.concatenate([xr * kr - xi * ki, xr * ki + xi * kr], axis=0).astype(BF16)
        o_ref[j] = jnp.dot(gc_ref[j], ys, preferred_element_type=F32)


def _convb_call(mb, gc, a, kf, order, to=8):
    O, _, P, N = a.shape
    return pl.pallas_call(
        functools.partial(_convb_kernel, to=to), grid=(P // to,),
        in_specs=[pl.BlockSpec(mb.shape, lambda i: (0, 0)),
                  pl.BlockSpec((to, 2 * O, 2 * O), lambda i: (i, 0, 0)),
                  _gather_spec(a, to),
                  pl.BlockSpec((to, 2 * O, N), lambda i: (i, 0, order))],
        out_specs=pl.BlockSpec((to, 2 * O, N), lambda i: (i, 0, 0)),
        out_shape=jax.ShapeDtypeStruct((P, 2 * O, N), F32),
        compiler_params=_cparams(("parallel",)), name="hy_spec_mul",
    )(mb, gc, a, kf)


def _convd_kernel(md_ref, c_ref, g_ref, u_ref, s_ref, o_ref, *, to, natural):
    for j in range(to):
        y = jnp.dot(md_ref[...], _gather_ri(c_ref, j), preferred_element_type=F32)
        z = g_ref[j] * (y + u_ref[j] * s_ref[...])
        if natural:
            o_ref[:, j, :] = z
        else:
            o_ref[j] = z


def _convd_call(md, c, gate, u, skip, natural, to=8):
    O, _, P, N = c.shape
    R = md.shape[0]
    tspec = pl.BlockSpec((to, R, N), lambda i: (i, 0, 0))
    if natural:
        ospec = pl.BlockSpec((R, to, N), lambda i: (0, i, 0))
        oshape = jax.ShapeDtypeStruct((R, P, N), F32)
    else:
        ospec, oshape = tspec, jax.ShapeDtypeStruct((P, R, N), F32)
    return pl.pallas_call(
        functools.partial(_convd_kernel, to=to, natural=natural), grid=(P // to,),
        in_specs=[pl.BlockSpec(md.shape, lambda i: (0, 0)), _gather_spec(c, to), tspec, tspec,
                  pl.BlockSpec((1, N), lambda i: (0, 0))],
        out_specs=ospec, out_shape=oshape,
        compiler_params=_cparams(("parallel",)), name="hy_fft_d",
    )(md, c, gate, u, skip.reshape(1, N))


def _dft_tables():
    n = FFT_N1 * FFT_N2
    k = np.arange(FFT_N1)
    f = np.exp(-2j * np.pi * np.outer(k, k) / FFT_N1)
    t = np.exp(-2j * np.pi * np.outer(k, k) / n)
    return f, t, n


def _dft_matrices():
    f, t, n = _dft_tables()
    fr, fi = jnp.asarray(f.real, F32), jnp.asarray(f.imag, F32)
    tr, ti = jnp.asarray(t.real, F32), jnp.asarray(t.imag, F32)
    half = FFT_N1 // 2
    er = fr[None] * tr[:, :, None] - fi[None] * ti[:, :, None]
    ei = fr[None] * ti[:, :, None] + fi[None] * tr[:, :, None]
    ga = jnp.concatenate([jnp.concatenate([er[:, :, :half], -ei[:, :, :half]], axis=2),
                          jnp.concatenate([ei[:, :, :half], er[:, :, :half]], axis=2)], axis=1)
    gaf = jnp.concatenate([er, ei], axis=1)
    mb = jnp.concatenate([jnp.concatenate([fr, -fi], axis=1),
                          jnp.concatenate([fi, fr], axis=1)], axis=0)
    tct = jnp.transpose(tr)[:, :, None]
    tst = -jnp.transpose(ti)[:, :, None]
    gr = tct * fr[None] - tst * (-fi[None])
    gi = tct * (-fi[None]) + tst * fr[None]
    gc = jnp.concatenate([jnp.concatenate([gr, -gi], axis=2),
                          jnp.concatenate([gi, gr], axis=2)], axis=1)
    hr, hi = fr[:half] / n, -fi[:half] / n
    md = jnp.concatenate([jnp.concatenate([hr, -hi], axis=1),
                          jnp.concatenate([hi, hr], axis=1)], axis=0)
    return (ga.astype(BF16), gaf.astype(BF16), mb.astype(BF16), gc.astype(BF16), md.astype(BF16))


def _hyena_positions(L):
    p = FFT_N2 * np.arange(FFT_N1)[None, :] + np.arange(FFT_N2)[:, None]
    pos = np.where(p < L, p, 2 * L - 1 - p).reshape(2 * L, 1).astype(np.float64)
    t = pos / (L - 1)
    w = 2.0 * math.pi * pos / L
    f = np.linspace(1e-4, HY_BANDS - 1, HY_BANDS)[None, :]
    z = np.concatenate([t, np.cos(f * w), -np.sin(f * w),
                        np.zeros((2 * L, FILT_FEAT - HY_EMB))], axis=-1)
    z_pack = (z.reshape(-1, 2, FILT_HALF, FILT_FEAT).transpose(0, 2, 1, 3)
              .reshape(-1, 2 * FILT_FEAT))
    max_decay = math.log(HY_DECAY_TARGET) / HY_FAST_DECAY_PCT
    min_decay = math.log(HY_DECAY_TARGET) / HY_SLOW_DECAY_PCT
    deltas = jnp.linspace(min_decay, max_decay, HY_D, dtype=F32)
    window = jnp.exp(-jnp.asarray(t, F32) * jnp.abs(deltas)[None, :])
    return jnp.asarray(z_pack, F32), window


def _filter_weights(w1, b1, f1, w2, b2, f2, w3):
    def bd(w):
        z = jnp.zeros_like(w)
        return jnp.concatenate([jnp.concatenate([w, z], axis=1),
                                jnp.concatenate([z, w], axis=1)], axis=0)

    def twice(v):
        return jnp.concatenate([v, v])[None]

    w1 = jnp.pad(w1, ((0, FILT_FEAT - w1.shape[0]), (0, 0)))
    w3 = w3.reshape(-1, 2, 2, HY_D).transpose(0, 2, 1, 3).reshape(-1, 4 * HY_D)
    return (*_split_bf16(bd(w1)), twice(b1), twice(f1), *_split_bf16(bd(w2)), twice(b2), twice(f2),
            *_split_bf16(bd(w3)))


def _hyena_layer(hy_in, conv_w, conv_b, fw, skip, z_perm, win_perm, mats):
    B, L, _ = hy_in.shape
    ga, gaf, mb, gc, md = mats
    v, x1, x2 = _hconv_call(hy_in, conv_w, conv_b[None])
    kperm = _filt_call(z_perm, fw, win_perm)
    ka = _bm_call(gaf, kperm.reshape(FFT_N2, FFT_N1, 2 * HY_D), "hy_fft_filt_a")
    kf = _filtb_call(mb, ka.reshape(FFT_N2, 2, FFT_N1, 2 * HY_D))

    z = v
    for i, gate in enumerate((x1, x2)):
        a = _bm_call(ga, z, "hy_fft_a")
        c = _convb_call(mb, gc, a.reshape(FFT_N2, 2, FFT_N1, HY_D), kf, i)
        z = _convd_call(md, c.reshape(FFT_N1, 2, FFT_N2, HY_D), gate, z, skip[i],
                        natural=(i == 1))
    return z.reshape(B, L, HY_D)


def _kout_kernel(x_ref, yh_ref, yg_ref, ym_ref, gh_ref, gg_ref, gm_ref, w_ref, gpost_ref, gffn_ref,
                 xo_ref, h_ref):
    a = _rms(yh_ref[0], gh_ref[...]).astype(BF16)
    b = _rms(yg_ref[0].astype(F32), gg_ref[...]).astype(BF16)
    c = _rms(ym_ref[0].astype(F32), gm_ref[...]).astype(BF16)
    o1 = HY_D
    o2 = o1 + GQA_HEADS * GQA_HEAD_DIM
    y = (jnp.dot(a, w_ref[:o1, :], preferred_element_type=F32)
         + jnp.dot(b, w_ref[o1:o2, :], preferred_element_type=F32)
         + jnp.dot(c, w_ref[o2:, :], preferred_element_type=F32))
    xo = x_ref[0] + _rms(y, gpost_ref[...])
    xo_ref[0] = xo
    h_ref[0] = _rms(xo, gffn_ref[...]).astype(BF16)


def _kout_call(x, yh, yg, ym, gh, gg, gm, w_p, gpost, gffn):
    B, L, D = x.shape
    tm = TM_OUT

    def rows(a):
        return pl.BlockSpec((1, tm, a.shape[2]), lambda b, i: (b, i, 0))

    def full(a):
        return pl.BlockSpec(a.shape, lambda b, i: (0,) * a.ndim)

    return pl.pallas_call(
        _kout_kernel, grid=(B, L // tm),
        in_specs=[rows(x), rows(yh), rows(yg), rows(ym), full(gh), full(gg), full(gm), full(w_p),
                  full(gpost), full(gffn)],
        out_specs=[rows(x), rows(x)],
        out_shape=[jax.ShapeDtypeStruct((B, L, D), F32), jax.ShapeDtypeStruct((B, L, D), BF16)],
        compiler_params=_cparams(("parallel", "parallel")), name="out_proj",
    )(x, yh, yg, ym, gh, gg, gm, w_p, gpost, gffn)


HALO = 16


def _ffn_kernel(h_ref, hp_ref, hn_ref, x_ref, wup_ref, cw_ref, cb_ref, wd_ref, gpost_ref,
                o_ref, act_ref):
    i = pl.program_id(1)
    tm = h_ref.shape[1]
    prev = jnp.where(i > 0, hp_ref[0], jnp.zeros_like(hp_ref[0]))
    nxt = jnp.where(i < pl.num_programs(1) - 1, hn_ref[0], jnp.zeros_like(hn_ref[0]))
    he = jnp.concatenate([prev, h_ref[0], nxt], axis=0)
    ext = tm + 2 * HALO
    tf = TF_FFN

    def conv(c0):
        up = jnp.dot(he, wup_ref[:, c0:c0 + tf], preferred_element_type=F32)
        um = pltpu.roll(up, 1, 0)[HALO:HALO + tm]
        upl = pltpu.roll(up, ext - 1, 0)[HALO:HALO + tm]
        return (um * cw_ref[0:1, c0:c0 + tf] + up[HALO:HALO + tm] * cw_ref[1:2, c0:c0 + tf]
                + upl * cw_ref[2:3, c0:c0 + tf] + cb_ref[:, c0:c0 + tf])

    for j in range(D_FF // tf):
        g = conv(j * tf)
        u = conv(D_FF + j * tf)
        gelu = 0.5 * g * (1.0 + jnp.tanh(math.sqrt(2.0 / math.pi) * (g + 0.044715 * (g * g * g))))
        act_ref[:, j * tf:(j + 1) * tf] = (gelu * u).astype(BF16)
    f = jnp.dot(act_ref[...], wd_ref[...], preferred_element_type=F32)
    o_ref[0] = x_ref[0] + _rms(f, gpost_ref[...])


def _ffn_call(h, x, w_up, cw, cb, w_down, gpost):
    B, L, D = x.shape
    tm = TM_FFN
    nb = tm // HALO
    last = L // HALO - 1

    def resident(a):
        return pl.BlockSpec(a.shape, lambda b, i: (0,) * a.ndim, pipeline_mode=pl.Buffered(1))

    cb = cb[None]
    return pl.pallas_call(
        _ffn_kernel, grid=(B, L // tm),
        in_specs=[pl.BlockSpec((1, tm, D), lambda b, i: (b, i, 0)),
                  pl.BlockSpec((1, HALO, D), lambda b, i: (b, jnp.maximum(i * nb - 1, 0), 0)),
                  pl.BlockSpec((1, HALO, D), lambda b, i: (b, jnp.minimum((i + 1) * nb, last), 0)),
                  pl.BlockSpec((1, tm, D), lambda b, i: (b, i, 0)),
                  resident(w_up), resident(cw), resident(cb), resident(w_down), resident(gpost)],
        out_specs=pl.BlockSpec((1, tm, D), lambda b, i: (b, i, 0)),
        out_shape=jax.ShapeDtypeStruct((B, L, D), F32),
        scratch_shapes=[pltpu.VMEM((tm, D_FF), BF16)],
        compiler_params=_cparams(("parallel", "parallel")), name="conv_ffn",
    )(h, h, h, x, w_up, cw, cb, w_down, gpost)


def _axial_tables(L, rot_dim):
    rows = L // GRID_W
    row_idx = jnp.broadcast_to(jnp.arange(rows)[:, None], (rows, GRID_W)).reshape(L)
    col_idx = jnp.broadcast_to(jnp.arange(GRID_W)[None, :], (rows, GRID_W)).reshape(L)
    n_axis = rot_dim // 4
    inv = ROPE_THETA ** (-jnp.arange(n_axis, dtype=F32) / n_axis)
    ang = jnp.concatenate([row_idx[:, None].astype(F32) * inv,
                           col_idx[:, None].astype(F32) * inv], axis=-1)
    return jnp.cos(ang), jnp.sin(ang)


def _rope_tables(L):
    def lanes(parts):
        used = sum(p.shape[1] for p in parts)
        return jnp.concatenate(parts + [jnp.zeros((L, LANES - used), F32)], axis=1)

    cg, sg = _axial_tables(L, GQA_HEAD_DIM)
    tg = (lanes([cg, cg]), lanes([sg, sg]))
    cm, sm = _axial_tables(L, MLA_ROPE_DIM)
    nope0 = jnp.zeros((L, MLA_NOPE_DIM), F32)
    nope1 = jnp.ones((L, MLA_NOPE_DIM), F32)
    tmk = (lanes([nope0, cm, cm]), lanes([nope0, sm, sm]))
    sc = (MLA_NOPE_DIM + MLA_ROPE_DIM) ** -0.5 * LOG2E
    tmq = (lanes([nope1, cm, cm]) * sc, tmk[1] * sc)
    return tg, tmq, tmk


def _partner(w, half, sign=-1.0):
    return jnp.concatenate([sign * w[..., half:], w[..., :half]], axis=-1)


def _pad_heads(w, n_heads, width):
    k = w.shape[0]
    return jnp.pad(w.reshape(k, n_heads, width), ((0, 0), (0, 0), (0, LANES - width))).reshape(
        k, n_heads * LANES)


def _pad_vec(g, n_heads, width):
    return jnp.pad(g.reshape(n_heads, width), ((0, 0), (0, LANES - width))).reshape(1, n_heads * LANES)


def kernel(x, mix_pre_norm, w_in, hy_conv_w, hy_conv_b, hy_filt_w1, hy_filt_b1, hy_filt_freq1,
           hy_filt_w2, hy_filt_b2, hy_filt_freq2, hy_filt_w3, hy_skip, gqa_q_norm, gqa_k_norm,
           mla_q_a_norm, mla_w_uq, mla_kv_a_norm, mla_w_ukv, hy_out_norm, gqa_out_norm,
           mla_out_norm, w_out, mix_post_norm, ffn_pre_norm, w_up, ffn_conv_w, ffn_conv_b,
           w_down, ffn_post_norm):
    B, L, D = x.shape
    assert B == 2 and 2 * L == FFT_N1 * FFT_N2 and D == D_MODEL
    depth = w_in.shape[0]
    tg, tmq, tmk = _rope_tables(L)
    z_perm, win_perm = _hyena_positions(L)
    mats = _dft_matrices()

    for l in range(depth):
        wl = w_in[l]
        hd, hh, rh = GQA_HEAD_DIM, GQA_HEAD_DIM // 2, MLA_ROPE_DIM // 2
        o1 = 768
        o2 = o1 + GQA_HEADS * hd
        o3 = o2 + GQA_KV_HEADS * hd
        o4 = o3 + GQA_KV_HEADS * hd
        o5 = o4 + MLA_Q_RANK
        o6 = o5 + MLA_KV_RANK
        wq = wl[:, o1:o2].reshape(D, GQA_HEADS, hd)
        wk = wl[:, o2:o3].reshape(D, GQA_KV_HEADS, hd)
        wkr = wl[:, o6:]
        pe_pad = ((0, 0), (MLA_NOPE_DIM, LANES - MLA_NOPE_DIM - MLA_ROPE_DIM))
        win_p = jnp.concatenate(
            [wl[:, :o1],
             _pad_heads(wq.reshape(D, -1), GQA_HEADS, hd),
             _pad_heads(_partner(wq, hh).reshape(D, -1), GQA_HEADS, hd),
             _pad_heads(wk.reshape(D, -1), GQA_KV_HEADS, hd),
             _pad_heads(_partner(wk, hh).reshape(D, -1), GQA_KV_HEADS, hd),
             wl[:, o3:o6], jnp.pad(wkr, pe_pad), jnp.pad(_partner(wkr, rh), pe_pad)],
            axis=1).astype(BF16)
        gq_gain = gqa_q_norm[l] * (hd ** -0.5 * LOG2E)
        gq = (_pad_vec(gq_gain, 1, hd), _pad_vec(_partner(gq_gain, hh, 1.0), 1, hd))
        gk = (_pad_vec(gqa_k_norm[l], 1, hd), _pad_vec(_partner(gqa_k_norm[l], hh, 1.0), 1, hd))
        wuq = mla_w_uq[l].reshape(MLA_Q_RANK, MLA_HEADS, MLA_NOPE_DIM + MLA_ROPE_DIM)
        wuq_pe = jnp.pad(_partner(wuq[:, :, MLA_NOPE_DIM:], rh),
                         ((0, 0), (0, 0), (MLA_NOPE_DIM, 0)))
        wuq_p = jnp.concatenate(
            [_pad_heads(mla_w_uq[l], MLA_HEADS, MLA_NOPE_DIM + MLA_ROPE_DIM),
             _pad_heads(wuq_pe.reshape(MLA_Q_RANK, -1), MLA_HEADS, MLA_NOPE_DIM + MLA_ROPE_DIM)],
            axis=1).astype(BF16)
        wukv = mla_w_ukv[l].reshape(MLA_KV_RANK, MLA_HEADS, MLA_NOPE_DIM + MLA_V_DIM)
        wukvk_p = _pad_heads(wukv[:, :, :MLA_NOPE_DIM].reshape(MLA_KV_RANK, -1), MLA_HEADS,
                             MLA_NOPE_DIM).astype(BF16)
        wukvv = wukv[:, :, MLA_NOPE_DIM:].reshape(MLA_KV_RANK, MLA_HEADS * MLA_V_DIM).astype(BF16)

        hy_in, qg, kg, vg, qm, km, vm = _kin_call(
            x, mix_pre_norm[l][None], win_p, gq, gk, tg, mla_q_a_norm[l][None], wuq_p, tmq,
            mla_kv_a_norm[l][None], wukvk_p, wukvv, tmk)

        fw = _filter_weights(hy_filt_w1[l], hy_filt_b1[l], hy_filt_freq1[l], hy_filt_w2[l],
                             hy_filt_b2[l], hy_filt_freq2[l], hy_filt_w3[l])
        y_hy = _hyena_layer(hy_in, hy_conv_w[l], hy_conv_b[l], fw, hy_skip[l], z_perm, win_perm, mats)

        y_gqa = _attn_call(qg, kg, vg.reshape(B, GQA_KV_HEADS, GQA_HEAD_DIM, L), n_kv=1,
                           n_rep=GQA_HEADS // GQA_KV_HEADS, tq=TQ_GQA, name="attn_gqa")
        y_mla = _attn_call(qm, km, vm.reshape(B, MLA_HEADS, MLA_V_DIM, L), n_kv=2, n_rep=1,
                           tq=TQ_MLA, name="attn_mla")

        x, h2 = _kout_call(x, y_hy, y_gqa, y_mla, hy_out_norm[l][None], gqa_out_norm[l][None],
                           mla_out_norm[l][None], w_out[l].astype(BF16),
                           mix_post_norm[l][None], ffn_pre_norm[l][None])
        x = _ffn_call(h2, x, w_up[l].astype(BF16), ffn_conv_w[l], ffn_conv_b[l],
                      w_down[l].astype(BF16), ffn_post_norm[l][None])
    return x
```

```python
import functools
import math

import numpy as np
import jax
import jax.numpy as jnp
from jax import lax
from jax.experimental import pallas as pl
from jax.experimental.pallas import tpu as pltpu

F32 = jnp.float32
BF16 = jnp.bfloat16

NORM_EPS = 1e-6
ROPE_THETA = 10000.0
GRID_W = 64
LOG2E = math.log2(math.e)

D_MODEL = 1024
HY_D = 256
HY_EMB = 33
HY_BANDS = 16
HY_DECAY_TARGET = 1e-2
HY_FAST_DECAY_PCT = 0.3
HY_SLOW_DECAY_PCT = 1.5
GQA_HEADS = 8
GQA_KV_HEADS = 2
GQA_HEAD_DIM = 64
MLA_HEADS = 4
MLA_Q_RANK = 256
MLA_KV_RANK = 128
MLA_NOPE_DIM = 64
MLA_ROPE_DIM = 32
MLA_V_DIM = 64
D_FF = 2816

LANES = 128
FFT_N1 = 128
FFT_N2 = 128

TM_IN = 512
TQ_GQA = 128
TQ_MLA = 256
TM_OUT = 512
TM_FFN = 512
TF_FFN = 256
TL_HCONV = 1024
FILT_HALF = 512
FILT_FEAT = 64
VMEM_LIMIT = 56 * 1024 * 1024


def _cparams(sem):
    return pltpu.CompilerParams(dimension_semantics=sem, vmem_limit_bytes=VMEM_LIMIT)


def _rms(x, g):
    return x * lax.rsqrt(jnp.mean(x * x, axis=-1, keepdims=True) + NORM_EPS) * g


def _kin_kernel(x_ref, gpre_ref, win_ref, gqn_ref, gqs_ref, gkn_ref, gks_ref, cg_ref, sg_ref,
                mqn_ref, wuq_ref, cmq_ref, smq_ref,
                mkvn_ref, wukvk_ref, wukvv_ref, cmk_ref, smk_ref,
                hy_ref, qg_ref, kg_ref, vg_ref, qm_ref, km_ref, vm_ref):
    x = x_ref[0]
    h = _rms(x, gpre_ref[...]).astype(BF16)
    cur = [0]

    def proj(n):
        lo = cur[0]
        cur[0] = lo + n
        return jnp.dot(h, win_ref[:, lo:lo + n], preferred_element_type=F32)

    hy_ref[0] = proj(768)

    cg, sg = cg_ref[...], sg_ref[...]

    def heads(n, gain, gain_sw, store):
        xa, xb = proj(n * LANES), proj(n * LANES)
        ca, sa = cg * gain, sg * gain_sw
        for j in range(n):
            xc, xs = xa[:, j * LANES:(j + 1) * LANES], xb[:, j * LANES:(j + 1) * LANES]
            ms = jnp.sum(xc * xc, axis=-1, keepdims=True) * (1.0 / GQA_HEAD_DIM)
            store(j, ((xc * ca + xs * sa) * lax.rsqrt(ms + NORM_EPS)).astype(BF16))

    def store_q(j, v):
        qg_ref[0, :, j * LANES:(j + 1) * LANES] = v

    def store_k(j, v):
        kg_ref[0, j] = v

    heads(GQA_HEADS, gqn_ref[...], gqs_ref[...], store_q)
    heads(GQA_KV_HEADS, gkn_ref[...], gks_ref[...], store_k)
    vg_ref[0] = proj(LANES).T.astype(BF16)

    cq = _rms(proj(MLA_Q_RANK), mqn_ref[...]).astype(BF16)
    qm = jnp.dot(cq, wuq_ref[...], preferred_element_type=F32)
    cmq, smq = cmq_ref[...], smq_ref[...]
    nq = MLA_HEADS * LANES
    for j in range(MLA_HEADS):
        qm_ref[0, :, j * LANES:(j + 1) * LANES] = (
            qm[:, j * LANES:(j + 1) * LANES] * cmq
            + qm[:, nq + j * LANES:nq + (j + 1) * LANES] * smq).astype(BF16)

    ckv = _rms(proj(MLA_KV_RANK), mkvn_ref[...]).astype(BF16)
    kpe = proj(LANES) * cmk_ref[...] + proj(LANES) * smk_ref[...]
    kn = jnp.dot(ckv, wukvk_ref[...], preferred_element_type=F32)
    for j in range(MLA_HEADS):
        km_ref[0, j] = (kn[:, j * LANES:(j + 1) * LANES] + kpe).astype(BF16)
    vm = jnp.dot(ckv, wukvv_ref[...], preferred_element_type=F32)
    for j in range(vm.shape[1] // LANES):
        vm_ref[0, j * LANES:(j + 1) * LANES, :] = vm[:, j * LANES:(j + 1) * LANES].T.astype(BF16)


def _kin_call(x, gpre, win_p, gq, gk, tg, mqn, wuq_p, tmq, mkvn, wukvk_p, wukvv, tmk):
    B, L, D = x.shape
    tm = TM_IN
    nt = L // tm

    def full(a):
        return pl.BlockSpec(a.shape, lambda b, i: (0,) * a.ndim, pipeline_mode=pl.Buffered(1))

    def rows(w):
        return pl.BlockSpec((tm, w), lambda b, i: (i, 0))

    in_specs = [pl.BlockSpec((1, tm, D), lambda b, i: (b, i, 0)), full(gpre), full(win_p),
                full(gq[0]), full(gq[1]), full(gk[0]), full(gk[1]), rows(LANES), rows(LANES),
                full(mqn), full(wuq_p), rows(LANES), rows(LANES),
                full(mkvn), full(wukvk_p), full(wukvv), rows(LANES), rows(LANES)]
    gv_rows = GQA_KV_HEADS * GQA_HEAD_DIM
    mv_rows = MLA_HEADS * MLA_V_DIM
    out_shape = [
        jax.ShapeDtypeStruct((B, L, 768), F32),
        jax.ShapeDtypeStruct((B, L, GQA_HEADS * LANES), BF16),
        jax.ShapeDtypeStruct((B, GQA_KV_HEADS, L, LANES), BF16),
        jax.ShapeDtypeStruct((B, gv_rows, L), BF16),
        jax.ShapeDtypeStruct((B, L, MLA_HEADS * LANES), BF16),
        jax.ShapeDtypeStruct((B, MLA_HEADS, L, LANES), BF16),
        jax.ShapeDtypeStruct((B, mv_rows, L), BF16),
    ]
    out_specs = [
        pl.BlockSpec((1, tm, 768), lambda b, i: (b, i, 0)),
        pl.BlockSpec((1, tm, GQA_HEADS * LANES), lambda b, i: (b, i, 0)),
        pl.BlockSpec((1, GQA_KV_HEADS, tm, LANES), lambda b, i: (b, 0, i, 0)),
        pl.BlockSpec((1, gv_rows, tm), lambda b, i: (b, 0, i)),
        pl.BlockSpec((1, tm, MLA_HEADS * LANES), lambda b, i: (b, i, 0)),
        pl.BlockSpec((1, MLA_HEADS, tm, LANES), lambda b, i: (b, 0, i, 0)),
        pl.BlockSpec((1, mv_rows, tm), lambda b, i: (b, 0, i)),
    ]
    return pl.pallas_call(
        _kin_kernel, grid=(B, nt), in_specs=in_specs, out_specs=out_specs, out_shape=out_shape,
        compiler_params=_cparams(("parallel", "parallel")), name="in_proj",
    )(x, gpre, win_p, *gq, *gk, *tg, mqn, wuq_p, *tmq, mkvn, wukvk_p, wukvv, *tmk)


def _attn_kernel(q_ref, k_ref, vt_ref, o_ref, sa_ref, sb_ref, ma_ref, mb_ref, *,
                 n_kv, n_rep, tq, n_chunks, tk):
    i = pl.program_id(2)
    cols = n_rep * tq
    width = n_kv * cols
    grp = tk // 8

    @pl.when(i == 0)
    def _():
        sb_ref[...] = jnp.zeros(sb_ref.shape, F32)
        mb_ref[...] = jnp.zeros(mb_ref.shape, F32)

    def step(sw_ref, mw_ref, sr_ref, mr_ref):
        qs = [jnp.concatenate([q_ref[0, (a * n_rep + j) * LANES:(a * n_rep + j + 1) * LANES, :]
                               for j in range(n_rep)], axis=1) for a in range(n_kv)]
        mx = jnp.max(mr_ref[...], axis=0, keepdims=True)
        m = jnp.full((8, width), -jnp.inf, F32)
        l = jnp.zeros((8, width), F32)
        accs = [jnp.zeros((vt_ref.shape[2], cols), F32) for _ in range(n_kv)]
        for c in range(n_chunks):
            st = jnp.concatenate(
                [jnp.dot(k_ref[0, a, c * tk:(c + 1) * tk, :], qs[a], preferred_element_type=F32)
                 for a in range(n_kv)], axis=1)
            sw_ref[c] = st
            m = jnp.maximum(m, jnp.max(st.reshape(grp, 8, width), axis=0))
            p = jnp.exp2(sr_ref[c] - mx)
            l = l + jnp.sum(p.reshape(grp, 8, width), axis=0)
            pb = p.astype(BF16)
            for a in range(n_kv):
                accs[a] = accs[a] + jnp.dot(vt_ref[0, a, :, c * tk:(c + 1) * tk],
                                            pb[:, a * cols:(a + 1) * cols],
                                            preferred_element_type=F32)
        mw_ref[...] = m
        ls = jnp.sum(l, axis=0, keepdims=True)
        heads = []
        for a in range(n_kv):
            oa = accs[a] / ls[:, a * cols:(a + 1) * cols]
            heads += [oa[:, j * tq:(j + 1) * tq] for j in range(n_rep)]
        o_ref[0] = jnp.concatenate(heads, axis=0).T.astype(BF16)

    @pl.when(i % 2 == 0)
    def _():
        step(sa_ref, ma_ref, sb_ref, mb_ref)

    @pl.when(i % 2 == 1)
    def _():
        step(sb_ref, mb_ref, sa_ref, ma_ref)


def _attn_call(qt, k, vt, *, n_kv, n_rep, tq, name):
    B, hq, L = qt.shape
    H = hq // LANES
    hkv, dv = vt.shape[1], vt.shape[2]
    G = hkv // n_kv
    hs = n_kv * n_rep
    tk = TM_IN
    n_chunks = L // tk
    nq = L // tq
    width = hs * tq
    kern = functools.partial(_attn_kernel, n_kv=n_kv, n_rep=n_rep, tq=tq, n_chunks=n_chunks, tk=tk)
    return pl.pallas_call(
        kern, grid=(B, G, nq + 1),
        in_specs=[pl.BlockSpec((1, hs * LANES, tq), lambda b, g, i: (b, g, jnp.minimum(i, nq - 1))),
                  pl.BlockSpec((1, n_kv, L, LANES), lambda b, g, i: (b, g, 0, 0),
                               pipeline_mode=pl.Buffered(1)),
                  pl.BlockSpec((1, n_kv, dv, L), lambda b, g, i: (b, g, 0, 0),
                               pipeline_mode=pl.Buffered(1))],
        out_specs=pl.BlockSpec((1, tq, hs * dv), lambda b, g, i: (b, jnp.maximum(i - 1, 0), g)),
        out_shape=jax.ShapeDtypeStruct((B, L, H * dv), BF16),
        scratch_shapes=[pltpu.VMEM((n_chunks, tk, width), F32), pltpu.VMEM((n_chunks, tk, width), F32),
                        pltpu.VMEM((8, width), F32), pltpu.VMEM((8, width), F32)],
        compiler_params=_cparams(("parallel", "parallel", "arbitrary")), name=name,
    )(qt, k, vt)


def _hconv_kernel(x_ref, xp_ref, xn_ref, w_ref, b_ref, v_ref, x1_ref, x2_ref):
    i = pl.program_id(1)
    x = x_ref[0]
    tl = x.shape[0]
    prev = jnp.where(i > 0, xp_ref[0][7:8, :], 0.0)
    nxt = jnp.where(i < pl.num_programs(1) - 1, xn_ref[0][0:1, :], 0.0)
    r = lax.broadcasted_iota(jnp.int32, x.shape, 0)
    xm = jnp.where(r == 0, prev, pltpu.roll(x, 1, 0))
    xp = jnp.where(r == tl - 1, nxt, pltpu.roll(x, tl - 1, 0))
    uc = xm * w_ref[0:1, :] + x * w_ref[1:2, :] + xp * w_ref[2:3, :] + b_ref[...]
    for r in range(tl // FFT_N2):
        blk = uc[r * FFT_N2:(r + 1) * FFT_N2]
        v_ref[:, r, :] = blk[:, :HY_D]
        x1_ref[:, r, :] = blk[:, HY_D:2 * HY_D]
        x2_ref[:, r, :] = blk[:, 2 * HY_D:]


def _hconv_call(hy_in, w, b):
    B, L, C = hy_in.shape
    tl = TL_HCONV
    nb = tl // 8
    last = L // 8 - 1
    nt = L // tl
    rows = tl // FFT_N2
    out = jax.ShapeDtypeStruct((FFT_N2, B * L // FFT_N2, HY_D), F32)
    ospec = pl.BlockSpec((FFT_N2, rows, HY_D), lambda b_, i: (0, b_ * nt + i, 0))
    return pl.pallas_call(
        _hconv_kernel, grid=(B, L // tl),
        in_specs=[pl.BlockSpec((1, tl, C), lambda b_, i: (b_, i, 0)),
                  pl.BlockSpec((1, 8, C), lambda b_, i: (b_, jnp.maximum(i * nb - 1, 0), 0)),
                  pl.BlockSpec((1, 8, C), lambda b_, i: (b_, jnp.minimum((i + 1) * nb, last), 0)),
                  pl.BlockSpec((3, C), lambda b_, i: (0, 0)),
                  pl.BlockSpec((1, C), lambda b_, i: (0, 0))],
        out_specs=[ospec, ospec, ospec], out_shape=[out, out, out],
        compiler_params=_cparams(("parallel", "parallel")), name="hy_conv3",
    )(hy_in, hy_in, hy_in, w, b)


def _split_bf16(a):
    hi = a.astype(BF16)
    return hi, (a - hi.astype(F32)).astype(BF16)


def _dot3(a, w_hi, w_lo):
    a_hi, a_lo = _split_bf16(a)
    dot = functools.partial(jnp.dot, preferred_element_type=F32)
    return dot(a_hi, w_hi) + dot(a_lo, w_hi) + dot(a_hi, w_lo)


def _filt_kernel(z_ref, w1h_ref, w1l_ref, b1_ref, f1_ref, w2h_ref, w2l_ref, b2_ref, f2_ref,
                 w3h_ref, w3l_ref, win_ref, o_ref):
    h = jnp.sin(f1_ref[...] * (_dot3(z_ref[...], w1h_ref[...], w1l_ref[...]) + b1_ref[...]))
    h = jnp.sin(f2_ref[...] * (_dot3(h, w2h_ref[...], w2l_ref[...]) + b2_ref[...]))
    k = _dot3(h, w3h_ref[...], w3l_ref[...])
    half = k.shape[0]
    wcols = 4 * HY_D
    n1 = lax.broadcasted_iota(jnp.int32, (half, 2 * HY_D), 0) % FFT_N1
    for s in range(2):
        ks = k[:, s * wcols:(s + 1) * wcols]
        kk = jnp.where(n1 >= FFT_N1 // 2, ks[:, 2 * HY_D:], ks[:, :2 * HY_D])
        win = win_ref[s * half:(s + 1) * half, :]
        o_ref[s * half:(s + 1) * half, :] = kk * jnp.concatenate([win, win], axis=-1)


def _filt_call(z_pack, fw, win_perm):
    tp = 2 * FILT_HALF
    n = win_perm.shape[0]

    def full(a):
        return pl.BlockSpec(a.shape, lambda i: (0,) * a.ndim)

    return pl.pallas_call(
        _filt_kernel, grid=(n // tp,),
        in_specs=[pl.BlockSpec((FILT_HALF, z_pack.shape[1]), lambda i: (i, 0))]
        + [full(a) for a in fw] + [pl.BlockSpec((tp, HY_D), lambda i: (i, 0))],
        out_specs=pl.BlockSpec((tp, 2 * HY_D), lambda i: (i, 0)),
        out_shape=jax.ShapeDtypeStruct((n, 2 * HY_D), F32),
        compiler_params=_cparams(("parallel",)), name="hy_filter",
    )(z_pack, *fw, win_perm)


def _bm_kernel(g_ref, x_ref, o_ref, *, to):
    for t in range(to):
        o_ref[t] = jnp.dot(g_ref[t], x_ref[t].astype(BF16), preferred_element_type=F32)


def _bm_call(g, x, name, to=8):
    O, K, N = x.shape
    M = g.shape[1]
    return pl.pallas_call(
        functools.partial(_bm_kernel, to=to), grid=(O // to,),
        in_specs=[pl.BlockSpec((to, M, K), lambda i: (i, 0, 0)),
                  pl.BlockSpec((to, K, N), lambda i: (i, 0, 0))],
        out_specs=pl.BlockSpec((to, M, N), lambda i: (i, 0, 0)),
        out_shape=jax.ShapeDtypeStruct((O, M, N), F32),
        compiler_params=_cparams(("parallel",)), name=name,
    )(g, x)


def _gather_ri(x_ref, j):
    return jnp.concatenate([x_ref[:, 0, j, :], x_ref[:, 1, j, :]], axis=0).astype(BF16)


def _gather_spec(a, to):
    return pl.BlockSpec((a.shape[0], 2, to, a.shape[3]), lambda i: (0, 0, i, 0))


def _filtb_kernel(mb_ref, x_ref, o_ref, *, to):
    for j in range(to):
        o_ref[j] = jnp.dot(mb_ref[...], _gather_ri(x_ref, j), preferred_element_type=F32)


def _filtb_call(mb, ka, to=8):
    O, _, P, N = ka.shape
    return pl.pallas_call(
        functools.partial(_filtb_kernel, to=to), grid=(P // to,),
        in_specs=[pl.BlockSpec(mb.shape, lambda i: (0, 0)), _gather_spec(ka, to)],
        out_specs=pl.BlockSpec((to, 2 * O, N), lambda i: (i, 0, 0)),
        out_shape=jax.ShapeDtypeStruct((P, 2 * O, N), F32),
        compiler_params=_cparams(("parallel",)), name="hy_fft_filt_b",
    )(mb, ka)


def _convb_kernel(mb_ref, gc_ref, x_ref, kf_ref, o_ref, *, to):
    h = FFT_N2
    for j in range(to):
        xs = jnp.dot(mb_ref[...], _gather_ri(x_ref, j), preferred_element_type=F32)
        xr, xi = xs[:h], xs[h:]
        kr, ki = kf_ref[j, :h, :], kf_ref[j, h:, :]
        ys = jnp.concatenate([xr * kr - xi * ki, xr * ki + xi * kr], axis=0).astype(BF16)
        o_ref[j] = jnp.dot(gc_ref[j], ys, preferred_element_type=F32)


def _convb_call(mb, gc, a, kf, order, to=8):
    O, _, P, N = a.shape
    return pl.pallas_call(
        functools.partial(_convb_kernel, to=to), grid=(P // to,),
        in_specs=[pl.BlockSpec(mb.shape, lambda i: (0, 0)),
                  pl.BlockSpec((to, 2 * O, 2 * O), lambda i: (i, 0, 0)),
                  _gather_spec(a, to),
                  pl.BlockSpec((to, 2 * O, N), lambda i: (i, 0, order))],
        out_specs=pl.BlockSpec((to, 2 * O, N), lambda i: (i, 0, 0)),
        out_shape=jax.ShapeDtypeStruct((P, 2 * O, N), F32),
        compiler_params=_cparams(("parallel",)), name="hy_spec_mul",
    )(mb, gc, a, kf)


def _convd_kernel(md_ref, c_ref, g_ref, u_ref, s_ref, o_ref, *, to, natural):
    for j in range(to):
        y = jnp.dot(md_ref[...], _gather_ri(c_ref, j), preferred_element_type=F32)
        z = g_ref[j] * (y + u_ref[j] * s_ref[...])
        if natural:
            o_ref[:, j, :] = z
        else:
            o_ref[j] = z


def _convd_call(md, c, gate, u, skip, natural, to=8):
    O, _, P, N = c.shape
    R = md.shape[0]
    tspec = pl.BlockSpec((to, R, N), lambda i: (i, 0, 0))
    if natural:
        ospec = pl.BlockSpec((R, to, N), lambda i: (0, i, 0))
        oshape = jax.ShapeDtypeStruct((R, P, N), F32)
    else:
        ospec, oshape = tspec, jax.ShapeDtypeStruct((P, R, N), F32)
    return pl.pallas_call(
        functools.partial(_convd_kernel, to=to, natural=natural), grid=(P // to,),
        in_specs=[pl.BlockSpec(md.shape, lambda i: (0, 0)), _gather_spec(c, to), tspec, tspec,
                  pl.BlockSpec((1, N), lambda i: (0, 0))],
        out_specs=ospec, out_shape=oshape,
        compiler_params=_cparams(("parallel",)), name="hy_fft_d",
    )(md, c, gate, u, skip.reshape(1, N))


def _dft_tables():
    n = FFT_N1 * FFT_N2
    k = np.arange(FFT_N1)
    f = np.exp(-2j * np.pi * np.outer(k, k) / FFT_N1)
    t = np.exp(-2j * np.pi * np.outer(k, k) / n)
    return f, t, n


def _dft_matrices():
    f, t, n = _dft_tables()
    fr, fi = jnp.asarray(f.real, F32), jnp.asarray(f.imag, F32)
    tr, ti = jnp.asarray(t.real, F32), jnp.asarray(t.imag, F32)
    half = FFT_N1 // 2
    er = fr[None] * tr[:, :, None] - fi[None] * ti[:, :, None]
    ei = fr[None] * ti[:, :, None] + fi[None] * tr[:, :, None]
    ga = jnp.concatenate([jnp.concatenate([er[:, :, :half], -ei[:, :, :half]], axis=2),
                          jnp.concatenate([ei[:, :, :half], er[:, :, :half]], axis=2)], axis=1)
    gaf = jnp.concatenate([er, ei], axis=1)
    mb = jnp.concatenate([jnp.concatenate([fr, -fi], axis=1),
                          jnp.concatenate([fi, fr], axis=1)], axis=0)
    tct = jnp.transpose(tr)[:, :, None]
    tst = -jnp.transpose(ti)[:, :, None]
    gr = tct * fr[None] - tst * (-fi[None])
    gi = tct * (-fi[None]) + tst * fr[None]
    gc = jnp.concatenate([jnp.concatenate([gr, -gi], axis=2),
                          jnp.concatenate([gi, gr], axis=2)], axis=1)
    hr, hi = fr[:half] / n, -fi[:half] / n
    md = jnp.concatenate([jnp.concatenate([hr, -hi], axis=1),
                          jnp.concatenate([hi, hr], axis=1)], axis=0)
    return (ga.astype(BF16), gaf.astype(BF16), mb.astype(BF16), gc.astype(BF16), md.astype(BF16))


def _hyena_positions(L):
    p = FFT_N2 * np.arange(FFT_N1)[None, :] + np.arange(FFT_N2)[:, None]
    pos = np.where(p < L, p, 2 * L - 1 - p).reshape(2 * L, 1).astype(np.float64)
    t = pos / (L - 1)
    w = 2.0 * math.pi * pos / L
    f = np.linspace(1e-4, HY_BANDS - 1, HY_BANDS)[None, :]
    z = np.concatenate([t, np.cos(f * w), -np.sin(f * w),
                        np.zeros((2 * L, FILT_FEAT - HY_EMB))], axis=-1)
    z_pack = (z.reshape(-1, 2, FILT_HALF, FILT_FEAT).transpose(0, 2, 1, 3)
              .reshape(-1, 2 * FILT_FEAT))
    max_decay = math.log(HY_DECAY_TARGET) / HY_FAST_DECAY_PCT
    min_decay = math.log(HY_DECAY_TARGET) / HY_SLOW_DECAY_PCT
    deltas = jnp.linspace(min_decay, max_decay, HY_D, dtype=F32)
    window = jnp.exp(-jnp.asarray(t, F32) * jnp.abs(deltas)[None, :])
    return jnp.asarray(z_pack, F32), window


def _filter_weights(w1, b1, f1, w2, b2, f2, w3):
    def bd(w):
        z = jnp.zeros_like(w)
        return jnp.concatenate([jnp.concatenate([w, z], axis=1),
                                jnp.concatenate([z, w], axis=1)], axis=0)

    def twice(v):
        return jnp.concatenate([v, v])[None]

    w1 = jnp.pad(w1, ((0, FILT_FEAT - w1.shape[0]), (0, 0)))
    w3 = w3.reshape(-1, 2, 2, HY_D).transpose(0, 2, 1, 3).reshape(-1, 4 * HY_D)
    return (*_split_bf16(bd(w1)), twice(b1), twice(f1), *_split_bf16(bd(w2)), twice(b2), twice(f2),
            *_split_bf16(bd(w3)))


def _hyena_layer(hy_in, conv_w, conv_b, fw, skip, z_perm, win_perm, mats):
    B, L, _ = hy_in.shape
    ga, gaf, mb, gc, md = mats
    v, x1, x2 = _hconv_call(hy_in, conv_w, conv_b[None])
    kperm = _filt_call(z_perm, fw, win_perm)
    ka = _bm_call(gaf, kperm.reshape(FFT_N2, FFT_N1, 2 * HY_D), "hy_fft_filt_a")
    kf = _filtb_call(mb, ka.reshape(FFT_N2, 2, FFT_N1, 2 * HY_D))

    z = v
    for i, gate in enumerate((x1, x2)):
        a = _bm_call(ga, z, "hy_fft_a")
        c = _convb_call(mb, gc, a.reshape(FFT_N2, 2, FFT_N1, HY_D), kf, i)
        z = _convd_call(md, c.reshape(FFT_N1, 2, FFT_N2, HY_D), gate, z, skip[i],
                        natural=(i == 1))
    return z.reshape(B, L, HY_D)


def _kout_kernel(x_ref, yh_ref, yg_ref, ym_ref, gh_ref, gg_ref, gm_ref, w_ref, gpost_ref, gffn_ref,
                 xo_ref, h_ref):
    a = _rms(yh_ref[0], gh_ref[...]).astype(BF16)
    b = _rms(yg_ref[0].astype(F32), gg_ref[...]).astype(BF16)
    c = _rms(ym_ref[0].astype(F32), gm_ref[...]).astype(BF16)
    o1 = HY_D
    o2 = o1 + GQA_HEADS * GQA_HEAD_DIM
    y = (jnp.dot(a, w_ref[:o1, :], preferred_element_type=F32)
         + jnp.dot(b, w_ref[o1:o2, :], preferred_element_type=F32)
         + jnp.dot(c, w_ref[o2:, :], preferred_element_type=F32))
    xo = x_ref[0] + _rms(y, gpost_ref[...])
    xo_ref[0] = xo
    h_ref[0] = _rms(xo, gffn_ref[...]).astype(BF16)


def _kout_call(x, yh, yg, ym, gh, gg, gm, w_p, gpost, gffn):
    B, L, D = x.shape
    tm = TM_OUT

    def rows(a):
        return pl.BlockSpec((1, tm, a.shape[2]), lambda b, i: (b, i, 0))

    def full(a):
        return pl.BlockSpec(a.shape, lambda b, i: (0,) * a.ndim)

    return pl.pallas_call(
        _kout_kernel, grid=(B, L // tm),
        in_specs=[rows(x), rows(yh), rows(yg), rows(ym), full(gh), full(gg), full(gm), full(w_p),
                  full(gpost), full(gffn)],
        out_specs=[rows(x), rows(x)],
        out_shape=[jax.ShapeDtypeStruct((B, L, D), F32), jax.ShapeDtypeStruct((B, L, D), BF16)],
        compiler_params=_cparams(("parallel", "parallel")), name="out_proj",
    )(x, yh, yg, ym, gh, gg, gm, w_p, gpost, gffn)


HALO = 16


def _ffn_kernel(h_ref, hp_ref, hn_ref, x_ref, wup_ref, cw_ref, cb_ref, wd_ref, gpost_ref,
                o_ref, act_ref):
    i = pl.program_id(1)
    tm = h_ref.shape[1]
    prev = jnp.where(i > 0, hp_ref[0], jnp.zeros_like(hp_ref[0]))
    nxt = jnp.where(i < pl.num_programs(1) - 1, hn_ref[0], jnp.zeros_like(hn_ref[0]))
    he = jnp.concatenate([prev, h_ref[0], nxt], axis=0)
    ext = tm + 2 * HALO
    tf = TF_FFN

    def conv(c0):
        up = jnp.dot(he, wup_ref[:, c0:c0 + tf], preferred_element_type=F32)
        um = pltpu.roll(up, 1, 0)[HALO:HALO + tm]
        upl = pltpu.roll(up, ext - 1, 0)[HALO:HALO + tm]
        return (um * cw_ref[0:1, c0:c0 + tf] + up[HALO:HALO + tm] * cw_ref[1:2, c0:c0 + tf]
                + upl * cw_ref[2:3, c0:c0 + tf] + cb_ref[:, c0:c0 + tf])

    for j in range(D_FF // tf):
        g = conv(j * tf)
        u = conv(D_FF + j * tf)
        gelu = 0.5 * g * (1.0 + jnp.tanh(math.sqrt(2.0 / math.pi) * (g + 0.044715 * (g * g * g))))
        act_ref[:, j * tf:(j + 1) * tf] = (gelu * u).astype(BF16)
    f = jnp.dot(act_ref[...], wd_ref[...], preferred_element_type=F32)
    o_ref[0] = x_ref[0] + _rms(f, gpost_ref[...])


def _ffn_call(h, x, w_up, cw, cb, w_down, gpost):
    B, L, D = x.shape
    tm = TM_FFN
    nb = tm // HALO
    last = L // HALO - 1

    def resident(a):
        return pl.BlockSpec(a.shape, lambda b, i: (0,) * a.ndim, pipeline_mode=pl.Buffered(1))

    cb = cb[None]
    return pl.pallas_call(
        _ffn_kernel, grid=(B, L // tm),
        in_specs=[pl.BlockSpec((1, tm, D), lambda b, i: (b, i, 0)),
                  pl.BlockSpec((1, HALO, D), lambda b, i: (b, jnp.maximum(i * nb - 1, 0), 0)),
                  pl.BlockSpec((1, HALO, D), lambda b, i: (b, jnp.minimum((i + 1) * nb, last), 0)),
                  pl.BlockSpec((1, tm, D), lambda b, i: (b, i, 0)),
                  resident(w_up), resident(cw), resident(cb), resident(w_down), resident(gpost)],
        out_specs=pl.BlockSpec((1, tm, D), lambda b, i: (b, i, 0)),
        out_shape=jax.ShapeDtypeStruct((B, L, D), F32),
        scratch_shapes=[pltpu.VMEM((tm, D_FF), BF16)],
        compiler_params=_cparams(("parallel", "parallel")), name="conv_ffn",
    )(h, h, h, x, w_up, cw, cb, w_down, gpost)


def _axial_tables(L, rot_dim):
    rows = L // GRID_W
    row_idx = jnp.broadcast_to(jnp.arange(rows)[:, None], (rows, GRID_W)).reshape(L)
    col_idx = jnp.broadcast_to(jnp.arange(GRID_W)[None, :], (rows, GRID_W)).reshape(L)
    n_axis = rot_dim // 4
    inv = ROPE_THETA ** (-jnp.arange(n_axis, dtype=F32) / n_axis)
    ang = jnp.concatenate([row_idx[:, None].astype(F32) * inv,
                           col_idx[:, None].astype(F32) * inv], axis=-1)
    return jnp.cos(ang), jnp.sin(ang)


def _rope_tables(L):
    def lanes(parts):
        used = sum(p.shape[1] for p in parts)
        return jnp.concatenate(parts + [jnp.zeros((L, LANES - used), F32)], axis=1)

    cg, sg = _axial_tables(L, GQA_HEAD_DIM)
    tg = (lanes([cg, cg]), lanes([sg, sg]))
    cm, sm = _axial_tables(L, MLA_ROPE_DIM)
    nope0 = jnp.zeros((L, MLA_NOPE_DIM), F32)
    nope1 = jnp.ones((L, MLA_NOPE_DIM), F32)
    tmk = (lanes([nope0, cm, cm]), lanes([nope0, sm, sm]))
    sc = (MLA_NOPE_DIM + MLA_ROPE_DIM) ** -0.5 * LOG2E
    tmq = (lanes([nope1, cm, cm]) * sc, tmk[1] * sc)
    return tg, tmq, tmk


def _partner(w, half, sign=-1.0):
    return jnp.concatenate([sign * w[..., half:], w[..., :half]], axis=-1)


def _pad_heads(w, n_heads, width):
    k = w.shape[0]
    return jnp.pad(w.reshape(k, n_heads, width), ((0, 0), (0, 0), (0, LANES - width))).reshape(
        k, n_heads * LANES)


def _pad_vec(g, n_heads, width):
    return jnp.pad(g.reshape(n_heads, width), ((0, 0), (0, LANES - width))).reshape(1, n_heads * LANES)


def kernel(x, mix_pre_norm, w_in, hy_conv_w, hy_conv_b, hy_filt_w1, hy_filt_b1, hy_filt_freq1,
           hy_filt_w2, hy_filt_b2, hy_filt_freq2, hy_filt_w3, hy_skip, gqa_q_norm, gqa_k_norm,
           mla_q_a_norm, mla_w_uq, mla_kv_a_norm, mla_w_ukv, hy_out_norm, gqa_out_norm,
           mla_out_norm, w_out, mix_post_norm, ffn_pre_norm, w_up, ffn_conv_w, ffn_conv_b,
           w_down, ffn_post_norm):
    B, L, D = x.shape
    assert B == 2 and 2 * L == FFT_N1 * FFT_N2 and D == D_MODEL
    depth = w_in.shape[0]
    tg, tmq, tmk = _rope_tables(L)
    z_perm, win_perm = _hyena_positions(L)
    mats = _dft_matrices()

    for l in range(depth):
        wl = w_in[l]
        hd, hh, rh = GQA_HEAD_DIM, GQA_HEAD_DIM // 2, MLA_ROPE_DIM // 2
        o1 = 768
        o2 = o1 + GQA_HEADS * hd
        o3 = o2 + GQA_KV_HEADS * hd
        o4 = o3 + GQA_KV_HEADS * hd
        o5 = o4 + MLA_Q_RANK
        o6 = o5 + MLA_KV_RANK
        wq = wl[:, o1:o2].reshape(D, GQA_HEADS, hd)
        wk = wl[:, o2:o3].reshape(D, GQA_KV_HEADS, hd)
        wkr = wl[:, o6:]
        pe_pad = ((0, 0), (MLA_NOPE_DIM, LANES - MLA_NOPE_DIM - MLA_ROPE_DIM))
        win_p = jnp.concatenate(
            [wl[:, :o1],
             _pad_heads(wq.reshape(D, -1), GQA_HEADS, hd),
             _pad_heads(_partner(wq, hh).reshape(D, -1), GQA_HEADS, hd),
             _pad_heads(wk.reshape(D, -1), GQA_KV_HEADS, hd),
             _pad_heads(_partner(wk, hh).reshape(D, -1), GQA_KV_HEADS, hd),
             wl[:, o3:o6], jnp.pad(wkr, pe_pad), jnp.pad(_partner(wkr, rh), pe_pad)],
            axis=1).astype(BF16)
        gq_gain = gqa_q_norm[l] * (hd ** -0.5 * LOG2E)
        gq = (_pad_vec(gq_gain, 1, hd), _pad_vec(_partner(gq_gain, hh, 1.0), 1, hd))
        gk = (_pad_vec(gqa_k_norm[l], 1, hd), _pad_vec(_partner(gqa_k_norm[l], hh, 1.0), 1, hd))
        wuq = mla_w_uq[l].reshape(MLA_Q_RANK, MLA_HEADS, MLA_NOPE_DIM + MLA_ROPE_DIM)
        wuq_pe = jnp.pad(_partner(wuq[:, :, MLA_NOPE_DIM:], rh),
                         ((0, 0), (0, 0), (MLA_NOPE_DIM, 0)))
        wuq_p = jnp.concatenate(
            [_pad_heads(mla_w_uq[l], MLA_HEADS, MLA_NOPE_DIM + MLA_ROPE_DIM),
             _pad_heads(wuq_pe.reshape(MLA_Q_RANK, -1), MLA_HEADS, MLA_NOPE_DIM + MLA_ROPE_DIM)],
            axis=1).astype(BF16)
        wukv = mla_w_ukv[l].reshape(MLA_KV_RANK, MLA_HEADS, MLA_NOPE_DIM + MLA_V_DIM)
        wukvk_p = _pad_heads(wukv[:, :, :MLA_NOPE_DIM].reshape(MLA_KV_RANK, -1), MLA_HEADS,
                             MLA_NOPE_DIM).astype(BF16)
        wukvv = wukv[:, :, MLA_NOPE_DIM:].reshape(MLA_KV_RANK, MLA_HEADS * MLA_V_DIM).astype(BF16)

        hy_in, qg, kg, vg, qm, km, vm = _kin_call(
            x, mix_pre_norm[l][None], win_p, gq, gk, tg, mla_q_a_norm[l][None], wuq_p, tmq,
            mla_kv_a_norm[l][None], wukvk_p, wukvv, tmk)

        fw = _filter_weights(hy_filt_w1[l], hy_filt_b1[l], hy_filt_freq1[l], hy_filt_w2[l],
                             hy_filt_b2[l], hy_filt_freq2[l], hy_filt_w3[l])
        y_hy = _hyena_layer(hy_in, hy_conv_w[l], hy_conv_b[l], fw, hy_skip[l], z_perm, win_perm, mats)

        y_gqa = _attn_call(qg.transpose(0, 2, 1), kg, vg.reshape(B, GQA_KV_HEADS, GQA_HEAD_DIM, L),
                           n_kv=1, n_rep=GQA_HEADS // GQA_KV_HEADS, tq=TQ_GQA, name="attn_gqa")
        y_mla = _attn_call(qm.transpose(0, 2, 1), km, vm.reshape(B, MLA_HEADS, MLA_V_DIM, L),
                           n_kv=2, n_rep=1, tq=TQ_MLA, name="attn_mla")

        x, h2 = _kout_call(x, y_hy, y_gqa, y_mla, hy_out_norm[l][None], gqa_out_norm[l][None],
                           mla_out_norm[l][None], w_out[l].astype(BF16),
                           mix_post_norm[l][None], ffn_pre_norm[l][None])
        x = _ffn_call(h2, x, w_up[l].astype(BF16), ffn_conv_w[l], ffn_conv_b[l],
                      w_down[l].astype(BF16), ffn_post_norm[l][None])
    return x
```

```python
import functools
import math

import numpy as np
import jax
import jax.numpy as jnp
from jax import lax
from jax.experimental import pallas as pl
from jax.experimental.pallas import tpu as pltpu

F32 = jnp.float32
BF16 = jnp.bfloat16

NORM_EPS = 1e-6
ROPE_THETA = 10000.0
GRID_W = 64
LOG2E = math.log2(math.e)

D_MODEL = 1024
HY_D = 256
HY_EMB = 33
HY_BANDS = 16
HY_DECAY_TARGET = 1e-2
HY_FAST_DECAY_PCT = 0.3
HY_SLOW_DECAY_PCT = 1.5
GQA_HEADS = 8
GQA_KV_HEADS = 2
GQA_HEAD_DIM = 64
MLA_HEADS = 4
MLA_Q_RANK = 256
MLA_KV_RANK = 128
MLA_NOPE_DIM = 64
MLA_ROPE_DIM = 32
MLA_V_DIM = 64
D_FF = 2816

LANES = 128
FFT_N1 = 128
FFT_N2 = 128

TM_IN = 512
TQ_GQA = 128
TQ_MLA = 256
TM_OUT = 512
TM_FFN = 512
TF_FFN = 256
TL_HCONV = 1024
VT_ONES = 16
VT_ROWS = 64 + VT_ONES
FILT_HALF = 512
FILT_FEAT = 64
VMEM_LIMIT = 56 * 1024 * 1024


def _cparams(sem):
    return pltpu.CompilerParams(dimension_semantics=sem, vmem_limit_bytes=VMEM_LIMIT)


def _rms(x, g):
    return x * lax.rsqrt(jnp.mean(x * x, axis=-1, keepdims=True) + NORM_EPS) * g


def _kin_kernel(x_ref, gpre_ref, win_ref, gqn_ref, gqs_ref, gkn_ref, gks_ref, cg_ref, sg_ref,
                mqn_ref, wuq_ref, cmq_ref, smq_ref,
                mkvn_ref, wukvk_ref, wukvv_ref, cmk_ref, smk_ref,
                hy_ref, qg_ref, kg_ref, vg_ref, qm_ref, km_ref, vm_ref):
    x = x_ref[0]
    h = _rms(x, gpre_ref[...]).astype(BF16)
    cur = [0]

    def proj(n):
        lo = cur[0]
        cur[0] = lo + n
        return jnp.dot(h, win_ref[:, lo:lo + n], preferred_element_type=F32)

    hy_ref[0] = proj(768)

    cg, sg = cg_ref[...], sg_ref[...]

    def heads(n, gain, gain_sw, store):
        xa, xb = proj(n * LANES), proj(n * LANES)
        ca, sa = cg * gain, sg * gain_sw
        for j in range(n):
            xc, xs = xa[:, j * LANES:(j + 1) * LANES], xb[:, j * LANES:(j + 1) * LANES]
            ms = jnp.sum(xc * xc, axis=-1, keepdims=True) * (1.0 / GQA_HEAD_DIM)
            store(j, ((xc * ca + xs * sa) * lax.rsqrt(ms + NORM_EPS)).astype(BF16))

    def store_q(j, v):
        qg_ref[0, :, j * LANES:(j + 1) * LANES] = v

    def store_k(j, v):
        kg_ref[0, j] = v

    heads(GQA_HEADS, gqn_ref[...], gqs_ref[...], store_q)
    heads(GQA_KV_HEADS, gkn_ref[...], gks_ref[...], store_k)
    def store_vt(ref, base, vt):
        ones = jnp.ones((VT_ONES, vt.shape[1]), BF16)
        for j in range(2):
            r0 = (base + j) * VT_ROWS
            ref[0, r0:r0 + GQA_HEAD_DIM, :] = vt[j * GQA_HEAD_DIM:(j + 1) * GQA_HEAD_DIM].astype(BF16)
            ref[0, r0 + GQA_HEAD_DIM:r0 + VT_ROWS, :] = ones

    store_vt(vg_ref, 0, proj(LANES).T)

    cq = _rms(proj(MLA_Q_RANK), mqn_ref[...]).astype(BF16)
    qm = jnp.dot(cq, wuq_ref[...], preferred_element_type=F32)
    cmq, smq = cmq_ref[...], smq_ref[...]
    nq = MLA_HEADS * LANES
    for j in range(MLA_HEADS):
        qm_ref[0, :, j * LANES:(j + 1) * LANES] = (
            qm[:, j * LANES:(j + 1) * LANES] * cmq
            + qm[:, nq + j * LANES:nq + (j + 1) * LANES] * smq).astype(BF16)

    ckv = _rms(proj(MLA_KV_RANK), mkvn_ref[...]).astype(BF16)
    kpe = proj(LANES) * cmk_ref[...] + proj(LANES) * smk_ref[...]
    kn = jnp.dot(ckv, wukvk_ref[...], preferred_element_type=F32)
    for j in range(MLA_HEADS):
        km_ref[0, j] = (kn[:, j * LANES:(j + 1) * LANES] + kpe).astype(BF16)
    vm = jnp.dot(ckv, wukvv_ref[...], preferred_element_type=F32)
    for j in range(vm.shape[1] // LANES):
        store_vt(vm_ref, 2 * j, vm[:, j * LANES:(j + 1) * LANES].T)


def _kin_call(x, gpre, win_p, gq, gk, tg, mqn, wuq_p, tmq, mkvn, wukvk_p, wukvv, tmk):
    B, L, D = x.shape
    tm = TM_IN
    nt = L // tm

    def full(a):
        return pl.BlockSpec(a.shape, lambda b, i: (0,) * a.ndim, pipeline_mode=pl.Buffered(1))

    def rows(w):
        return pl.BlockSpec((tm, w), lambda b, i: (i, 0))

    in_specs = [pl.BlockSpec((1, tm, D), lambda b, i: (b, i, 0)), full(gpre), full(win_p),
                full(gq[0]), full(gq[1]), full(gk[0]), full(gk[1]), rows(LANES), rows(LANES),
                full(mqn), full(wuq_p), rows(LANES), rows(LANES),
                full(mkvn), full(wukvk_p), full(wukvv), rows(LANES), rows(LANES)]
    gv_rows = GQA_KV_HEADS * VT_ROWS
    mv_rows = MLA_HEADS * VT_ROWS
    out_shape = [
        jax.ShapeDtypeStruct((B, L, 768), F32),
        jax.ShapeDtypeStruct((B, L, GQA_HEADS * LANES), BF16),
        jax.ShapeDtypeStruct((B, GQA_KV_HEADS, L, LANES), BF16),
        jax.ShapeDtypeStruct((B, gv_rows, L), BF16),
        jax.ShapeDtypeStruct((B, L, MLA_HEADS * LANES), BF16),
        jax.ShapeDtypeStruct((B, MLA_HEADS, L, LANES), BF16),
        jax.ShapeDtypeStruct((B, mv_rows, L), BF16),
    ]
    out_specs = [
        pl.BlockSpec((1, tm, 768), lambda b, i: (b, i, 0)),
        pl.BlockSpec((1, tm, GQA_HEADS * LANES), lambda b, i: (b, i, 0)),
        pl.BlockSpec((1, GQA_KV_HEADS, tm, LANES), lambda b, i: (b, 0, i, 0)),
        pl.BlockSpec((1, gv_rows, tm), lambda b, i: (b, 0, i)),
        pl.BlockSpec((1, tm, MLA_HEADS * LANES), lambda b, i: (b, i, 0)),
        pl.BlockSpec((1, MLA_HEADS, tm, LANES), lambda b, i: (b, 0, i, 0)),
        pl.BlockSpec((1, mv_rows, tm), lambda b, i: (b, 0, i)),
    ]
    return pl.pallas_call(
        _kin_kernel, grid=(B, nt), in_specs=in_specs, out_specs=out_specs, out_shape=out_shape,
        compiler_params=_cparams(("parallel", "parallel")), name="in_proj",
    )(x, gpre, win_p, *gq, *gk, *tg, mqn, wuq_p, *tmq, mkvn, wukvk_p, wukvv, *tmk)


def _attn_kernel(q_ref, k_ref, vt_ref, o_ref, sa_ref, sb_ref, ma_ref, mb_ref, *,
                 n_kv, n_rep, tq, n_chunks, tk):
    i = pl.program_id(2)
    cols = n_rep * tq
    width = n_kv * cols
    grp = tk // 8

    @pl.when(i == 0)
    def _():
        sb_ref[...] = jnp.zeros(sb_ref.shape, F32)
        mb_ref[...] = jnp.zeros(mb_ref.shape, F32)

    def step(sw_ref, mw_ref, sr_ref, mr_ref):
        qs = [jnp.concatenate([q_ref[0, (a * n_rep + j) * LANES:(a * n_rep + j + 1) * LANES, :]
                               for j in range(n_rep)], axis=1) for a in range(n_kv)]
        mx = jnp.max(mr_ref[...], axis=0, keepdims=True)
        m = jnp.full((8, width), -jnp.inf, F32)
        accs = [jnp.zeros((vt_ref.shape[2], cols), F32) for _ in range(n_kv)]
        for c in range(n_chunks):
            st = jnp.concatenate(
                [jnp.dot(k_ref[0, a, c * tk:(c + 1) * tk, :], qs[a], preferred_element_type=F32)
                 for a in range(n_kv)], axis=1)
            sw_ref[c] = st
            m = jnp.maximum(m, jnp.max(st.reshape(grp, 8, width), axis=0))
            pb = jnp.exp2((sr_ref[c] - mx).astype(BF16))
            for a in range(n_kv):
                accs[a] = accs[a] + jnp.dot(vt_ref[0, a, :, c * tk:(c + 1) * tk],
                                            pb[:, a * cols:(a + 1) * cols],
                                            preferred_element_type=F32)
        mw_ref[...] = m
        dv = vt_ref.shape[2] - VT_ONES
        heads = []
        for a in range(n_kv):
            oa = accs[a][:dv] / accs[a][dv:dv + 1]
            heads += [oa[:, j * tq:(j + 1) * tq] for j in range(n_rep)]
        o_ref[0] = jnp.concatenate(heads, axis=0).T.astype(BF16)

    @pl.when(i % 2 == 0)
    def _():
        step(sa_ref, ma_ref, sb_ref, mb_ref)

    @pl.when(i % 2 == 1)
    def _():
        step(sb_ref, mb_ref, sa_ref, ma_ref)


def _attn_call(qt, k, vt, *, n_kv, n_rep, tq, name):
    B, hq, L = qt.shape
    H = hq // LANES
    hkv, vrows = vt.shape[1], vt.shape[2]
    dv = vrows - VT_ONES
    G = hkv // n_kv
    hs = n_kv * n_rep
    tk = TM_IN
    n_chunks = L // tk
    nq = L // tq
    width = hs * tq
    kern = functools.partial(_attn_kernel, n_kv=n_kv, n_rep=n_rep, tq=tq, n_chunks=n_chunks, tk=tk)
    return pl.pallas_call(
        kern, grid=(B, G, nq + 1),
        in_specs=[pl.BlockSpec((1, hs * LANES, tq), lambda b, g, i: (b, g, jnp.minimum(i, nq - 1))),
                  pl.BlockSpec((1, n_kv, L, LANES), lambda b, g, i: (b, g, 0, 0),
                               pipeline_mode=pl.Buffered(1)),
                  pl.BlockSpec((1, n_kv, vrows, L), lambda b, g, i: (b, g, 0, 0),
                               pipeline_mode=pl.Buffered(1))],
        out_specs=pl.BlockSpec((1, tq, hs * dv), lambda b, g, i: (b, jnp.maximum(i - 1, 0), g)),
        out_shape=jax.ShapeDtypeStruct((B, L, H * dv), BF16),
        scratch_shapes=[pltpu.VMEM((n_chunks, tk, width), F32), pltpu.VMEM((n_chunks, tk, width), F32),
                        pltpu.VMEM((8, width), F32), pltpu.VMEM((8, width), F32)],
        compiler_params=_cparams(("parallel", "parallel", "arbitrary")), name=name,
    )(qt, k, vt)


def _hconv_kernel(x_ref, xp_ref, xn_ref, w_ref, b_ref, v_ref, x1_ref, x2_ref):
    i = pl.program_id(1)
    x = x_ref[0]
    tl = x.shape[0]
    prev = jnp.where(i > 0, xp_ref[0][7:8, :], 0.0)
    nxt = jnp.where(i < pl.num_programs(1) - 1, xn_ref[0][0:1, :], 0.0)
    r = lax.broadcasted_iota(jnp.int32, x.shape, 0)
    xm = jnp.where(r == 0, prev, pltpu.roll(x, 1, 0))
    xp = jnp.where(r == tl - 1, nxt, pltpu.roll(x, tl - 1, 0))
    uc = xm * w_ref[0:1, :] + x * w_ref[1:2, :] + xp * w_ref[2:3, :] + b_ref[...]
    for r in range(tl // FFT_N2):
        blk = uc[r * FFT_N2:(r + 1) * FFT_N2]
        v_ref[:, r, :] = blk[:, :HY_D]
        x1_ref[:, r, :] = blk[:, HY_D:2 * HY_D]
        x2_ref[:, r, :] = blk[:, 2 * HY_D:]


def _hconv_call(hy_in, w, b):
    B, L, C = hy_in.shape
    tl = TL_HCONV
    nb = tl // 8
    last = L // 8 - 1
    nt = L // tl
    rows = tl // FFT_N2
    out = jax.ShapeDtypeStruct((FFT_N2, B * L // FFT_N2, HY_D), F32)
    ospec = pl.BlockSpec((FFT_N2, rows, HY_D), lambda b_, i: (0, b_ * nt + i, 0))
    return pl.pallas_call(
        _hconv_kernel, grid=(B, L // tl),
        in_specs=[pl.BlockSpec((1, tl, C), lambda b_, i: (b_, i, 0)),
                  pl.BlockSpec((1, 8, C), lambda b_, i: (b_, jnp.maximum(i * nb - 1, 0), 0)),
                  pl.BlockSpec((1, 8, C), lambda b_, i: (b_, jnp.minimum((i + 1) * nb, last), 0)),
                  pl.BlockSpec((3, C), lambda b_, i: (0, 0)),
                  pl.BlockSpec((1, C), lambda b_, i: (0, 0))],
        out_specs=[ospec, ospec, ospec], out_shape=[out, out, out],
        compiler_params=_cparams(("parallel", "parallel")), name="hy_conv3",
    )(hy_in, hy_in, hy_in, w, b)


def _split_bf16(a):
    hi = a.astype(BF16)
    return hi, (a - hi.astype(F32)).astype(BF16)


def _dot3(a, w_hi, w_lo):
    a_hi, a_lo = _split_bf16(a)
    dot = functools.partial(jnp.dot, preferred_element_type=F32)
    return dot(a_hi, w_hi) + dot(a_lo, w_hi) + dot(a_hi, w_lo)


def _filt_kernel(z_ref, w1h_ref, w1l_ref, b1_ref, f1_ref, w2h_ref, w2l_ref, b2_ref, f2_ref,
                 w3h_ref, w3l_ref, win_ref, o_ref):
    h = jnp.sin(f1_ref[...] * (_dot3(z_ref[...], w1h_ref[...], w1l_ref[...]) + b1_ref[...]))
    h = jnp.sin(f2_ref[...] * (_dot3(h, w2h_ref[...], w2l_ref[...]) + b2_ref[...]))
    k = _dot3(h, w3h_ref[...], w3l_ref[...])
    half = k.shape[0]
    wcols = 4 * HY_D
    n1 = lax.broadcasted_iota(jnp.int32, (half, 2 * HY_D), 0) % FFT_N1
    for s in range(2):
        ks = k[:, s * wcols:(s + 1) * wcols]
        kk = jnp.where(n1 >= FFT_N1 // 2, ks[:, 2 * HY_D:], ks[:, :2 * HY_D])
        win = win_ref[s * half:(s + 1) * half, :]
        o_ref[s * half:(s + 1) * half, :] = kk * jnp.concatenate([win, win], axis=-1)


def _filt_call(z_pack, fw, win_perm):
    tp = 2 * FILT_HALF
    n = win_perm.shape[0]

    def full(a):
        return pl.BlockSpec(a.shape, lambda i: (0,) * a.ndim)

    return pl.pallas_call(
        _filt_kernel, grid=(n // tp,),
        in_specs=[pl.BlockSpec((FILT_HALF, z_pack.shape[1]), lambda i: (i, 0))]
        + [full(a) for a in fw] + [pl.BlockSpec((tp, HY_D), lambda i: (i, 0))],
        out_specs=pl.BlockSpec((tp, 2 * HY_D), lambda i: (i, 0)),
        out_shape=jax.ShapeDtypeStruct((n, 2 * HY_D), F32),
        compiler_params=_cparams(("parallel",)), name="hy_filter",
    )(z_pack, *fw, win_perm)


def _bm_kernel(g_ref, x_ref, o_ref, *, to):
    for t in range(to):
        o_ref[t] = jnp.dot(g_ref[t], x_ref[t].astype(BF16), preferred_element_type=F32)


def _bm_call(g, x, name, to=8):
    O, K, N = x.shape
    M = g.shape[1]
    return pl.pallas_call(
        functools.partial(_bm_kernel, to=to), grid=(O // to,),
        in_specs=[pl.BlockSpec((to, M, K), lambda i: (i, 0, 0)),
                  pl.BlockSpec((to, K, N), lambda i: (i, 0, 0))],
        out_specs=pl.BlockSpec((to, M, N), lambda i: (i, 0, 0)),
        out_shape=jax.ShapeDtypeStruct((O, M, N), F32),
        compiler_params=_cparams(("parallel",)), name=name,
    )(g, x)


def _gather_ri(x_ref, j):
    return jnp.concatenate([x_ref[:, 0, j, :], x_ref[:, 1, j, :]], axis=0).astype(BF16)


def _gather_spec(a, to):
    return pl.BlockSpec((a.shape[0], 2, to, a.shape[3]), lambda i: (0, 0, i, 0))


def _filtb_kernel(mb_ref, x_ref, o_ref, *, to):
    for j in range(to):
        o_ref[j] = jnp.dot(mb_ref[...], _gather_ri(x_ref, j), preferred_element_type=F32)


def _filtb_call(mb, ka, to=8):
    O, _, P, N = ka.shape
    return pl.pallas_call(
        functools.partial(_filtb_kernel, to=to), grid=(P // to,),
        in_specs=[pl.BlockSpec(mb.shape, lambda i: (0, 0)), _gather_spec(ka, to)],
        out_specs=pl.BlockSpec((to, 2 * O, N), lambda i: (i, 0, 0)),
        out_shape=jax.ShapeDtypeStruct((P, 2 * O, N), F32),
        compiler_params=_cparams(("parallel",)), name="hy_fft_filt_b",
    )(mb, ka)


def _convb_kernel(mb_ref, gc_ref, x_ref, kf_ref, o_ref, *, to):
    h = FFT_N2
    for j in range(to):
        xs = jnp.dot(mb_ref[...], _gather_ri(x_ref, j), preferred_element_type=F32)
        xr, xi = xs[:h], xs[h:]
        kr, ki = kf_ref[j, :h, :], kf_ref[j, h:, :]
        ys = jnp.concatenate([xr * kr - xi * ki, xr * ki + xi * kr], axis=0).astype(BF16)
        o_ref[j] = jnp.dot(gc_ref[j], ys, preferred_element_type=F32)


def _convb_call(mb, gc, a, kf, order, to=8):
    O, _, P, N = a.shape
    return pl.pallas_call(
        functools.partial(_convb_kernel, to=to), grid=(P // to,),
        in_specs=[pl.BlockSpec(mb.shape, lambda i: (0, 0)),
                  pl.BlockSpec((to, 2 * O, 2 * O), lambda i: (i, 0, 0)),
                  _gather_spec(a, to),
                  pl.BlockSpec((to, 2 * O, N), lambda i: (i, 0, order))],
        out_specs=pl.BlockSpec((to, 2 * O, N), lambda i: (i, 0, 0)),
        out_shape=jax.ShapeDtypeStruct((P, 2 * O, N), F32),
        compiler_params=_cparams(("parallel",)), name="hy_spec_mul",
    )(mb, gc, a, kf)


def _convd_kernel(md_ref, c_ref, g_ref, u_ref, s_ref, o_ref, *, to, natural):
    for j in range(to):
        y = jnp.dot(md_ref[...], _gather_ri(c_ref, j), preferred_element_type=F32)
        z = g_ref[j] * (y + u_ref[j] * s_ref[...])
        if natural:
            o_ref[:, j, :] = z
        else:
            o_ref[j] = z


def _convd_call(md, c, gate, u, skip, natural, to=8):
    O, _, P, N = c.shape
    R = md.shape[0]
    tspec = pl.BlockSpec((to, R, N), lambda i: (i, 0, 0))
    if natural:
        ospec = pl.BlockSpec((R, to, N), lambda i: (0, i, 0))
        oshape = jax.ShapeDtypeStruct((R, P, N), F32)
    else:
        ospec, oshape = tspec, jax.ShapeDtypeStruct((P, R, N), F32)
    return pl.pallas_call(
        functools.partial(_convd_kernel, to=to, natural=natural), grid=(P // to,),
        in_specs=[pl.BlockSpec(md.shape, lambda i: (0, 0)), _gather_spec(c, to), tspec, tspec,
                  pl.BlockSpec((1, N), lambda i: (0, 0))],
        out_specs=ospec, out_shape=oshape,
        compiler_params=_cparams(("parallel",)), name="hy_fft_d",
    )(md, c, gate, u, skip.reshape(1, N))


def _dft_tables():
    n = FFT_N1 * FFT_N2
    k = np.arange(FFT_N1)
    f = np.exp(-2j * np.pi * np.outer(k, k) / FFT_N1)
    t = np.exp(-2j * np.pi * np.outer(k, k) / n)
    return f, t, n


def _dft_matrices():
    f, t, n = _dft_tables()
    fr, fi = jnp.asarray(f.real, F32), jnp.asarray(f.imag, F32)
    tr, ti = jnp.asarray(t.real, F32), jnp.asarray(t.imag, F32)
    half = FFT_N1 // 2
    er = fr[None] * tr[:, :, None] - fi[None] * ti[:, :, None]
    ei = fr[None] * ti[:, :, None] + fi[None] * tr[:, :, None]
    ga = jnp.concatenate([jnp.concatenate([er[:, :, :half], -ei[:, :, :half]], axis=2),
                          jnp.concatenate([ei[:, :, :half], er[:, :, :half]], axis=2)], axis=1)
    gaf = jnp.concatenate([er, ei], axis=1)
    mb = jnp.concatenate([jnp.concatenate([fr, -fi], axis=1),
                          jnp.concatenate([fi, fr], axis=1)], axis=0)
    tct = jnp.transpose(tr)[:, :, None]
    tst = -jnp.transpose(ti)[:, :, None]
    gr = tct * fr[None] - tst * (-fi[None])
    gi = tct * (-fi[None]) + tst * fr[None]
    gc = jnp.concatenate([jnp.concatenate([gr, -gi], axis=2),
                          jnp.concatenate([gi, gr], axis=2)], axis=1)
    hr, hi = fr[:half] / n, -fi[:half] / n
    md = jnp.concatenate([jnp.concatenate([hr, -hi], axis=1),
                          jnp.concatenate([hi, hr], axis=1)], axis=0)
    return (ga.astype(BF16), gaf.astype(BF16), mb.astype(BF16), gc.astype(BF16), md.astype(BF16))


def _hyena_positions(L):
    p = FFT_N2 * np.arange(FFT_N1)[None, :] + np.arange(FFT_N2)[:, None]
    pos = np.where(p < L, p, 2 * L - 1 - p).reshape(2 * L, 1).astype(np.float64)
    t = pos / (L - 1)
    w = 2.0 * math.pi * pos / L
    f = np.linspace(1e-4, HY_BANDS - 1, HY_BANDS)[None, :]
    z = np.concatenate([t, np.cos(f * w), -np.sin(f * w),
                        np.zeros((2 * L, FILT_FEAT - HY_EMB))], axis=-1)
    z_pack = (z.reshape(-1, 2, FILT_HALF, FILT_FEAT).transpose(0, 2, 1, 3)
              .reshape(-1, 2 * FILT_FEAT))
    max_decay = math.log(HY_DECAY_TARGET) / HY_FAST_DECAY_PCT
    min_decay = math.log(HY_DECAY_TARGET) / HY_SLOW_DECAY_PCT
    deltas = jnp.linspace(min_decay, max_decay, HY_D, dtype=F32)
    window = jnp.exp(-jnp.asarray(t, F32) * jnp.abs(deltas)[None, :])
    return jnp.asarray(z_pack, F32), window


def _filter_weights(w1, b1, f1, w2, b2, f2, w3):
    def bd(w):
        z = jnp.zeros_like(w)
        return jnp.concatenate([jnp.concatenate([w, z], axis=1),
                                jnp.concatenate([z, w], axis=1)], axis=0)

    def twice(v):
        return jnp.concatenate([v, v])[None]

    w1 = jnp.pad(w1, ((0, FILT_FEAT - w1.shape[0]), (0, 0)))
    w3 = w3.reshape(-1, 2, 2, HY_D).transpose(0, 2, 1, 3).reshape(-1, 4 * HY_D)
    return (*_split_bf16(bd(w1)), twice(b1), twice(f1), *_split_bf16(bd(w2)), twice(b2), twice(f2),
            *_split_bf16(bd(w3)))


def _hyena_layer(hy_in, conv_w, conv_b, fw, skip, z_perm, win_perm, mats):
    B, L, _ = hy_in.shape
    ga, gaf, mb, gc, md = mats
    v, x1, x2 = _hconv_call(hy_in, conv_w, conv_b[None])
    kperm = _filt_call(z_perm, fw, win_perm)
    ka = _bm_call(gaf, kperm.reshape(FFT_N2, FFT_N1, 2 * HY_D), "hy_fft_filt_a")
    kf = _filtb_call(mb, ka.reshape(FFT_N2, 2, FFT_N1, 2 * HY_D))

    z = v
    for i, gate in enumerate((x1, x2)):
        a = _bm_call(ga, z, "hy_fft_a")
        c = _convb_call(mb, gc, a.reshape(FFT_N2, 2, FFT_N1, HY_D), kf, i)
        z = _convd_call(md, c.reshape(FFT_N1, 2, FFT_N2, HY_D), gate, z, skip[i],
                        natural=(i == 1))
    return z.reshape(B, L, HY_D)


def _kout_kernel(x_ref, yh_ref, yg_ref, ym_ref, gh_ref, gg_ref, gm_ref, w_ref, gpost_ref, gffn_ref,
                 xo_ref, h_ref):
    a = _rms(yh_ref[0], gh_ref[...]).astype(BF16)
    b = _rms(yg_ref[0].astype(F32), gg_ref[...]).astype(BF16)
    c = _rms(ym_ref[0].astype(F32), gm_ref[...]).astype(BF16)
    o1 = HY_D
    o2 = o1 + GQA_HEADS * GQA_HEAD_DIM
    y = (jnp.dot(a, w_ref[:o1, :], preferred_element_type=F32)
         + jnp.dot(b, w_ref[o1:o2, :], preferred_element_type=F32)
         + jnp.dot(c, w_ref[o2:, :], preferred_element_type=F32))
    xo = x_ref[0] + _rms(y, gpost_ref[...])
    xo_ref[0] = xo
    h_ref[0] = _rms(xo, gffn_ref[...]).astype(BF16)


def _kout_call(x, yh, yg, ym, gh, gg, gm, w_p, gpost, gffn):
    B, L, D = x.shape
    tm = TM_OUT

    def rows(a):
        return pl.BlockSpec((1, tm, a.shape[2]), lambda b, i: (b, i, 0))

    def full(a):
        return pl.BlockSpec(a.shape, lambda b, i: (0,) * a.ndim)

    return pl.pallas_call(
        _kout_kernel, grid=(B, L // tm),
        in_specs=[rows(x), rows(yh), rows(yg), rows(ym), full(gh), full(gg), full(gm), full(w_p),
                  full(gpost), full(gffn)],
        out_specs=[rows(x), rows(x)],
        out_shape=[jax.ShapeDtypeStruct((B, L, D), F32), jax.ShapeDtypeStruct((B, L, D), BF16)],
        compiler_params=_cparams(("parallel", "parallel")), name="out_proj",
    )(x, yh, yg, ym, gh, gg, gm, w_p, gpost, gffn)


HALO = 16


def _ffn_kernel(h_ref, hp_ref, hn_ref, x_ref, wup_ref, cw_ref, cb_ref, wd_ref, gpost_ref,
                o_ref, act_ref):
    i = pl.program_id(1)
    tm = h_ref.shape[1]
    prev = jnp.where(i > 0, hp_ref[0], jnp.zeros_like(hp_ref[0]))
    nxt = jnp.where(i < pl.num_programs(1) - 1, hn_ref[0], jnp.zeros_like(hn_ref[0]))
    he = jnp.concatenate([prev, h_ref[0], nxt], axis=0)
    ext = tm + 2 * HALO
    tf = TF_FFN

    def conv(c0):
        up = jnp.dot(he, wup_ref[:, c0:c0 + tf], preferred_element_type=F32)
        um = pltpu.roll(up, 1, 0)[HALO:HALO + tm]
        upl = pltpu.roll(up, ext - 1, 0)[HALO:HALO + tm]
        return (um * cw_ref[0:1, c0:c0 + tf] + up[HALO:HALO + tm] * cw_ref[1:2, c0:c0 + tf]
                + upl * cw_ref[2:3, c0:c0 + tf] + cb_ref[:, c0:c0 + tf])

    for j in range(D_FF // tf):
        g = conv(j * tf)
        u = conv(D_FF + j * tf)
        gelu = 0.5 * g * (1.0 + jnp.tanh(math.sqrt(2.0 / math.pi) * (g + 0.044715 * (g * g * g))))
        act_ref[:, j * tf:(j + 1) * tf] = (gelu * u).astype(BF16)
    f = jnp.dot(act_ref[...], wd_ref[...], preferred_element_type=F32)
    o_ref[0] = x_ref[0] + _rms(f, gpost_ref[...])


def _ffn_call(h, x, w_up, cw, cb, w_down, gpost):
    B, L, D = x.shape
    tm = TM_FFN
    nb = tm // HALO
    last = L // HALO - 1

    def resident(a):
        return pl.BlockSpec(a.shape, lambda b, i: (0,) * a.ndim, pipeline_mode=pl.Buffered(1))

    cb = cb[None]
    return pl.pallas_call(
        _ffn_kernel, grid=(B, L // tm),
        in_specs=[pl.BlockSpec((1, tm, D), lambda b, i: (b, i, 0)),
                  pl.BlockSpec((1, HALO, D), lambda b, i: (b, jnp.maximum(i * nb - 1, 0), 0)),
                  pl.BlockSpec((1, HALO, D), lambda b, i: (b, jnp.minimum((i + 1) * nb, last), 0)),
                  pl.BlockSpec((1, tm, D), lambda b, i: (b, i, 0)),
                  resident(w_up), resident(cw), resident(cb), resident(w_down), resident(gpost)],
        out_specs=pl.BlockSpec((1, tm, D), lambda b, i: (b, i, 0)),
        out_shape=jax.ShapeDtypeStruct((B, L, D), F32),
        scratch_shapes=[pltpu.VMEM((tm, D_FF), BF16)],
        compiler_params=_cparams(("parallel", "parallel")), name="conv_ffn",
    )(h, h, h, x, w_up, cw, cb, w_down, gpost)


def _axial_tables(L, rot_dim):
    rows = L // GRID_W
    row_idx = jnp.broadcast_to(jnp.arange(rows)[:, None], (rows, GRID_W)).reshape(L)
    col_idx = jnp.broadcast_to(jnp.arange(GRID_W)[None, :], (rows, GRID_W)).reshape(L)
    n_axis = rot_dim // 4
    inv = ROPE_THETA ** (-jnp.arange(n_axis, dtype=F32) / n_axis)
    ang = jnp.concatenate([row_idx[:, None].astype(F32) * inv,
                           col_idx[:, None].astype(F32) * inv], axis=-1)
    return jnp.cos(ang), jnp.sin(ang)


def _rope_tables(L):
    def lanes(parts):
        used = sum(p.shape[1] for p in parts)
        return jnp.concatenate(parts + [jnp.zeros((L, LANES - used), F32)], axis=1)

    cg, sg = _axial_tables(L, GQA_HEAD_DIM)
    tg = (lanes([cg, cg]), lanes([sg, sg]))
    cm, sm = _axial_tables(L, MLA_ROPE_DIM)
    nope0 = jnp.zeros((L, MLA_NOPE_DIM), F32)
    nope1 = jnp.ones((L, MLA_NOPE_DIM), F32)
    tmk = (lanes([nope0, cm, cm]), lanes([nope0, sm, sm]))
    sc = (MLA_NOPE_DIM + MLA_ROPE_DIM) ** -0.5 * LOG2E
    tmq = (lanes([nope1, cm, cm]) * sc, tmk[1] * sc)
    return tg, tmq, tmk


def _partner(w, half, sign=-1.0):
    return jnp.concatenate([sign * w[..., half:], w[..., :half]], axis=-1)


def _pad_heads(w, n_heads, width):
    k = w.shape[0]
    return jnp.pad(w.reshape(k, n_heads, width), ((0, 0), (0, 0), (0, LANES - width))).reshape(
        k, n_heads * LANES)


def _pad_vec(g, n_heads, width):
    return jnp.pad(g.reshape(n_heads, width), ((0, 0), (0, LANES - width))).reshape(1, n_heads * LANES)


def kernel(x, mix_pre_norm, w_in, hy_conv_w, hy_conv_b, hy_filt_w1, hy_filt_b1, hy_filt_freq1,
           hy_filt_w2, hy_filt_b2, hy_filt_freq2, hy_filt_w3, hy_skip, gqa_q_norm, gqa_k_norm,
           mla_q_a_norm, mla_w_uq, mla_kv_a_norm, mla_w_ukv, hy_out_norm, gqa_out_norm,
           mla_out_norm, w_out, mix_post_norm, ffn_pre_norm, w_up, ffn_conv_w, ffn_conv_b,
           w_down, ffn_post_norm):
    B, L, D = x.shape
    assert B == 2 and 2 * L == FFT_N1 * FFT_N2 and D == D_MODEL
    depth = w_in.shape[0]
    tg, tmq, tmk = _rope_tables(L)
    z_perm, win_perm = _hyena_positions(L)
    mats = _dft_matrices()

    for l in range(depth):
        wl = w_in[l]
        hd, hh, rh = GQA_HEAD_DIM, GQA_HEAD_DIM // 2, MLA_ROPE_DIM // 2
        o1 = 768
        o2 = o1 + GQA_HEADS * hd
        o3 = o2 + GQA_KV_HEADS * hd
        o4 = o3 + GQA_KV_HEADS * hd
        o5 = o4 + MLA_Q_RANK
        o6 = o5 + MLA_KV_RANK
        wq = wl[:, o1:o2].reshape(D, GQA_HEADS, hd)
        wk = wl[:, o2:o3].reshape(D, GQA_KV_HEADS, hd)
        wkr = wl[:, o6:]
        pe_pad = ((0, 0), (MLA_NOPE_DIM, LANES - MLA_NOPE_DIM - MLA_ROPE_DIM))
        win_p = jnp.concatenate(
            [wl[:, :o1],
             _pad_heads(wq.reshape(D, -1), GQA_HEADS, hd),
             _pad_heads(_partner(wq, hh).reshape(D, -1), GQA_HEADS, hd),
             _pad_heads(wk.reshape(D, -1), GQA_KV_HEADS, hd),
             _pad_heads(_partner(wk, hh).reshape(D, -1), GQA_KV_HEADS, hd),
             wl[:, o3:o6], jnp.pad(wkr, pe_pad), jnp.pad(_partner(wkr, rh), pe_pad)],
            axis=1).astype(BF16)
        gq_gain = gqa_q_norm[l] * (hd ** -0.5 * LOG2E)
        gq = (_pad_vec(gq_gain, 1, hd), _pad_vec(_partner(gq_gain, hh, 1.0), 1, hd))
        gk = (_pad_vec(gqa_k_norm[l], 1, hd), _pad_vec(_partner(gqa_k_norm[l], hh, 1.0), 1, hd))
        wuq = mla_w_uq[l].reshape(MLA_Q_RANK, MLA_HEADS, MLA_NOPE_DIM + MLA_ROPE_DIM)
        wuq_pe = jnp.pad(_partner(wuq[:, :, MLA_NOPE_DIM:], rh),
                         ((0, 0), (0, 0), (MLA_NOPE_DIM, 0)))
        wuq_p = jnp.concatenate(
            [_pad_heads(mla_w_uq[l], MLA_HEADS, MLA_NOPE_DIM + MLA_ROPE_DIM),
             _pad_heads(wuq_pe.reshape(MLA_Q_RANK, -1), MLA_HEADS, MLA_NOPE_DIM + MLA_ROPE_DIM)],
            axis=1).astype(BF16)
        wukv = mla_w_ukv[l].reshape(MLA_KV_RANK, MLA_HEADS, MLA_NOPE_DIM + MLA_V_DIM)
        wukvk_p = _pad_heads(wukv[:, :, :MLA_NOPE_DIM].reshape(MLA_KV_RANK, -1), MLA_HEADS,
                             MLA_NOPE_DIM).astype(BF16)
        wukvv = wukv[:, :, MLA_NOPE_DIM:].reshape(MLA_KV_RANK, MLA_HEADS * MLA_V_DIM).astype(BF16)

        hy_in, qg, kg, vg, qm, km, vm = _kin_call(
            x, mix_pre_norm[l][None], win_p, gq, gk, tg, mla_q_a_norm[l][None], wuq_p, tmq,
            mla_kv_a_norm[l][None], wukvk_p, wukvv, tmk)

        fw = _filter_weights(hy_filt_w1[l], hy_filt_b1[l], hy_filt_freq1[l], hy_filt_w2[l],
                             hy_filt_b2[l], hy_filt_freq2[l], hy_filt_w3[l])
        y_hy = _hyena_layer(hy_in, hy_conv_w[l], hy_conv_b[l], fw, hy_skip[l], z_perm, win_perm, mats)

        y_gqa = _attn_call(qg.transpose(0, 2, 1), kg, vg.reshape(B, GQA_KV_HEADS, VT_ROWS, L),
                           n_kv=1, n_rep=GQA_HEADS // GQA_KV_HEADS, tq=TQ_GQA, name="attn_gqa")
        y_mla = _attn_call(qm.transpose(0, 2, 1), km, vm.reshape(B, MLA_HEADS, VT_ROWS, L),
                           n_kv=2, n_rep=1, tq=TQ_MLA, name="attn_mla")

        x, h2 = _kout_call(x, y_hy, y_gqa, y_mla, hy_out_norm[l][None], gqa_out_norm[l][None],
                           mla_out_norm[l][None], w_out[l].astype(BF16),
                           mix_post_norm[l][None], ffn_pre_norm[l][None])
        x = _ffn_call(h2, x, w_up[l].astype(BF16), ffn_conv_w[l], ffn_conv_b[l],
                      w_down[l].astype(BF16), ffn_post_norm[l][None])
    return x
```

```python
import functools
import math

import numpy as np
import jax
import jax.numpy as jnp
from jax import lax
from jax.experimental import pallas as pl
from jax.experimental.pallas import tpu as pltpu

F32 = jnp.float32
BF16 = jnp.bfloat16

NORM_EPS = 1e-6
ROPE_THETA = 10000.0
GRID_W = 64
LOG2E = math.log2(math.e)

D_MODEL = 1024
HY_D = 256
HY_EMB = 33
HY_BANDS = 16
HY_DECAY_TARGET = 1e-2
HY_FAST_DECAY_PCT = 0.3
HY_SLOW_DECAY_PCT = 1.5
GQA_HEADS = 8
GQA_KV_HEADS = 2
GQA_HEAD_DIM = 64
MLA_HEADS = 4
MLA_Q_RANK = 256
MLA_KV_RANK = 128
MLA_NOPE_DIM = 64
MLA_ROPE_DIM = 32
MLA_V_DIM = 64
D_FF = 2816

LANES = 128
FFT_N1 = 128
FFT_N2 = 128

TM_IN = 512
TK_ATTN = 512
TQ_GQA = 128
TQ_MLA = 256
TM_OUT = 512
TM_FFN = 512
TF_FFN = 256
TL_HCONV = 1024
FILT_HALF = 512
FILT_FEAT = 64
VMEM_LIMIT = 56 * 1024 * 1024


def _cparams(sem):
    return pltpu.CompilerParams(dimension_semantics=sem, vmem_limit_bytes=VMEM_LIMIT)


def _rms(x, g):
    return x * lax.rsqrt(jnp.mean(x * x, axis=-1, keepdims=True) + NORM_EPS) * g


def _kin_kernel(x_ref, gpre_ref, win_ref, gqn_ref, gqs_ref, gkn_ref, gks_ref, cg_ref, sg_ref,
                mqn_ref, wuq_ref, cmq_ref, smq_ref,
                mkvn_ref, wukvk_ref, wukvv_ref, cmk_ref, smk_ref,
                hy_ref, qg_ref, kg_ref, vg_ref, qm_ref, km_ref, vm_ref):
    x = x_ref[0]
    h = _rms(x, gpre_ref[...]).astype(BF16)
    cur = [0]

    def proj(n):
        lo = cur[0]
        cur[0] = lo + n
        return jnp.dot(h, win_ref[:, lo:lo + n], preferred_element_type=F32)

    hy_ref[0] = proj(768)

    cg, sg = cg_ref[...], sg_ref[...]

    def heads(n, gain, gain_sw, store):
        xa, xb = proj(n * LANES), proj(n * LANES)
        ca, sa = cg * gain, sg * gain_sw
        for j in range(n):
            xc, xs = xa[:, j * LANES:(j + 1) * LANES], xb[:, j * LANES:(j + 1) * LANES]
            ms = jnp.sum(xc * xc, axis=-1, keepdims=True) * (1.0 / GQA_HEAD_DIM)
            store(j, ((xc * ca + xs * sa) * lax.rsqrt(ms + NORM_EPS)).astype(BF16))

    def store_q(j, v):
        qg_ref[0, :, j * LANES:(j + 1) * LANES] = v

    def store_k(j, v):
        kg_ref[0, j] = v

    heads(GQA_HEADS, gqn_ref[...], gqs_ref[...], store_q)
    heads(GQA_KV_HEADS, gkn_ref[...], gks_ref[...], store_k)
    vg_ref[0] = proj(LANES).T.astype(BF16)

    cq = _rms(proj(MLA_Q_RANK), mqn_ref[...]).astype(BF16)
    qm = jnp.dot(cq, wuq_ref[...], preferred_element_type=F32)
    cmq, smq = cmq_ref[...], smq_ref[...]
    nq = MLA_HEADS * LANES
    for j in range(MLA_HEADS):
        qm_ref[0, :, j * LANES:(j + 1) * LANES] = (
            qm[:, j * LANES:(j + 1) * LANES] * cmq
            + qm[:, nq + j * LANES:nq + (j + 1) * LANES] * smq).astype(BF16)

    ckv = _rms(proj(MLA_KV_RANK), mkvn_ref[...]).astype(BF16)
    kpe = proj(LANES) * cmk_ref[...] + proj(LANES) * smk_ref[...]
    kn = jnp.dot(ckv, wukvk_ref[...], preferred_element_type=F32)
    for j in range(MLA_HEADS):
        km_ref[0, j] = (kn[:, j * LANES:(j + 1) * LANES] + kpe).astype(BF16)
    vm = jnp.dot(ckv, wukvv_ref[...], preferred_element_type=F32)
    for j in range(vm.shape[1] // LANES):
        vm_ref[0, j * LANES:(j + 1) * LANES, :] = vm[:, j * LANES:(j + 1) * LANES].T.astype(BF16)


def _kin_call(x, gpre, win_p, gq, gk, tg, mqn, wuq_p, tmq, mkvn, wukvk_p, wukvv, tmk):
    B, L, D = x.shape
    tm = TM_IN
    nt = L // tm

    def full(a):
        return pl.BlockSpec(a.shape, lambda b, i: (0,) * a.ndim, pipeline_mode=pl.Buffered(1))

    def rows(w):
        return pl.BlockSpec((tm, w), lambda b, i: (i, 0))

    in_specs = [pl.BlockSpec((1, tm, D), lambda b, i: (b, i, 0)), full(gpre), full(win_p),
                full(gq[0]), full(gq[1]), full(gk[0]), full(gk[1]), rows(LANES), rows(LANES),
                full(mqn), full(wuq_p), rows(LANES), rows(LANES),
                full(mkvn), full(wukvk_p), full(wukvv), rows(LANES), rows(LANES)]
    gv_rows = GQA_KV_HEADS * GQA_HEAD_DIM
    mv_rows = MLA_HEADS * MLA_V_DIM
    out_shape = [
        jax.ShapeDtypeStruct((B, L, 768), F32),
        jax.ShapeDtypeStruct((B, L, GQA_HEADS * LANES), BF16),
        jax.ShapeDtypeStruct((B, GQA_KV_HEADS, L, LANES), BF16),
        jax.ShapeDtypeStruct((B, gv_rows, L), BF16),
        jax.ShapeDtypeStruct((B, L, MLA_HEADS * LANES), BF16),
        jax.ShapeDtypeStruct((B, MLA_HEADS, L, LANES), BF16),
        jax.ShapeDtypeStruct((B, mv_rows, L), BF16),
    ]
    out_specs = [
        pl.BlockSpec((1, tm, 768), lambda b, i: (b, i, 0)),
        pl.BlockSpec((1, tm, GQA_HEADS * LANES), lambda b, i: (b, i, 0)),
        pl.BlockSpec((1, GQA_KV_HEADS, tm, LANES), lambda b, i: (b, 0, i, 0)),
        pl.BlockSpec((1, gv_rows, tm), lambda b, i: (b, 0, i)),
        pl.BlockSpec((1, tm, MLA_HEADS * LANES), lambda b, i: (b, i, 0)),
        pl.BlockSpec((1, MLA_HEADS, tm, LANES), lambda b, i: (b, 0, i, 0)),
        pl.BlockSpec((1, mv_rows, tm), lambda b, i: (b, 0, i)),
    ]
    return pl.pallas_call(
        _kin_kernel, grid=(B, nt), in_specs=in_specs, out_specs=out_specs, out_shape=out_shape,
        compiler_params=_cparams(("parallel", "parallel")), name="in_proj",
    )(x, gpre, win_p, *gq, *gk, *tg, mqn, wuq_p, *tmq, mkvn, wukvk_p, wukvv, *tmk)


def _attn_kernel(q_ref, k_ref, vt_ref, o_ref, sa_ref, sb_ref, ma_ref, mb_ref, *,
                 n_kv, n_rep, tq, n_chunks, tk):
    i = pl.program_id(0)
    cols = n_rep * tq
    width = n_kv * cols
    grp = tk // 8

    @pl.when(i == 0)
    def _():
        sb_ref[...] = jnp.zeros(sb_ref.shape, F32)
        mb_ref[...] = jnp.zeros(mb_ref.shape, F32)

    def step(sw_ref, mw_ref, sr_ref, mr_ref):
        qs = [jnp.concatenate([q_ref[0, (a * n_rep + j) * LANES:(a * n_rep + j + 1) * LANES, :]
                               for j in range(n_rep)], axis=1) for a in range(n_kv)]
        mx = jnp.max(mr_ref[...], axis=0, keepdims=True)
        m = jnp.full((8, width), -jnp.inf, F32)
        l = jnp.zeros((8, width), F32)
        accs = [jnp.zeros((vt_ref.shape[2], cols), F32) for _ in range(n_kv)]
        for c in range(n_chunks):
            st = jnp.concatenate(
                [jnp.dot(k_ref[0, a, c * tk:(c + 1) * tk, :], qs[a], preferred_element_type=F32)
                 for a in range(n_kv)], axis=1)
            sw_ref[c] = st
            m = jnp.maximum(m, jnp.max(st.reshape(grp, 8, width), axis=0))
            p = jnp.exp2(sr_ref[c] - mx)
            l = l + jnp.sum(p.reshape(grp, 8, width), axis=0)
            pb = p.astype(BF16)
            for a in range(n_kv):
                accs[a] = accs[a] + jnp.dot(vt_ref[0, a, :, c * tk:(c + 1) * tk],
                                            pb[:, a * cols:(a + 1) * cols],
                                            preferred_element_type=F32)
        mw_ref[...] = m
        ls = jnp.sum(l, axis=0, keepdims=True)
        heads = []
        for a in range(n_kv):
            oa = accs[a] / ls[:, a * cols:(a + 1) * cols]
            heads += [oa[:, j * tq:(j + 1) * tq] for j in range(n_rep)]
        o_ref[0] = jnp.concatenate(heads, axis=0).T.astype(BF16)

    @pl.when(i % 2 == 0)
    def _():
        step(sa_ref, ma_ref, sb_ref, mb_ref)

    @pl.when(i % 2 == 1)
    def _():
        step(sb_ref, mb_ref, sa_ref, ma_ref)


def _attn_call(qt, k, vt, *, n_kv, n_rep, tq, name):
    B, hq, L = qt.shape
    H = hq // LANES
    hkv, dv = vt.shape[1], vt.shape[2]
    G = hkv // n_kv
    hs = n_kv * n_rep
    tk = TK_ATTN
    n_chunks = L // tk
    nq = L // tq
    width = hs * tq
    kern = functools.partial(_attn_kernel, n_kv=n_kv, n_rep=n_rep, tq=tq, n_chunks=n_chunks, tk=tk)
    total = B * G * nq

    def blk(s):
        return s // (G * nq), (s // nq) % G, s % nq

    def q_map(s):
        b, g, i = blk(jnp.minimum(s, total - 1))
        return b, g, i

    def k_map(s):
        b, g, _ = blk(jnp.minimum(s, total - 1))
        return b, g, 0, 0

    def v_map(s):
        b, g, _ = blk(jnp.maximum(s - 1, 0))
        return b, g, 0, 0

    def o_map(s):
        b, g, i = blk(jnp.maximum(s - 1, 0))
        return b, i, g

    return pl.pallas_call(
        kern, grid=(total + 1,),
        in_specs=[pl.BlockSpec((1, hs * LANES, tq), q_map),
                  pl.BlockSpec((1, n_kv, L, LANES), k_map, pipeline_mode=pl.Buffered(1)),
                  pl.BlockSpec((1, n_kv, dv, L), v_map, pipeline_mode=pl.Buffered(1))],
        out_specs=pl.BlockSpec((1, tq, hs * dv), o_map),
        out_shape=jax.ShapeDtypeStruct((B, L, H * dv), BF16),
        scratch_shapes=[pltpu.VMEM((n_chunks, tk, width), F32), pltpu.VMEM((n_chunks, tk, width), F32),
                        pltpu.VMEM((8, width), F32), pltpu.VMEM((8, width), F32)],
        compiler_params=_cparams(("arbitrary",)), name=name,
    )(qt, k, vt)


def _hconv_kernel(x_ref, xp_ref, xn_ref, w_ref, b_ref, v_ref, x1_ref, x2_ref):
    i = pl.program_id(1)
    x = x_ref[0]
    tl = x.shape[0]
    prev = jnp.where(i > 0, xp_ref[0][7:8, :], 0.0)
    nxt = jnp.where(i < pl.num_programs(1) - 1, xn_ref[0][0:1, :], 0.0)
    r = lax.broadcasted_iota(jnp.int32, x.shape, 0)
    xm = jnp.where(r == 0, prev, pltpu.roll(x, 1, 0))
    xp = jnp.where(r == tl - 1, nxt, pltpu.roll(x, tl - 1, 0))
    uc = xm * w_ref[0:1, :] + x * w_ref[1:2, :] + xp * w_ref[2:3, :] + b_ref[...]
    for r in range(tl // FFT_N2):
        blk = uc[r * FFT_N2:(r + 1) * FFT_N2]
        v_ref[:, r, :] = blk[:, :HY_D]
        x1_ref[:, r, :] = blk[:, HY_D:2 * HY_D]
        x2_ref[:, r, :] = blk[:, 2 * HY_D:]


def _hconv_call(hy_in, w, b):
    B, L, C = hy_in.shape
    tl = TL_HCONV
    nb = tl // 8
    last = L // 8 - 1
    nt = L // tl
    rows = tl // FFT_N2
    out = jax.ShapeDtypeStruct((FFT_N2, B * L // FFT_N2, HY_D), F32)
    ospec = pl.BlockSpec((FFT_N2, rows, HY_D), lambda b_, i: (0, b_ * nt + i, 0))
    return pl.pallas_call(
        _hconv_kernel, grid=(B, L // tl),
        in_specs=[pl.BlockSpec((1, tl, C), lambda b_, i: (b_, i, 0)),
                  pl.BlockSpec((1, 8, C), lambda b_, i: (b_, jnp.maximum(i * nb - 1, 0), 0)),
                  pl.BlockSpec((1, 8, C), lambda b_, i: (b_, jnp.minimum((i + 1) * nb, last), 0)),
                  pl.BlockSpec((3, C), lambda b_, i: (0, 0)),
                  pl.BlockSpec((1, C), lambda b_, i: (0, 0))],
        out_specs=[ospec, ospec, ospec], out_shape=[out, out, out],
        compiler_params=_cparams(("parallel", "parallel")), name="hy_conv3",
    )(hy_in, hy_in, hy_in, w, b)


def _split_bf16(a):
    hi = a.astype(BF16)
    return hi, (a - hi.astype(F32)).astype(BF16)


def _dot3(a, w_hi, w_lo):
    a_hi, a_lo = _split_bf16(a)
    dot = functools.partial(jnp.dot, preferred_element_type=F32)
    return dot(a_hi, w_hi) + dot(a_lo, w_hi) + dot(a_hi, w_lo)


def _filt_kernel(z_ref, w1h_ref, w1l_ref, b1_ref, f1_ref, w2h_ref, w2l_ref, b2_ref, f2_ref,
                 w3h_ref, w3l_ref, win_ref, o_ref):
    h = jnp.sin(f1_ref[...] * (_dot3(z_ref[...], w1h_ref[...], w1l_ref[...]) + b1_ref[...]))
    h = jnp.sin(f2_ref[...] * (_dot3(h, w2h_ref[...], w2l_ref[...]) + b2_ref[...]))
    k = _dot3(h, w3h_ref[...], w3l_ref[...])
    half = k.shape[0]
    wcols = 4 * HY_D
    n1 = lax.broadcasted_iota(jnp.int32, (half, 2 * HY_D), 0) % FFT_N1
    for s in range(2):
        ks = k[:, s * wcols:(s + 1) * wcols]
        kk = jnp.where(n1 >= FFT_N1 // 2, ks[:, 2 * HY_D:], ks[:, :2 * HY_D])
        win = win_ref[s * half:(s + 1) * half, :]
        o_ref[s * half:(s + 1) * half, :] = kk * jnp.concatenate([win, win], axis=-1)


def _filt_call(z_pack, fw, win_perm):
    tp = 2 * FILT_HALF
    n = win_perm.shape[0]

    def full(a):
        return pl.BlockSpec(a.shape, lambda i: (0,) * a.ndim)

    return pl.pallas_call(
        _filt_kernel, grid=(n // tp,),
        in_specs=[pl.BlockSpec((FILT_HALF, z_pack.shape[1]), lambda i: (i, 0))]
        + [full(a) for a in fw] + [pl.BlockSpec((tp, HY_D), lambda i: (i, 0))],
        out_specs=pl.BlockSpec((tp, 2 * HY_D), lambda i: (i, 0)),
        out_shape=jax.ShapeDtypeStruct((n, 2 * HY_D), F32),
        compiler_params=_cparams(("parallel",)), name="hy_filter",
    )(z_pack, *fw, win_perm)


def _bm_kernel(g_ref, x_ref, o_ref, *, to):
    for t in range(to):
        o_ref[t] = jnp.dot(g_ref[t], x_ref[t].astype(BF16), preferred_element_type=F32)


def _bm_call(g, x, name, to=8):
    O, K, N = x.shape
    M = g.shape[1]
    return pl.pallas_call(
        functools.partial(_bm_kernel, to=to), grid=(O // to,),
        in_specs=[pl.BlockSpec((to, M, K), lambda i: (i, 0, 0)),
                  pl.BlockSpec((to, K, N), lambda i: (i, 0, 0))],
        out_specs=pl.BlockSpec((to, M, N), lambda i: (i, 0, 0)),
        out_shape=jax.ShapeDtypeStruct((O, M, N), F32),
        compiler_params=_cparams(("parallel",)), name=name,
    )(g, x)


def _gather_ri(x_ref, j):
    return jnp.concatenate([x_ref[:, 0, j, :], x_ref[:, 1, j, :]], axis=0).astype(BF16)


def _gather_spec(a, to):
    return pl.BlockSpec((a.shape[0], 2, to, a.shape[3]), lambda i: (0, 0, i, 0))


def _filtb_kernel(mb_ref, x_ref, o_ref, *, to):
    for j in range(to):
        o_ref[j] = jnp.dot(mb_ref[...], _gather_ri(x_ref, j),
                           preferred_element_type=F32).astype(o_ref.dtype)


def _filtb_call(mb, ka, to=8):
    O, _, P, N = ka.shape
    return pl.pallas_call(
        functools.partial(_filtb_kernel, to=to), grid=(P // to,),
        in_specs=[pl.BlockSpec(mb.shape, lambda i: (0, 0)), _gather_spec(ka, to)],
        out_specs=pl.BlockSpec((to, 2 * O, N), lambda i: (i, 0, 0)),
        out_shape=jax.ShapeDtypeStruct((P, 2 * O, N), BF16),
        compiler_params=_cparams(("parallel",)), name="hy_fft_filt_b",
    )(mb, ka)


def _convb_kernel(mb_ref, gc_ref, x_ref, kf_ref, o_ref, *, to):
    h = FFT_N2
    for j in range(to):
        xs = jnp.dot(mb_ref[...], _gather_ri(x_ref, j), preferred_element_type=F32)
        xr, xi = xs[:h], xs[h:]
        kr, ki = kf_ref[j, :h, :].astype(F32), kf_ref[j, h:, :].astype(F32)
        ys = jnp.concatenate([xr * kr - xi * ki, xr * ki + xi * kr], axis=0).astype(BF16)
        o_ref[j] = jnp.dot(gc_ref[j], ys, preferred_element_type=F32)


def _convb_call(mb, gc, a, kf, order, to=8):
    O, _, P, N = a.shape
    return pl.pallas_call(
        functools.partial(_convb_kernel, to=to), grid=(P // to,),
        in_specs=[pl.BlockSpec(mb.shape, lambda i: (0, 0)),
                  pl.BlockSpec((to, 2 * O, 2 * O), lambda i: (i, 0, 0)),
                  _gather_spec(a, to),
                  pl.BlockSpec((to, 2 * O, N), lambda i: (i, 0, order))],
        out_specs=pl.BlockSpec((to, 2 * O, N), lambda i: (i, 0, 0)),
        out_shape=jax.ShapeDtypeStruct((P, 2 * O, N), F32),
        compiler_params=_cparams(("parallel",)), name="hy_spec_mul",
    )(mb, gc, a, kf)


def _convd_kernel(md_ref, c_ref, g_ref, u_ref, s_ref, *rest, to, chain):
    if chain:
        ga_ref, z_ref, a_ref = rest
    else:
        (o_ref,) = rest
    for j in range(to):
        y = jnp.dot(md_ref[...], _gather_ri(c_ref, j), preferred_element_type=F32)
        z = g_ref[j] * (y + u_ref[j] * s_ref[...])
        if chain:
            z_ref[j] = z
            a_ref[j] = jnp.dot(ga_ref[j], z.astype(BF16), preferred_element_type=F32)
        else:
            o_ref[:, j, :] = z


def _convd_call(md, c, gate, u, skip, ga=None, to=8):
    O, _, P, N = c.shape
    R = md.shape[0]
    tspec = pl.BlockSpec((to, R, N), lambda i: (i, 0, 0))
    in_specs = [pl.BlockSpec(md.shape, lambda i: (0, 0)), _gather_spec(c, to), tspec, tspec,
                pl.BlockSpec((1, N), lambda i: (0, 0))]
    args = [md, c, gate, u, skip.reshape(1, N)]
    if ga is not None:
        M = ga.shape[1]
        in_specs.append(pl.BlockSpec((to, M, R), lambda i: (i, 0, 0)))
        args.append(ga)
        out_specs = [tspec, pl.BlockSpec((to, M, N), lambda i: (i, 0, 0))]
        out_shape = [jax.ShapeDtypeStruct((P, R, N), F32), jax.ShapeDtypeStruct((P, M, N), F32)]
    else:
        out_specs = pl.BlockSpec((R, to, N), lambda i: (0, i, 0))
        out_shape = jax.ShapeDtypeStruct((R, P, N), F32)
    return pl.pallas_call(
        functools.partial(_convd_kernel, to=to, chain=ga is not None), grid=(P // to,),
        in_specs=in_specs, out_specs=out_specs, out_shape=out_shape,
        compiler_params=_cparams(("parallel",)), name="hy_fft_d",
    )(*args)


def _dft_tables():
    n = FFT_N1 * FFT_N2
    k = np.arange(FFT_N1)
    f = np.exp(-2j * np.pi * np.outer(k, k) / FFT_N1)
    t = np.exp(-2j * np.pi * np.outer(k, k) / n)
    return f, t, n


def _dft_matrices():
    f, t, n = _dft_tables()
    fr, fi = jnp.asarray(f.real, F32), jnp.asarray(f.imag, F32)
    tr, ti = jnp.asarray(t.real, F32), jnp.asarray(t.imag, F32)
    half = FFT_N1 // 2
    er = fr[None] * tr[:, :, None] - fi[None] * ti[:, :, None]
    ei = fr[None] * ti[:, :, None] + fi[None] * tr[:, :, None]
    ga = jnp.concatenate([jnp.concatenate([er[:, :, :half], -ei[:, :, :half]], axis=2),
                          jnp.concatenate([ei[:, :, :half], er[:, :, :half]], axis=2)], axis=1)
    gaf = jnp.concatenate([er, ei], axis=1)
    mb = jnp.concatenate([jnp.concatenate([fr, -fi], axis=1),
                          jnp.concatenate([fi, fr], axis=1)], axis=0)
    tct = jnp.transpose(tr)[:, :, None]
    tst = -jnp.transpose(ti)[:, :, None]
    gr = tct * fr[None] - tst * (-fi[None])
    gi = tct * (-fi[None]) + tst * fr[None]
    gc = jnp.concatenate([jnp.concatenate([gr, -gi], axis=2),
                          jnp.concatenate([gi, gr], axis=2)], axis=1)
    hr, hi = fr[:half] / n, -fi[:half] / n
    md = jnp.concatenate([jnp.concatenate([hr, -hi], axis=1),
                          jnp.concatenate([hi, hr], axis=1)], axis=0)
    return (ga.astype(BF16), gaf.astype(BF16), mb.astype(BF16), gc.astype(BF16), md.astype(BF16))


def _hyena_positions(L):
    p = FFT_N2 * np.arange(FFT_N1)[None, :] + np.arange(FFT_N2)[:, None]
    pos = np.where(p < L, p, 2 * L - 1 - p).reshape(2 * L, 1).astype(np.float64)
    t = pos / (L - 1)
    w = 2.0 * math.pi * pos / L
    f = np.linspace(1e-4, HY_BANDS - 1, HY_BANDS)[None, :]
    z = np.concatenate([t, np.cos(f * w), -np.sin(f * w),
                        np.zeros((2 * L, FILT_FEAT - HY_EMB))], axis=-1)
    z_pack = (z.reshape(-1, 2, FILT_HALF, FILT_FEAT).transpose(0, 2, 1, 3)
              .reshape(-1, 2 * FILT_FEAT))
    max_decay = math.log(HY_DECAY_TARGET) / HY_FAST_DECAY_PCT
    min_decay = math.log(HY_DECAY_TARGET) / HY_SLOW_DECAY_PCT
    deltas = jnp.linspace(min_decay, max_decay, HY_D, dtype=F32)
    window = jnp.exp(-jnp.asarray(t, F32) * jnp.abs(deltas)[None, :])
    return jnp.asarray(z_pack, F32), window


def _filter_weights(w1, b1, f1, w2, b2, f2, w3):
    def bd(w):
        z = jnp.zeros_like(w)
        return jnp.concatenate([jnp.concatenate([w, z], axis=1),
                                jnp.concatenate([z, w], axis=1)], axis=0)

    def twice(v):
        return jnp.concatenate([v, v])[None]

    w1 = jnp.pad(w1, ((0, FILT_FEAT - w1.shape[0]), (0, 0)))
    w3 = w3.reshape(-1, 2, 2, HY_D).transpose(0, 2, 1, 3).reshape(-1, 4 * HY_D)
    return (*_split_bf16(bd(w1)), twice(b1), twice(f1), *_split_bf16(bd(w2)), twice(b2), twice(f2),
            *_split_bf16(bd(w3)))


def _hyena_layer(hy_in, conv_w, conv_b, fw, skip, z_perm, win_perm, mats):
    B, L, _ = hy_in.shape
    ga, gaf, mb, gc, md = mats
    v, x1, x2 = _hconv_call(hy_in, conv_w, conv_b[None])
    kperm = _filt_call(z_perm, fw, win_perm)
    ka = _bm_call(gaf, kperm.reshape(FFT_N2, FFT_N1, 2 * HY_D), "hy_fft_filt_a")
    kf = _filtb_call(mb, ka.reshape(FFT_N2, 2, FFT_N1, 2 * HY_D))

    a = _bm_call(ga, v, "hy_fft_a")
    c = _convb_call(mb, gc, a.reshape(FFT_N2, 2, FFT_N1, HY_D), kf, 0)
    z, a = _convd_call(md, c.reshape(FFT_N1, 2, FFT_N2, HY_D), x1, v, skip[0], ga=ga)
    c = _convb_call(mb, gc, a.reshape(FFT_N2, 2, FFT_N1, HY_D), kf, 1)
    z = _convd_call(md, c.reshape(FFT_N1, 2, FFT_N2, HY_D), x2, z, skip[1])
    return z.reshape(B, L, HY_D)


def _kout_kernel(x_ref, yh_ref, yg_ref, ym_ref, gh_ref, gg_ref, gm_ref, w_ref, gpost_ref, gffn_ref,
                 xo_ref, h_ref):
    a = _rms(yh_ref[0], gh_ref[...]).astype(BF16)
    b = _rms(yg_ref[0].astype(F32), gg_ref[...]).astype(BF16)
    c = _rms(ym_ref[0].astype(F32), gm_ref[...]).astype(BF16)
    o1 = HY_D
    o2 = o1 + GQA_HEADS * GQA_HEAD_DIM
    y = (jnp.dot(a, w_ref[:o1, :], preferred_element_type=F32)
         + jnp.dot(b, w_ref[o1:o2, :], preferred_element_type=F32)
         + jnp.dot(c, w_ref[o2:, :], preferred_element_type=F32))
    xo = x_ref[0] + _rms(y, gpost_ref[...])
    xo_ref[0] = xo
    h_ref[0] = _rms(xo, gffn_ref[...]).astype(BF16)


def _kout_call(x, yh, yg, ym, gh, gg, gm, w_p, gpost, gffn):
    B, L, D = x.shape
    tm = TM_OUT

    def rows(a):
        return pl.BlockSpec((1, tm, a.shape[2]), lambda b, i: (b, i, 0))

    def full(a):
        return pl.BlockSpec(a.shape, lambda b, i: (0,) * a.ndim)

    return pl.pallas_call(
        _kout_kernel, grid=(B, L // tm),
        in_specs=[rows(x), rows(yh), rows(yg), rows(ym), full(gh), full(gg), full(gm), full(w_p),
                  full(gpost), full(gffn)],
        out_specs=[rows(x), rows(x)],
        out_shape=[jax.ShapeDtypeStruct((B, L, D), F32), jax.ShapeDtypeStruct((B, L, D), BF16)],
        compiler_params=_cparams(("parallel", "parallel")), name="out_proj",
    )(x, yh, yg, ym, gh, gg, gm, w_p, gpost, gffn)


HALO = 16


def _ffn_kernel(h_ref, hp_ref, hn_ref, x_ref, wup_ref, cw_ref, cb_ref, wd_ref, gpost_ref,
                o_ref, act_ref):
    i = pl.program_id(1)
    tm = h_ref.shape[1]
    prev = jnp.where(i > 0, hp_ref[0], jnp.zeros_like(hp_ref[0]))
    nxt = jnp.where(i < pl.num_programs(1) - 1, hn_ref[0], jnp.zeros_like(hn_ref[0]))
    he = jnp.concatenate([prev, h_ref[0], nxt], axis=0)
    ext = tm + 2 * HALO
    tf = TF_FFN

    def conv(c0):
        up = jnp.dot(he, wup_ref[:, c0:c0 + tf], preferred_element_type=F32)
        um = pltpu.roll(up, 1, 0)[HALO:HALO + tm]
        upl = pltpu.roll(up, ext - 1, 0)[HALO:HALO + tm]
        return (um * cw_ref[0:1, c0:c0 + tf] + up[HALO:HALO + tm] * cw_ref[1:2, c0:c0 + tf]
                + upl * cw_ref[2:3, c0:c0 + tf] + cb_ref[:, c0:c0 + tf])

    for j in range(D_FF // tf):
        g = conv(j * tf)
        u = conv(D_FF + j * tf)
        gelu = 0.5 * g * (1.0 + jnp.tanh(math.sqrt(2.0 / math.pi) * (g + 0.044715 * (g * g * g))))
        act_ref[:, j * tf:(j + 1) * tf] = (gelu * u).astype(BF16)
    f = jnp.dot(act_ref[...], wd_ref[...], preferred_element_type=F32)
    o_ref[0] = x_ref[0] + _rms(f, gpost_ref[...])


def _ffn_call(h, x, w_up, cw, cb, w_down, gpost):
    B, L, D = x.shape
    tm = TM_FFN
    nb = tm // HALO
    last = L // HALO - 1

    def resident(a):
        return pl.BlockSpec(a.shape, lambda b, i: (0,) * a.ndim, pipeline_mode=pl.Buffered(1))

    cb = cb[None]
    return pl.pallas_call(
        _ffn_kernel, grid=(B, L // tm),
        in_specs=[pl.BlockSpec((1, tm, D), lambda b, i: (b, i, 0)),
                  pl.BlockSpec((1, HALO, D), lambda b, i: (b, jnp.maximum(i * nb - 1, 0), 0)),
                  pl.BlockSpec((1, HALO, D), lambda b, i: (b, jnp.minimum((i + 1) * nb, last), 0)),
                  pl.BlockSpec((1, tm, D), lambda b, i: (b, i, 0)),
                  resident(w_up), resident(cw), resident(cb), resident(w_down), resident(gpost)],
        out_specs=pl.BlockSpec((1, tm, D), lambda b, i: (b, i, 0)),
        out_shape=jax.ShapeDtypeStruct((B, L, D), F32),
        scratch_shapes=[pltpu.VMEM((tm, D_FF), BF16)],
        compiler_params=_cparams(("parallel", "parallel")), name="conv_ffn",
    )(h, h, h, x, w_up, cw, cb, w_down, gpost)


def _axial_tables(L, rot_dim):
    pos = np.arange(L)
    n_axis = rot_dim // 4
    inv = ROPE_THETA ** (-np.arange(n_axis) / n_axis)
    ang = np.concatenate([(pos // GRID_W)[:, None] * inv, (pos % GRID_W)[:, None] * inv], axis=-1)
    return jnp.asarray(np.cos(ang), F32), jnp.asarray(np.sin(ang), F32)


def _rope_tables(L):
    def lanes(parts):
        used = sum(p.shape[1] for p in parts)
        return jnp.concatenate(parts + [jnp.zeros((L, LANES - used), F32)], axis=1)

    cg, sg = _axial_tables(L, GQA_HEAD_DIM)
    tg = (lanes([cg, cg]), lanes([sg, sg]))
    cm, sm = _axial_tables(L, MLA_ROPE_DIM)
    nope0 = jnp.zeros((L, MLA_NOPE_DIM), F32)
    nope1 = jnp.ones((L, MLA_NOPE_DIM), F32)
    tmk = (lanes([nope0, cm, cm]), lanes([nope0, sm, sm]))
    sc = (MLA_NOPE_DIM + MLA_ROPE_DIM) ** -0.5 * LOG2E
    tmq = (lanes([nope1, cm, cm]) * sc, tmk[1] * sc)
    return tg, tmq, tmk


def _partner(w, half, sign=-1.0):
    return jnp.concatenate([sign * w[..., half:], w[..., :half]], axis=-1)


def _pad_heads(w, n_heads, width):
    k = w.shape[0]
    return jnp.pad(w.reshape(k, n_heads, width), ((0, 0), (0, 0), (0, LANES - width))).reshape(
        k, n_heads * LANES)


def _pad_vec(g, n_heads, width):
    return jnp.pad(g.reshape(n_heads, width), ((0, 0), (0, LANES - width))).reshape(1, n_heads * LANES)


def kernel(x, mix_pre_norm, w_in, hy_conv_w, hy_conv_b, hy_filt_w1, hy_filt_b1, hy_filt_freq1,
           hy_filt_w2, hy_filt_b2, hy_filt_freq2, hy_filt_w3, hy_skip, gqa_q_norm, gqa_k_norm,
           mla_q_a_norm, mla_w_uq, mla_kv_a_norm, mla_w_ukv, hy_out_norm, gqa_out_norm,
           mla_out_norm, w_out, mix_post_norm, ffn_pre_norm, w_up, ffn_conv_w, ffn_conv_b,
           w_down, ffn_post_norm):
    B, L, D = x.shape
    assert B == 2 and 2 * L == FFT_N1 * FFT_N2 and D == D_MODEL
    depth = w_in.shape[0]
    tg, tmq, tmk = _rope_tables(L)
    z_perm, win_perm = _hyena_positions(L)
    mats = _dft_matrices()

    for l in range(depth):
        wl = w_in[l]
        hd, hh, rh = GQA_HEAD_DIM, GQA_HEAD_DIM // 2, MLA_ROPE_DIM // 2
        o1 = 768
        o2 = o1 + GQA_HEADS * hd
        o3 = o2 + GQA_KV_HEADS * hd
        o4 = o3 + GQA_KV_HEADS * hd
        o5 = o4 + MLA_Q_RANK
        o6 = o5 + MLA_KV_RANK
        wq = wl[:, o1:o2].reshape(D, GQA_HEADS, hd)
        wk = wl[:, o2:o3].reshape(D, GQA_KV_HEADS, hd)
        wkr = wl[:, o6:]
        pe_pad = ((0, 0), (MLA_NOPE_DIM, LANES - MLA_NOPE_DIM - MLA_ROPE_DIM))
        win_p = jnp.concatenate(
            [wl[:, :o1],
             _pad_heads(wq.reshape(D, -1), GQA_HEADS, hd),
             _pad_heads(_partner(wq, hh).reshape(D, -1), GQA_HEADS, hd),
             _pad_heads(wk.reshape(D, -1), GQA_KV_HEADS, hd),
             _pad_heads(_partner(wk, hh).reshape(D, -1), GQA_KV_HEADS, hd),
             wl[:, o3:o6], jnp.pad(wkr, pe_pad), jnp.pad(_partner(wkr, rh), pe_pad)],
            axis=1).astype(BF16)
        gq_gain = gqa_q_norm[l] * (hd ** -0.5 * LOG2E)
        gq = (_pad_vec(gq_gain, 1, hd), _pad_vec(_partner(gq_gain, hh, 1.0), 1, hd))
        gk = (_pad_vec(gqa_k_norm[l], 1, hd), _pad_vec(_partner(gqa_k_norm[l], hh, 1.0), 1, hd))
        wuq = mla_w_uq[l].reshape(MLA_Q_RANK, MLA_HEADS, MLA_NOPE_DIM + MLA_ROPE_DIM)
        wuq_pe = jnp.pad(_partner(wuq[:, :, MLA_NOPE_DIM:], rh),
                         ((0, 0), (0, 0), (MLA_NOPE_DIM, 0)))
        wuq_p = jnp.concatenate(
            [_pad_heads(mla_w_uq[l], MLA_HEADS, MLA_NOPE_DIM + MLA_ROPE_DIM),
             _pad_heads(wuq_pe.reshape(MLA_Q_RANK, -1), MLA_HEADS, MLA_NOPE_DIM + MLA_ROPE_DIM)],
            axis=1).astype(BF16)
        wukv = mla_w_ukv[l].reshape(MLA_KV_RANK, MLA_HEADS, MLA_NOPE_DIM + MLA_V_DIM)
        wukvk_p = _pad_heads(wukv[:, :, :MLA_NOPE_DIM].reshape(MLA_KV_RANK, -1), MLA_HEADS,
                             MLA_NOPE_DIM).astype(BF16)
        wukvv = wukv[:, :, MLA_NOPE_DIM:].reshape(MLA_KV_RANK, MLA_HEADS * MLA_V_DIM).astype(BF16)

        hy_in, qg, kg, vg, qm, km, vm = _kin_call(
            x, mix_pre_norm[l][None], win_p, gq, gk, tg, mla_q_a_norm[l][None], wuq_p, tmq,
            mla_kv_a_norm[l][None], wukvk_p, wukvv, tmk)

        fw = _filter_weights(hy_filt_w1[l], hy_filt_b1[l], hy_filt_freq1[l], hy_filt_w2[l],
                             hy_filt_b2[l], hy_filt_freq2[l], hy_filt_w3[l])
        y_hy = _hyena_layer(hy_in, hy_conv_w[l], hy_conv_b[l], fw, hy_skip[l], z_perm, win_perm, mats)

        y_gqa = _attn_call(qg.transpose(0, 2, 1), kg, vg.reshape(B, GQA_KV_HEADS, GQA_HEAD_DIM, L),
                           n_kv=1, n_rep=GQA_HEADS // GQA_KV_HEADS, tq=TQ_GQA, name="attn_gqa")
        y_mla = _attn_call(qm.transpose(0, 2, 1), km, vm.reshape(B, MLA_HEADS, MLA_V_DIM, L),
                           n_kv=2, n_rep=1, tq=TQ_MLA, name="attn_mla")

        x, h2 = _kout_call(x, y_hy, y_gqa, y_mla, hy_out_norm[l][None], gqa_out_norm[l][None],
                           mla_out_norm[l][None], w_out[l].astype(BF16),
                           mix_post_norm[l][None], ffn_pre_norm[l][None])
        x = _ffn_call(h2, x, w_up[l].astype(BF16), ffn_conv_w[l], ffn_conv_b[l],
                      w_down[l].astype(BF16), ffn_post_norm[l][None])
    return x
```

```python
import functools
import math

import numpy as np
import jax
import jax.numpy as jnp
from jax import lax
from jax.experimental import pallas as pl
from jax.experimental.pallas import tpu as pltpu

F32 = jnp.float32
BF16 = jnp.bfloat16

NORM_EPS = 1e-6
ROPE_THETA = 10000.0
GRID_W = 64
LOG2E = math.log2(math.e)

D_MODEL = 1024
HY_D = 256
HY_EMB = 33
HY_BANDS = 16
HY_DECAY_TARGET = 1e-2
HY_FAST_DECAY_PCT = 0.3
HY_SLOW_DECAY_PCT = 1.5
GQA_HEADS = 8
GQA_KV_HEADS = 2
GQA_HEAD_DIM = 64
MLA_HEADS = 4
MLA_Q_RANK = 256
MLA_KV_RANK = 128
MLA_NOPE_DIM = 64
MLA_ROPE_DIM = 32
MLA_V_DIM = 64
D_FF = 2816

LANES = 128
FFT_N1 = 128
FFT_N2 = 128

TM_IN = 512
TK_ATTN = 512
TQ_GQA = 128
TQ_MLA = 256
TM_OUT = 512
TM_FFN = 512
TF_FFN = 256
TL_HCONV = 1024
FILT_HALF = 512
FILT_FEAT = 64
VMEM_LIMIT = 56 * 1024 * 1024


def _cparams(sem):
    return pltpu.CompilerParams(dimension_semantics=sem, vmem_limit_bytes=VMEM_LIMIT)


def _rms(x, g):
    return x * lax.rsqrt(jnp.mean(x * x, axis=-1, keepdims=True) + NORM_EPS) * g


def _kin_kernel(x_ref, gpre_ref, win_ref, gqn_ref, gqs_ref, gkn_ref, gks_ref, cg_ref, sg_ref,
                mqn_ref, wuq_ref, cmq_ref, smq_ref,
                mkvn_ref, wukvk_ref, wukvv_ref, cmk_ref, smk_ref,
                hy_ref, qg_ref, kg_ref, vg_ref, qm_ref, km_ref, vm_ref):
    x = x_ref[0]
    h = _rms(x, gpre_ref[...]).astype(BF16)
    cur = [0]

    def proj(n):
        lo = cur[0]
        cur[0] = lo + n
        return jnp.dot(h, win_ref[:, lo:lo + n], preferred_element_type=F32)

    hy_ref[0] = proj(768)

    cg, sg = cg_ref[...], sg_ref[...]
    low = lax.broadcasted_iota(jnp.int32, (x.shape[0], LANES), 1) < GQA_HEAD_DIM

    def head_pairs(n_tiles, gain, gain_sw):
        xa, xb = proj(n_tiles * LANES), proj(n_tiles * LANES)
        ca, sa = cg * gain, sg * gain_sw
        out = []
        for t in range(n_tiles):
            xc, xs = xa[:, t * LANES:(t + 1) * LANES], xb[:, t * LANES:(t + 1) * LANES]
            sq = xc * xc
            tot = jnp.sum(sq, axis=-1, keepdims=True)
            lo = jnp.sum(jnp.where(low, sq, 0.0), axis=-1, keepdims=True)
            ms = jnp.where(low, lo, tot - lo) * (1.0 / GQA_HEAD_DIM)
            out.append((xc * ca + xs * sa) * lax.rsqrt(ms + NORM_EPS))
        return out

    for t, q in enumerate(head_pairs(GQA_HEADS // 2, gqn_ref[...], gqs_ref[...])):
        qg_ref[0, :, t * LANES:(t + 1) * LANES] = q.astype(BF16)
    (kk,) = head_pairs(GQA_KV_HEADS // 2, gkn_ref[...], gks_ref[...])
    ksw = pltpu.roll(kk, GQA_HEAD_DIM, 1)
    kg_ref[0, 0] = jnp.where(low, kk, ksw).astype(BF16)
    kg_ref[0, 1] = jnp.where(low, ksw, kk).astype(BF16)
    vg_ref[0] = proj(LANES).T.astype(BF16)

    cq = _rms(proj(MLA_Q_RANK), mqn_ref[...]).astype(BF16)
    qm = jnp.dot(cq, wuq_ref[...], preferred_element_type=F32)
    cmq, smq = cmq_ref[...], smq_ref[...]
    nq = MLA_HEADS * LANES
    for j in range(MLA_HEADS):
        qm_ref[0, :, j * LANES:(j + 1) * LANES] = (
            qm[:, j * LANES:(j + 1) * LANES] * cmq
            + qm[:, nq + j * LANES:nq + (j + 1) * LANES] * smq).astype(BF16)

    ckv = _rms(proj(MLA_KV_RANK), mkvn_ref[...]).astype(BF16)
    kpe = proj(LANES) * cmk_ref[...] + proj(LANES) * smk_ref[...]
    kn = jnp.dot(ckv, wukvk_ref[...], preferred_element_type=F32)
    for j in range(MLA_HEADS):
        km_ref[0, j] = (kn[:, j * LANES:(j + 1) * LANES] + kpe).astype(BF16)
    vm = jnp.dot(ckv, wukvv_ref[...], preferred_element_type=F32)
    for j in range(vm.shape[1] // LANES):
        vm_ref[0, j * LANES:(j + 1) * LANES, :] = vm[:, j * LANES:(j + 1) * LANES].T.astype(BF16)


def _kin_call(x, gpre, win_p, gq, gk, tg, mqn, wuq_p, tmq, mkvn, wukvk_p, wukvv, tmk):
    B, L, D = x.shape
    tm = TM_IN
    nt = L // tm

    def full(a):
        return pl.BlockSpec(a.shape, lambda b, i: (0,) * a.ndim, pipeline_mode=pl.Buffered(1))

    def rows(w):
        return pl.BlockSpec((tm, w), lambda b, i: (i, 0))

    in_specs = [pl.BlockSpec((1, tm, D), lambda b, i: (b, i, 0)), full(gpre), full(win_p),
                full(gq[0]), full(gq[1]), full(gk[0]), full(gk[1]), rows(LANES), rows(LANES),
                full(mqn), full(wuq_p), rows(LANES), rows(LANES),
                full(mkvn), full(wukvk_p), full(wukvv), rows(LANES), rows(LANES)]
    gv_rows = GQA_KV_HEADS * GQA_HEAD_DIM
    mv_rows = MLA_HEADS * MLA_V_DIM
    out_shape = [
        jax.ShapeDtypeStruct((B, L, 768), F32),
        jax.ShapeDtypeStruct((B, L, GQA_HEADS * GQA_HEAD_DIM), BF16),
        jax.ShapeDtypeStruct((B, GQA_KV_HEADS, L, LANES), BF16),
        jax.ShapeDtypeStruct((B, gv_rows, L), BF16),
        jax.ShapeDtypeStruct((B, L, MLA_HEADS * LANES), BF16),
        jax.ShapeDtypeStruct((B, MLA_HEADS, L, LANES), BF16),
        jax.ShapeDtypeStruct((B, mv_rows, L), BF16),
    ]
    out_specs = [
        pl.BlockSpec((1, tm, 768), lambda b, i: (b, i, 0)),
        pl.BlockSpec((1, tm, GQA_HEADS * GQA_HEAD_DIM), lambda b, i: (b, i, 0)),
        pl.BlockSpec((1, GQA_KV_HEADS, tm, LANES), lambda b, i: (b, 0, i, 0)),
        pl.BlockSpec((1, gv_rows, tm), lambda b, i: (b, 0, i)),
        pl.BlockSpec((1, tm, MLA_HEADS * LANES), lambda b, i: (b, i, 0)),
        pl.BlockSpec((1, MLA_HEADS, tm, LANES), lambda b, i: (b, 0, i, 0)),
        pl.BlockSpec((1, mv_rows, tm), lambda b, i: (b, 0, i)),
    ]
    return pl.pallas_call(
        _kin_kernel, grid=(B, nt), in_specs=in_specs, out_specs=out_specs, out_shape=out_shape,
        compiler_params=_cparams(("parallel", "parallel")), name="in_proj",
    )(x, gpre, win_p, *gq, *gk, *tg, mqn, wuq_p, *tmq, mkvn, wukvk_p, wukvv, *tmk)


def _attn_kernel(q_ref, k_ref, vt_ref, o_ref, sa_ref, sb_ref, ma_ref, mb_ref, *,
                 n_kv, n_rep, pack, tq, n_chunks, tk):
    i = pl.program_id(0)
    cols = n_rep * tq
    width = n_kv * cols
    grp = tk // 8

    @pl.when(i == 0)
    def _():
        sb_ref[...] = jnp.zeros(sb_ref.shape, F32)
        mb_ref[...] = jnp.zeros(mb_ref.shape, F32)

    def step(sw_ref, mw_ref, sr_ref, mr_ref):
        def q_head(h):
            tile, half = divmod(h, pack)
            qt = q_ref[0, tile * LANES:(tile + 1) * LANES, :]
            if pack == 2:
                z = jnp.zeros((LANES // 2, qt.shape[1]), qt.dtype)
                qt = jnp.concatenate([qt[:LANES // 2], z] if half == 0 else [z, qt[LANES // 2:]],
                                     axis=0)
            return qt

        qs = [jnp.concatenate([q_head(a * n_rep + j) for j in range(n_rep)], axis=1)
              for a in range(n_kv)]
        mx = jnp.max(mr_ref[...], axis=0, keepdims=True)
        m = jnp.full((8, width), -jnp.inf, F32)
        l = jnp.zeros((8, width), F32)
        accs = [jnp.zeros((vt_ref.shape[2], cols), F32) for _ in range(n_kv)]
        for c in range(n_chunks):
            st = jnp.concatenate(
                [jnp.dot(k_ref[0, a, c * tk:(c + 1) * tk, :], qs[a], preferred_element_type=F32)
                 for a in range(n_kv)], axis=1)
            sw_ref[c] = st
            m = jnp.maximum(m, jnp.max(st.reshape(grp, 8, width), axis=0))
            p = jnp.exp2(sr_ref[c] - mx)
            l = l + jnp.sum(p.reshape(grp, 8, width), axis=0)
            pb = p.astype(BF16)
            for a in range(n_kv):
                accs[a] = accs[a] + jnp.dot(vt_ref[0, a, :, c * tk:(c + 1) * tk],
                                            pb[:, a * cols:(a + 1) * cols],
                                            preferred_element_type=F32)
        mw_ref[...] = m
        ls = jnp.sum(l, axis=0, keepdims=True)
        heads = []
        for a in range(n_kv):
            oa = accs[a] / ls[:, a * cols:(a + 1) * cols]
            heads += [oa[:, j * tq:(j + 1) * tq] for j in range(n_rep)]
        o_ref[0] = jnp.concatenate(heads, axis=0).T.astype(BF16)

    @pl.when(i % 2 == 0)
    def _():
        step(sa_ref, ma_ref, sb_ref, mb_ref)

    @pl.when(i % 2 == 1)
    def _():
        step(sb_ref, mb_ref, sa_ref, ma_ref)


def _attn_call(qt, k, vt, *, n_kv, n_rep, pack, tq, name):
    B, hq, L = qt.shape
    H = hq * pack // LANES
    hkv, dv = vt.shape[1], vt.shape[2]
    G = hkv // n_kv
    hs = n_kv * n_rep
    tk = TK_ATTN
    n_chunks = L // tk
    nq = L // tq
    width = hs * tq
    kern = functools.partial(_attn_kernel, n_kv=n_kv, n_rep=n_rep, pack=pack, tq=tq,
                             n_chunks=n_chunks, tk=tk)
    total = B * G * nq

    def blk(s):
        return s // (G * nq), (s // nq) % G, s % nq

    def q_map(s):
        b, g, i = blk(jnp.minimum(s, total - 1))
        return b, g, i

    def k_map(s):
        b, g, _ = blk(jnp.minimum(s, total - 1))
        return b, g, 0, 0

    def v_map(s):
        b, g, _ = blk(jnp.maximum(s - 1, 0))
        return b, g, 0, 0

    def o_map(s):
        b, g, i = blk(jnp.maximum(s - 1, 0))
        return b, i, g

    return pl.pallas_call(
        kern, grid=(total + 1,),
        in_specs=[pl.BlockSpec((1, hs * LANES // pack, tq), q_map),
                  pl.BlockSpec((1, n_kv, L, LANES), k_map, pipeline_mode=pl.Buffered(1)),
                  pl.BlockSpec((1, n_kv, dv, L), v_map, pipeline_mode=pl.Buffered(1))],
        out_specs=pl.BlockSpec((1, tq, hs * dv), o_map),
        out_shape=jax.ShapeDtypeStruct((B, L, H * dv), BF16),
        scratch_shapes=[pltpu.VMEM((n_chunks, tk, width), F32), pltpu.VMEM((n_chunks, tk, width), F32),
                        pltpu.VMEM((8, width), F32), pltpu.VMEM((8, width), F32)],
        compiler_params=_cparams(("arbitrary",)), name=name,
    )(qt, k, vt)


def _hconv_kernel(x_ref, xp_ref, xn_ref, w_ref, b_ref, v_ref, x1_ref, x2_ref):
    i = pl.program_id(1)
    x = x_ref[0]
    tl = x.shape[0]
    prev = jnp.where(i > 0, xp_ref[0][7:8, :], 0.0)
    nxt = jnp.where(i < pl.num_programs(1) - 1, xn_ref[0][0:1, :], 0.0)
    r = lax.broadcasted_iota(jnp.int32, x.shape, 0)
    xm = jnp.where(r == 0, prev, pltpu.roll(x, 1, 0))
    xp = jnp.where(r == tl - 1, nxt, pltpu.roll(x, tl - 1, 0))
    uc = xm * w_ref[0:1, :] + x * w_ref[1:2, :] + xp * w_ref[2:3, :] + b_ref[...]
    for r in range(tl // FFT_N2):
        blk = uc[r * FFT_N2:(r + 1) * FFT_N2]
        v_ref[:, r, :] = blk[:, :HY_D]
        x1_ref[:, r, :] = blk[:, HY_D:2 * HY_D]
        x2_ref[:, r, :] = blk[:, 2 * HY_D:]


def _hconv_call(hy_in, w, b):
    B, L, C = hy_in.shape
    tl = TL_HCONV
    nb = tl // 8
    last = L // 8 - 1
    nt = L // tl
    rows = tl // FFT_N2
    out = jax.ShapeDtypeStruct((FFT_N2, B * L // FFT_N2, HY_D), F32)
    ospec = pl.BlockSpec((FFT_N2, rows, HY_D), lambda b_, i: (0, b_ * nt + i, 0))
    return pl.pallas_call(
        _hconv_kernel, grid=(B, L // tl),
        in_specs=[pl.BlockSpec((1, tl, C), lambda b_, i: (b_, i, 0)),
                  pl.BlockSpec((1, 8, C), lambda b_, i: (b_, jnp.maximum(i * nb - 1, 0), 0)),
                  pl.BlockSpec((1, 8, C), lambda b_, i: (b_, jnp.minimum((i + 1) * nb, last), 0)),
                  pl.BlockSpec((3, C), lambda b_, i: (0, 0)),
                  pl.BlockSpec((1, C), lambda b_, i: (0, 0))],
        out_specs=[ospec, ospec, ospec], out_shape=[out, out, out],
        compiler_params=_cparams(("parallel", "parallel")), name="hy_conv3",
    )(hy_in, hy_in, hy_in, w, b)


def _split_bf16(a):
    hi = a.astype(BF16)
    return hi, (a - hi.astype(F32)).astype(BF16)


def _dot3(a, w_hi, w_lo):
    a_hi, a_lo = _split_bf16(a)
    dot = functools.partial(jnp.dot, preferred_element_type=F32)
    return dot(a_hi, w_hi) + dot(a_lo, w_hi) + dot(a_hi, w_lo)


def _filt_kernel(z_ref, w1h_ref, w1l_ref, b1_ref, f1_ref, w2h_ref, w2l_ref, b2_ref, f2_ref,
                 w3h_ref, w3l_ref, win_ref, gaf_ref, o_ref):
    h = jnp.sin(f1_ref[...] * (_dot3(z_ref[...], w1h_ref[...], w1l_ref[...]) + b1_ref[...]))
    h = jnp.sin(f2_ref[...] * (_dot3(h, w2h_ref[...], w2l_ref[...]) + b2_ref[...]))
    k = _dot3(h, w3h_ref[...], w3l_ref[...])
    half = k.shape[0]
    wcols = 4 * HY_D
    per_half = half // FFT_N1
    n1 = lax.broadcasted_iota(jnp.int32, (half, 2 * HY_D), 0) % FFT_N1
    for s in range(2):
        ks = k[:, s * wcols:(s + 1) * wcols]
        kk = jnp.where(n1 >= FFT_N1 // 2, ks[:, 2 * HY_D:], ks[:, :2 * HY_D])
        win = win_ref[s * half:(s + 1) * half, :]
        filt = (kk * jnp.concatenate([win, win], axis=-1)).astype(BF16)
        for j in range(per_half):
            o_ref[s * per_half + j] = jnp.dot(gaf_ref[s * per_half + j],
                                              filt[j * FFT_N1:(j + 1) * FFT_N1],
                                              preferred_element_type=F32)


def _filt_call(z_pack, fw, win_perm, gaf):
    tp = 2 * FILT_HALF
    n = win_perm.shape[0]
    to = tp // FFT_N1
    M = gaf.shape[1]

    def full(a):
        return pl.BlockSpec(a.shape, lambda i: (0,) * a.ndim)

    return pl.pallas_call(
        _filt_kernel, grid=(n // tp,),
        in_specs=[pl.BlockSpec((FILT_HALF, z_pack.shape[1]), lambda i: (i, 0))]
        + [full(a) for a in fw] + [pl.BlockSpec((tp, HY_D), lambda i: (i, 0)),
                                   pl.BlockSpec((to, M, FFT_N1), lambda i: (i, 0, 0))],
        out_specs=pl.BlockSpec((to, M, 2 * HY_D), lambda i: (i, 0, 0)),
        out_shape=jax.ShapeDtypeStruct((n // FFT_N1, M, 2 * HY_D), F32),
        compiler_params=_cparams(("parallel",)), name="hy_filter",
    )(z_pack, *fw, win_perm, gaf)


def _bm_kernel(g_ref, x_ref, o_ref, *, to):
    for t in range(to):
        o_ref[t] = jnp.dot(g_ref[t], x_ref[t].astype(BF16), preferred_element_type=F32)


def _bm_call(g, x, name, to=8):
    O, K, N = x.shape
    M = g.shape[1]
    return pl.pallas_call(
        functools.partial(_bm_kernel, to=to), grid=(O // to,),
        in_specs=[pl.BlockSpec((to, M, K), lambda i: (i, 0, 0)),
                  pl.BlockSpec((to, K, N), lambda i: (i, 0, 0))],
        out_specs=pl.BlockSpec((to, M, N), lambda i: (i, 0, 0)),
        out_shape=jax.ShapeDtypeStruct((O, M, N), F32),
        compiler_params=_cparams(("parallel",)), name=name,
    )(g, x)


def _gather_ri(x_ref, j):
    return jnp.concatenate([x_ref[:, 0, j, :], x_ref[:, 1, j, :]], axis=0).astype(BF16)


def _gather_spec(a, to):
    return pl.BlockSpec((a.shape[0], 2, to, a.shape[3]), lambda i: (0, 0, i, 0))


def _filtb_kernel(mb_ref, x_ref, o_ref, *, to):
    for j in range(to):
        o_ref[j] = jnp.dot(mb_ref[...], _gather_ri(x_ref, j),
                           preferred_element_type=F32).astype(o_ref.dtype)


def _filtb_call(mb, ka, to=8):
    O, _, P, N = ka.shape
    return pl.pallas_call(
        functools.partial(_filtb_kernel, to=to), grid=(P // to,),
        in_specs=[pl.BlockSpec(mb.shape, lambda i: (0, 0)), _gather_spec(ka, to)],
        out_specs=pl.BlockSpec((to, 2 * O, N), lambda i: (i, 0, 0)),
        out_shape=jax.ShapeDtypeStruct((P, 2 * O, N), BF16),
        compiler_params=_cparams(("parallel",)), name="hy_fft_filt_b",
    )(mb, ka)


def _convb_kernel(mb_ref, gc_ref, x_ref, kf_ref, o_ref, *, to):
    h = FFT_N2
    for j in range(to):
        xs = jnp.dot(mb_ref[...], _gather_ri(x_ref, j), preferred_element_type=F32)
        xr, xi = xs[:h], xs[h:]
        kr, ki = kf_ref[j, :h, :].astype(F32), kf_ref[j, h:, :].astype(F32)
        ys = jnp.concatenate([xr * kr - xi * ki, xr * ki + xi * kr], axis=0).astype(BF16)
        o_ref[j] = jnp.dot(gc_ref[j], ys, preferred_element_type=F32)


def _convb_call(mb, gc, a, kf, order, to=8):
    O, _, P, N = a.shape
    return pl.pallas_call(
        functools.partial(_convb_kernel, to=to), grid=(P // to,),
        in_specs=[pl.BlockSpec(mb.shape, lambda i: (0, 0)),
                  pl.BlockSpec((to, 2 * O, 2 * O), lambda i: (i, 0, 0)),
                  _gather_spec(a, to),
                  pl.BlockSpec((to, 2 * O, N), lambda i: (i, 0, order))],
        out_specs=pl.BlockSpec((to, 2 * O, N), lambda i: (i, 0, 0)),
        out_shape=jax.ShapeDtypeStruct((P, 2 * O, N), F32),
        compiler_params=_cparams(("parallel",)), name="hy_spec_mul",
    )(mb, gc, a, kf)


def _convd_kernel(md_ref, c_ref, g_ref, u_ref, s_ref, *rest, to, chain):
    if chain:
        ga_ref, z_ref, a_ref = rest
    else:
        (o_ref,) = rest
    for j in range(to):
        y = jnp.dot(md_ref[...], _gather_ri(c_ref, j), preferred_element_type=F32)
        z = g_ref[j] * (y + u_ref[j] * s_ref[...])
        if chain:
            z_ref[j] = z
            a_ref[j] = jnp.dot(ga_ref[j], z.astype(BF16), preferred_element_type=F32)
        else:
            o_ref[:, j, :] = z


def _convd_call(md, c, gate, u, skip, ga=None, to=8):
    O, _, P, N = c.shape
    R = md.shape[0]
    tspec = pl.BlockSpec((to, R, N), lambda i: (i, 0, 0))
    in_specs = [pl.BlockSpec(md.shape, lambda i: (0, 0)), _gather_spec(c, to), tspec, tspec,
                pl.BlockSpec((1, N), lambda i: (0, 0))]
    args = [md, c, gate, u, skip.reshape(1, N)]
    if ga is not None:
        M = ga.shape[1]
        in_specs.append(pl.BlockSpec((to, M, R), lambda i: (i, 0, 0)))
        args.append(ga)
        out_specs = [tspec, pl.BlockSpec((to, M, N), lambda i: (i, 0, 0))]
        out_shape = [jax.ShapeDtypeStruct((P, R, N), F32), jax.ShapeDtypeStruct((P, M, N), F32)]
    else:
        out_specs = pl.BlockSpec((R, to, N), lambda i: (0, i, 0))
        out_shape = jax.ShapeDtypeStruct((R, P, N), F32)
    return pl.pallas_call(
        functools.partial(_convd_kernel, to=to, chain=ga is not None), grid=(P // to,),
        in_specs=in_specs, out_specs=out_specs, out_shape=out_shape,
        compiler_params=_cparams(("parallel",)), name="hy_fft_d",
    )(*args)


def _dft_tables():
    n = FFT_N1 * FFT_N2
    k = np.arange(FFT_N1)
    f = np.exp(-2j * np.pi * np.outer(k, k) / FFT_N1)
    t = np.exp(-2j * np.pi * np.outer(k, k) / n)
    return f, t, n


def _dft_matrices():
    f, t, n = _dft_tables()
    fr, fi = jnp.asarray(f.real, F32), jnp.asarray(f.imag, F32)
    tr, ti = jnp.asarray(t.real, F32), jnp.asarray(t.imag, F32)
    half = FFT_N1 // 2
    er = fr[None] * tr[:, :, None] - fi[None] * ti[:, :, None]
    ei = fr[None] * ti[:, :, None] + fi[None] * tr[:, :, None]
    ga = jnp.concatenate([jnp.concatenate([er[:, :, :half], -ei[:, :, :half]], axis=2),
                          jnp.concatenate([ei[:, :, :half], er[:, :, :half]], axis=2)], axis=1)
    gaf = jnp.concatenate([er, ei], axis=1)
    mb = jnp.concatenate([jnp.concatenate([fr, -fi], axis=1),
                          jnp.concatenate([fi, fr], axis=1)], axis=0)
    tct = jnp.transpose(tr)[:, :, None]
    tst = -jnp.transpose(ti)[:, :, None]
    gr = tct * fr[None] - tst * (-fi[None])
    gi = tct * (-fi[None]) + tst * fr[None]
    gc = jnp.concatenate([jnp.concatenate([gr, -gi], axis=2),
                          jnp.concatenate([gi, gr], axis=2)], axis=1)
    hr, hi = fr[:half] / n, -fi[:half] / n
    md = jnp.concatenate([jnp.concatenate([hr, -hi], axis=1),
                          jnp.concatenate([hi, hr], axis=1)], axis=0)
    return (ga.astype(BF16), gaf.astype(BF16), mb.astype(BF16), gc.astype(BF16), md.astype(BF16))


def _hyena_positions(L):
    p = FFT_N2 * np.arange(FFT_N1)[None, :] + np.arange(FFT_N2)[:, None]
    pos = np.where(p < L, p, 2 * L - 1 - p).reshape(2 * L, 1).astype(np.float64)
    t = pos / (L - 1)
    w = 2.0 * math.pi * pos / L
    f = np.linspace(1e-4, HY_BANDS - 1, HY_BANDS)[None, :]
    z = np.concatenate([t, np.cos(f * w), -np.sin(f * w),
                        np.zeros((2 * L, FILT_FEAT - HY_EMB))], axis=-1)
    z_pack = (z.reshape(-1, 2, FILT_HALF, FILT_FEAT).transpose(0, 2, 1, 3)
              .reshape(-1, 2 * FILT_FEAT))
    max_decay = math.log(HY_DECAY_TARGET) / HY_FAST_DECAY_PCT
    min_decay = math.log(HY_DECAY_TARGET) / HY_SLOW_DECAY_PCT
    deltas = jnp.linspace(min_decay, max_decay, HY_D, dtype=F32)
    window = jnp.exp(-jnp.asarray(t, F32) * jnp.abs(deltas)[None, :])
    return jnp.asarray(z_pack, F32), window


def _filter_weights(w1, b1, f1, w2, b2, f2, w3):
    def bd(w):
        z = jnp.zeros_like(w)
        return jnp.concatenate([jnp.concatenate([w, z], axis=1),
                                jnp.concatenate([z, w], axis=1)], axis=0)

    def twice(v):
        return jnp.concatenate([v, v])[None]

    w1 = jnp.pad(w1, ((0, FILT_FEAT - w1.shape[0]), (0, 0)))
    w3 = w3.reshape(-1, 2, 2, HY_D).transpose(0, 2, 1, 3).reshape(-1, 4 * HY_D)
    return (*_split_bf16(bd(w1)), twice(b1), twice(f1), *_split_bf16(bd(w2)), twice(b2), twice(f2),
            *_split_bf16(bd(w3)))


def _hyena_layer(hy_in, conv_w, conv_b, fw, skip, z_perm, win_perm, mats):
    B, L, _ = hy_in.shape
    ga, gaf, mb, gc, md = mats
    v, x1, x2 = _hconv_call(hy_in, conv_w, conv_b[None])
    ka = _filt_call(z_perm, fw, win_perm, gaf)
    kf = _filtb_call(mb, ka.reshape(FFT_N2, 2, FFT_N1, 2 * HY_D))

    a = _bm_call(ga, v, "hy_fft_a")
    c = _convb_call(mb, gc, a.reshape(FFT_N2, 2, FFT_N1, HY_D), kf, 0)
    z, a = _convd_call(md, c.reshape(FFT_N1, 2, FFT_N2, HY_D), x1, v, skip[0], ga=ga)
    c = _convb_call(mb, gc, a.reshape(FFT_N2, 2, FFT_N1, HY_D), kf, 1)
    z = _convd_call(md, c.reshape(FFT_N1, 2, FFT_N2, HY_D), x2, z, skip[1])
    return z.reshape(B, L, HY_D)


def _kout_kernel(x_ref, yh_ref, yg_ref, ym_ref, gh_ref, gg_ref, gm_ref, w_ref, gpost_ref, gffn_ref,
                 xo_ref, h_ref):
    a = _rms(yh_ref[0], gh_ref[...]).astype(BF16)
    b = _rms(yg_ref[0].astype(F32), gg_ref[...]).astype(BF16)
    c = _rms(ym_ref[0].astype(F32), gm_ref[...]).astype(BF16)
    o1 = HY_D
    o2 = o1 + GQA_HEADS * GQA_HEAD_DIM
    y = (jnp.dot(a, w_ref[:o1, :], preferred_element_type=F32)
         + jnp.dot(b, w_ref[o1:o2, :], preferred_element_type=F32)
         + jnp.dot(c, w_ref[o2:, :], preferred_element_type=F32))
    xo = x_ref[0] + _rms(y, gpost_ref[...])
    xo_ref[0] = xo
    h_ref[0] = _rms(xo, gffn_ref[...]).astype(BF16)


def _kout_call(x, yh, yg, ym, gh, gg, gm, w_p, gpost, gffn):
    B, L, D = x.shape
    tm = TM_OUT

    def rows(a):
        return pl.BlockSpec((1, tm, a.shape[2]), lambda b, i: (b, i, 0))

    def full(a):
        return pl.BlockSpec(a.shape, lambda b, i: (0,) * a.ndim)

    return pl.pallas_call(
        _kout_kernel, grid=(B, L // tm),
        in_specs=[rows(x), rows(yh), rows(yg), rows(ym), full(gh), full(gg), full(gm), full(w_p),
                  full(gpost), full(gffn)],
        out_specs=[rows(x), rows(x)],
        out_shape=[jax.ShapeDtypeStruct((B, L, D), F32), jax.ShapeDtypeStruct((B, L, D), BF16)],
        compiler_params=_cparams(("parallel", "parallel")), name="out_proj",
    )(x, yh, yg, ym, gh, gg, gm, w_p, gpost, gffn)


HALO = 16


def _ffn_kernel(h_ref, hp_ref, hn_ref, x_ref, wup_ref, cw_ref, cb_ref, wd_ref, gpost_ref,
                o_ref, act_ref):
    i = pl.program_id(1)
    tm = h_ref.shape[1]
    prev = jnp.where(i > 0, hp_ref[0], jnp.zeros_like(hp_ref[0]))
    nxt = jnp.where(i < pl.num_programs(1) - 1, hn_ref[0], jnp.zeros_like(hn_ref[0]))
    he = jnp.concatenate([prev, h_ref[0], nxt], axis=0)
    ext = tm + 2 * HALO
    tf = TF_FFN

    def conv(c0):
        up = jnp.dot(he, wup_ref[:, c0:c0 + tf], preferred_element_type=F32)
        um = pltpu.roll(up, 1, 0)[HALO:HALO + tm]
        upl = pltpu.roll(up, ext - 1, 0)[HALO:HALO + tm]
        return (um * cw_ref[0:1, c0:c0 + tf] + up[HALO:HALO + tm] * cw_ref[1:2, c0:c0 + tf]
                + upl * cw_ref[2:3, c0:c0 + tf] + cb_ref[:, c0:c0 + tf])

    for j in range(D_FF // tf):
        g = conv(j * tf)
        u = conv(D_FF + j * tf)
        gelu = 0.5 * g * (1.0 + jnp.tanh(math.sqrt(2.0 / math.pi) * (g + 0.044715 * (g * g * g))))
        act_ref[:, j * tf:(j + 1) * tf] = (gelu * u).astype(BF16)
    f = jnp.dot(act_ref[...], wd_ref[...], preferred_element_type=F32)
    o_ref[0] = x_ref[0] + _rms(f, gpost_ref[...])


def _ffn_call(h, x, w_up, cw, cb, w_down, gpost):
    B, L, D = x.shape
    tm = TM_FFN
    nb = tm // HALO
    last = L // HALO - 1

    def resident(a):
        return pl.BlockSpec(a.shape, lambda b, i: (0,) * a.ndim, pipeline_mode=pl.Buffered(1))

    cb = cb[None]
    return pl.pallas_call(
        _ffn_kernel, grid=(B, L // tm),
        in_specs=[pl.BlockSpec((1, tm, D), lambda b, i: (b, i, 0)),
                  pl.BlockSpec((1, HALO, D), lambda b, i: (b, jnp.maximum(i * nb - 1, 0), 0)),
                  pl.BlockSpec((1, HALO, D), lambda b, i: (b, jnp.minimum((i + 1) * nb, last), 0)),
                  pl.BlockSpec((1, tm, D), lambda b, i: (b, i, 0)),
                  resident(w_up), resident(cw), resident(cb), resident(w_down), resident(gpost)],
        out_specs=pl.BlockSpec((1, tm, D), lambda b, i: (b, i, 0)),
        out_shape=jax.ShapeDtypeStruct((B, L, D), F32),
        scratch_shapes=[pltpu.VMEM((tm, D_FF), BF16)],
        compiler_params=_cparams(("parallel", "parallel")), name="conv_ffn",
    )(h, h, h, x, w_up, cw, cb, w_down, gpost)


def _axial_tables(L, rot_dim):
    pos = np.arange(L)
    n_axis = rot_dim // 4
    inv = ROPE_THETA ** (-np.arange(n_axis) / n_axis)
    ang = np.concatenate([(pos // GRID_W)[:, None] * inv, (pos % GRID_W)[:, None] * inv], axis=-1)
    return jnp.asarray(np.cos(ang), F32), jnp.asarray(np.sin(ang), F32)


def _rope_tables(L):
    def lanes(parts):
        used = sum(p.shape[1] for p in parts)
        return jnp.concatenate(parts + [jnp.zeros((L, LANES - used), F32)], axis=1)

    cg, sg = _axial_tables(L, GQA_HEAD_DIM)
    tg = (lanes([cg, cg, cg, cg]), lanes([sg, sg, sg, sg]))
    cm, sm = _axial_tables(L, MLA_ROPE_DIM)
    nope0 = jnp.zeros((L, MLA_NOPE_DIM), F32)
    nope1 = jnp.ones((L, MLA_NOPE_DIM), F32)
    tmk = (lanes([nope0, cm, cm]), lanes([nope0, sm, sm]))
    sc = (MLA_NOPE_DIM + MLA_ROPE_DIM) ** -0.5 * LOG2E
    tmq = (lanes([nope1, cm, cm]) * sc, tmk[1] * sc)
    return tg, tmq, tmk


def _partner(w, half, sign=-1.0):
    return jnp.concatenate([sign * w[..., half:], w[..., :half]], axis=-1)


def _pad_heads(w, n_heads, width):
    k = w.shape[0]
    return jnp.pad(w.reshape(k, n_heads, width), ((0, 0), (0, 0), (0, LANES - width))).reshape(
        k, n_heads * LANES)


def kernel(x, mix_pre_norm, w_in, hy_conv_w, hy_conv_b, hy_filt_w1, hy_filt_b1, hy_filt_freq1,
           hy_filt_w2, hy_filt_b2, hy_filt_freq2, hy_filt_w3, hy_skip, gqa_q_norm, gqa_k_norm,
           mla_q_a_norm, mla_w_uq, mla_kv_a_norm, mla_w_ukv, hy_out_norm, gqa_out_norm,
           mla_out_norm, w_out, mix_post_norm, ffn_pre_norm, w_up, ffn_conv_w, ffn_conv_b,
           w_down, ffn_post_norm):
    B, L, D = x.shape
    assert B == 2 and 2 * L == FFT_N1 * FFT_N2 and D == D_MODEL
    depth = w_in.shape[0]
    tg, tmq, tmk = _rope_tables(L)
    z_perm, win_perm = _hyena_positions(L)
    mats = _dft_matrices()

    for l in range(depth):
        wl = w_in[l]
        hd, hh, rh = GQA_HEAD_DIM, GQA_HEAD_DIM // 2, MLA_ROPE_DIM // 2
        o1 = 768
        o2 = o1 + GQA_HEADS * hd
        o3 = o2 + GQA_KV_HEADS * hd
        o4 = o3 + GQA_KV_HEADS * hd
        o5 = o4 + MLA_Q_RANK
        o6 = o5 + MLA_KV_RANK
        wq = wl[:, o1:o2].reshape(D, GQA_HEADS, hd)
        wk = wl[:, o2:o3].reshape(D, GQA_KV_HEADS, hd)
        wkr = wl[:, o6:]
        pe_pad = ((0, 0), (MLA_NOPE_DIM, LANES - MLA_NOPE_DIM - MLA_ROPE_DIM))
        win_p = jnp.concatenate(
            [wl[:, :o1],
             wq.reshape(D, -1), _partner(wq, hh).reshape(D, -1),
             wk.reshape(D, -1), _partner(wk, hh).reshape(D, -1),
             wl[:, o3:o6], jnp.pad(wkr, pe_pad), jnp.pad(_partner(wkr, rh), pe_pad)],
            axis=1).astype(BF16)
        gq_gain = gqa_q_norm[l] * (hd ** -0.5 * LOG2E)
        def pair(g):
            return jnp.tile(g, 2)[None], jnp.tile(_partner(g, hh, 1.0), 2)[None]

        gq, gk = pair(gq_gain), pair(gqa_k_norm[l])
        wuq = mla_w_uq[l].reshape(MLA_Q_RANK, MLA_HEADS, MLA_NOPE_DIM + MLA_ROPE_DIM)
        wuq_pe = jnp.pad(_partner(wuq[:, :, MLA_NOPE_DIM:], rh),
                         ((0, 0), (0, 0), (MLA_NOPE_DIM, 0)))
        wuq_p = jnp.concatenate(
            [_pad_heads(mla_w_uq[l], MLA_HEADS, MLA_NOPE_DIM + MLA_ROPE_DIM),
             _pad_heads(wuq_pe.reshape(MLA_Q_RANK, -1), MLA_HEADS, MLA_NOPE_DIM + MLA_ROPE_DIM)],
            axis=1).astype(BF16)
        wukv = mla_w_ukv[l].reshape(MLA_KV_RANK, MLA_HEADS, MLA_NOPE_DIM + MLA_V_DIM)
        wukvk_p = _pad_heads(wukv[:, :, :MLA_NOPE_DIM].reshape(MLA_KV_RANK, -1), MLA_HEADS,
                             MLA_NOPE_DIM).astype(BF16)
        wukvv = wukv[:, :, MLA_NOPE_DIM:].reshape(MLA_KV_RANK, MLA_HEADS * MLA_V_DIM).astype(BF16)

        hy_in, qg, kg, vg, qm, km, vm = _kin_call(
            x, mix_pre_norm[l][None], win_p, gq, gk, tg, mla_q_a_norm[l][None], wuq_p, tmq,
            mla_kv_a_norm[l][None], wukvk_p, wukvv, tmk)

        fw = _filter_weights(hy_filt_w1[l], hy_filt_b1[l], hy_filt_freq1[l], hy_filt_w2[l],
                             hy_filt_b2[l], hy_filt_freq2[l], hy_filt_w3[l])
        y_hy = _hyena_layer(hy_in, hy_conv_w[l], hy_conv_b[l], fw, hy_skip[l], z_perm, win_perm, mats)

        y_gqa = _attn_call(qg.transpose(0, 2, 1), kg, vg.reshape(B, GQA_KV_HEADS, GQA_HEAD_DIM, L),
                           n_kv=1, n_rep=GQA_HEADS // GQA_KV_HEADS, pack=2, tq=TQ_GQA,
                           name="attn_gqa")
        y_mla = _attn_call(qm.transpose(0, 2, 1), km, vm.reshape(B, MLA_HEADS, MLA_V_DIM, L),
                           n_kv=2, n_rep=1, pack=1, tq=TQ_MLA, name="attn_mla")

        x, h2 = _kout_call(x, y_hy, y_gqa, y_mla, hy_out_norm[l][None], gqa_out_norm[l][None],
                           mla_out_norm[l][None], w_out[l].astype(BF16),
                           mix_post_norm[l][None], ffn_pre_norm[l][None])
        x = _ffn_call(h2, x, w_up[l].astype(BF16), ffn_conv_w[l], ffn_conv_b[l],
                      w_down[l].astype(BF16), ffn_post_norm[l][None])
    return x
```

```python
import functools
import math

import numpy as np
import jax
import jax.numpy as jnp
from jax import lax
from jax.experimental import pallas as pl
from jax.experimental.pallas import tpu as pltpu

F32 = jnp.float32
BF16 = jnp.bfloat16

NORM_EPS = 1e-6
ROPE_THETA = 10000.0
GRID_W = 64
LOG2E = math.log2(math.e)

D_MODEL = 1024
HY_D = 256
HY_EMB = 33
HY_BANDS = 16
HY_DECAY_TARGET = 1e-2
HY_FAST_DECAY_PCT = 0.3
HY_SLOW_DECAY_PCT = 1.5
GQA_HEADS = 8
GQA_KV_HEADS = 2
GQA_HEAD_DIM = 64
MLA_HEADS = 4
MLA_Q_RANK = 256
MLA_KV_RANK = 128
MLA_NOPE_DIM = 64
MLA_ROPE_DIM = 32
MLA_V_DIM = 64
D_FF = 2816

LANES = 128
FFT_N1 = 128
FFT_N2 = 128

TM_IN = 512
TK_ATTN = 512
TQ_GQA = 128
TQ_MLA = 256
TM_OUT = 512
TM_FFN = 512
TF_FFN = 256
TL_HCONV = 1024
FILT_HALF = 512
FILT_FEAT = 64
VMEM_LIMIT = 56 * 1024 * 1024


def _cparams(sem):
    return pltpu.CompilerParams(dimension_semantics=sem, vmem_limit_bytes=VMEM_LIMIT)


def _rms(x, g):
    return x * lax.rsqrt(jnp.mean(x * x, axis=-1, keepdims=True) + NORM_EPS) * g


def _kin_kernel(x_ref, gpre_ref, win_ref, gqn_ref, gqs_ref, gkn_ref, gks_ref, cg_ref, sg_ref,
                mqn_ref, wuq_ref, cmq_ref, smq_ref,
                mkvn_ref, wukvk_ref, wukvv_ref, cmk_ref, smk_ref,
                hy_ref, qg_ref, kg_ref, vg_ref, qm_ref, km_ref, vm_ref):
    x = x_ref[0]
    h = _rms(x, gpre_ref[...]).astype(BF16)
    cur = [0]

    def proj(n):
        lo = cur[0]
        cur[0] = lo + n
        return jnp.dot(h, win_ref[:, lo:lo + n], preferred_element_type=F32)

    hy_ref[0] = proj(768)

    cg, sg = cg_ref[...], sg_ref[...]
    low = lax.broadcasted_iota(jnp.int32, (x.shape[0], LANES), 1) < GQA_HEAD_DIM

    def head_pairs(n_tiles, gain, gain_sw):
        xa, xb = proj(n_tiles * LANES), proj(n_tiles * LANES)
        ca, sa = cg * gain, sg * gain_sw
        out = []
        for t in range(n_tiles):
            xc, xs = xa[:, t * LANES:(t + 1) * LANES], xb[:, t * LANES:(t + 1) * LANES]
            sq = xc * xc
            tot = jnp.sum(sq, axis=-1, keepdims=True)
            lo = jnp.sum(jnp.where(low, sq, 0.0), axis=-1, keepdims=True)
            ms = jnp.where(low, lo, tot - lo) * (1.0 / GQA_HEAD_DIM)
            out.append((xc * ca + xs * sa) * lax.rsqrt(ms + NORM_EPS))
        return out

    for t, q in enumerate(head_pairs(GQA_HEADS // 2, gqn_ref[...], gqs_ref[...])):
        qg_ref[0, :, t * LANES:(t + 1) * LANES] = q.astype(BF16)
    (kk,) = head_pairs(GQA_KV_HEADS // 2, gkn_ref[...], gks_ref[...])
    ksw = pltpu.roll(kk, GQA_HEAD_DIM, 1)
    kg_ref[0, 0] = jnp.where(low, kk, ksw).astype(BF16)
    kg_ref[0, 1] = jnp.where(low, ksw, kk).astype(BF16)
    vg_ref[0] = proj(LANES).T.astype(BF16)

    cq = _rms(proj(MLA_Q_RANK), mqn_ref[...]).astype(BF16)
    qm = jnp.dot(cq, wuq_ref[...], preferred_element_type=F32)
    cmq, smq = cmq_ref[...], smq_ref[...]
    nq = MLA_HEADS * LANES
    for j in range(MLA_HEADS):
        qm_ref[0, :, j * LANES:(j + 1) * LANES] = (
            qm[:, j * LANES:(j + 1) * LANES] * cmq
            + qm[:, nq + j * LANES:nq + (j + 1) * LANES] * smq).astype(BF16)

    ckv = _rms(proj(MLA_KV_RANK), mkvn_ref[...]).astype(BF16)
    kpe = proj(LANES) * cmk_ref[...] + proj(LANES) * smk_ref[...]
    kn = jnp.dot(ckv, wukvk_ref[...], preferred_element_type=F32)
    for j in range(MLA_HEADS):
        km_ref[0, j] = (kn[:, j * LANES:(j + 1) * LANES] + kpe).astype(BF16)
    vm = jnp.dot(ckv, wukvv_ref[...], preferred_element_type=F32)
    for j in range(vm.shape[1] // LANES):
        vm_ref[0, j * LANES:(j + 1) * LANES, :] = vm[:, j * LANES:(j + 1) * LANES].T.astype(BF16)


def _kin_call(x, gpre, win_p, gq, gk, tg, mqn, wuq_p, tmq, mkvn, wukvk_p, wukvv, tmk):
    B, L, D = x.shape
    tm = TM_IN
    nt = L // tm

    def full(a):
        return pl.BlockSpec(a.shape, lambda b, i: (0,) * a.ndim, pipeline_mode=pl.Buffered(1))

    def rows(w):
        return pl.BlockSpec((tm, w), lambda b, i: (i, 0))

    in_specs = [pl.BlockSpec((1, tm, D), lambda b, i: (b, i, 0)), full(gpre), full(win_p),
                full(gq[0]), full(gq[1]), full(gk[0]), full(gk[1]), rows(LANES), rows(LANES),
                full(mqn), full(wuq_p), rows(LANES), rows(LANES),
                full(mkvn), full(wukvk_p), full(wukvv), rows(LANES), rows(LANES)]
    gv_rows = GQA_KV_HEADS * GQA_HEAD_DIM
    mv_rows = MLA_HEADS * MLA_V_DIM
    out_shape = [
        jax.ShapeDtypeStruct((B, L, 768), F32),
        jax.ShapeDtypeStruct((B, L, GQA_HEADS * GQA_HEAD_DIM), BF16),
        jax.ShapeDtypeStruct((B, GQA_KV_HEADS, L, LANES), BF16),
        jax.ShapeDtypeStruct((B, gv_rows, L), BF16),
        jax.ShapeDtypeStruct((B, L, MLA_HEADS * LANES), BF16),
        jax.ShapeDtypeStruct((B, MLA_HEADS, L, LANES), BF16),
        jax.ShapeDtypeStruct((B, mv_rows, L), BF16),
    ]
    out_specs = [
        pl.BlockSpec((1, tm, 768), lambda b, i: (b, i, 0)),
        pl.BlockSpec((1, tm, GQA_HEADS * GQA_HEAD_DIM), lambda b, i: (b, i, 0)),
        pl.BlockSpec((1, GQA_KV_HEADS, tm, LANES), lambda b, i: (b, 0, i, 0)),
        pl.BlockSpec((1, gv_rows, tm), lambda b, i: (b, 0, i)),
        pl.BlockSpec((1, tm, MLA_HEADS * LANES), lambda b, i: (b, i, 0)),
        pl.BlockSpec((1, MLA_HEADS, tm, LANES), lambda b, i: (b, 0, i, 0)),
        pl.BlockSpec((1, mv_rows, tm), lambda b, i: (b, 0, i)),
    ]
    return pl.pallas_call(
        _kin_kernel, grid=(B, nt), in_specs=in_specs, out_specs=out_specs, out_shape=out_shape,
        compiler_params=_cparams(("parallel", "parallel")), name="in_proj",
    )(x, gpre, win_p, *gq, *gk, *tg, mqn, wuq_p, *tmq, mkvn, wukvk_p, wukvv, *tmk)


def _attn_kernel(*refs, n_kv, n_rep, pack, tq, n_chunks, tk, bounded):
    if bounded:
        q_ref, k_ref, vt_ref, km_ref, o_ref, sa_ref, sb_ref, ta_ref, tb_ref = refs
    else:
        q_ref, k_ref, vt_ref, o_ref, sa_ref, sb_ref, ta_ref, tb_ref = refs
    i = pl.program_id(0)
    cols = n_rep * tq
    width = n_kv * cols
    grp = tk // 8

    @pl.when(i == 0)
    def _():
        sb_ref[...] = jnp.zeros(sb_ref.shape, sb_ref.dtype)
        tb_ref[...] = jnp.full(tb_ref.shape, 1.0 if bounded else 0.0, F32)

    def step(sw_ref, tw_ref, sr_ref, tr_ref):
        def q_head(h):
            tile, half = divmod(h, pack)
            qt = q_ref[0, tile * LANES:(tile + 1) * LANES, :]
            if pack == 2:
                z = jnp.zeros((LANES // 2, qt.shape[1]), qt.dtype)
                qt = jnp.concatenate([qt[:LANES // 2], z] if half == 0 else [z, qt[LANES // 2:]],
                                     axis=0)
            return qt

        qs = [jnp.concatenate([q_head(a * n_rep + j) for j in range(n_rep)], axis=1)
              for a in range(n_kv)]
        if bounded:
            def col_norm(q):
                qf = q.astype(F32)
                return jnp.sqrt(jnp.sum(qf * qf, axis=0, keepdims=True))

            shift = jnp.concatenate([col_norm(qs[a]) * km_ref[0, a, 0:1, 0:1]
                                     for a in range(n_kv)], axis=1)
        else:
            mx = jnp.max(tr_ref[...], axis=0, keepdims=True)
            m = jnp.full((8, width), -jnp.inf, F32)
        l = jnp.zeros((8, width), F32)
        accs = [jnp.zeros((vt_ref.shape[2], cols), F32) for _ in range(n_kv)]
        for c in range(n_chunks):
            st = jnp.concatenate(
                [jnp.dot(k_ref[0, a, c * tk:(c + 1) * tk, :], qs[a], preferred_element_type=F32)
                 for a in range(n_kv)], axis=1)
            if bounded:
                p = jnp.exp2(st - shift)
                sw_ref[c] = p.astype(BF16)
                pb = sr_ref[c]
            else:
                sw_ref[c] = st
                m = jnp.maximum(m, jnp.max(st.reshape(grp, 8, width), axis=0))
                p = jnp.exp2(sr_ref[c] - mx)
                pb = p.astype(BF16)
            l = l + jnp.sum(p.reshape(grp, 8, width), axis=0)
            for a in range(n_kv):
                accs[a] = accs[a] + jnp.dot(vt_ref[0, a, :, c * tk:(c + 1) * tk],
                                            pb[:, a * cols:(a + 1) * cols],
                                            preferred_element_type=F32)
        if bounded:
            tw_ref[...] = l
            ls = jnp.sum(tr_ref[...], axis=0, keepdims=True)
        else:
            tw_ref[...] = m
            ls = jnp.sum(l, axis=0, keepdims=True)
        heads = []
        for a in range(n_kv):
            oa = accs[a] / ls[:, a * cols:(a + 1) * cols]
            heads += [oa[:, j * tq:(j + 1) * tq] for j in range(n_rep)]
        o_ref[0] = jnp.concatenate(heads, axis=0).T.astype(BF16)

    @pl.when(i % 2 == 0)
    def _():
        step(sa_ref, ta_ref, sb_ref, tb_ref)

    @pl.when(i % 2 == 1)
    def _():
        step(sb_ref, tb_ref, sa_ref, ta_ref)


def _attn_call(qt, k, vt, kmax, *, n_kv, n_rep, pack, tq, name):
    bounded = kmax is not None
    B, hq, L = qt.shape
    H = hq * pack // LANES
    hkv, dv = vt.shape[1], vt.shape[2]
    G = hkv // n_kv
    hs = n_kv * n_rep
    tk = TK_ATTN
    n_chunks = L // tk
    nq = L // tq
    width = hs * tq
    kern = functools.partial(_attn_kernel, n_kv=n_kv, n_rep=n_rep, pack=pack, tq=tq,
                             n_chunks=n_chunks, tk=tk, bounded=bounded)
    total = B * G * nq

    def blk(s):
        return s // (G * nq), (s // nq) % G, s % nq

    def q_map(s):
        b, g, i = blk(jnp.minimum(s, total - 1))
        return b, g, i

    def k_map(s):
        b, g, _ = blk(jnp.minimum(s, total - 1))
        return b, g, 0, 0

    def v_map(s):
        b, g, _ = blk(jnp.maximum(s - 1, 0))
        return b, g, 0, 0

    def o_map(s):
        b, g, i = blk(jnp.maximum(s - 1, 0))
        return b, i, g

    in_specs = [pl.BlockSpec((1, hs * LANES // pack, tq), q_map),
                pl.BlockSpec((1, n_kv, L, LANES), k_map, pipeline_mode=pl.Buffered(1)),
                pl.BlockSpec((1, n_kv, dv, L), v_map, pipeline_mode=pl.Buffered(1))]
    args = [qt, k, vt]
    if bounded:
        in_specs.append(pl.BlockSpec((1, n_kv, 8, LANES), k_map))
        args.append(kmax)
    stage = pltpu.VMEM((n_chunks, tk, width), BF16 if bounded else F32)
    stat = pltpu.VMEM((8, width), F32)
    return pl.pallas_call(
        kern, grid=(total + 1,), in_specs=in_specs,
        out_specs=pl.BlockSpec((1, tq, hs * dv), o_map),
        out_shape=jax.ShapeDtypeStruct((B, L, H * dv), BF16),
        scratch_shapes=[stage, stage, stat, stat],
        compiler_params=_cparams(("arbitrary",)), name=name + ("_bounded" if bounded else ""),
    )(*args)


ATTN_BOUND_LIMIT = 50.0


def _attention(q, k, vt, *, n_kv, n_rep, pack, tq, name):
    B, L, _ = q.shape
    hw = LANES // pack
    qn2 = jnp.sum(jnp.square(q.astype(F32)).reshape(B, L, -1, hw), axis=-1)
    kn2 = jnp.sum(jnp.square(k.astype(F32)), axis=-1) * (1.0 / pack)
    kmax = jnp.sqrt(jnp.max(kn2, axis=-1))
    bound = jnp.sqrt(jnp.max(qn2)) * jnp.max(kmax)
    kmax_t = jnp.broadcast_to(kmax[:, :, None, None], kmax.shape + (8, LANES))
    qt = q.transpose(0, 2, 1)
    call = functools.partial(_attn_call, n_kv=n_kv, n_rep=n_rep, pack=pack, tq=tq, name=name)
    return lax.cond(bound < ATTN_BOUND_LIMIT,
                    lambda: call(qt, k, vt, kmax_t), lambda: call(qt, k, vt, None))


def _hconv_kernel(x_ref, xp_ref, xn_ref, w_ref, b_ref, v_ref, x1_ref, x2_ref):
    i = pl.program_id(1)
    x = x_ref[0]
    tl = x.shape[0]
    prev = jnp.where(i > 0, xp_ref[0][7:8, :], 0.0)
    nxt = jnp.where(i < pl.num_programs(1) - 1, xn_ref[0][0:1, :], 0.0)
    r = lax.broadcasted_iota(jnp.int32, x.shape, 0)
    xm = jnp.where(r == 0, prev, pltpu.roll(x, 1, 0))
    xp = jnp.where(r == tl - 1, nxt, pltpu.roll(x, tl - 1, 0))
    uc = xm * w_ref[0:1, :] + x * w_ref[1:2, :] + xp * w_ref[2:3, :] + b_ref[...]
    for r in range(tl // FFT_N2):
        blk = uc[r * FFT_N2:(r + 1) * FFT_N2]
        v_ref[:, r, :] = blk[:, :HY_D]
        x1_ref[:, r, :] = blk[:, HY_D:2 * HY_D]
        x2_ref[:, r, :] = blk[:, 2 * HY_D:]


def _hconv_call(hy_in, w, b):
    B, L, C = hy_in.shape
    tl = TL_HCONV
    nb = tl // 8
    last = L // 8 - 1
    nt = L // tl
    rows = tl // FFT_N2
    out = jax.ShapeDtypeStruct((FFT_N2, B * L // FFT_N2, HY_D), F32)
    ospec = pl.BlockSpec((FFT_N2, rows, HY_D), lambda b_, i: (0, b_ * nt + i, 0))
    return pl.pallas_call(
        _hconv_kernel, grid=(B, L // tl),
        in_specs=[pl.BlockSpec((1, tl, C), lambda b_, i: (b_, i, 0)),
                  pl.BlockSpec((1, 8, C), lambda b_, i: (b_, jnp.maximum(i * nb - 1, 0), 0)),
                  pl.BlockSpec((1, 8, C), lambda b_, i: (b_, jnp.minimum((i + 1) * nb, last), 0)),
                  pl.BlockSpec((3, C), lambda b_, i: (0, 0)),
                  pl.BlockSpec((1, C), lambda b_, i: (0, 0))],
        out_specs=[ospec, ospec, ospec], out_shape=[out, out, out],
        compiler_params=_cparams(("parallel", "parallel")), name="hy_conv3",
    )(hy_in, hy_in, hy_in, w, b)


def _split_bf16(a):
    hi = a.astype(BF16)
    return hi, (a - hi.astype(F32)).astype(BF16)


def _dot3(a, w_hi, w_lo):
    a_hi, a_lo = _split_bf16(a)
    dot = functools.partial(jnp.dot, preferred_element_type=F32)
    return dot(a_hi, w_hi) + dot(a_lo, w_hi) + dot(a_hi, w_lo)


def _filt_kernel(z_ref, w1h_ref, w1l_ref, b1_ref, f1_ref, w2h_ref, w2l_ref, b2_ref, f2_ref,
                 w3h_ref, w3l_ref, win_ref, gaf_ref, o_ref):
    h = jnp.sin(f1_ref[...] * (_dot3(z_ref[...], w1h_ref[...], w1l_ref[...]) + b1_ref[...]))
    h = jnp.sin(f2_ref[...] * (_dot3(h, w2h_ref[...], w2l_ref[...]) + b2_ref[...]))
    k = _dot3(h, w3h_ref[...], w3l_ref[...])
    half = k.shape[0]
    wcols = 4 * HY_D
    per_half = half // FFT_N1
    n1 = lax.broadcasted_iota(jnp.int32, (half, 2 * HY_D), 0) % FFT_N1
    for s in range(2):
        ks = k[:, s * wcols:(s + 1) * wcols]
        kk = jnp.where(n1 >= FFT_N1 // 2, ks[:, 2 * HY_D:], ks[:, :2 * HY_D])
        win = win_ref[s * half:(s + 1) * half, :]
        filt = (kk * jnp.concatenate([win, win], axis=-1)).astype(BF16)
        for j in range(per_half):
            o_ref[s * per_half + j] = jnp.dot(gaf_ref[s * per_half + j],
                                              filt[j * FFT_N1:(j + 1) * FFT_N1],
                                              preferred_element_type=F32)


def _filt_call(z_pack, fw, win_perm, gaf):
    tp = 2 * FILT_HALF
    n = win_perm.shape[0]
    to = tp // FFT_N1
    M = gaf.shape[1]

    def full(a):
        return pl.BlockSpec(a.shape, lambda i: (0,) * a.ndim)

    return pl.pallas_call(
        _filt_kernel, grid=(n // tp,),
        in_specs=[pl.BlockSpec((FILT_HALF, z_pack.shape[1]), lambda i: (i, 0))]
        + [full(a) for a in fw] + [pl.BlockSpec((tp, HY_D), lambda i: (i, 0)),
                                   pl.BlockSpec((to, M, FFT_N1), lambda i: (i, 0, 0))],
        out_specs=pl.BlockSpec((to, M, 2 * HY_D), lambda i: (i, 0, 0)),
        out_shape=jax.ShapeDtypeStruct((n // FFT_N1, M, 2 * HY_D), F32),
        compiler_params=_cparams(("parallel",)), name="hy_filter",
    )(z_pack, *fw, win_perm, gaf)


def _bm_kernel(g_ref, x_ref, o_ref, *, to):
    for t in range(to):
        o_ref[t] = jnp.dot(g_ref[t], x_ref[t].astype(BF16), preferred_element_type=F32)


def _bm_call(g, x, name, to=8):
    O, K, N = x.shape
    M = g.shape[1]
    return pl.pallas_call(
        functools.partial(_bm_kernel, to=to), grid=(O // to,),
        in_specs=[pl.BlockSpec((to, M, K), lambda i: (i, 0, 0)),
                  pl.BlockSpec((to, K, N), lambda i: (i, 0, 0))],
        out_specs=pl.BlockSpec((to, M, N), lambda i: (i, 0, 0)),
        out_shape=jax.ShapeDtypeStruct((O, M, N), F32),
        compiler_params=_cparams(("parallel",)), name=name,
    )(g, x)


def _gather_ri(x_ref, j):
    return jnp.concatenate([x_ref[:, 0, j, :], x_ref[:, 1, j, :]], axis=0).astype(BF16)


def _gather_spec(a, to):
    return pl.BlockSpec((a.shape[0], 2, to, a.shape[3]), lambda i: (0, 0, i, 0))


def _filtb_kernel(mb_ref, x_ref, o_ref, *, to):
    for j in range(to):
        o_ref[j] = jnp.dot(mb_ref[...], _gather_ri(x_ref, j),
                           preferred_element_type=F32).astype(o_ref.dtype)


def _filtb_call(mb, ka, to=8):
    O, _, P, N = ka.shape
    return pl.pallas_call(
        functools.partial(_filtb_kernel, to=to), grid=(P // to,),
        in_specs=[pl.BlockSpec(mb.shape, lambda i: (0, 0)), _gather_spec(ka, to)],
        out_specs=pl.BlockSpec((to, 2 * O, N), lambda i: (i, 0, 0)),
        out_shape=jax.ShapeDtypeStruct((P, 2 * O, N), BF16),
        compiler_params=_cparams(("parallel",)), name="hy_fft_filt_b",
    )(mb, ka)


def _convb_kernel(mb_ref, gc_ref, x_ref, kf_ref, o_ref, *, to):
    h = FFT_N2
    for j in range(to):
        xs = jnp.dot(mb_ref[...], _gather_ri(x_ref, j), preferred_element_type=F32)
        xr, xi = xs[:h], xs[h:]
        kr, ki = kf_ref[j, :h, :].astype(F32), kf_ref[j, h:, :].astype(F32)
        ys = jnp.concatenate([xr * kr - xi * ki, xr * ki + xi * kr], axis=0).astype(BF16)
        o_ref[j] = jnp.dot(gc_ref[j], ys, preferred_element_type=F32)


def _convb_call(mb, gc, a, kf, order, to=8):
    O, _, P, N = a.shape
    return pl.pallas_call(
        functools.partial(_convb_kernel, to=to), grid=(P // to,),
        in_specs=[pl.BlockSpec(mb.shape, lambda i: (0, 0)),
                  pl.BlockSpec((to, 2 * O, 2 * O), lambda i: (i, 0, 0)),
                  _gather_spec(a, to),
                  pl.BlockSpec((to, 2 * O, N), lambda i: (i, 0, order))],
        out_specs=pl.BlockSpec((to, 2 * O, N), lambda i: (i, 0, 0)),
        out_shape=jax.ShapeDtypeStruct((P, 2 * O, N), F32),
        compiler_params=_cparams(("parallel",)), name="hy_spec_mul",
    )(mb, gc, a, kf)


def _convd_kernel(md_ref, c_ref, g_ref, u_ref, s_ref, *rest, to, chain):
    if chain:
        ga_ref, z_ref, a_ref = rest
    else:
        (o_ref,) = rest
    for j in range(to):
        y = jnp.dot(md_ref[...], _gather_ri(c_ref, j), preferred_element_type=F32)
        z = g_ref[j] * (y + u_ref[j] * s_ref[...])
        if chain:
            z_ref[j] = z
            a_ref[j] = jnp.dot(ga_ref[j], z.astype(BF16), preferred_element_type=F32)
        else:
            o_ref[:, j, :] = z


def _convd_call(md, c, gate, u, skip, ga=None, to=8):
    O, _, P, N = c.shape
    R = md.shape[0]
    tspec = pl.BlockSpec((to, R, N), lambda i: (i, 0, 0))
    in_specs = [pl.BlockSpec(md.shape, lambda i: (0, 0)), _gather_spec(c, to), tspec, tspec,
                pl.BlockSpec((1, N), lambda i: (0, 0))]
    args = [md, c, gate, u, skip.reshape(1, N)]
    if ga is not None:
        M = ga.shape[1]
        in_specs.append(pl.BlockSpec((to, M, R), lambda i: (i, 0, 0)))
        args.append(ga)
        out_specs = [tspec, pl.BlockSpec((to, M, N), lambda i: (i, 0, 0))]
        out_shape = [jax.ShapeDtypeStruct((P, R, N), F32), jax.ShapeDtypeStruct((P, M, N), F32)]
    else:
        out_specs = pl.BlockSpec((R, to, N), lambda i: (0, i, 0))
        out_shape = jax.ShapeDtypeStruct((R, P, N), F32)
    return pl.pallas_call(
        functools.partial(_convd_kernel, to=to, chain=ga is not None), grid=(P // to,),
        in_specs=in_specs, out_specs=out_specs, out_shape=out_shape,
        compiler_params=_cparams(("parallel",)), name="hy_fft_d",
    )(*args)


def _dft_tables():
    n = FFT_N1 * FFT_N2
    k = np.arange(FFT_N1)
    f = np.exp(-2j * np.pi * np.outer(k, k) / FFT_N1)
    t = np.exp(-2j * np.pi * np.outer(k, k) / n)
    return f, t, n


def _dft_matrices():
    f, t, n = _dft_tables()
    fr, fi = jnp.asarray(f.real, F32), jnp.asarray(f.imag, F32)
    tr, ti = jnp.asarray(t.real, F32), jnp.asarray(t.imag, F32)
    half = FFT_N1 // 2
    er = fr[None] * tr[:, :, None] - fi[None] * ti[:, :, None]
    ei = fr[None] * ti[:, :, None] + fi[None] * tr[:, :, None]
    ga = jnp.concatenate([jnp.concatenate([er[:, :, :half], -ei[:, :, :half]], axis=2),
                          jnp.concatenate([ei[:, :, :half], er[:, :, :half]], axis=2)], axis=1)
    gaf = jnp.concatenate([er, ei], axis=1)
    mb = jnp.concatenate([jnp.concatenate([fr, -fi], axis=1),
                          jnp.concatenate([fi, fr], axis=1)], axis=0)
    tct = jnp.transpose(tr)[:, :, None]
    tst = -jnp.transpose(ti)[:, :, None]
    gr = tct * fr[None] - tst * (-fi[None])
    gi = tct * (-fi[None]) + tst * fr[None]
    gc = jnp.concatenate([jnp.concatenate([gr, -gi], axis=2),
                          jnp.concatenate([gi, gr], axis=2)], axis=1)
    hr, hi = fr[:half] / n, -fi[:half] / n
    md = jnp.concatenate([jnp.concatenate([hr, -hi], axis=1),
                          jnp.concatenate([hi, hr], axis=1)], axis=0)
    return (ga.astype(BF16), gaf.astype(BF16), mb.astype(BF16), gc.astype(BF16), md.astype(BF16))


def _hyena_positions(L):
    p = FFT_N2 * np.arange(FFT_N1)[None, :] + np.arange(FFT_N2)[:, None]
    pos = np.where(p < L, p, 2 * L - 1 - p).reshape(2 * L, 1).astype(np.float64)
    t = pos / (L - 1)
    w = 2.0 * math.pi * pos / L
    f = np.linspace(1e-4, HY_BANDS - 1, HY_BANDS)[None, :]
    z = np.concatenate([t, np.cos(f * w), -np.sin(f * w),
                        np.zeros((2 * L, FILT_FEAT - HY_EMB))], axis=-1)
    z_pack = (z.reshape(-1, 2, FILT_HALF, FILT_FEAT).transpose(0, 2, 1, 3)
              .reshape(-1, 2 * FILT_FEAT))
    max_decay = math.log(HY_DECAY_TARGET) / HY_FAST_DECAY_PCT
    min_decay = math.log(HY_DECAY_TARGET) / HY_SLOW_DECAY_PCT
    deltas = jnp.linspace(min_decay, max_decay, HY_D, dtype=F32)
    window = jnp.exp(-jnp.asarray(t, F32) * jnp.abs(deltas)[None, :])
    return jnp.asarray(z_pack, F32), window


def _filter_weights(w1, b1, f1, w2, b2, f2, w3):
    def bd(w):
        z = jnp.zeros_like(w)
        return jnp.concatenate([jnp.concatenate([w, z], axis=1),
                                jnp.concatenate([z, w], axis=1)], axis=0)

    def twice(v):
        return jnp.concatenate([v, v])[None]

    w1 = jnp.pad(w1, ((0, FILT_FEAT - w1.shape[0]), (0, 0)))
    w3 = w3.reshape(-1, 2, 2, HY_D).transpose(0, 2, 1, 3).reshape(-1, 4 * HY_D)
    return (*_split_bf16(bd(w1)), twice(b1), twice(f1), *_split_bf16(bd(w2)), twice(b2), twice(f2),
            *_split_bf16(bd(w3)))


def _hyena_layer(hy_in, conv_w, conv_b, fw, skip, z_perm, win_perm, mats):
    B, L, _ = hy_in.shape
    ga, gaf, mb, gc, md = mats
    v, x1, x2 = _hconv_call(hy_in, conv_w, conv_b[None])
    ka = _filt_call(z_perm, fw, win_perm, gaf)
    kf = _filtb_call(mb, ka.reshape(FFT_N2, 2, FFT_N1, 2 * HY_D))

    a = _bm_call(ga, v, "hy_fft_a")
    c = _convb_call(mb, gc, a.reshape(FFT_N2, 2, FFT_N1, HY_D), kf, 0)
    z, a = _convd_call(md, c.reshape(FFT_N1, 2, FFT_N2, HY_D), x1, v, skip[0], ga=ga)
    c = _convb_call(mb, gc, a.reshape(FFT_N2, 2, FFT_N1, HY_D), kf, 1)
    z = _convd_call(md, c.reshape(FFT_N1, 2, FFT_N2, HY_D), x2, z, skip[1])
    return z.reshape(B, L, HY_D)


def _kout_kernel(x_ref, yh_ref, yg_ref, ym_ref, gh_ref, gg_ref, gm_ref, w_ref, gpost_ref, gffn_ref,
                 xo_ref, h_ref):
    a = _rms(yh_ref[0], gh_ref[...]).astype(BF16)
    b = _rms(yg_ref[0].astype(F32), gg_ref[...]).astype(BF16)
    c = _rms(ym_ref[0].astype(F32), gm_ref[...]).astype(BF16)
    o1 = HY_D
    o2 = o1 + GQA_HEADS * GQA_HEAD_DIM
    y = (jnp.dot(a, w_ref[:o1, :], preferred_element_type=F32)
         + jnp.dot(b, w_ref[o1:o2, :], preferred_element_type=F32)
         + jnp.dot(c, w_ref[o2:, :], preferred_element_type=F32))
    xo = x_ref[0] + _rms(y, gpost_ref[...])
    xo_ref[0] = xo
    h_ref[0] = _rms(xo, gffn_ref[...]).astype(BF16)


def _kout_call(x, yh, yg, ym, gh, gg, gm, w_p, gpost, gffn):
    B, L, D = x.shape
    tm = TM_OUT

    def rows(a):
        return pl.BlockSpec((1, tm, a.shape[2]), lambda b, i: (b, i, 0))

    def full(a):
        return pl.BlockSpec(a.shape, lambda b, i: (0,) * a.ndim)

    return pl.pallas_call(
        _kout_kernel, grid=(B, L // tm),
        in_specs=[rows(x), rows(yh), rows(yg), rows(ym), full(gh), full(gg), full(gm), full(w_p),
                  full(gpost), full(gffn)],
        out_specs=[rows(x), rows(x)],
        out_shape=[jax.ShapeDtypeStruct((B, L, D), F32), jax.ShapeDtypeStruct((B, L, D), BF16)],
        compiler_params=_cparams(("parallel", "parallel")), name="out_proj",
    )(x, yh, yg, ym, gh, gg, gm, w_p, gpost, gffn)


HALO = 16


def _ffn_kernel(h_ref, hp_ref, hn_ref, x_ref, wup_ref, cw_ref, cb_ref, wd_ref, gpost_ref,
                o_ref, act_ref):
    i = pl.program_id(1)
    tm = h_ref.shape[1]
    prev = jnp.where(i > 0, hp_ref[0], jnp.zeros_like(hp_ref[0]))
    nxt = jnp.where(i < pl.num_programs(1) - 1, hn_ref[0], jnp.zeros_like(hn_ref[0]))
    he = jnp.concatenate([prev, h_ref[0], nxt], axis=0)
    ext = tm + 2 * HALO
    tf = TF_FFN

    def conv(c0):
        up = jnp.dot(he, wup_ref[:, c0:c0 + tf], preferred_element_type=F32)
        um = pltpu.roll(up, 1, 0)[HALO:HALO + tm]
        upl = pltpu.roll(up, ext - 1, 0)[HALO:HALO + tm]
        return (um * cw_ref[0:1, c0:c0 + tf] + up[HALO:HALO + tm] * cw_ref[1:2, c0:c0 + tf]
                + upl * cw_ref[2:3, c0:c0 + tf] + cb_ref[:, c0:c0 + tf])

    for j in range(D_FF // tf):
        g = conv(j * tf)
        u = conv(D_FF + j * tf)
        gelu = 0.5 * g * (1.0 + jnp.tanh(math.sqrt(2.0 / math.pi) * (g + 0.044715 * (g * g * g))))
        act_ref[:, j * tf:(j + 1) * tf] = (gelu * u).astype(BF16)
    f = jnp.dot(act_ref[...], wd_ref[...], preferred_element_type=F32)
    o_ref[0] = x_ref[0] + _rms(f, gpost_ref[...])


def _ffn_call(h, x, w_up, cw, cb, w_down, gpost):
    B, L, D = x.shape
    tm = TM_FFN
    nb = tm // HALO
    last = L // HALO - 1

    def resident(a):
        return pl.BlockSpec(a.shape, lambda b, i: (0,) * a.ndim, pipeline_mode=pl.Buffered(1))

    cb = cb[None]
    return pl.pallas_call(
        _ffn_kernel, grid=(B, L // tm),
        in_specs=[pl.BlockSpec((1, tm, D), lambda b, i: (b, i, 0)),
                  pl.BlockSpec((1, HALO, D), lambda b, i: (b, jnp.maximum(i * nb - 1, 0), 0)),
                  pl.BlockSpec((1, HALO, D), lambda b, i: (b, jnp.minimum((i + 1) * nb, last), 0)),
                  pl.BlockSpec((1, tm, D), lambda b, i: (b, i, 0)),
                  resident(w_up), resident(cw), resident(cb), resident(w_down), resident(gpost)],
        out_specs=pl.BlockSpec((1, tm, D), lambda b, i: (b, i, 0)),
        out_shape=jax.ShapeDtypeStruct((B, L, D), F32),
        scratch_shapes=[pltpu.VMEM((tm, D_FF), BF16)],
        compiler_params=_cparams(("parallel", "parallel")), name="conv_ffn",
    )(h, h, h, x, w_up, cw, cb, w_down, gpost)


def _axial_tables(L, rot_dim):
    pos = np.arange(L)
    n_axis = rot_dim // 4
    inv = ROPE_THETA ** (-np.arange(n_axis) / n_axis)
    ang = np.concatenate([(pos // GRID_W)[:, None] * inv, (pos % GRID_W)[:, None] * inv], axis=-1)
    return jnp.asarray(np.cos(ang), F32), jnp.asarray(np.sin(ang), F32)


def _rope_tables(L):
    def lanes(parts):
        used = sum(p.shape[1] for p in parts)
        return jnp.concatenate(parts + [jnp.zeros((L, LANES - used), F32)], axis=1)

    cg, sg = _axial_tables(L, GQA_HEAD_DIM)
    tg = (lanes([cg, cg, cg, cg]), lanes([sg, sg, sg, sg]))
    cm, sm = _axial_tables(L, MLA_ROPE_DIM)
    nope0 = jnp.zeros((L, MLA_NOPE_DIM), F32)
    nope1 = jnp.ones((L, MLA_NOPE_DIM), F32)
    tmk = (lanes([nope0, cm, cm]), lanes([nope0, sm, sm]))
    sc = (MLA_NOPE_DIM + MLA_ROPE_DIM) ** -0.5 * LOG2E
    tmq = (lanes([nope1, cm, cm]) * sc, tmk[1] * sc)
    return tg, tmq, tmk


def _partner(w, half, sign=-1.0):
    return jnp.concatenate([sign * w[..., half:], w[..., :half]], axis=-1)


def _pad_heads(w, n_heads, width):
    k = w.shape[0]
    return jnp.pad(w.reshape(k, n_heads, width), ((0, 0), (0, 0), (0, LANES - width))).reshape(
        k, n_heads * LANES)


def kernel(x, mix_pre_norm, w_in, hy_conv_w, hy_conv_b, hy_filt_w1, hy_filt_b1, hy_filt_freq1,
           hy_filt_w2, hy_filt_b2, hy_filt_freq2, hy_filt_w3, hy_skip, gqa_q_norm, gqa_k_norm,
           mla_q_a_norm, mla_w_uq, mla_kv_a_norm, mla_w_ukv, hy_out_norm, gqa_out_norm,
           mla_out_norm, w_out, mix_post_norm, ffn_pre_norm, w_up, ffn_conv_w, ffn_conv_b,
           w_down, ffn_post_norm):
    B, L, D = x.shape
    assert B == 2 and 2 * L == FFT_N1 * FFT_N2 and D == D_MODEL
    depth = w_in.shape[0]
    tg, tmq, tmk = _rope_tables(L)
    z_perm, win_perm = _hyena_positions(L)
    mats = _dft_matrices()

    for l in range(depth):
        wl = w_in[l]
        hd, hh, rh = GQA_HEAD_DIM, GQA_HEAD_DIM // 2, MLA_ROPE_DIM // 2
        o1 = 768
        o2 = o1 + GQA_HEADS * hd
        o3 = o2 + GQA_KV_HEADS * hd
        o4 = o3 + GQA_KV_HEADS * hd
        o5 = o4 + MLA_Q_RANK
        o6 = o5 + MLA_KV_RANK
        wq = wl[:, o1:o2].reshape(D, GQA_HEADS, hd)
        wk = wl[:, o2:o3].reshape(D, GQA_KV_HEADS, hd)
        wkr = wl[:, o6:]
        pe_pad = ((0, 0), (MLA_NOPE_DIM, LANES - MLA_NOPE_DIM - MLA_ROPE_DIM))
        win_p = jnp.concatenate(
            [wl[:, :o1],
             wq.reshape(D, -1), _partner(wq, hh).reshape(D, -1),
             wk.reshape(D, -1), _partner(wk, hh).reshape(D, -1),
             wl[:, o3:o6], jnp.pad(wkr, pe_pad), jnp.pad(_partner(wkr, rh), pe_pad)],
            axis=1).astype(BF16)
        gq_gain = gqa_q_norm[l] * (hd ** -0.5 * LOG2E)
        def pair(g):
            return jnp.tile(g, 2)[None], jnp.tile(_partner(g, hh, 1.0), 2)[None]

        gq, gk = pair(gq_gain), pair(gqa_k_norm[l])
        wuq = mla_w_uq[l].reshape(MLA_Q_RANK, MLA_HEADS, MLA_NOPE_DIM + MLA_ROPE_DIM)
        wuq_pe = jnp.pad(_partner(wuq[:, :, MLA_NOPE_DIM:], rh),
                         ((0, 0), (0, 0), (MLA_NOPE_DIM, 0)))
        wuq_p = jnp.concatenate(
            [_pad_heads(mla_w_uq[l], MLA_HEADS, MLA_NOPE_DIM + MLA_ROPE_DIM),
             _pad_heads(wuq_pe.reshape(MLA_Q_RANK, -1), MLA_HEADS, MLA_NOPE_DIM + MLA_ROPE_DIM)],
            axis=1).astype(BF16)
        wukv = mla_w_ukv[l].reshape(MLA_KV_RANK, MLA_HEADS, MLA_NOPE_DIM + MLA_V_DIM)
        wukvk_p = _pad_heads(wukv[:, :, :MLA_NOPE_DIM].reshape(MLA_KV_RANK, -1), MLA_HEADS,
                             MLA_NOPE_DIM).astype(BF16)
        wukvv = wukv[:, :, MLA_NOPE_DIM:].reshape(MLA_KV_RANK, MLA_HEADS * MLA_V_DIM).astype(BF16)

        hy_in, qg, kg, vg, qm, km, vm = _kin_call(
            x, mix_pre_norm[l][None], win_p, gq, gk, tg, mla_q_a_norm[l][None], wuq_p, tmq,
            mla_kv_a_norm[l][None], wukvk_p, wukvv, tmk)

        fw = _filter_weights(hy_filt_w1[l], hy_filt_b1[l], hy_filt_freq1[l], hy_filt_w2[l],
                             hy_filt_b2[l], hy_filt_freq2[l], hy_filt_w3[l])
        y_hy = _hyena_layer(hy_in, hy_conv_w[l], hy_conv_b[l], fw, hy_skip[l], z_perm, win_perm, mats)

        y_gqa = _attention(qg, kg, vg.reshape(B, GQA_KV_HEADS, GQA_HEAD_DIM, L),
                           n_kv=1, n_rep=GQA_HEADS // GQA_KV_HEADS, pack=2, tq=TQ_GQA,
                           name="attn_gqa")
        y_mla = _attention(qm, km, vm.reshape(B, MLA_HEADS, MLA_V_DIM, L),
                           n_kv=2, n_rep=1, pack=1, tq=TQ_MLA, name="attn_mla")

        x, h2 = _kout_call(x, y_hy, y_gqa, y_mla, hy_out_norm[l][None], gqa_out_norm[l][None],
                           mla_out_norm[l][None], w_out[l].astype(BF16),
                           mix_post_norm[l][None], ffn_pre_norm[l][None])
        x = _ffn_call(h2, x, w_up[l].astype(BF16), ffn_conv_w[l], ffn_conv_b[l],
                      w_down[l].astype(BF16), ffn_post_norm[l][None])
    return x
```

```python
import functools
import math

import numpy as np
import jax
import jax.numpy as jnp
from jax import lax
from jax.experimental import pallas as pl
from jax.experimental.pallas import tpu as pltpu

F32 = jnp.float32
BF16 = jnp.bfloat16

NORM_EPS = 1e-6
ROPE_THETA = 10000.0
GRID_W = 64
LOG2E = math.log2(math.e)

D_MODEL = 1024
HY_D = 256
HY_COLS = 3 * HY_D
HY_EMB = 33
HY_BANDS = 16
HY_DECAY_TARGET = 1e-2
HY_FAST_DECAY_PCT = 0.3
HY_SLOW_DECAY_PCT = 1.5
GQA_HEADS = 8
GQA_KV_HEADS = 2
GQA_HEAD_DIM = 64
MLA_HEADS = 4
MLA_Q_RANK = 256
MLA_KV_RANK = 128
MLA_NOPE_DIM = 64
MLA_ROPE_DIM = 32
MLA_V_DIM = 64
D_FF = 2816

LANES = 128
FFT_N1 = 128
FFT_N2 = 128

TM_IN = 512
TK_ATTN = 512
TQ_GQA = 128
TQ_MLA = 256
TM_OUT = 512
TM_FFN = 512
TF_FFN = 256
TL_HCONV = 1024
FILT_HALF = 512
FILT_FEAT = 64
VMEM_LIMIT = 56 * 1024 * 1024


def _cparams(sem):
    return pltpu.CompilerParams(dimension_semantics=sem, vmem_limit_bytes=VMEM_LIMIT)


def _rms(x, g):
    return x * lax.rsqrt(jnp.mean(x * x, axis=-1, keepdims=True) + NORM_EPS) * g


def _kin_kernel(x_ref, gpre_ref, win_ref, gqn_ref, gqs_ref, gkn_ref, gks_ref, cg_ref, sg_ref,
                mqn_ref, wuq_ref, cmq_ref, smq_ref,
                mkvn_ref, wukvk_ref, wukvv_ref, cmk_ref, smk_ref,
                hy_ref, qg_ref, kg_ref, vg_ref, qm_ref, km_ref, vm_ref):
    x = x_ref[0]
    h = _rms(x, gpre_ref[...]).astype(BF16)
    cur = [0]

    def proj(n):
        lo = cur[0]
        cur[0] = lo + n
        return jnp.dot(h, win_ref[:, lo:lo + n], preferred_element_type=F32)

    hy_ref[0] = proj(HY_COLS)

    cg, sg = cg_ref[...], sg_ref[...]
    low = lax.broadcasted_iota(jnp.int32, (x.shape[0], LANES), 1) < GQA_HEAD_DIM

    def head_pairs(n_tiles, gain, gain_sw):
        xa, xb = proj(n_tiles * LANES), proj(n_tiles * LANES)
        ca, sa = cg * gain, sg * gain_sw
        out = []
        for t in range(n_tiles):
            xc, xs = xa[:, t * LANES:(t + 1) * LANES], xb[:, t * LANES:(t + 1) * LANES]
            sq = xc * xc
            tot = jnp.sum(sq, axis=-1, keepdims=True)
            lo = jnp.sum(jnp.where(low, sq, 0.0), axis=-1, keepdims=True)
            ms = jnp.where(low, lo, tot - lo) * (1.0 / GQA_HEAD_DIM)
            out.append((xc * ca + xs * sa) * lax.rsqrt(ms + NORM_EPS))
        return out

    for t, q in enumerate(head_pairs(GQA_HEADS // 2, gqn_ref[...], gqs_ref[...])):
        qg_ref[0, :, t * LANES:(t + 1) * LANES] = q.astype(BF16)
    (kk,) = head_pairs(GQA_KV_HEADS // 2, gkn_ref[...], gks_ref[...])
    ksw = pltpu.roll(kk, GQA_HEAD_DIM, 1)
    kg_ref[0, 0] = jnp.where(low, kk, ksw).astype(BF16)
    kg_ref[0, 1] = jnp.where(low, ksw, kk).astype(BF16)
    vg_ref[0] = proj(LANES).T.astype(BF16)

    cq = _rms(proj(MLA_Q_RANK), mqn_ref[...]).astype(BF16)
    qm = jnp.dot(cq, wuq_ref[...], preferred_element_type=F32)
    cmq, smq = cmq_ref[...], smq_ref[...]
    nq = MLA_HEADS * LANES
    for j in range(MLA_HEADS):
        qm_ref[0, :, j * LANES:(j + 1) * LANES] = (
            qm[:, j * LANES:(j + 1) * LANES] * cmq
            + qm[:, nq + j * LANES:nq + (j + 1) * LANES] * smq).astype(BF16)

    ckv = _rms(proj(MLA_KV_RANK), mkvn_ref[...]).astype(BF16)
    kpe = proj(LANES) * cmk_ref[...] + proj(LANES) * smk_ref[...]
    kn = jnp.dot(ckv, wukvk_ref[...], preferred_element_type=F32)
    for j in range(MLA_HEADS):
        km_ref[0, j] = (kn[:, j * LANES:(j + 1) * LANES] + kpe).astype(BF16)
    vm = jnp.dot(ckv, wukvv_ref[...], preferred_element_type=F32)
    for j in range(vm.shape[1] // LANES):
        vm_ref[0, j * LANES:(j + 1) * LANES, :] = vm[:, j * LANES:(j + 1) * LANES].T.astype(BF16)


def _kin_call(x, gpre, win_p, gq, gk, tg, mqn, wuq_p, tmq, mkvn, wukvk_p, wukvv, tmk):
    B, L, D = x.shape
    tm = TM_IN
    nt = L // tm

    def full(a):
        return pl.BlockSpec(a.shape, lambda b, i: (0,) * a.ndim, pipeline_mode=pl.Buffered(1))

    def rows(w):
        return pl.BlockSpec((tm, w), lambda b, i: (i, 0))

    in_specs = [pl.BlockSpec((1, tm, D), lambda b, i: (b, i, 0)), full(gpre), full(win_p),
                full(gq[0]), full(gq[1]), full(gk[0]), full(gk[1]), rows(LANES), rows(LANES),
                full(mqn), full(wuq_p), rows(LANES), rows(LANES),
                full(mkvn), full(wukvk_p), full(wukvv), rows(LANES), rows(LANES)]
    gv_rows = GQA_KV_HEADS * GQA_HEAD_DIM
    mv_rows = MLA_HEADS * MLA_V_DIM
    out_shape = [
        jax.ShapeDtypeStruct((B, L, HY_COLS), F32),
        jax.ShapeDtypeStruct((B, L, GQA_HEADS * GQA_HEAD_DIM), BF16),
        jax.ShapeDtypeStruct((B, GQA_KV_HEADS, L, LANES), BF16),
        jax.ShapeDtypeStruct((B, gv_rows, L), BF16),
        jax.ShapeDtypeStruct((B, L, MLA_HEADS * LANES), BF16),
        jax.ShapeDtypeStruct((B, MLA_HEADS, L, LANES), BF16),
        jax.ShapeDtypeStruct((B, mv_rows, L), BF16),
    ]
    out_specs = [
        pl.BlockSpec((1, tm, HY_COLS), lambda b, i: (b, i, 0)),
        pl.BlockSpec((1, tm, GQA_HEADS * GQA_HEAD_DIM), lambda b, i: (b, i, 0)),
        pl.BlockSpec((1, GQA_KV_HEADS, tm, LANES), lambda b, i: (b, 0, i, 0)),
        pl.BlockSpec((1, gv_rows, tm), lambda b, i: (b, 0, i)),
        pl.BlockSpec((1, tm, MLA_HEADS * LANES), lambda b, i: (b, i, 0)),
        pl.BlockSpec((1, MLA_HEADS, tm, LANES), lambda b, i: (b, 0, i, 0)),
        pl.BlockSpec((1, mv_rows, tm), lambda b, i: (b, 0, i)),
    ]
    return pl.pallas_call(
        _kin_kernel, grid=(B, nt), in_specs=in_specs, out_specs=out_specs, out_shape=out_shape,
        compiler_params=_cparams(("parallel", "parallel")), name="in_proj",
    )(x, gpre, win_p, *gq, *gk, *tg, mqn, wuq_p, *tmq, mkvn, wukvk_p, wukvv, *tmk)


def _attn_kernel(q_ref, k_ref, vt_ref, o_ref, sa_ref, sb_ref, ma_ref, mb_ref, *,
                 n_kv, n_rep, pack, tq, n_chunks, tk):
    i = pl.program_id(0)
    cols = n_rep * tq
    width = n_kv * cols
    grp = tk // 8

    @pl.when(i == 0)
    def _():
        sb_ref[...] = jnp.zeros(sb_ref.shape, F32)
        mb_ref[...] = jnp.zeros(mb_ref.shape, F32)

    def step(sw_ref, mw_ref, sr_ref, mr_ref):
        def q_head(h):
            tile, half = divmod(h, pack)
            qt = q_ref[0, tile * LANES:(tile + 1) * LANES, :]
            if pack == 2:
                z = jnp.zeros((LANES // 2, qt.shape[1]), qt.dtype)
                qt = jnp.concatenate([qt[:LANES // 2], z] if half == 0 else [z, qt[LANES // 2:]],
                                     axis=0)
            return qt

        qs = [jnp.concatenate([q_head(a * n_rep + j) for j in range(n_rep)], axis=1)
              for a in range(n_kv)]
        mx = jnp.max(mr_ref[...], axis=0, keepdims=True)
        m = jnp.full((8, width), -jnp.inf, F32)
        l = jnp.zeros((8, width), F32)
        accs = [jnp.zeros((vt_ref.shape[2], cols), F32) for _ in range(n_kv)]
        for c in range(n_chunks):
            st = jnp.concatenate(
                [jnp.dot(k_ref[0, a, c * tk:(c + 1) * tk, :], qs[a], preferred_element_type=F32)
                 for a in range(n_kv)], axis=1)
            sw_ref[c] = st
            m = jnp.maximum(m, jnp.max(st.reshape(grp, 8, width), axis=0))
            p = jnp.exp2(sr_ref[c] - mx)
            l = l + jnp.sum(p.reshape(grp, 8, width), axis=0)
            pb = p.astype(BF16)
            for a in range(n_kv):
                accs[a] = accs[a] + jnp.dot(vt_ref[0, a, :, c * tk:(c + 1) * tk],
                                            pb[:, a * cols:(a + 1) * cols],
                                            preferred_element_type=F32)
        mw_ref[...] = m
        ls = jnp.sum(l, axis=0, keepdims=True)
        heads = []
        for a in range(n_kv):
            oa = accs[a] / ls[:, a * cols:(a + 1) * cols]
            heads += [oa[:, j * tq:(j + 1) * tq] for j in range(n_rep)]
        o_ref[0] = jnp.concatenate(heads, axis=0).T.astype(BF16)

    @pl.when(i % 2 == 0)
    def _():
        step(sa_ref, ma_ref, sb_ref, mb_ref)

    @pl.when(i % 2 == 1)
    def _():
        step(sb_ref, mb_ref, sa_ref, ma_ref)


def _attn_call(qt, k, vt, *, n_kv, n_rep, pack, tq, name):
    B, hq, L = qt.shape
    H = hq * pack // LANES
    hkv, dv = vt.shape[1], vt.shape[2]
    G = hkv // n_kv
    hs = n_kv * n_rep
    tk = TK_ATTN
    n_chunks = L // tk
    nq = L // tq
    width = hs * tq
    kern = functools.partial(_attn_kernel, n_kv=n_kv, n_rep=n_rep, pack=pack, tq=tq,
                             n_chunks=n_chunks, tk=tk)
    total = B * G * nq

    def blk(s):
        return s // (G * nq), (s // nq) % G, s % nq

    def q_map(s):
        b, g, i = blk(jnp.minimum(s, total - 1))
        return b, g, i

    def k_map(s):
        b, g, _ = blk(jnp.minimum(s, total - 1))
        return b, g, 0, 0

    def v_map(s):
        b, g, _ = blk(jnp.maximum(s - 1, 0))
        return b, g, 0, 0

    def o_map(s):
        b, g, i = blk(jnp.maximum(s - 1, 0))
        return b, i, g

    return pl.pallas_call(
        kern, grid=(total + 1,),
        in_specs=[pl.BlockSpec((1, hs * LANES // pack, tq), q_map),
                  pl.BlockSpec((1, n_kv, L, LANES), k_map, pipeline_mode=pl.Buffered(1)),
                  pl.BlockSpec((1, n_kv, dv, L), v_map, pipeline_mode=pl.Buffered(1))],
        out_specs=pl.BlockSpec((1, tq, hs * dv), o_map),
        out_shape=jax.ShapeDtypeStruct((B, L, H * dv), BF16),
        scratch_shapes=[pltpu.VMEM((n_chunks, tk, width), F32), pltpu.VMEM((n_chunks, tk, width), F32),
                        pltpu.VMEM((8, width), F32), pltpu.VMEM((8, width), F32)],
        compiler_params=_cparams(("arbitrary",)), name=name,
    )(qt, k, vt)


def _hconv_kernel(x_ref, xp_ref, xn_ref, w_ref, b_ref, v_ref, x1_ref, x2_ref):
    i = pl.program_id(1)
    x = x_ref[0]
    tl = x.shape[0]
    prev = jnp.where(i > 0, xp_ref[0][7:8, :], 0.0)
    nxt = jnp.where(i < pl.num_programs(1) - 1, xn_ref[0][0:1, :], 0.0)
    r = lax.broadcasted_iota(jnp.int32, x.shape, 0)
    xm = jnp.where(r == 0, prev, pltpu.roll(x, 1, 0))
    xp = jnp.where(r == tl - 1, nxt, pltpu.roll(x, tl - 1, 0))
    uc = xm * w_ref[0:1, :] + x * w_ref[1:2, :] + xp * w_ref[2:3, :] + b_ref[...]
    for r in range(tl // FFT_N2):
        blk = uc[r * FFT_N2:(r + 1) * FFT_N2]
        v_ref[:, r, :] = blk[:, :HY_D]
        x1_ref[:, r, :] = blk[:, HY_D:2 * HY_D]
        x2_ref[:, r, :] = blk[:, 2 * HY_D:]


def _hconv_call(hy_in, w, b):
    B, L, C = hy_in.shape
    tl = TL_HCONV
    nb = tl // 8
    last = L // 8 - 1
    nt = L // tl
    rows = tl // FFT_N2
    out = jax.ShapeDtypeStruct((FFT_N2, B * L // FFT_N2, HY_D), F32)
    ospec = pl.BlockSpec((FFT_N2, rows, HY_D), lambda b_, i: (0, b_ * nt + i, 0))
    return pl.pallas_call(
        _hconv_kernel, grid=(B, L // tl),
        in_specs=[pl.BlockSpec((1, tl, C), lambda b_, i: (b_, i, 0)),
                  pl.BlockSpec((1, 8, C), lambda b_, i: (b_, jnp.maximum(i * nb - 1, 0), 0)),
                  pl.BlockSpec((1, 8, C), lambda b_, i: (b_, jnp.minimum((i + 1) * nb, last), 0)),
                  pl.BlockSpec((3, C), lambda b_, i: (0, 0)),
                  pl.BlockSpec((1, C), lambda b_, i: (0, 0))],
        out_specs=[ospec, ospec, ospec], out_shape=[out, out, out],
        compiler_params=_cparams(("parallel", "parallel")), name="hy_conv3",
    )(hy_in, hy_in, hy_in, w, b)


def _split_bf16(a):
    hi = a.astype(BF16)
    return hi, (a - hi.astype(F32)).astype(BF16)


def _dot3(a, w_hi, w_lo):
    a_hi, a_lo = _split_bf16(a)
    dot = functools.partial(jnp.dot, preferred_element_type=F32)
    return dot(a_hi, w_hi) + dot(a_lo, w_hi) + dot(a_hi, w_lo)


def _filt_kernel(z_ref, w1h_ref, w1l_ref, b1_ref, f1_ref, w2h_ref, w2l_ref, b2_ref, f2_ref,
                 w3h_ref, w3l_ref, win_ref, gaf_ref, o_ref):
    h = jnp.sin(f1_ref[...] * (_dot3(z_ref[...], w1h_ref[...], w1l_ref[...]) + b1_ref[...]))
    h = jnp.sin(f2_ref[...] * (_dot3(h, w2h_ref[...], w2l_ref[...]) + b2_ref[...]))
    k = _dot3(h, w3h_ref[...], w3l_ref[...])
    half = k.shape[0]
    wcols = 4 * HY_D
    per_half = half // FFT_N1
    n1 = lax.broadcasted_iota(jnp.int32, (half, 2 * HY_D), 0) % FFT_N1
    for s in range(2):
        ks = k[:, s * wcols:(s + 1) * wcols]
        kk = jnp.where(n1 >= FFT_N1 // 2, ks[:, 2 * HY_D:], ks[:, :2 * HY_D])
        win = win_ref[s * half:(s + 1) * half, :]
        filt = (kk * jnp.concatenate([win, win], axis=-1)).astype(BF16)
        for j in range(per_half):
            o_ref[s * per_half + j] = jnp.dot(gaf_ref[s * per_half + j],
                                              filt[j * FFT_N1:(j + 1) * FFT_N1],
                                              preferred_element_type=F32)


def _filt_call(z_pack, fw, win_perm, gaf):
    tp = 2 * FILT_HALF
    n = win_perm.shape[0]
    to = tp // FFT_N1
    M = gaf.shape[1]

    def full(a):
        return pl.BlockSpec(a.shape, lambda i: (0,) * a.ndim)

    return pl.pallas_call(
        _filt_kernel, grid=(n // tp,),
        in_specs=[pl.BlockSpec((FILT_HALF, z_pack.shape[1]), lambda i: (i, 0))]
        + [full(a) for a in fw] + [pl.BlockSpec((tp, HY_D), lambda i: (i, 0)),
                                   pl.BlockSpec((to, M, FFT_N1), lambda i: (i, 0, 0))],
        out_specs=pl.BlockSpec((to, M, 2 * HY_D), lambda i: (i, 0, 0)),
        out_shape=jax.ShapeDtypeStruct((n // FFT_N1, M, 2 * HY_D), F32),
        compiler_params=_cparams(("parallel",)), name="hy_filter",
    )(z_pack, *fw, win_perm, gaf)


def _bm_kernel(g_ref, x_ref, o_ref, *, to):
    for t in range(to):
        o_ref[t] = jnp.dot(g_ref[t], x_ref[t].astype(BF16), preferred_element_type=F32)


def _bm_call(g, x, name, to=8):
    O, K, N = x.shape
    M = g.shape[1]
    return pl.pallas_call(
        functools.partial(_bm_kernel, to=to), grid=(O // to,),
        in_specs=[pl.BlockSpec((to, M, K), lambda i: (i, 0, 0)),
                  pl.BlockSpec((to, K, N), lambda i: (i, 0, 0))],
        out_specs=pl.BlockSpec((to, M, N), lambda i: (i, 0, 0)),
        out_shape=jax.ShapeDtypeStruct((O, M, N), F32),
        compiler_params=_cparams(("parallel",)), name=name,
    )(g, x)


def _gather_ri(x_ref, j):
    return jnp.concatenate([x_ref[:, 0, j, :], x_ref[:, 1, j, :]], axis=0).astype(BF16)


def _gather_spec(a, to):
    return pl.BlockSpec((a.shape[0], 2, to, a.shape[3]), lambda i: (0, 0, i, 0))


def _filtb_kernel(mb_ref, x_ref, o_ref, *, to):
    for j in range(to):
        o_ref[j] = jnp.dot(mb_ref[...], _gather_ri(x_ref, j),
                           preferred_element_type=F32).astype(o_ref.dtype)


def _filtb_call(mb, ka, to=8):
    O, _, P, N = ka.shape
    return pl.pallas_call(
        functools.partial(_filtb_kernel, to=to), grid=(P // to,),
        in_specs=[pl.BlockSpec(mb.shape, lambda i: (0, 0)), _gather_spec(ka, to)],
        out_specs=pl.BlockSpec((to, 2 * O, N), lambda i: (i, 0, 0)),
        out_shape=jax.ShapeDtypeStruct((P, 2 * O, N), BF16),
        compiler_params=_cparams(("parallel",)), name="hy_fft_filt_b",
    )(mb, ka)


def _convb_kernel(mb_ref, gc_ref, x_ref, kf_ref, o_ref, *, to):
    h = FFT_N2
    for j in range(to):
        xs = jnp.dot(mb_ref[...], _gather_ri(x_ref, j), preferred_element_type=F32)
        xr, xi = xs[:h], xs[h:]
        kr, ki = kf_ref[j, :h, :].astype(F32), kf_ref[j, h:, :].astype(F32)
        ys = jnp.concatenate([xr * kr - xi * ki, xr * ki + xi * kr], axis=0).astype(BF16)
        o_ref[j] = jnp.dot(gc_ref[j], ys, preferred_element_type=F32)


def _convb_call(mb, gc, a, kf, order, to=8):
    O, _, P, N = a.shape
    return pl.pallas_call(
        functools.partial(_convb_kernel, to=to), grid=(P // to,),
        in_specs=[pl.BlockSpec(mb.shape, lambda i: (0, 0)),
                  pl.BlockSpec((to, 2 * O, 2 * O), lambda i: (i, 0, 0)),
                  _gather_spec(a, to),
                  pl.BlockSpec((to, 2 * O, N), lambda i: (i, 0, order))],
        out_specs=pl.BlockSpec((to, 2 * O, N), lambda i: (i, 0, 0)),
        out_shape=jax.ShapeDtypeStruct((P, 2 * O, N), F32),
        compiler_params=_cparams(("parallel",)), name="hy_spec_mul",
    )(mb, gc, a, kf)


def _convd_kernel(md_ref, c_ref, g_ref, u_ref, s_ref, *rest, to, chain):
    if chain:
        ga_ref, z_ref, a_ref = rest
    else:
        (o_ref,) = rest
    for j in range(to):
        y = jnp.dot(md_ref[...], _gather_ri(c_ref, j), preferred_element_type=F32)
        z = g_ref[j] * (y + u_ref[j] * s_ref[...])
        if chain:
            z_ref[j] = z
            a_ref[j] = jnp.dot(ga_ref[j], z.astype(BF16), preferred_element_type=F32)
        else:
            o_ref[:, j, :] = z


def _convd_call(md, c, gate, u, skip, ga=None, to=8):
    O, _, P, N = c.shape
    R = md.shape[0]
    tspec = pl.BlockSpec((to, R, N), lambda i: (i, 0, 0))
    in_specs = [pl.BlockSpec(md.shape, lambda i: (0, 0)), _gather_spec(c, to), tspec, tspec,
                pl.BlockSpec((1, N), lambda i: (0, 0))]
    args = [md, c, gate, u, skip.reshape(1, N)]
    if ga is not None:
        M = ga.shape[1]
        in_specs.append(pl.BlockSpec((to, M, R), lambda i: (i, 0, 0)))
        args.append(ga)
        out_specs = [tspec, pl.BlockSpec((to, M, N), lambda i: (i, 0, 0))]
        out_shape = [jax.ShapeDtypeStruct((P, R, N), F32), jax.ShapeDtypeStruct((P, M, N), F32)]
    else:
        out_specs = pl.BlockSpec((R, to, N), lambda i: (0, i, 0))
        out_shape = jax.ShapeDtypeStruct((R, P, N), F32)
    return pl.pallas_call(
        functools.partial(_convd_kernel, to=to, chain=ga is not None), grid=(P // to,),
        in_specs=in_specs, out_specs=out_specs, out_shape=out_shape,
        compiler_params=_cparams(("parallel",)), name="hy_fft_d",
    )(*args)


def _dft_tables():
    n = FFT_N1 * FFT_N2
    k = np.arange(FFT_N1)
    f = np.exp(-2j * np.pi * np.outer(k, k) / FFT_N1)
    t = np.exp(-2j * np.pi * np.outer(k, k) / n)
    return f, t, n


def _dft_matrices():
    f, t, n = _dft_tables()
    fr, fi = jnp.asarray(f.real, F32), jnp.asarray(f.imag, F32)
    tr, ti = jnp.asarray(t.real, F32), jnp.asarray(t.imag, F32)
    half = FFT_N1 // 2
    er = fr[None] * tr[:, :, None] - fi[None] * ti[:, :, None]
    ei = fr[None] * ti[:, :, None] + fi[None] * tr[:, :, None]
    ga = jnp.concatenate([jnp.concatenate([er[:, :, :half], -ei[:, :, :half]], axis=2),
                          jnp.concatenate([ei[:, :, :half], er[:, :, :half]], axis=2)], axis=1)
    gaf = jnp.concatenate([er, ei], axis=1)
    mb = jnp.concatenate([jnp.concatenate([fr, -fi], axis=1),
                          jnp.concatenate([fi, fr], axis=1)], axis=0)
    tct = jnp.transpose(tr)[:, :, None]
    tst = -jnp.transpose(ti)[:, :, None]
    gr = tct * fr[None] - tst * (-fi[None])
    gi = tct * (-fi[None]) + tst * fr[None]
    gc = jnp.concatenate([jnp.concatenate([gr, -gi], axis=2),
                          jnp.concatenate([gi, gr], axis=2)], axis=1)
    hr, hi = fr[:half] / n, -fi[:half] / n
    md = jnp.concatenate([jnp.concatenate([hr, -hi], axis=1),
                          jnp.concatenate([hi, hr], axis=1)], axis=0)
    return (ga.astype(BF16), gaf.astype(BF16), mb.astype(BF16), gc.astype(BF16), md.astype(BF16))


def _hyena_positions(L):
    p = FFT_N2 * np.arange(FFT_N1)[None, :] + np.arange(FFT_N2)[:, None]
    pos = np.where(p < L, p, 2 * L - 1 - p).reshape(2 * L, 1).astype(np.float64)
    t = pos / (L - 1)
    w = 2.0 * math.pi * pos / L
    f = np.linspace(1e-4, HY_BANDS - 1, HY_BANDS)[None, :]
    z = np.concatenate([t, np.cos(f * w), -np.sin(f * w),
                        np.zeros((2 * L, FILT_FEAT - HY_EMB))], axis=-1)
    z_pack = (z.reshape(-1, 2, FILT_HALF, FILT_FEAT).transpose(0, 2, 1, 3)
              .reshape(-1, 2 * FILT_FEAT))
    max_decay = math.log(HY_DECAY_TARGET) / HY_FAST_DECAY_PCT
    min_decay = math.log(HY_DECAY_TARGET) / HY_SLOW_DECAY_PCT
    deltas = jnp.linspace(min_decay, max_decay, HY_D, dtype=F32)
    window = jnp.exp(-jnp.asarray(t, F32) * jnp.abs(deltas)[None, :])
    return jnp.asarray(z_pack, F32), window


def _filter_weights(w1, b1, f1, w2, b2, f2, w3):
    def bd(w):
        z = jnp.zeros_like(w)
        return jnp.concatenate([jnp.concatenate([w, z], axis=1),
                                jnp.concatenate([z, w], axis=1)], axis=0)

    def twice(v):
        return jnp.concatenate([v, v])[None]

    w1 = jnp.pad(w1, ((0, FILT_FEAT - w1.shape[0]), (0, 0)))
    w3 = w3.reshape(-1, 2, 2, HY_D).transpose(0, 2, 1, 3).reshape(-1, 4 * HY_D)
    return (*_split_bf16(bd(w1)), twice(b1), twice(f1), *_split_bf16(bd(w2)), twice(b2), twice(f2),
            *_split_bf16(bd(w3)))


def _hyena_layer(hy_in, conv_w, conv_b, fw, skip, z_perm, win_perm, mats):
    B, L, _ = hy_in.shape
    ga, gaf, mb, gc, md = mats
    v, x1, x2 = _hconv_call(hy_in, conv_w, conv_b[None])
    ka = _filt_call(z_perm, fw, win_perm, gaf)
    kf = _filtb_call(mb, ka.reshape(FFT_N2, 2, FFT_N1, 2 * HY_D))

    a = _bm_call(ga, v, "hy_fft_a")
    c = _convb_call(mb, gc, a.reshape(FFT_N2, 2, FFT_N1, HY_D), kf, 0)
    z, a = _convd_call(md, c.reshape(FFT_N1, 2, FFT_N2, HY_D), x1, v, skip[0], ga=ga)
    c = _convb_call(mb, gc, a.reshape(FFT_N2, 2, FFT_N1, HY_D), kf, 1)
    z = _convd_call(md, c.reshape(FFT_N1, 2, FFT_N2, HY_D), x2, z, skip[1])
    return z.reshape(B, L, HY_D)


def _kout_kernel(x_ref, yh_ref, yg_ref, ym_ref, gh_ref, gg_ref, gm_ref, w_ref, gpost_ref, gffn_ref,
                 xo_ref, h_ref):
    a = _rms(yh_ref[0], gh_ref[...]).astype(BF16)
    b = _rms(yg_ref[0].astype(F32), gg_ref[...]).astype(BF16)
    c = _rms(ym_ref[0].astype(F32), gm_ref[...]).astype(BF16)
    o1 = HY_D
    o2 = o1 + GQA_HEADS * GQA_HEAD_DIM
    y = (jnp.dot(a, w_ref[:o1, :], preferred_element_type=F32)
         + jnp.dot(b, w_ref[o1:o2, :], preferred_element_type=F32)
         + jnp.dot(c, w_ref[o2:, :], preferred_element_type=F32))
    xo = x_ref[0] + _rms(y, gpost_ref[...])
    xo_ref[0] = xo
    h_ref[0] = _rms(xo, gffn_ref[...]).astype(BF16)


def _kout_call(x, yh, yg, ym, gh, gg, gm, w_p, gpost, gffn):
    B, L, D = x.shape
    tm = TM_OUT

    def rows(a):
        return pl.BlockSpec((1, tm, a.shape[2]), lambda b, i: (b, i, 0))

    def full(a):
        return pl.BlockSpec(a.shape, lambda b, i: (0,) * a.ndim)

    return pl.pallas_call(
        _kout_kernel, grid=(B, L // tm),
        in_specs=[rows(x), rows(yh), rows(yg), rows(ym), full(gh), full(gg), full(gm), full(w_p),
                  full(gpost), full(gffn)],
        out_specs=[rows(x), rows(x)],
        out_shape=[jax.ShapeDtypeStruct((B, L, D), F32), jax.ShapeDtypeStruct((B, L, D), BF16)],
        compiler_params=_cparams(("parallel", "parallel")), name="out_proj",
    )(x, yh, yg, ym, gh, gg, gm, w_p, gpost, gffn)


HALO = 16


def _ffn_kernel(h_ref, hp_ref, hn_ref, x_ref, wup_ref, cw_ref, cb_ref, wd_ref, gpost_ref,
                o_ref, act_ref):
    i = pl.program_id(1)
    tm = h_ref.shape[1]
    prev = jnp.where(i > 0, hp_ref[0], jnp.zeros_like(hp_ref[0]))
    nxt = jnp.where(i < pl.num_programs(1) - 1, hn_ref[0], jnp.zeros_like(hn_ref[0]))
    he = jnp.concatenate([prev, h_ref[0], nxt], axis=0)
    ext = tm + 2 * HALO
    tf = TF_FFN

    def conv(c0):
        up = jnp.dot(he, wup_ref[:, c0:c0 + tf], preferred_element_type=F32)
        um = pltpu.roll(up, 1, 0)[HALO:HALO + tm]
        upl = pltpu.roll(up, ext - 1, 0)[HALO:HALO + tm]
        return (um * cw_ref[0:1, c0:c0 + tf] + up[HALO:HALO + tm] * cw_ref[1:2, c0:c0 + tf]
                + upl * cw_ref[2:3, c0:c0 + tf] + cb_ref[:, c0:c0 + tf])

    for j in range(D_FF // tf):
        g = conv(j * tf)
        u = conv(D_FF + j * tf)
        gelu = 0.5 * g * (1.0 + jnp.tanh(math.sqrt(2.0 / math.pi) * (g + 0.044715 * (g * g * g))))
        act_ref[:, j * tf:(j + 1) * tf] = (gelu * u).astype(BF16)
    f = jnp.dot(act_ref[...], wd_ref[...], preferred_element_type=F32)
    o_ref[0] = x_ref[0] + _rms(f, gpost_ref[...])


def _ffn_call(h, x, w_up, cw, cb, w_down, gpost):
    B, L, D = x.shape
    tm = TM_FFN
    nb = tm // HALO
    last = L // HALO - 1

    def resident(a):
        return pl.BlockSpec(a.shape, lambda b, i: (0,) * a.ndim, pipeline_mode=pl.Buffered(1))

    cb = cb[None]
    return pl.pallas_call(
        _ffn_kernel, grid=(B, L // tm),
        in_specs=[pl.BlockSpec((1, tm, D), lambda b, i: (b, i, 0)),
                  pl.BlockSpec((1, HALO, D), lambda b, i: (b, jnp.maximum(i * nb - 1, 0), 0)),
                  pl.BlockSpec((1, HALO, D), lambda b, i: (b, jnp.minimum((i + 1) * nb, last), 0)),
                  pl.BlockSpec((1, tm, D), lambda b, i: (b, i, 0)),
                  resident(w_up), resident(cw), resident(cb), resident(w_down), resident(gpost)],
        out_specs=pl.BlockSpec((1, tm, D), lambda b, i: (b, i, 0)),
        out_shape=jax.ShapeDtypeStruct((B, L, D), F32),
        scratch_shapes=[pltpu.VMEM((tm, D_FF), BF16)],
        compiler_params=_cparams(("parallel", "parallel")), name="conv_ffn",
    )(h, h, h, x, w_up, cw, cb, w_down, gpost)


def _axial_tables(L, rot_dim):
    pos = np.arange(L)
    n_axis = rot_dim // 4
    inv = ROPE_THETA ** (-np.arange(n_axis) / n_axis)
    ang = np.concatenate([(pos // GRID_W)[:, None] * inv, (pos % GRID_W)[:, None] * inv], axis=-1)
    return jnp.asarray(np.cos(ang), F32), jnp.asarray(np.sin(ang), F32)


def _rope_tables(L):
    def lanes(parts):
        used = sum(p.shape[1] for p in parts)
        return jnp.concatenate(parts + [jnp.zeros((L, LANES - used), F32)], axis=1)

    cg, sg = _axial_tables(L, GQA_HEAD_DIM)
    tg = (lanes([cg, cg, cg, cg]), lanes([sg, sg, sg, sg]))
    cm, sm = _axial_tables(L, MLA_ROPE_DIM)
    nope0 = jnp.zeros((L, MLA_NOPE_DIM), F32)
    nope1 = jnp.ones((L, MLA_NOPE_DIM), F32)
    tmk = (lanes([nope0, cm, cm]), lanes([nope0, sm, sm]))
    sc = (MLA_NOPE_DIM + MLA_ROPE_DIM) ** -0.5 * LOG2E
    tmq = (lanes([nope1, cm, cm]) * sc, tmk[1] * sc)
    return tg, tmq, tmk


def _partner(w, half, sign=-1.0):
    return jnp.concatenate([sign * w[..., half:], w[..., :half]], axis=-1)


def _gain_pair(g, half):
    return jnp.tile(g, 2)[None], jnp.tile(_partner(g, half, 1.0), 2)[None]


def _pad_heads(w, n_heads, width):
    k = w.shape[0]
    return jnp.pad(w.reshape(k, n_heads, width), ((0, 0), (0, 0), (0, LANES - width))).reshape(
        k, n_heads * LANES)


def kernel(x, mix_pre_norm, w_in, hy_conv_w, hy_conv_b, hy_filt_w1, hy_filt_b1, hy_filt_freq1,
           hy_filt_w2, hy_filt_b2, hy_filt_freq2, hy_filt_w3, hy_skip, gqa_q_norm, gqa_k_norm,
           mla_q_a_norm, mla_w_uq, mla_kv_a_norm, mla_w_ukv, hy_out_norm, gqa_out_norm,
           mla_out_norm, w_out, mix_post_norm, ffn_pre_norm, w_up, ffn_conv_w, ffn_conv_b,
           w_down, ffn_post_norm):
    B, L, D = x.shape
    assert B == 2 and 2 * L == FFT_N1 * FFT_N2 and D == D_MODEL
    depth = w_in.shape[0]
    tg, tmq, tmk = _rope_tables(L)
    z_perm, win_perm = _hyena_positions(L)
    mats = _dft_matrices()

    for l in range(depth):
        wl = w_in[l]
        hd, hh, rh = GQA_HEAD_DIM, GQA_HEAD_DIM // 2, MLA_ROPE_DIM // 2
        o1 = HY_COLS
        o2 = o1 + GQA_HEADS * hd
        o3 = o2 + GQA_KV_HEADS * hd
        o4 = o3 + GQA_KV_HEADS * hd
        o5 = o4 + MLA_Q_RANK
        o6 = o5 + MLA_KV_RANK
        wq = wl[:, o1:o2].reshape(D, GQA_HEADS, hd)
        wk = wl[:, o2:o3].reshape(D, GQA_KV_HEADS, hd)
        wkr = wl[:, o6:]
        pe_pad = ((0, 0), (MLA_NOPE_DIM, LANES - MLA_NOPE_DIM - MLA_ROPE_DIM))
        win_p = jnp.concatenate(
            [wl[:, :o1],
             wq.reshape(D, -1), _partner(wq, hh).reshape(D, -1),
             wk.reshape(D, -1), _partner(wk, hh).reshape(D, -1),
             wl[:, o3:o6], jnp.pad(wkr, pe_pad), jnp.pad(_partner(wkr, rh), pe_pad)],
            axis=1).astype(BF16)
        gq = _gain_pair(gqa_q_norm[l] * (hd ** -0.5 * LOG2E), hh)
        gk = _gain_pair(gqa_k_norm[l], hh)
        wuq = mla_w_uq[l].reshape(MLA_Q_RANK, MLA_HEADS, MLA_NOPE_DIM + MLA_ROPE_DIM)
        wuq_pe = jnp.pad(_partner(wuq[:, :, MLA_NOPE_DIM:], rh),
                         ((0, 0), (0, 0), (MLA_NOPE_DIM, 0)))
        wuq_p = jnp.concatenate(
            [_pad_heads(mla_w_uq[l], MLA_HEADS, MLA_NOPE_DIM + MLA_ROPE_DIM),
             _pad_heads(wuq_pe.reshape(MLA_Q_RANK, -1), MLA_HEADS, MLA_NOPE_DIM + MLA_ROPE_DIM)],
            axis=1).astype(BF16)
        wukv = mla_w_ukv[l].reshape(MLA_KV_RANK, MLA_HEADS, MLA_NOPE_DIM + MLA_V_DIM)
        wukvk_p = _pad_heads(wukv[:, :, :MLA_NOPE_DIM].reshape(MLA_KV_RANK, -1), MLA_HEADS,
                             MLA_NOPE_DIM).astype(BF16)
        wukvv = wukv[:, :, MLA_NOPE_DIM:].reshape(MLA_KV_RANK, MLA_HEADS * MLA_V_DIM).astype(BF16)

        hy_in, qg, kg, vg, qm, km, vm = _kin_call(
            x, mix_pre_norm[l][None], win_p, gq, gk, tg, mla_q_a_norm[l][None], wuq_p, tmq,
            mla_kv_a_norm[l][None], wukvk_p, wukvv, tmk)

        fw = _filter_weights(hy_filt_w1[l], hy_filt_b1[l], hy_filt_freq1[l], hy_filt_w2[l],
                             hy_filt_b2[l], hy_filt_freq2[l], hy_filt_w3[l])
        y_hy = _hyena_layer(hy_in, hy_conv_w[l], hy_conv_b[l], fw, hy_skip[l], z_perm, win_perm, mats)

        y_gqa = _attn_call(qg.transpose(0, 2, 1), kg, vg.reshape(B, GQA_KV_HEADS, GQA_HEAD_DIM, L),
                           n_kv=1, n_rep=GQA_HEADS // GQA_KV_HEADS, pack=2, tq=TQ_GQA,
                           name="attn_gqa")
        y_mla = _attn_call(qm.transpose(0, 2, 1), km, vm.reshape(B, MLA_HEADS, MLA_V_DIM, L),
                           n_kv=2, n_rep=1, pack=1, tq=TQ_MLA, name="attn_mla")

        x, h2 = _kout_call(x, y_hy, y_gqa, y_mla, hy_out_norm[l][None], gqa_out_norm[l][None],
                           mla_out_norm[l][None], w_out[l].astype(BF16),
                           mix_post_norm[l][None], ffn_pre_norm[l][None])
        x = _ffn_call(h2, x, w_up[l].astype(BF16), ffn_conv_w[l], ffn_conv_b[l],
                      w_down[l].astype(BF16), ffn_post_norm[l][None])
    return x
```

```python
import functools
import math

import numpy as np
import jax
import jax.numpy as jnp
from jax import lax
from jax.experimental import pallas as pl
from jax.experimental.pallas import tpu as pltpu

F32 = jnp.float32
BF16 = jnp.bfloat16

NORM_EPS = 1e-6
ROPE_THETA = 10000.0
GRID_W = 64
LOG2E = math.log2(math.e)

D_MODEL = 1024
HY_D = 256
HY_COLS = 3 * HY_D
HY_EMB = 33
HY_BANDS = 16
HY_DECAY_TARGET = 1e-2
HY_FAST_DECAY_PCT = 0.3
HY_SLOW_DECAY_PCT = 1.5
GQA_HEADS = 8
GQA_KV_HEADS = 2
GQA_HEAD_DIM = 64
MLA_HEADS = 4
MLA_Q_RANK = 256
MLA_KV_RANK = 128
MLA_NOPE_DIM = 64
MLA_ROPE_DIM = 32
MLA_V_DIM = 64
D_FF = 2816

LANES = 128
FFT_N1 = 128
FFT_N2 = 128

TM_IN = 512
TK_ATTN = 512
VALUE_SPLIT = 2
TQ_GQA = 128
TQ_MLA = 256
TM_OUT = 512
TM_FFN = 512
TF_FFN = 256
TL_HCONV = 1024
FILT_HALF = 512
FILT_FEAT = 64
VMEM_LIMIT = 56 * 1024 * 1024


def _cparams(sem):
    return pltpu.CompilerParams(dimension_semantics=sem, vmem_limit_bytes=VMEM_LIMIT)


def _rms(x, g):
    return x * lax.rsqrt(jnp.mean(x * x, axis=-1, keepdims=True) + NORM_EPS) * g


def _kin_kernel(x_ref, gpre_ref, win_ref, gqn_ref, gqs_ref, gkn_ref, gks_ref, cg_ref, sg_ref,
                mqn_ref, wuq_ref, cmq_ref, smq_ref,
                mkvn_ref, wukvk_ref, wukvv_ref, cmk_ref, smk_ref,
                hy_ref, qg_ref, kg_ref, vg_ref, qm_ref, km_ref, vm_ref):
    x = x_ref[0]
    h = _rms(x, gpre_ref[...]).astype(BF16)
    cur = [0]

    def proj(n):
        lo = cur[0]
        cur[0] = lo + n
        return jnp.dot(h, win_ref[:, lo:lo + n], preferred_element_type=F32)

    hy_ref[0] = proj(HY_COLS)

    cg, sg = cg_ref[...], sg_ref[...]
    low = lax.broadcasted_iota(jnp.int32, (x.shape[0], LANES), 1) < GQA_HEAD_DIM

    def head_pairs(n_tiles, gain, gain_sw):
        xa, xb = proj(n_tiles * LANES), proj(n_tiles * LANES)
        ca, sa = cg * gain, sg * gain_sw
        out = []
        for t in range(n_tiles):
            xc, xs = xa[:, t * LANES:(t + 1) * LANES], xb[:, t * LANES:(t + 1) * LANES]
            sq = xc * xc
            tot = jnp.sum(sq, axis=-1, keepdims=True)
            lo = jnp.sum(jnp.where(low, sq, 0.0), axis=-1, keepdims=True)
            ms = jnp.where(low, lo, tot - lo) * (1.0 / GQA_HEAD_DIM)
            out.append((xc * ca + xs * sa) * lax.rsqrt(ms + NORM_EPS))
        return out

    for t, q in enumerate(head_pairs(GQA_HEADS // 2, gqn_ref[...], gqs_ref[...])):
        qg_ref[0, :, t * LANES:(t + 1) * LANES] = q.astype(BF16)
    (kk,) = head_pairs(GQA_KV_HEADS // 2, gkn_ref[...], gks_ref[...])
    ksw = pltpu.roll(kk, GQA_HEAD_DIM, 1)
    kg_ref[0, 0] = jnp.where(low, kk, ksw).astype(BF16)
    kg_ref[0, 1] = jnp.where(low, ksw, kk).astype(BF16)
    vg_ref[0] = proj(LANES).T.astype(BF16)

    cq = _rms(proj(MLA_Q_RANK), mqn_ref[...]).astype(BF16)
    qm = jnp.dot(cq, wuq_ref[...], preferred_element_type=F32)
    cmq, smq = cmq_ref[...], smq_ref[...]
    nq = MLA_HEADS * LANES
    for j in range(MLA_HEADS):
        qm_ref[0, :, j * LANES:(j + 1) * LANES] = (
            qm[:, j * LANES:(j + 1) * LANES] * cmq
            + qm[:, nq + j * LANES:nq + (j + 1) * LANES] * smq).astype(BF16)

    ckv = _rms(proj(MLA_KV_RANK), mkvn_ref[...]).astype(BF16)
    kpe = proj(LANES) * cmk_ref[...] + proj(LANES) * smk_ref[...]
    kn = jnp.dot(ckv, wukvk_ref[...], preferred_element_type=F32)
    for j in range(MLA_HEADS):
        km_ref[0, j] = (kn[:, j * LANES:(j + 1) * LANES] + kpe).astype(BF16)
    vm = jnp.dot(ckv, wukvv_ref[...], preferred_element_type=F32)
    for j in range(vm.shape[1] // LANES):
        vm_ref[0, j * LANES:(j + 1) * LANES, :] = vm[:, j * LANES:(j + 1) * LANES].T.astype(BF16)


def _kin_call(x, gpre, win_p, gq, gk, tg, mqn, wuq_p, tmq, mkvn, wukvk_p, wukvv, tmk):
    B, L, D = x.shape
    tm = TM_IN
    nt = L // tm

    def full(a):
        return pl.BlockSpec(a.shape, lambda b, i: (0,) * a.ndim, pipeline_mode=pl.Buffered(1))

    def rows(w):
        return pl.BlockSpec((tm, w), lambda b, i: (i, 0))

    in_specs = [pl.BlockSpec((1, tm, D), lambda b, i: (b, i, 0)), full(gpre), full(win_p),
                full(gq[0]), full(gq[1]), full(gk[0]), full(gk[1]), rows(LANES), rows(LANES),
                full(mqn), full(wuq_p), rows(LANES), rows(LANES),
                full(mkvn), full(wukvk_p), full(wukvv), rows(LANES), rows(LANES)]
    gv_rows = GQA_KV_HEADS * GQA_HEAD_DIM
    mv_rows = MLA_HEADS * MLA_V_DIM
    out_shape = [
        jax.ShapeDtypeStruct((B, L, HY_COLS), F32),
        jax.ShapeDtypeStruct((B, L, GQA_HEADS * GQA_HEAD_DIM), BF16),
        jax.ShapeDtypeStruct((B, GQA_KV_HEADS, L, LANES), BF16),
        jax.ShapeDtypeStruct((B, gv_rows, L), BF16),
        jax.ShapeDtypeStruct((B, L, MLA_HEADS * LANES), BF16),
        jax.ShapeDtypeStruct((B, MLA_HEADS, L, LANES), BF16),
        jax.ShapeDtypeStruct((B, mv_rows, L), BF16),
    ]
    out_specs = [
        pl.BlockSpec((1, tm, HY_COLS), lambda b, i: (b, i, 0)),
        pl.BlockSpec((1, tm, GQA_HEADS * GQA_HEAD_DIM), lambda b, i: (b, i, 0)),
        pl.BlockSpec((1, GQA_KV_HEADS, tm, LANES), lambda b, i: (b, 0, i, 0)),
        pl.BlockSpec((1, gv_rows, tm), lambda b, i: (b, 0, i)),
        pl.BlockSpec((1, tm, MLA_HEADS * LANES), lambda b, i: (b, i, 0)),
        pl.BlockSpec((1, MLA_HEADS, tm, LANES), lambda b, i: (b, 0, i, 0)),
        pl.BlockSpec((1, mv_rows, tm), lambda b, i: (b, 0, i)),
    ]
    return pl.pallas_call(
        _kin_kernel, grid=(B, nt), in_specs=in_specs, out_specs=out_specs, out_shape=out_shape,
        compiler_params=_cparams(("parallel", "parallel")), name="in_proj",
    )(x, gpre, win_p, *gq, *gk, *tg, mqn, wuq_p, *tmq, mkvn, wukvk_p, wukvv, *tmk)


def _attn_kernel(q_ref, k_ref, vt_ref, o_ref, sa_ref, sb_ref, ma_ref, mb_ref, *,
                 n_kv, n_rep, pack, tq, n_chunks, tk):
    i = pl.program_id(0)
    cols = n_rep * tq
    width = n_kv * cols
    grp = tk // 8

    @pl.when(i == 0)
    def _():
        sb_ref[...] = jnp.zeros(sb_ref.shape, F32)
        mb_ref[...] = jnp.zeros(mb_ref.shape, F32)

    def step(sw_ref, mw_ref, sr_ref, mr_ref):
        def q_head(h):
            tile, half = divmod(h, pack)
            qt = q_ref[0, tile * LANES:(tile + 1) * LANES, :]
            if pack == 2:
                z = jnp.zeros((LANES // 2, qt.shape[1]), qt.dtype)
                qt = jnp.concatenate([qt[:LANES // 2], z] if half == 0 else [z, qt[LANES // 2:]],
                                     axis=0)
            return qt

        qs = [jnp.concatenate([q_head(a * n_rep + j) for j in range(n_rep)], axis=1)
              for a in range(n_kv)]
        mx = jnp.max(mr_ref[...], axis=0, keepdims=True)
        m = jnp.full((8, width), -jnp.inf, F32)
        l = jnp.zeros((8, width), F32)
        accs = [jnp.zeros((vt_ref.shape[2], cols), F32) for _ in range(n_kv)]
        for c in range(n_chunks):
            st = jnp.concatenate(
                [jnp.dot(k_ref[0, a, c * tk:(c + 1) * tk, :], qs[a], preferred_element_type=F32)
                 for a in range(n_kv)], axis=1)
            sw_ref[c] = st
            m = jnp.maximum(m, jnp.max(st.reshape(grp, 8, width), axis=0))
            for h in range(VALUE_SPLIT):
                r0, r1 = h * tk // VALUE_SPLIT, (h + 1) * tk // VALUE_SPLIT
                p = jnp.exp2(sr_ref[c, r0:r1, :] - mx)
                l = l + jnp.sum(p.reshape(grp // VALUE_SPLIT, 8, width), axis=0)
                pb = p.astype(BF16)
                for a in range(n_kv):
                    accs[a] = accs[a] + jnp.dot(vt_ref[0, a, :, c * tk + r0:c * tk + r1],
                                                pb[:, a * cols:(a + 1) * cols],
                                                preferred_element_type=F32)
        mw_ref[...] = m
        ls = jnp.sum(l, axis=0, keepdims=True)
        heads = []
        for a in range(n_kv):
            oa = accs[a] / ls[:, a * cols:(a + 1) * cols]
            heads += [oa[:, j * tq:(j + 1) * tq] for j in range(n_rep)]
        o_ref[0] = jnp.concatenate(heads, axis=0).T.astype(BF16)

    @pl.when(i % 2 == 0)
    def _():
        step(sa_ref, ma_ref, sb_ref, mb_ref)

    @pl.when(i % 2 == 1)
    def _():
        step(sb_ref, mb_ref, sa_ref, ma_ref)


def _attn_call(qt, k, vt, *, n_kv, n_rep, pack, tq, name):
    B, hq, L = qt.shape
    H = hq * pack // LANES
    hkv, dv = vt.shape[1], vt.shape[2]
    G = hkv // n_kv
    hs = n_kv * n_rep
    tk = TK_ATTN
    n_chunks = L // tk
    nq = L // tq
    width = hs * tq
    kern = functools.partial(_attn_kernel, n_kv=n_kv, n_rep=n_rep, pack=pack, tq=tq,
                             n_chunks=n_chunks, tk=tk)
    total = B * G * nq

    def blk(s):
        return s // (G * nq), (s // nq) % G, s % nq

    def q_map(s):
        b, g, i = blk(jnp.minimum(s, total - 1))
        return b, g, i

    def k_map(s):
        b, g, _ = blk(jnp.minimum(s, total - 1))
        return b, g, 0, 0

    def v_map(s):
        b, g, _ = blk(jnp.maximum(s - 1, 0))
        return b, g, 0, 0

    def o_map(s):
        b, g, i = blk(jnp.maximum(s - 1, 0))
        return b, i, g

    return pl.pallas_call(
        kern, grid=(total + 1,),
        in_specs=[pl.BlockSpec((1, hs * LANES // pack, tq), q_map),
                  pl.BlockSpec((1, n_kv, L, LANES), k_map, pipeline_mode=pl.Buffered(1)),
                  pl.BlockSpec((1, n_kv, dv, L), v_map, pipeline_mode=pl.Buffered(1))],
        out_specs=pl.BlockSpec((1, tq, hs * dv), o_map),
        out_shape=jax.ShapeDtypeStruct((B, L, H * dv), BF16),
        scratch_shapes=[pltpu.VMEM((n_chunks, tk, width), F32), pltpu.VMEM((n_chunks, tk, width), F32),
                        pltpu.VMEM((8, width), F32), pltpu.VMEM((8, width), F32)],
        compiler_params=_cparams(("arbitrary",)), name=name,
    )(qt, k, vt)


def _hconv_kernel(x_ref, xp_ref, xn_ref, w_ref, b_ref, v_ref, x1_ref, x2_ref):
    i = pl.program_id(1)
    x = x_ref[0]
    tl = x.shape[0]
    prev = jnp.where(i > 0, xp_ref[0][7:8, :], 0.0)
    nxt = jnp.where(i < pl.num_programs(1) - 1, xn_ref[0][0:1, :], 0.0)
    r = lax.broadcasted_iota(jnp.int32, x.shape, 0)
    xm = jnp.where(r == 0, prev, pltpu.roll(x, 1, 0))
    xp = jnp.where(r == tl - 1, nxt, pltpu.roll(x, tl - 1, 0))
    uc = xm * w_ref[0:1, :] + x * w_ref[1:2, :] + xp * w_ref[2:3, :] + b_ref[...]
    for r in range(tl // FFT_N2):
        blk = uc[r * FFT_N2:(r + 1) * FFT_N2]
        v_ref[:, r, :] = blk[:, :HY_D]
        x1_ref[:, r, :] = blk[:, HY_D:2 * HY_D]
        x2_ref[:, r, :] = blk[:, 2 * HY_D:]


def _hconv_call(hy_in, w, b):
    B, L, C = hy_in.shape
    tl = TL_HCONV
    nb = tl // 8
    last = L // 8 - 1
    nt = L // tl
    rows = tl // FFT_N2
    out = jax.ShapeDtypeStruct((FFT_N2, B * L // FFT_N2, HY_D), F32)
    ospec = pl.BlockSpec((FFT_N2, rows, HY_D), lambda b_, i: (0, b_ * nt + i, 0))
    return pl.pallas_call(
        _hconv_kernel, grid=(B, L // tl),
        in_specs=[pl.BlockSpec((1, tl, C), lambda b_, i: (b_, i, 0)),
                  pl.BlockSpec((1, 8, C), lambda b_, i: (b_, jnp.maximum(i * nb - 1, 0), 0)),
                  pl.BlockSpec((1, 8, C), lambda b_, i: (b_, jnp.minimum((i + 1) * nb, last), 0)),
                  pl.BlockSpec((3, C), lambda b_, i: (0, 0)),
                  pl.BlockSpec((1, C), lambda b_, i: (0, 0))],
        out_specs=[ospec, ospec, ospec], out_shape=[out, out, out],
        compiler_params=_cparams(("parallel", "parallel")), name="hy_conv3",
    )(hy_in, hy_in, hy_in, w, b)


def _split_bf16(a):
    hi = a.astype(BF16)
    return hi, (a - hi.astype(F32)).astype(BF16)


def _dot3(a, w_hi, w_lo):
    a_hi, a_lo = _split_bf16(a)
    dot = functools.partial(jnp.dot, preferred_element_type=F32)
    return dot(a_hi, w_hi) + dot(a_lo, w_hi) + dot(a_hi, w_lo)


def _filt_kernel(z_ref, w1h_ref, w1l_ref, b1_ref, f1_ref, w2h_ref, w2l_ref, b2_ref, f2_ref,
                 w3h_ref, w3l_ref, win_ref, gaf_ref, o_ref):
    h = jnp.sin(f1_ref[...] * (_dot3(z_ref[...], w1h_ref[...], w1l_ref[...]) + b1_ref[...]))
    h = jnp.sin(f2_ref[...] * (_dot3(h, w2h_ref[...], w2l_ref[...]) + b2_ref[...]))
    k = _dot3(h, w3h_ref[...], w3l_ref[...])
    half = k.shape[0]
    wcols = 4 * HY_D
    per_half = half // FFT_N1
    n1 = lax.broadcasted_iota(jnp.int32, (half, 2 * HY_D), 0) % FFT_N1
    for s in range(2):
        ks = k[:, s * wcols:(s + 1) * wcols]
        kk = jnp.where(n1 >= FFT_N1 // 2, ks[:, 2 * HY_D:], ks[:, :2 * HY_D])
        win = win_ref[s * half:(s + 1) * half, :]
        filt = (kk * jnp.concatenate([win, win], axis=-1)).astype(BF16)
        for j in range(per_half):
            o_ref[s * per_half + j] = jnp.dot(gaf_ref[s * per_half + j],
                                              filt[j * FFT_N1:(j + 1) * FFT_N1],
                                              preferred_element_type=F32)


def _filt_call(z_pack, fw, win_perm, gaf):
    tp = 2 * FILT_HALF
    n = win_perm.shape[0]
    to = tp // FFT_N1
    M = gaf.shape[1]

    def full(a):
        return pl.BlockSpec(a.shape, lambda i: (0,) * a.ndim)

    return pl.pallas_call(
        _filt_kernel, grid=(n // tp,),
        in_specs=[pl.BlockSpec((FILT_HALF, z_pack.shape[1]), lambda i: (i, 0))]
        + [full(a) for a in fw] + [pl.BlockSpec((tp, HY_D), lambda i: (i, 0)),
                                   pl.BlockSpec((to, M, FFT_N1), lambda i: (i, 0, 0))],
        out_specs=pl.BlockSpec((to, M, 2 * HY_D), lambda i: (i, 0, 0)),
        out_shape=jax.ShapeDtypeStruct((n // FFT_N1, M, 2 * HY_D), F32),
        compiler_params=_cparams(("parallel",)), name="hy_filter",
    )(z_pack, *fw, win_perm, gaf)


def _bm_kernel(g_ref, x_ref, o_ref, *, to):
    for t in range(to):
        o_ref[t] = jnp.dot(g_ref[t], x_ref[t].astype(BF16), preferred_element_type=F32)


def _bm_call(g, x, name, to=8):
    O, K, N = x.shape
    M = g.shape[1]
    return pl.pallas_call(
        functools.partial(_bm_kernel, to=to), grid=(O // to,),
        in_specs=[pl.BlockSpec((to, M, K), lambda i: (i, 0, 0)),
                  pl.BlockSpec((to, K, N), lambda i: (i, 0, 0))],
        out_specs=pl.BlockSpec((to, M, N), lambda i: (i, 0, 0)),
        out_shape=jax.ShapeDtypeStruct((O, M, N), F32),
        compiler_params=_cparams(("parallel",)), name=name,
    )(g, x)


def _gather_ri(x_ref, j):
    return jnp.concatenate([x_ref[:, 0, j, :], x_ref[:, 1, j, :]], axis=0).astype(BF16)


def _gather_spec(a, to):
    return pl.BlockSpec((a.shape[0], 2, to, a.shape[3]), lambda i: (0, 0, i, 0))


def _filtb_kernel(mb_ref, x_ref, o_ref, *, to):
    for j in range(to):
        o_ref[j] = jnp.dot(mb_ref[...], _gather_ri(x_ref, j),
                           preferred_element_type=F32).astype(o_ref.dtype)


def _filtb_call(mb, ka, to=8):
    O, _, P, N = ka.shape
    return pl.pallas_call(
        functools.partial(_filtb_kernel, to=to), grid=(P // to,),
        in_specs=[pl.BlockSpec(mb.shape, lambda i: (0, 0)), _gather_spec(ka, to)],
        out_specs=pl.BlockSpec((to, 2 * O, N), lambda i: (i, 0, 0)),
        out_shape=jax.ShapeDtypeStruct((P, 2 * O, N), BF16),
        compiler_params=_cparams(("parallel",)), name="hy_fft_filt_b",
    )(mb, ka)


def _convb_kernel(mb_ref, gc_ref, x_ref, kf_ref, o_ref, *, to):
    h = FFT_N2
    for j in range(to):
        xs = jnp.dot(mb_ref[...], _gather_ri(x_ref, j), preferred_element_type=F32)
        xr, xi = xs[:h], xs[h:]
        kr, ki = kf_ref[j, :h, :].astype(F32), kf_ref[j, h:, :].astype(F32)
        ys = jnp.concatenate([xr * kr - xi * ki, xr * ki + xi * kr], axis=0).astype(BF16)
        o_ref[j] = jnp.dot(gc_ref[j], ys, preferred_element_type=F32)


def _convb_call(mb, gc, a, kf, order, to=8):
    O, _, P, N = a.shape
    return pl.pallas_call(
        functools.partial(_convb_kernel, to=to), grid=(P // to,),
        in_specs=[pl.BlockSpec(mb.shape, lambda i: (0, 0)),
                  pl.BlockSpec((to, 2 * O, 2 * O), lambda i: (i, 0, 0)),
                  _gather_spec(a, to),
                  pl.BlockSpec((to, 2 * O, N), lambda i: (i, 0, order))],
        out_specs=pl.BlockSpec((to, 2 * O, N), lambda i: (i, 0, 0)),
        out_shape=jax.ShapeDtypeStruct((P, 2 * O, N), F32),
        compiler_params=_cparams(("parallel",)), name="hy_spec_mul",
    )(mb, gc, a, kf)


def _convd_kernel(md_ref, c_ref, g_ref, u_ref, s_ref, *rest, to, chain):
    if chain:
        ga_ref, z_ref, a_ref = rest
    else:
        (o_ref,) = rest
    for j in range(to):
        y = jnp.dot(md_ref[...], _gather_ri(c_ref, j), preferred_element_type=F32)
        z = g_ref[j] * (y + u_ref[j] * s_ref[...])
        if chain:
            z_ref[j] = z
            a_ref[j] = jnp.dot(ga_ref[j], z.astype(BF16), preferred_element_type=F32)
        else:
            o_ref[:, j, :] = z


def _convd_call(md, c, gate, u, skip, ga=None, to=8):
    O, _, P, N = c.shape
    R = md.shape[0]
    tspec = pl.BlockSpec((to, R, N), lambda i: (i, 0, 0))
    in_specs = [pl.BlockSpec(md.shape, lambda i: (0, 0)), _gather_spec(c, to), tspec, tspec,
                pl.BlockSpec((1, N), lambda i: (0, 0))]
    args = [md, c, gate, u, skip.reshape(1, N)]
    if ga is not None:
        M = ga.shape[1]
        in_specs.append(pl.BlockSpec((to, M, R), lambda i: (i, 0, 0)))
        args.append(ga)
        out_specs = [tspec, pl.BlockSpec((to, M, N), lambda i: (i, 0, 0))]
        out_shape = [jax.ShapeDtypeStruct((P, R, N), F32), jax.ShapeDtypeStruct((P, M, N), F32)]
    else:
        out_specs = pl.BlockSpec((R, to, N), lambda i: (0, i, 0))
        out_shape = jax.ShapeDtypeStruct((R, P, N), F32)
    return pl.pallas_call(
        functools.partial(_convd_kernel, to=to, chain=ga is not None), grid=(P // to,),
        in_specs=in_specs, out_specs=out_specs, out_shape=out_shape,
        compiler_params=_cparams(("parallel",)), name="hy_fft_d",
    )(*args)


def _dft_tables():
    n = FFT_N1 * FFT_N2
    k = np.arange(FFT_N1)
    f = np.exp(-2j * np.pi * np.outer(k, k) / FFT_N1)
    t = np.exp(-2j * np.pi * np.outer(k, k) / n)
    return f, t, n


def _dft_matrices():
    f, t, n = _dft_tables()
    fr, fi = jnp.asarray(f.real, F32), jnp.asarray(f.imag, F32)
    tr, ti = jnp.asarray(t.real, F32), jnp.asarray(t.imag, F32)
    half = FFT_N1 // 2
    er = fr[None] * tr[:, :, None] - fi[None] * ti[:, :, None]
    ei = fr[None] * ti[:, :, None] + fi[None] * tr[:, :, None]
    ga = jnp.concatenate([jnp.concatenate([er[:, :, :half], -ei[:, :, :half]], axis=2),
                          jnp.concatenate([ei[:, :, :half], er[:, :, :half]], axis=2)], axis=1)
    gaf = jnp.concatenate([er, ei], axis=1)
    mb = jnp.concatenate([jnp.concatenate([fr, -fi], axis=1),
                          jnp.concatenate([fi, fr], axis=1)], axis=0)
    tct = jnp.transpose(tr)[:, :, None]
    tst = -jnp.transpose(ti)[:, :, None]
    gr = tct * fr[None] - tst * (-fi[None])
    gi = tct * (-fi[None]) + tst * fr[None]
    gc = jnp.concatenate([jnp.concatenate([gr, -gi], axis=2),
                          jnp.concatenate([gi, gr], axis=2)], axis=1)
    hr, hi = fr[:half] / n, -fi[:half] / n
    md = jnp.concatenate([jnp.concatenate([hr, -hi], axis=1),
                          jnp.concatenate([hi, hr], axis=1)], axis=0)
    return (ga.astype(BF16), gaf.astype(BF16), mb.astype(BF16), gc.astype(BF16), md.astype(BF16))


def _hyena_positions(L):
    p = FFT_N2 * np.arange(FFT_N1)[None, :] + np.arange(FFT_N2)[:, None]
    pos = np.where(p < L, p, 2 * L - 1 - p).reshape(2 * L, 1).astype(np.float64)
    t = pos / (L - 1)
    w = 2.0 * math.pi * pos / L
    f = np.linspace(1e-4, HY_BANDS - 1, HY_BANDS)[None, :]
    z = np.concatenate([t, np.cos(f * w), -np.sin(f * w),
                        np.zeros((2 * L, FILT_FEAT - HY_EMB))], axis=-1)
    z_pack = (z.reshape(-1, 2, FILT_HALF, FILT_FEAT).transpose(0, 2, 1, 3)
              .reshape(-1, 2 * FILT_FEAT))
    max_decay = math.log(HY_DECAY_TARGET) / HY_FAST_DECAY_PCT
    min_decay = math.log(HY_DECAY_TARGET) / HY_SLOW_DECAY_PCT
    deltas = jnp.linspace(min_decay, max_decay, HY_D, dtype=F32)
    window = jnp.exp(-jnp.asarray(t, F32) * jnp.abs(deltas)[None, :])
    return jnp.asarray(z_pack, F32), window


def _filter_weights(w1, b1, f1, w2, b2, f2, w3):
    def bd(w):
        z = jnp.zeros_like(w)
        return jnp.concatenate([jnp.concatenate([w, z], axis=1),
                                jnp.concatenate([z, w], axis=1)], axis=0)

    def twice(v):
        return jnp.concatenate([v, v])[None]

    w1 = jnp.pad(w1, ((0, FILT_FEAT - w1.shape[0]), (0, 0)))
    w3 = w3.reshape(-1, 2, 2, HY_D).transpose(0, 2, 1, 3).reshape(-1, 4 * HY_D)
    return (*_split_bf16(bd(w1)), twice(b1), twice(f1), *_split_bf16(bd(w2)), twice(b2), twice(f2),
            *_split_bf16(bd(w3)))


def _hyena_layer(hy_in, conv_w, conv_b, fw, skip, z_perm, win_perm, mats):
    B, L, _ = hy_in.shape
    ga, gaf, mb, gc, md = mats
    v, x1, x2 = _hconv_call(hy_in, conv_w, conv_b[None])
    ka = _filt_call(z_perm, fw, win_perm, gaf)
    kf = _filtb_call(mb, ka.reshape(FFT_N2, 2, FFT_N1, 2 * HY_D))

    a = _bm_call(ga, v, "hy_fft_a")
    c = _convb_call(mb, gc, a.reshape(FFT_N2, 2, FFT_N1, HY_D), kf, 0)
    z, a = _convd_call(md, c.reshape(FFT_N1, 2, FFT_N2, HY_D), x1, v, skip[0], ga=ga)
    c = _convb_call(mb, gc, a.reshape(FFT_N2, 2, FFT_N1, HY_D), kf, 1)
    z = _convd_call(md, c.reshape(FFT_N1, 2, FFT_N2, HY_D), x2, z, skip[1])
    return z.reshape(B, L, HY_D)


def _kout_kernel(x_ref, yh_ref, yg_ref, ym_ref, gh_ref, gg_ref, gm_ref, w_ref, gpost_ref, gffn_ref,
                 xo_ref, h_ref):
    a = _rms(yh_ref[0], gh_ref[...]).astype(BF16)
    b = _rms(yg_ref[0].astype(F32), gg_ref[...]).astype(BF16)
    c = _rms(ym_ref[0].astype(F32), gm_ref[...]).astype(BF16)
    o1 = HY_D
    o2 = o1 + GQA_HEADS * GQA_HEAD_DIM
    y = (jnp.dot(a, w_ref[:o1, :], preferred_element_type=F32)
         + jnp.dot(b, w_ref[o1:o2, :], preferred_element_type=F32)
         + jnp.dot(c, w_ref[o2:, :], preferred_element_type=F32))
    xo = x_ref[0] + _rms(y, gpost_ref[...])
    xo_ref[0] = xo
    h_ref[0] = _rms(xo, gffn_ref[...]).astype(BF16)


def _kout_call(x, yh, yg, ym, gh, gg, gm, w_p, gpost, gffn):
    B, L, D = x.shape
    tm = TM_OUT

    def rows(a):
        return pl.BlockSpec((1, tm, a.shape[2]), lambda b, i: (b, i, 0))

    def full(a):
        return pl.BlockSpec(a.shape, lambda b, i: (0,) * a.ndim)

    return pl.pallas_call(
        _kout_kernel, grid=(B, L // tm),
        in_specs=[rows(x), rows(yh), rows(yg), rows(ym), full(gh), full(gg), full(gm), full(w_p),
                  full(gpost), full(gffn)],
        out_specs=[rows(x), rows(x)],
        out_shape=[jax.ShapeDtypeStruct((B, L, D), F32), jax.ShapeDtypeStruct((B, L, D), BF16)],
        compiler_params=_cparams(("parallel", "parallel")), name="out_proj",
    )(x, yh, yg, ym, gh, gg, gm, w_p, gpost, gffn)


HALO = 16


def _ffn_kernel(h_ref, hp_ref, hn_ref, x_ref, wup_ref, cw_ref, cb_ref, wd_ref, gpost_ref,
                o_ref, act_ref):
    i = pl.program_id(1)
    tm = h_ref.shape[1]
    prev = jnp.where(i > 0, hp_ref[0], jnp.zeros_like(hp_ref[0]))
    nxt = jnp.where(i < pl.num_programs(1) - 1, hn_ref[0], jnp.zeros_like(hn_ref[0]))
    he = jnp.concatenate([prev, h_ref[0], nxt], axis=0)
    ext = tm + 2 * HALO
    tf = TF_FFN

    def conv(c0):
        up = jnp.dot(he, wup_ref[:, c0:c0 + tf], preferred_element_type=F32)
        um = pltpu.roll(up, 1, 0)[HALO:HALO + tm]
        upl = pltpu.roll(up, ext - 1, 0)[HALO:HALO + tm]
        return (um * cw_ref[0:1, c0:c0 + tf] + up[HALO:HALO + tm] * cw_ref[1:2, c0:c0 + tf]
                + upl * cw_ref[2:3, c0:c0 + tf] + cb_ref[:, c0:c0 + tf])

    for j in range(D_FF // tf):
        g = conv(j * tf)
        u = conv(D_FF + j * tf)
        gelu = 0.5 * g * (1.0 + jnp.tanh(math.sqrt(2.0 / math.pi) * (g + 0.044715 * (g * g * g))))
        act_ref[:, j * tf:(j + 1) * tf] = (gelu * u).astype(BF16)
    f = jnp.dot(act_ref[...], wd_ref[...], preferred_element_type=F32)
    o_ref[0] = x_ref[0] + _rms(f, gpost_ref[...])


def _ffn_call(h, x, w_up, cw, cb, w_down, gpost):
    B, L, D = x.shape
    tm = TM_FFN
    nb = tm // HALO
    last = L // HALO - 1

    def resident(a):
        return pl.BlockSpec(a.shape, lambda b, i: (0,) * a.ndim, pipeline_mode=pl.Buffered(1))

    cb = cb[None]
    return pl.pallas_call(
        _ffn_kernel, grid=(B, L // tm),
        in_specs=[pl.BlockSpec((1, tm, D), lambda b, i: (b, i, 0)),
                  pl.BlockSpec((1, HALO, D), lambda b, i: (b, jnp.maximum(i * nb - 1, 0), 0)),
                  pl.BlockSpec((1, HALO, D), lambda b, i: (b, jnp.minimum((i + 1) * nb, last), 0)),
                  pl.BlockSpec((1, tm, D), lambda b, i: (b, i, 0)),
                  resident(w_up), resident(cw), resident(cb), resident(w_down), resident(gpost)],
        out_specs=pl.BlockSpec((1, tm, D), lambda b, i: (b, i, 0)),
        out_shape=jax.ShapeDtypeStruct((B, L, D), F32),
        scratch_shapes=[pltpu.VMEM((tm, D_FF), BF16)],
        compiler_params=_cparams(("parallel", "parallel")), name="conv_ffn",
    )(h, h, h, x, w_up, cw, cb, w_down, gpost)


def _axial_tables(L, rot_dim):
    pos = np.arange(L)
    n_axis = rot_dim // 4
    inv = ROPE_THETA ** (-np.arange(n_axis) / n_axis)
    ang = np.concatenate([(pos // GRID_W)[:, None] * inv, (pos % GRID_W)[:, None] * inv], axis=-1)
    return jnp.asarray(np.cos(ang), F32), jnp.asarray(np.sin(ang), F32)


def _rope_tables(L):
    def lanes(parts):
        used = sum(p.shape[1] for p in parts)
        return jnp.concatenate(parts + [jnp.zeros((L, LANES - used), F32)], axis=1)

    cg, sg = _axial_tables(L, GQA_HEAD_DIM)
    tg = (lanes([cg, cg, cg, cg]), lanes([sg, sg, sg, sg]))
    cm, sm = _axial_tables(L, MLA_ROPE_DIM)
    nope0 = jnp.zeros((L, MLA_NOPE_DIM), F32)
    nope1 = jnp.ones((L, MLA_NOPE_DIM), F32)
    tmk = (lanes([nope0, cm, cm]), lanes([nope0, sm, sm]))
    sc = (MLA_NOPE_DIM + MLA_ROPE_DIM) ** -0.5 * LOG2E
    tmq = (lanes([nope1, cm, cm]) * sc, tmk[1] * sc)
    return tg, tmq, tmk


def _partner(w, half, sign=-1.0):
    return jnp.concatenate([sign * w[..., half:], w[..., :half]], axis=-1)


def _gain_pair(g, half):
    return jnp.tile(g, 2)[None], jnp.tile(_partner(g, half, 1.0), 2)[None]


def _pad_heads(w, n_heads, width):
    k = w.shape[0]
    return jnp.pad(w.reshape(k, n_heads, width), ((0, 0), (0, 0), (0, LANES - width))).reshape(
        k, n_heads * LANES)


def kernel(x, mix_pre_norm, w_in, hy_conv_w, hy_conv_b, hy_filt_w1, hy_filt_b1, hy_filt_freq1,
           hy_filt_w2, hy_filt_b2, hy_filt_freq2, hy_filt_w3, hy_skip, gqa_q_norm, gqa_k_norm,
           mla_q_a_norm, mla_w_uq, mla_kv_a_norm, mla_w_ukv, hy_out_norm, gqa_out_norm,
           mla_out_norm, w_out, mix_post_norm, ffn_pre_norm, w_up, ffn_conv_w, ffn_conv_b,
           w_down, ffn_post_norm):
    B, L, D = x.shape
    assert B == 2 and 2 * L == FFT_N1 * FFT_N2 and D == D_MODEL
    depth = w_in.shape[0]
    tg, tmq, tmk = _rope_tables(L)
    z_perm, win_perm = _hyena_positions(L)
    mats = _dft_matrices()

    for l in range(depth):
        wl = w_in[l]
        hd, hh, rh = GQA_HEAD_DIM, GQA_HEAD_DIM // 2, MLA_ROPE_DIM // 2
        o1 = HY_COLS
        o2 = o1 + GQA_HEADS * hd
        o3 = o2 + GQA_KV_HEADS * hd
        o4 = o3 + GQA_KV_HEADS * hd
        o5 = o4 + MLA_Q_RANK
        o6 = o5 + MLA_KV_RANK
        wq = wl[:, o1:o2].reshape(D, GQA_HEADS, hd)
        wk = wl[:, o2:o3].reshape(D, GQA_KV_HEADS, hd)
        wkr = wl[:, o6:]
        pe_pad = ((0, 0), (MLA_NOPE_DIM, LANES - MLA_NOPE_DIM - MLA_ROPE_DIM))
        win_p = jnp.concatenate(
            [wl[:, :o1],
             wq.reshape(D, -1), _partner(wq, hh).reshape(D, -1),
             wk.reshape(D, -1), _partner(wk, hh).reshape(D, -1),
             wl[:, o3:o6], jnp.pad(wkr, pe_pad), jnp.pad(_partner(wkr, rh), pe_pad)],
            axis=1).astype(BF16)
        gq = _gain_pair(gqa_q_norm[l] * (hd ** -0.5 * LOG2E), hh)
        gk = _gain_pair(gqa_k_norm[l], hh)
        wuq = mla_w_uq[l].reshape(MLA_Q_RANK, MLA_HEADS, MLA_NOPE_DIM + MLA_ROPE_DIM)
        wuq_pe = jnp.pad(_partner(wuq[:, :, MLA_NOPE_DIM:], rh),
                         ((0, 0), (0, 0), (MLA_NOPE_DIM, 0)))
        wuq_p = jnp.concatenate(
            [_pad_heads(mla_w_uq[l], MLA_HEADS, MLA_NOPE_DIM + MLA_ROPE_DIM),
             _pad_heads(wuq_pe.reshape(MLA_Q_RANK, -1), MLA_HEADS, MLA_NOPE_DIM + MLA_ROPE_DIM)],
            axis=1).astype(BF16)
        wukv = mla_w_ukv[l].reshape(MLA_KV_RANK, MLA_HEADS, MLA_NOPE_DIM + MLA_V_DIM)
        wukvk_p = _pad_heads(wukv[:, :, :MLA_NOPE_DIM].reshape(MLA_KV_RANK, -1), MLA_HEADS,
                             MLA_NOPE_DIM).astype(BF16)
        wukvv = wukv[:, :, MLA_NOPE_DIM:].reshape(MLA_KV_RANK, MLA_HEADS * MLA_V_DIM).astype(BF16)

        hy_in, qg, kg, vg, qm, km, vm = _kin_call(
            x, mix_pre_norm[l][None], win_p, gq, gk, tg, mla_q_a_norm[l][None], wuq_p, tmq,
            mla_kv_a_norm[l][None], wukvk_p, wukvv, tmk)

        fw = _filter_weights(hy_filt_w1[l], hy_filt_b1[l], hy_filt_freq1[l], hy_filt_w2[l],
                             hy_filt_b2[l], hy_filt_freq2[l], hy_filt_w3[l])
        y_hy = _hyena_layer(hy_in, hy_conv_w[l], hy_conv_b[l], fw, hy_skip[l], z_perm, win_perm, mats)

        y_gqa = _attn_call(qg.transpose(0, 2, 1), kg, vg.reshape(B, GQA_KV_HEADS, GQA_HEAD_DIM, L),
                           n_kv=1, n_rep=GQA_HEADS // GQA_KV_HEADS, pack=2, tq=TQ_GQA,
                           name="attn_gqa")
        y_mla = _attn_call(qm.transpose(0, 2, 1), km, vm.reshape(B, MLA_HEADS, MLA_V_DIM, L),
                           n_kv=2, n_rep=1, pack=1, tq=TQ_MLA, name="attn_mla")

        x, h2 = _kout_call(x, y_hy, y_gqa, y_mla, hy_out_norm[l][None], gqa_out_norm[l][None],
                           mla_out_norm[l][None], w_out[l].astype(BF16),
                           mix_post_norm[l][None], ffn_pre_norm[l][None])
        x = _ffn_call(h2, x, w_up[l].astype(BF16), ffn_conv_w[l], ffn_conv_b[l],
                      w_down[l].astype(BF16), ffn_post_norm[l][None])
    return x
```

```python
import functools
import math

import numpy as np
import jax
import jax.numpy as jnp
from jax import lax
from jax.experimental import pallas as pl
from jax.experimental.pallas import tpu as pltpu

F32 = jnp.float32
BF16 = jnp.bfloat16

NORM_EPS = 1e-6
ROPE_THETA = 10000.0
GRID_W = 64
LOG2E = math.log2(math.e)

D_MODEL = 1024
HY_D = 256
HY_COLS = 3 * HY_D
HY_EMB = 33
HY_BANDS = 16
HY_DECAY_TARGET = 1e-2
HY_FAST_DECAY_PCT = 0.3
HY_SLOW_DECAY_PCT = 1.5
GQA_HEADS = 8
GQA_KV_HEADS = 2
GQA_HEAD_DIM = 64
MLA_HEADS = 4
MLA_Q_RANK = 256
MLA_KV_RANK = 128
MLA_NOPE_DIM = 64
MLA_ROPE_DIM = 32
MLA_V_DIM = 64
D_FF = 2816

LANES = 128
FFT_N1 = 128
FFT_N2 = 128

TM_IN = 512
TK_ATTN = 512
TQ_GQA = 128
TQ_MLA = 256
TM_OUT = 512
TM_FFN = 512
TF_FFN = 256
TL_HCONV = 1024
FILT_HALF = 512
FILT_FEAT = 64
VMEM_LIMIT = 56 * 1024 * 1024


def _cparams(sem):
    return pltpu.CompilerParams(dimension_semantics=sem, vmem_limit_bytes=VMEM_LIMIT)


def _rms(x, g):
    return x * lax.rsqrt(jnp.mean(x * x, axis=-1, keepdims=True) + NORM_EPS) * g


def _kin_kernel(x_ref, gpre_ref, win_ref, gqn_ref, gqs_ref, gkn_ref, gks_ref, cg_ref, sg_ref,
                mqn_ref, wuq_ref, cmq_ref, smq_ref,
                mkvn_ref, wukvk_ref, wukvv_ref, cmk_ref, smk_ref,
                hy_ref, qg_ref, kg_ref, vg_ref, qm_ref, km_ref, vm_ref):
    x = x_ref[0]
    h = _rms(x, gpre_ref[...]).astype(BF16)
    cur = [0]

    def proj(n):
        lo = cur[0]
        cur[0] = lo + n
        return jnp.dot(h, win_ref[:, lo:lo + n], preferred_element_type=F32)

    hy_ref[0] = proj(HY_COLS)

    cg, sg = cg_ref[...], sg_ref[...]
    low = lax.broadcasted_iota(jnp.int32, (x.shape[0], LANES), 1) < GQA_HEAD_DIM

    def head_pairs(n_tiles, gain, gain_sw):
        xa, xb = proj(n_tiles * LANES), proj(n_tiles * LANES)
        ca, sa = cg * gain, sg * gain_sw
        out = []
        for t in range(n_tiles):
            xc, xs = xa[:, t * LANES:(t + 1) * LANES], xb[:, t * LANES:(t + 1) * LANES]
            sq = xc * xc
            tot = jnp.sum(sq, axis=-1, keepdims=True)
            lo = jnp.sum(jnp.where(low, sq, 0.0), axis=-1, keepdims=True)
            ms = jnp.where(low, lo, tot - lo) * (1.0 / GQA_HEAD_DIM)
            out.append((xc * ca + xs * sa) * lax.rsqrt(ms + NORM_EPS))
        return out

    for t, q in enumerate(head_pairs(GQA_HEADS // 2, gqn_ref[...], gqs_ref[...])):
        qg_ref[0, :, t * LANES:(t + 1) * LANES] = q.astype(BF16)
    (kk,) = head_pairs(GQA_KV_HEADS // 2, gkn_ref[...], gks_ref[...])
    ksw = pltpu.roll(kk, GQA_HEAD_DIM, 1)
    kg_ref[0, 0] = jnp.where(low, kk, ksw).astype(BF16)
    kg_ref[0, 1] = jnp.where(low, ksw, kk).astype(BF16)
    vg_ref[0] = proj(LANES).T.astype(BF16)

    cq = _rms(proj(MLA_Q_RANK), mqn_ref[...]).astype(BF16)
    qm = jnp.dot(cq, wuq_ref[...], preferred_element_type=F32)
    cmq, smq = cmq_ref[...], smq_ref[...]
    nq = MLA_HEADS * LANES
    for j in range(MLA_HEADS):
        qm_ref[0, :, j * LANES:(j + 1) * LANES] = (
            qm[:, j * LANES:(j + 1) * LANES] * cmq
            + qm[:, nq + j * LANES:nq + (j + 1) * LANES] * smq).astype(BF16)

    ckv = _rms(proj(MLA_KV_RANK), mkvn_ref[...]).astype(BF16)
    kpe = proj(LANES) * cmk_ref[...] + proj(LANES) * smk_ref[...]
    kn = jnp.dot(ckv, wukvk_ref[...], preferred_element_type=F32)
    for j in range(MLA_HEADS):
        km_ref[0, j] = (kn[:, j * LANES:(j + 1) * LANES] + kpe).astype(BF16)
    vm = jnp.dot(ckv, wukvv_ref[...], preferred_element_type=F32)
    for j in range(vm.shape[1] // LANES):
        vm_ref[0, j * LANES:(j + 1) * LANES, :] = vm[:, j * LANES:(j + 1) * LANES].T.astype(BF16)


def _kin_call(x, gpre, win_p, gq, gk, tg, mqn, wuq_p, tmq, mkvn, wukvk_p, wukvv, tmk):
    B, L, D = x.shape
    tm = TM_IN
    nt = L // tm

    def full(a):
        return pl.BlockSpec(a.shape, lambda b, i: (0,) * a.ndim, pipeline_mode=pl.Buffered(1))

    def rows(w):
        return pl.BlockSpec((tm, w), lambda b, i: (i, 0))

    in_specs = [pl.BlockSpec((1, tm, D), lambda b, i: (b, i, 0)), full(gpre), full(win_p),
                full(gq[0]), full(gq[1]), full(gk[0]), full(gk[1]), rows(LANES), rows(LANES),
                full(mqn), full(wuq_p), rows(LANES), rows(LANES),
                full(mkvn), full(wukvk_p), full(wukvv), rows(LANES), rows(LANES)]
    gv_rows = GQA_KV_HEADS * GQA_HEAD_DIM
    mv_rows = MLA_HEADS * MLA_V_DIM
    out_shape = [
        jax.ShapeDtypeStruct((B, L, HY_COLS), F32),
        jax.ShapeDtypeStruct((B, L, GQA_HEADS * GQA_HEAD_DIM), BF16),
        jax.ShapeDtypeStruct((B, GQA_KV_HEADS, L, LANES), BF16),
        jax.ShapeDtypeStruct((B, gv_rows, L), BF16),
        jax.ShapeDtypeStruct((B, L, MLA_HEADS * LANES), BF16),
        jax.ShapeDtypeStruct((B, MLA_HEADS, L, LANES), BF16),
        jax.ShapeDtypeStruct((B, mv_rows, L), BF16),
    ]
    out_specs = [
        pl.BlockSpec((1, tm, HY_COLS), lambda b, i: (b, i, 0)),
        pl.BlockSpec((1, tm, GQA_HEADS * GQA_HEAD_DIM), lambda b, i: (b, i, 0)),
        pl.BlockSpec((1, GQA_KV_HEADS, tm, LANES), lambda b, i: (b, 0, i, 0)),
        pl.BlockSpec((1, gv_rows, tm), lambda b, i: (b, 0, i)),
        pl.BlockSpec((1, tm, MLA_HEADS * LANES), lambda b, i: (b, i, 0)),
        pl.BlockSpec((1, MLA_HEADS, tm, LANES), lambda b, i: (b, 0, i, 0)),
        pl.BlockSpec((1, mv_rows, tm), lambda b, i: (b, 0, i)),
    ]
    return pl.pallas_call(
        _kin_kernel, grid=(B, nt), in_specs=in_specs, out_specs=out_specs, out_shape=out_shape,
        compiler_params=_cparams(("parallel", "parallel")), name="in_proj",
    )(x, gpre, win_p, *gq, *gk, *tg, mqn, wuq_p, *tmq, mkvn, wukvk_p, wukvv, *tmk)


def _attn_kernel(q_ref, k_ref, vt_ref, o_ref, sa_ref, sb_ref, ma_ref, mb_ref, *,
                 n_kv, n_rep, pack, tq, n_chunks, tk):
    i = pl.program_id(0)
    cols = n_rep * tq
    width = n_kv * cols
    grp = tk // 8

    @pl.when(i == 0)
    def _():
        sb_ref[...] = jnp.zeros(sb_ref.shape, F32)
        mb_ref[...] = jnp.zeros(mb_ref.shape, F32)

    def step(sw_ref, mw_ref, sr_ref, mr_ref):
        def q_head(h):
            tile, half = divmod(h, pack)
            qt = q_ref[0, tile * LANES:(tile + 1) * LANES, :]
            if pack == 2:
                z = jnp.zeros((LANES // 2, qt.shape[1]), qt.dtype)
                qt = jnp.concatenate([qt[:LANES // 2], z] if half == 0 else [z, qt[LANES // 2:]],
                                     axis=0)
            return qt

        qs = [jnp.concatenate([q_head(a * n_rep + j) for j in range(n_rep)], axis=1)
              for a in range(n_kv)]
        mx = jnp.max(mr_ref[...], axis=0, keepdims=True)
        m = jnp.full((8, width), -jnp.inf, F32)
        l = jnp.zeros((8, width), F32)
        accs = [jnp.zeros((vt_ref.shape[2], cols), F32) for _ in range(n_kv)]
        for c in range(n_chunks):
            st = jnp.concatenate(
                [jnp.dot(k_ref[0, a, c * tk:(c + 1) * tk, :], qs[a], preferred_element_type=F32)
                 for a in range(n_kv)], axis=1)
            sw_ref[c] = st
            m = jnp.maximum(m, jnp.max(st.reshape(grp, 8, width), axis=0))
            p = jnp.exp2(sr_ref[c] - mx)
            l = l + jnp.sum(p.reshape(grp, 8, width), axis=0)
            pb = p.astype(BF16)
            for a in range(n_kv):
                accs[a] = accs[a] + jnp.dot(vt_ref[0, a, :, c * tk:(c + 1) * tk],
                                            pb[:, a * cols:(a + 1) * cols],
                                            preferred_element_type=F32)
        mw_ref[...] = m
        ls = jnp.sum(l, axis=0, keepdims=True)
        heads = []
        for a in range(n_kv):
            oa = accs[a] / ls[:, a * cols:(a + 1) * cols]
            heads += [oa[:, j * tq:(j + 1) * tq] for j in range(n_rep)]
        o_ref[0] = jnp.concatenate(heads, axis=0).T.astype(BF16)

    @pl.when(i % 2 == 0)
    def _():
        step(sa_ref, ma_ref, sb_ref, mb_ref)

    @pl.when(i % 2 == 1)
    def _():
        step(sb_ref, mb_ref, sa_ref, ma_ref)


def _attn_call(qt, k, vt, *, n_kv, n_rep, pack, tq, name):
    B, hq, L = qt.shape
    H = hq * pack // LANES
    hkv, dv = vt.shape[1], vt.shape[2]
    G = hkv // n_kv
    hs = n_kv * n_rep
    tk = TK_ATTN
    n_chunks = L // tk
    nq = L // tq
    width = hs * tq
    kern = functools.partial(_attn_kernel, n_kv=n_kv, n_rep=n_rep, pack=pack, tq=tq,
                             n_chunks=n_chunks, tk=tk)
    total = B * G * nq

    def blk(s):
        return s // (G * nq), (s // nq) % G, s % nq

    def q_map(s):
        b, g, i = blk(jnp.minimum(s, total - 1))
        return b, g, i

    def k_map(s):
        b, g, _ = blk(jnp.minimum(s, total - 1))
        return b, g, 0, 0

    def v_map(s):
        b, g, _ = blk(jnp.maximum(s - 1, 0))
        return b, g, 0, 0

    def o_map(s):
        b, g, i = blk(jnp.maximum(s - 1, 0))
        return b, i, g

    return pl.pallas_call(
        kern, grid=(total + 1,),
        in_specs=[pl.BlockSpec((1, hs * LANES // pack, tq), q_map),
                  pl.BlockSpec((1, n_kv, L, LANES), k_map, pipeline_mode=pl.Buffered(1)),
                  pl.BlockSpec((1, n_kv, dv, L), v_map, pipeline_mode=pl.Buffered(1))],
        out_specs=pl.BlockSpec((1, tq, hs * dv), o_map),
        out_shape=jax.ShapeDtypeStruct((B, L, H * dv), BF16),
        scratch_shapes=[pltpu.VMEM((n_chunks, tk, width), F32), pltpu.VMEM((n_chunks, tk, width), F32),
                        pltpu.VMEM((8, width), F32), pltpu.VMEM((8, width), F32)],
        compiler_params=_cparams(("arbitrary",)), name=name,
    )(qt, k, vt)


def _hconv_kernel(x_ref, xp_ref, xn_ref, w_ref, b_ref, v_ref, x1_ref, x2_ref):
    i = pl.program_id(1)
    x = x_ref[0]
    tl = x.shape[0]
    prev = jnp.where(i > 0, xp_ref[0][7:8, :], 0.0)
    nxt = jnp.where(i < pl.num_programs(1) - 1, xn_ref[0][0:1, :], 0.0)
    r = lax.broadcasted_iota(jnp.int32, x.shape, 0)
    xm = jnp.where(r == 0, prev, pltpu.roll(x, 1, 0))
    xp = jnp.where(r == tl - 1, nxt, pltpu.roll(x, tl - 1, 0))
    uc = xm * w_ref[0:1, :] + x * w_ref[1:2, :] + xp * w_ref[2:3, :] + b_ref[...]
    for r in range(tl // FFT_N2):
        blk = uc[r * FFT_N2:(r + 1) * FFT_N2]
        v_ref[:, r, :] = blk[:, :HY_D]
        x1_ref[:, r, :] = blk[:, HY_D:2 * HY_D]
        x2_ref[:, r, :] = blk[:, 2 * HY_D:]


def _hconv_call(hy_in, w, b):
    B, L, C = hy_in.shape
    tl = TL_HCONV
    nb = tl // 8
    last = L // 8 - 1
    nt = L // tl
    rows = tl // FFT_N2
    out = jax.ShapeDtypeStruct((FFT_N2, B * L // FFT_N2, HY_D), F32)
    ospec = pl.BlockSpec((FFT_N2, rows, HY_D), lambda b_, i: (0, b_ * nt + i, 0))
    return pl.pallas_call(
        _hconv_kernel, grid=(B, L // tl),
        in_specs=[pl.BlockSpec((1, tl, C), lambda b_, i: (b_, i, 0)),
                  pl.BlockSpec((1, 8, C), lambda b_, i: (b_, jnp.maximum(i * nb - 1, 0), 0)),
                  pl.BlockSpec((1, 8, C), lambda b_, i: (b_, jnp.minimum((i + 1) * nb, last), 0)),
                  pl.BlockSpec((3, C), lambda b_, i: (0, 0)),
                  pl.BlockSpec((1, C), lambda b_, i: (0, 0))],
        out_specs=[ospec, ospec, ospec], out_shape=[out, out, out],
        compiler_params=_cparams(("parallel", "parallel")), name="hy_conv3",
    )(hy_in, hy_in, hy_in, w, b)


def _split_bf16(a):
    hi = a.astype(BF16)
    return hi, (a - hi.astype(F32)).astype(BF16)


def _dot3(a, w_hi, w_lo):
    a_hi, a_lo = _split_bf16(a)
    dot = functools.partial(jnp.dot, preferred_element_type=F32)
    return dot(a_hi, w_hi) + dot(a_lo, w_hi) + dot(a_hi, w_lo)


def _filt_kernel(z_ref, w1h_ref, w1l_ref, b1_ref, f1_ref, w2h_ref, w2l_ref, b2_ref, f2_ref,
                 w3h_ref, w3l_ref, win_ref, gaf_ref, o_ref):
    h = jnp.sin(f1_ref[...] * (_dot3(z_ref[...], w1h_ref[...], w1l_ref[...]) + b1_ref[...]))
    h = jnp.sin(f2_ref[...] * (_dot3(h, w2h_ref[...], w2l_ref[...]) + b2_ref[...]))
    k = _dot3(h, w3h_ref[...], w3l_ref[...])
    half = k.shape[0]
    wcols = 4 * HY_D
    per_half = half // FFT_N1
    n1 = lax.broadcasted_iota(jnp.int32, (half, 2 * HY_D), 0) % FFT_N1
    for s in range(2):
        ks = k[:, s * wcols:(s + 1) * wcols]
        kk = jnp.where(n1 >= FFT_N1 // 2, ks[:, 2 * HY_D:], ks[:, :2 * HY_D])
        win = win_ref[s * half:(s + 1) * half, :]
        filt = (kk * jnp.concatenate([win, win], axis=-1)).astype(BF16)
        for j in range(per_half):
            o_ref[s * per_half + j] = _pack_ri(jnp.dot(gaf_ref[s * per_half + j],
                                                       filt[j * FFT_N1:(j + 1) * FFT_N1],
                                                       preferred_element_type=F32))


def _filt_call(z_pack, fw, win_perm, gaf):
    tp = 2 * FILT_HALF
    n = win_perm.shape[0]
    to = tp // FFT_N1
    M = gaf.shape[1]
    Mh = M // 2

    def full(a):
        return pl.BlockSpec(a.shape, lambda i: (0,) * a.ndim)

    return pl.pallas_call(
        _filt_kernel, grid=(n // tp,),
        in_specs=[pl.BlockSpec((FILT_HALF, z_pack.shape[1]), lambda i: (i, 0))]
        + [full(a) for a in fw] + [pl.BlockSpec((tp, HY_D), lambda i: (i, 0)),
                                   pl.BlockSpec((to, M, FFT_N1), lambda i: (i, 0, 0))],
        out_specs=pl.BlockSpec((to, Mh, 2 * HY_D), lambda i: (i, 0, 0)),
        out_shape=jax.ShapeDtypeStruct((n // FFT_N1, Mh, 2 * HY_D), jnp.int32),
        compiler_params=_cparams(("parallel",)), name="hy_filter",
    )(z_pack, *fw, win_perm, gaf)


def _pack_ri(y):
    half = y.shape[0] // 2
    return pltpu.pack_elementwise([y[:half], y[half:]], packed_dtype=BF16)


def _gather_ri(x_ref, j):
    w = x_ref[:, j, :]
    re = pltpu.unpack_elementwise(w, index=0, packed_dtype=BF16, unpacked_dtype=F32)
    im = pltpu.unpack_elementwise(w, index=1, packed_dtype=BF16, unpacked_dtype=F32)
    return jnp.concatenate([re, im], axis=0).astype(BF16)


def _gather_spec(a, to):
    return pl.BlockSpec((a.shape[0], to, a.shape[2]), lambda i: (0, i, 0))


def _bm_kernel(g_ref, x_ref, o_ref, *, to):
    for t in range(to):
        o_ref[t] = _pack_ri(jnp.dot(g_ref[t], x_ref[t].astype(BF16), preferred_element_type=F32))


def _bm_call(g, x, name, to=8):
    O, K, N = x.shape
    M = g.shape[1] // 2
    return pl.pallas_call(
        functools.partial(_bm_kernel, to=to), grid=(O // to,),
        in_specs=[pl.BlockSpec((to, 2 * M, K), lambda i: (i, 0, 0)),
                  pl.BlockSpec((to, K, N), lambda i: (i, 0, 0))],
        out_specs=pl.BlockSpec((to, M, N), lambda i: (i, 0, 0)),
        out_shape=jax.ShapeDtypeStruct((O, M, N), jnp.int32),
        compiler_params=_cparams(("parallel",)), name=name,
    )(g, x)


def _filtb_kernel(mb_ref, x_ref, o_ref, *, to):
    for j in range(to):
        o_ref[j] = jnp.dot(mb_ref[...], _gather_ri(x_ref, j),
                           preferred_element_type=F32).astype(o_ref.dtype)


def _filtb_call(mb, ka, to=8):
    O, P, N = ka.shape
    return pl.pallas_call(
        functools.partial(_filtb_kernel, to=to), grid=(P // to,),
        in_specs=[pl.BlockSpec(mb.shape, lambda i: (0, 0)), _gather_spec(ka, to)],
        out_specs=pl.BlockSpec((to, 2 * O, N), lambda i: (i, 0, 0)),
        out_shape=jax.ShapeDtypeStruct((P, 2 * O, N), BF16),
        compiler_params=_cparams(("parallel",)), name="hy_fft_filt_b",
    )(mb, ka)


def _convb_kernel(mb_ref, gc_ref, x_ref, kf_ref, o_ref, *, to):
    h = FFT_N2
    for j in range(to):
        xs = jnp.dot(mb_ref[...], _gather_ri(x_ref, j), preferred_element_type=F32)
        xr, xi = xs[:h], xs[h:]
        kr, ki = kf_ref[j, :h, :].astype(F32), kf_ref[j, h:, :].astype(F32)
        ys = jnp.concatenate([xr * kr - xi * ki, xr * ki + xi * kr], axis=0).astype(BF16)
        o_ref[j] = _pack_ri(jnp.dot(gc_ref[j], ys, preferred_element_type=F32))


def _convb_call(mb, gc, a, kf, order, to=8):
    O, P, N = a.shape
    return pl.pallas_call(
        functools.partial(_convb_kernel, to=to), grid=(P // to,),
        in_specs=[pl.BlockSpec(mb.shape, lambda i: (0, 0)),
                  pl.BlockSpec((to, 2 * O, 2 * O), lambda i: (i, 0, 0)),
                  _gather_spec(a, to),
                  pl.BlockSpec((to, 2 * O, N), lambda i: (i, 0, order))],
        out_specs=pl.BlockSpec((to, O, N), lambda i: (i, 0, 0)),
        out_shape=jax.ShapeDtypeStruct((P, O, N), jnp.int32),
        compiler_params=_cparams(("parallel",)), name="hy_spec_mul",
    )(mb, gc, a, kf)


def _convd_kernel(md_ref, c_ref, g_ref, u_ref, s_ref, *rest, to, chain):
    if chain:
        ga_ref, z_ref, a_ref = rest
    else:
        (o_ref,) = rest
    for j in range(to):
        y = jnp.dot(md_ref[...], _gather_ri(c_ref, j), preferred_element_type=F32)
        z = g_ref[j] * (y + u_ref[j] * s_ref[...])
        if chain:
            z_ref[j] = z
            a_ref[j] = _pack_ri(jnp.dot(ga_ref[j], z.astype(BF16), preferred_element_type=F32))
        else:
            o_ref[:, j, :] = z


def _convd_call(md, c, gate, u, skip, ga=None, to=8):
    O, P, N = c.shape
    R = md.shape[0]
    tspec = pl.BlockSpec((to, R, N), lambda i: (i, 0, 0))
    in_specs = [pl.BlockSpec(md.shape, lambda i: (0, 0)), _gather_spec(c, to), tspec, tspec,
                pl.BlockSpec((1, N), lambda i: (0, 0))]
    args = [md, c, gate, u, skip.reshape(1, N)]
    if ga is not None:
        M = ga.shape[1]
        in_specs.append(pl.BlockSpec((to, M, R), lambda i: (i, 0, 0)))
        args.append(ga)
        out_specs = [tspec, pl.BlockSpec((to, M // 2, N), lambda i: (i, 0, 0))]
        out_shape = [jax.ShapeDtypeStruct((P, R, N), F32),
                     jax.ShapeDtypeStruct((P, M // 2, N), jnp.int32)]
    else:
        out_specs = pl.BlockSpec((R, to, N), lambda i: (0, i, 0))
        out_shape = jax.ShapeDtypeStruct((R, P, N), F32)
    return pl.pallas_call(
        functools.partial(_convd_kernel, to=to, chain=ga is not None), grid=(P // to,),
        in_specs=in_specs, out_specs=out_specs, out_shape=out_shape,
        compiler_params=_cparams(("parallel",)), name="hy_fft_d",
    )(*args)


def _dft_tables():
    n = FFT_N1 * FFT_N2
    k = np.arange(FFT_N1)
    f = np.exp(-2j * np.pi * np.outer(k, k) / FFT_N1)
    t = np.exp(-2j * np.pi * np.outer(k, k) / n)
    return f, t, n


def _dft_matrices():
    f, t, n = _dft_tables()
    fr, fi = jnp.asarray(f.real, F32), jnp.asarray(f.imag, F32)
    tr, ti = jnp.asarray(t.real, F32), jnp.asarray(t.imag, F32)
    half = FFT_N1 // 2
    er = fr[None] * tr[:, :, None] - fi[None] * ti[:, :, None]
    ei = fr[None] * ti[:, :, None] + fi[None] * tr[:, :, None]
    ga = jnp.concatenate([jnp.concatenate([er[:, :, :half], -ei[:, :, :half]], axis=2),
                          jnp.concatenate([ei[:, :, :half], er[:, :, :half]], axis=2)], axis=1)
    gaf = jnp.concatenate([er, ei], axis=1)
    mb = jnp.concatenate([jnp.concatenate([fr, -fi], axis=1),
                          jnp.concatenate([fi, fr], axis=1)], axis=0)
    tct = jnp.transpose(tr)[:, :, None]
    tst = -jnp.transpose(ti)[:, :, None]
    gr = tct * fr[None] - tst * (-fi[None])
    gi = tct * (-fi[None]) + tst * fr[None]
    gc = jnp.concatenate([jnp.concatenate([gr, -gi], axis=2),
                          jnp.concatenate([gi, gr], axis=2)], axis=1)
    hr, hi = fr[:half] / n, -fi[:half] / n
    md = jnp.concatenate([jnp.concatenate([hr, -hi], axis=1),
                          jnp.concatenate([hi, hr], axis=1)], axis=0)
    return (ga.astype(BF16), gaf.astype(BF16), mb.astype(BF16), gc.astype(BF16), md.astype(BF16))


def _hyena_positions(L):
    p = FFT_N2 * np.arange(FFT_N1)[None, :] + np.arange(FFT_N2)[:, None]
    pos = np.where(p < L, p, 2 * L - 1 - p).reshape(2 * L, 1).astype(np.float64)
    t = pos / (L - 1)
    w = 2.0 * math.pi * pos / L
    f = np.linspace(1e-4, HY_BANDS - 1, HY_BANDS)[None, :]
    z = np.concatenate([t, np.cos(f * w), -np.sin(f * w),
                        np.zeros((2 * L, FILT_FEAT - HY_EMB))], axis=-1)
    z_pack = (z.reshape(-1, 2, FILT_HALF, FILT_FEAT).transpose(0, 2, 1, 3)
              .reshape(-1, 2 * FILT_FEAT))
    max_decay = math.log(HY_DECAY_TARGET) / HY_FAST_DECAY_PCT
    min_decay = math.log(HY_DECAY_TARGET) / HY_SLOW_DECAY_PCT
    deltas = jnp.linspace(min_decay, max_decay, HY_D, dtype=F32)
    window = jnp.exp(-jnp.asarray(t, F32) * jnp.abs(deltas)[None, :])
    return jnp.asarray(z_pack, F32), window


def _filter_weights(w1, b1, f1, w2, b2, f2, w3):
    def bd(w):
        z = jnp.zeros_like(w)
        return jnp.concatenate([jnp.concatenate([w, z], axis=1),
                                jnp.concatenate([z, w], axis=1)], axis=0)

    def twice(v):
        return jnp.concatenate([v, v])[None]

    w1 = jnp.pad(w1, ((0, FILT_FEAT - w1.shape[0]), (0, 0)))
    w3 = w3.reshape(-1, 2, 2, HY_D).transpose(0, 2, 1, 3).reshape(-1, 4 * HY_D)
    return (*_split_bf16(bd(w1)), twice(b1), twice(f1), *_split_bf16(bd(w2)), twice(b2), twice(f2),
            *_split_bf16(bd(w3)))


def _hyena_layer(hy_in, conv_w, conv_b, fw, skip, z_perm, win_perm, mats):
    B, L, _ = hy_in.shape
    ga, gaf, mb, gc, md = mats
    v, x1, x2 = _hconv_call(hy_in, conv_w, conv_b[None])
    ka = _filt_call(z_perm, fw, win_perm, gaf)
    kf = _filtb_call(mb, ka)

    a = _bm_call(ga, v, "hy_fft_a")
    c = _convb_call(mb, gc, a, kf, 0)
    z, a = _convd_call(md, c, x1, v, skip[0], ga=ga)
    c = _convb_call(mb, gc, a, kf, 1)
    z = _convd_call(md, c, x2, z, skip[1])
    return z.reshape(B, L, HY_D)


def _kout_kernel(x_ref, yh_ref, yg_ref, ym_ref, gh_ref, gg_ref, gm_ref, w_ref, gpost_ref, gffn_ref,
                 xo_ref, h_ref):
    a = _rms(yh_ref[0], gh_ref[...]).astype(BF16)
    b = _rms(yg_ref[0].astype(F32), gg_ref[...]).astype(BF16)
    c = _rms(ym_ref[0].astype(F32), gm_ref[...]).astype(BF16)
    o1 = HY_D
    o2 = o1 + GQA_HEADS * GQA_HEAD_DIM
    y = (jnp.dot(a, w_ref[:o1, :], preferred_element_type=F32)
         + jnp.dot(b, w_ref[o1:o2, :], preferred_element_type=F32)
         + jnp.dot(c, w_ref[o2:, :], preferred_element_type=F32))
    xo = x_ref[0] + _rms(y, gpost_ref[...])
    xo_ref[0] = xo
    h_ref[0] = _rms(xo, gffn_ref[...]).astype(BF16)


def _kout_call(x, yh, yg, ym, gh, gg, gm, w_p, gpost, gffn):
    B, L, D = x.shape
    tm = TM_OUT

    def rows(a):
        return pl.BlockSpec((1, tm, a.shape[2]), lambda b, i: (b, i, 0))

    def full(a):
        return pl.BlockSpec(a.shape, lambda b, i: (0,) * a.ndim)

    return pl.pallas_call(
        _kout_kernel, grid=(B, L // tm),
        in_specs=[rows(x), rows(yh), rows(yg), rows(ym), full(gh), full(gg), full(gm), full(w_p),
                  full(gpost), full(gffn)],
        out_specs=[rows(x), rows(x)],
        out_shape=[jax.ShapeDtypeStruct((B, L, D), F32), jax.ShapeDtypeStruct((B, L, D), BF16)],
        compiler_params=_cparams(("parallel", "parallel")), name="out_proj",
    )(x, yh, yg, ym, gh, gg, gm, w_p, gpost, gffn)


HALO = 16


def _ffn_kernel(h_ref, hp_ref, hn_ref, x_ref, wup_ref, cw_ref, cb_ref, wd_ref, gpost_ref,
                o_ref, act_ref):
    i = pl.program_id(1)
    tm = h_ref.shape[1]
    prev = jnp.where(i > 0, hp_ref[0], jnp.zeros_like(hp_ref[0]))
    nxt = jnp.where(i < pl.num_programs(1) - 1, hn_ref[0], jnp.zeros_like(hn_ref[0]))
    he = jnp.concatenate([prev, h_ref[0], nxt], axis=0)
    ext = tm + 2 * HALO
    tf = TF_FFN

    def conv(c0):
        up = jnp.dot(he, wup_ref[:, c0:c0 + tf], preferred_element_type=F32)
        um = pltpu.roll(up, 1, 0)[HALO:HALO + tm]
        upl = pltpu.roll(up, ext - 1, 0)[HALO:HALO + tm]
        return (um * cw_ref[0:1, c0:c0 + tf] + up[HALO:HALO + tm] * cw_ref[1:2, c0:c0 + tf]
                + upl * cw_ref[2:3, c0:c0 + tf] + cb_ref[:, c0:c0 + tf])

    for j in range(D_FF // tf):
        g = conv(j * tf)
        u = conv(D_FF + j * tf)
        gelu = 0.5 * g * (1.0 + jnp.tanh(math.sqrt(2.0 / math.pi) * (g + 0.044715 * (g * g * g))))
        act_ref[:, j * tf:(j + 1) * tf] = (gelu * u).astype(BF16)
    f = jnp.dot(act_ref[...], wd_ref[...], preferred_element_type=F32)
    o_ref[0] = x_ref[0] + _rms(f, gpost_ref[...])


def _ffn_call(h, x, w_up, cw, cb, w_down, gpost):
    B, L, D = x.shape
    tm = TM_FFN
    nb = tm // HALO
    last = L // HALO - 1

    def resident(a):
        return pl.BlockSpec(a.shape, lambda b, i: (0,) * a.ndim, pipeline_mode=pl.Buffered(1))

    cb = cb[None]
    return pl.pallas_call(
        _ffn_kernel, grid=(B, L // tm),
        in_specs=[pl.BlockSpec((1, tm, D), lambda b, i: (b, i, 0)),
                  pl.BlockSpec((1, HALO, D), lambda b, i: (b, jnp.maximum(i * nb - 1, 0), 0)),
                  pl.BlockSpec((1, HALO, D), lambda b, i: (b, jnp.minimum((i + 1) * nb, last), 0)),
                  pl.BlockSpec((1, tm, D), lambda b, i: (b, i, 0)),
                  resident(w_up), resident(cw), resident(cb), resident(w_down), resident(gpost)],
        out_specs=pl.BlockSpec((1, tm, D), lambda b, i: (b, i, 0)),
        out_shape=jax.ShapeDtypeStruct((B, L, D), F32),
        scratch_shapes=[pltpu.VMEM((tm, D_FF), BF16)],
        compiler_params=_cparams(("parallel", "parallel")), name="conv_ffn",
    )(h, h, h, x, w_up, cw, cb, w_down, gpost)


def _axial_tables(L, rot_dim):
    pos = np.arange(L)
    n_axis = rot_dim // 4
    inv = ROPE_THETA ** (-np.arange(n_axis) / n_axis)
    ang = np.concatenate([(pos // GRID_W)[:, None] * inv, (pos % GRID_W)[:, None] * inv], axis=-1)
    return jnp.asarray(np.cos(ang), F32), jnp.asarray(np.sin(ang), F32)


def _rope_tables(L):
    def lanes(parts):
        used = sum(p.shape[1] for p in parts)
        return jnp.concatenate(parts + [jnp.zeros((L, LANES - used), F32)], axis=1)

    cg, sg = _axial_tables(L, GQA_HEAD_DIM)
    tg = (lanes([cg, cg, cg, cg]), lanes([sg, sg, sg, sg]))
    cm, sm = _axial_tables(L, MLA_ROPE_DIM)
    nope0 = jnp.zeros((L, MLA_NOPE_DIM), F32)
    nope1 = jnp.ones((L, MLA_NOPE_DIM), F32)
    tmk = (lanes([nope0, cm, cm]), lanes([nope0, sm, sm]))
    sc = (MLA_NOPE_DIM + MLA_ROPE_DIM) ** -0.5 * LOG2E
    tmq = (lanes([nope1, cm, cm]) * sc, tmk[1] * sc)
    return tg, tmq, tmk


def _partner(w, half, sign=-1.0):
    return jnp.concatenate([sign * w[..., half:], w[..., :half]], axis=-1)


def _gain_pair(g, half):
    return jnp.tile(g, 2)[None], jnp.tile(_partner(g, half, 1.0), 2)[None]


def _pad_heads(w, n_heads, width):
    k = w.shape[0]
    return jnp.pad(w.reshape(k, n_heads, width), ((0, 0), (0, 0), (0, LANES - width))).reshape(
        k, n_heads * LANES)


def kernel(x, mix_pre_norm, w_in, hy_conv_w, hy_conv_b, hy_filt_w1, hy_filt_b1, hy_filt_freq1,
           hy_filt_w2, hy_filt_b2, hy_filt_freq2, hy_filt_w3, hy_skip, gqa_q_norm, gqa_k_norm,
           mla_q_a_norm, mla_w_uq, mla_kv_a_norm, mla_w_ukv, hy_out_norm, gqa_out_norm,
           mla_out_norm, w_out, mix_post_norm, ffn_pre_norm, w_up, ffn_conv_w, ffn_conv_b,
           w_down, ffn_post_norm):
    B, L, D = x.shape
    assert B == 2 and 2 * L == FFT_N1 * FFT_N2 and D == D_MODEL
    depth = w_in.shape[0]
    tg, tmq, tmk = _rope_tables(L)
    z_perm, win_perm = _hyena_positions(L)
    mats = _dft_matrices()

    for l in range(depth):
        wl = w_in[l]
        hd, hh, rh = GQA_HEAD_DIM, GQA_HEAD_DIM // 2, MLA_ROPE_DIM // 2
        o1 = HY_COLS
        o2 = o1 + GQA_HEADS * hd
        o3 = o2 + GQA_KV_HEADS * hd
        o4 = o3 + GQA_KV_HEADS * hd
        o5 = o4 + MLA_Q_RANK
        o6 = o5 + MLA_KV_RANK
        wq = wl[:, o1:o2].reshape(D, GQA_HEADS, hd)
        wk = wl[:, o2:o3].reshape(D, GQA_KV_HEADS, hd)
        wkr = wl[:, o6:]
        pe_pad = ((0, 0), (MLA_NOPE_DIM, LANES - MLA_NOPE_DIM - MLA_ROPE_DIM))
        win_p = jnp.concatenate(
            [wl[:, :o1],
             wq.reshape(D, -1), _partner(wq, hh).reshape(D, -1),
             wk.reshape(D, -1), _partner(wk, hh).reshape(D, -1),
             wl[:, o3:o6], jnp.pad(wkr, pe_pad), jnp.pad(_partner(wkr, rh), pe_pad)],
            axis=1).astype(BF16)
        gq = _gain_pair(gqa_q_norm[l] * (hd ** -0.5 * LOG2E), hh)
        gk = _gain_pair(gqa_k_norm[l], hh)
        wuq = mla_w_uq[l].reshape(MLA_Q_RANK, MLA_HEADS, MLA_NOPE_DIM + MLA_ROPE_DIM)
        wuq_pe = jnp.pad(_partner(wuq[:, :, MLA_NOPE_DIM:], rh),
                         ((0, 0), (0, 0), (MLA_NOPE_DIM, 0)))
        wuq_p = jnp.concatenate(
            [_pad_heads(mla_w_uq[l], MLA_HEADS, MLA_NOPE_DIM + MLA_ROPE_DIM),
             _pad_heads(wuq_pe.reshape(MLA_Q_RANK, -1), MLA_HEADS, MLA_NOPE_DIM + MLA_ROPE_DIM)],
            axis=1).astype(BF16)
        wukv = mla_w_ukv[l].reshape(MLA_KV_RANK, MLA_HEADS, MLA_NOPE_DIM + MLA_V_DIM)
        wukvk_p = _pad_heads(wukv[:, :, :MLA_NOPE_DIM].reshape(MLA_KV_RANK, -1), MLA_HEADS,
                             MLA_NOPE_DIM).astype(BF16)
        wukvv = wukv[:, :, MLA_NOPE_DIM:].reshape(MLA_KV_RANK, MLA_HEADS * MLA_V_DIM).astype(BF16)

        hy_in, qg, kg, vg, qm, km, vm = _kin_call(
            x, mix_pre_norm[l][None], win_p, gq, gk, tg, mla_q_a_norm[l][None], wuq_p, tmq,
            mla_kv_a_norm[l][None], wukvk_p, wukvv, tmk)

        fw = _filter_weights(hy_filt_w1[l], hy_filt_b1[l], hy_filt_freq1[l], hy_filt_w2[l],
                             hy_filt_b2[l], hy_filt_freq2[l], hy_filt_w3[l])
        y_hy = _hyena_layer(hy_in, hy_conv_w[l], hy_conv_b[l], fw, hy_skip[l], z_perm, win_perm, mats)

        y_gqa = _attn_call(qg.transpose(0, 2, 1), kg, vg.reshape(B, GQA_KV_HEADS, GQA_HEAD_DIM, L),
                           n_kv=1, n_rep=GQA_HEADS // GQA_KV_HEADS, pack=2, tq=TQ_GQA,
                           name="attn_gqa")
        y_mla = _attn_call(qm.transpose(0, 2, 1), km, vm.reshape(B, MLA_HEADS, MLA_V_DIM, L),
                           n_kv=2, n_rep=1, pack=1, tq=TQ_MLA, name="attn_mla")

        x, h2 = _kout_call(x, y_hy, y_gqa, y_mla, hy_out_norm[l][None], gqa_out_norm[l][None],
                           mla_out_norm[l][None], w_out[l].astype(BF16),
                           mix_post_norm[l][None], ffn_pre_norm[l][None])
        x = _ffn_call(h2, x, w_up[l].astype(BF16), ffn_conv_w[l], ffn_conv_b[l],
                      w_down[l].astype(BF16), ffn_post_norm[l][None])
    return x
```

```python
import functools
import math

import numpy as np
import jax
import jax.numpy as jnp
from jax import lax
from jax.experimental import pallas as pl
from jax.experimental.pallas import tpu as pltpu

F32 = jnp.float32
BF16 = jnp.bfloat16

NORM_EPS = 1e-6
ROPE_THETA = 10000.0
GRID_W = 64
LOG2E = math.log2(math.e)

D_MODEL = 1024
HY_D = 256
HY_COLS = 3 * HY_D
HY_EMB = 33
HY_BANDS = 16
HY_DECAY_TARGET = 1e-2
HY_FAST_DECAY_PCT = 0.3
HY_SLOW_DECAY_PCT = 1.5
GQA_HEADS = 8
GQA_KV_HEADS = 2
GQA_HEAD_DIM = 64
MLA_HEADS = 4
MLA_Q_RANK = 256
MLA_KV_RANK = 128
MLA_NOPE_DIM = 64
MLA_ROPE_DIM = 32
MLA_V_DIM = 64
D_FF = 2816

LANES = 128
FFT_N1 = 128
FFT_N2 = 128

TM_IN = 512
TK_ATTN = 512
TQ_GQA = 128
TQ_MLA = 256
TM_OUT = 512
TM_FFN = 512
TF_FFN = 256
TL_HCONV = 1024
FILT_HALF = 512
FILT_FEAT = 64
VMEM_LIMIT = 56 * 1024 * 1024


def _cparams(sem):
    return pltpu.CompilerParams(dimension_semantics=sem, vmem_limit_bytes=VMEM_LIMIT)


def _rms(x, g):
    return x * lax.rsqrt(jnp.mean(x * x, axis=-1, keepdims=True) + NORM_EPS) * g


def _kin_kernel(x_ref, gpre_ref, win_ref, gqn_ref, gqs_ref, gkn_ref, gks_ref, cg_ref, sg_ref,
                mqn_ref, wuq_ref, cmq_ref, smq_ref,
                mkvn_ref, wukvk_ref, wukvv_ref, cmk_ref, smk_ref,
                hy_ref, qg_ref, kg_ref, vg_ref, qm_ref, km_ref, vm_ref):
    x = x_ref[0]
    h = _rms(x, gpre_ref[...]).astype(BF16)
    cur = [0]

    def proj(n):
        lo = cur[0]
        cur[0] = lo + n
        return jnp.dot(h, win_ref[:, lo:lo + n], preferred_element_type=F32)

    hy_ref[0] = proj(HY_COLS)

    cg, sg = cg_ref[...], sg_ref[...]
    low = lax.broadcasted_iota(jnp.int32, (x.shape[0], LANES), 1) < GQA_HEAD_DIM

    def head_pairs(n_tiles, gain, gain_sw):
        xa, xb = proj(n_tiles * LANES), proj(n_tiles * LANES)
        ca, sa = cg * gain, sg * gain_sw
        out = []
        for t in range(n_tiles):
            xc, xs = xa[:, t * LANES:(t + 1) * LANES], xb[:, t * LANES:(t + 1) * LANES]
            sq = xc * xc
            tot = jnp.sum(sq, axis=-1, keepdims=True)
            lo = jnp.sum(jnp.where(low, sq, 0.0), axis=-1, keepdims=True)
            ms = jnp.where(low, lo, tot - lo) * (1.0 / GQA_HEAD_DIM)
            out.append((xc * ca + xs * sa) * lax.rsqrt(ms + NORM_EPS))
        return out

    for t, q in enumerate(head_pairs(GQA_HEADS // 2, gqn_ref[...], gqs_ref[...])):
        qg_ref[0, :, t * LANES:(t + 1) * LANES] = q.astype(BF16)
    (kk,) = head_pairs(GQA_KV_HEADS // 2, gkn_ref[...], gks_ref[...])
    ksw = pltpu.roll(kk, GQA_HEAD_DIM, 1)
    kg_ref[0, 0] = jnp.where(low, kk, ksw).astype(BF16)
    kg_ref[0, 1] = jnp.where(low, ksw, kk).astype(BF16)
    vg_ref[0] = proj(LANES).T.astype(BF16)

    cq = _rms(proj(MLA_Q_RANK), mqn_ref[...]).astype(BF16)
    qm = jnp.dot(cq, wuq_ref[...], preferred_element_type=F32)
    cmq, smq = cmq_ref[...], smq_ref[...]
    nq = MLA_HEADS * LANES
    for j in range(MLA_HEADS):
        qm_ref[0, :, j * LANES:(j + 1) * LANES] = (
            qm[:, j * LANES:(j + 1) * LANES] * cmq
            + qm[:, nq + j * LANES:nq + (j + 1) * LANES] * smq).astype(BF16)

    ckv = _rms(proj(MLA_KV_RANK), mkvn_ref[...]).astype(BF16)
    kpe = proj(LANES) * cmk_ref[...] + proj(LANES) * smk_ref[...]
    kn = jnp.dot(ckv, wukvk_ref[...], preferred_element_type=F32)
    for j in range(MLA_HEADS):
        km_ref[0, j] = (kn[:, j * LANES:(j + 1) * LANES] + kpe).astype(BF16)
    vm = jnp.dot(ckv, wukvv_ref[...], preferred_element_type=F32)
    for j in range(vm.shape[1] // LANES):
        vm_ref[0, j * LANES:(j + 1) * LANES, :] = vm[:, j * LANES:(j + 1) * LANES].T.astype(BF16)


def _kin_call(x, gpre, win_p, gq, gk, tg, mqn, wuq_p, tmq, mkvn, wukvk_p, wukvv, tmk):
    B, L, D = x.shape
    tm = TM_IN
    nt = L // tm

    def full(a):
        return pl.BlockSpec(a.shape, lambda b, i: (0,) * a.ndim, pipeline_mode=pl.Buffered(1))

    def rows(w):
        return pl.BlockSpec((tm, w), lambda b, i: (i, 0))

    in_specs = [pl.BlockSpec((1, tm, D), lambda b, i: (b, i, 0)), full(gpre), full(win_p),
                full(gq[0]), full(gq[1]), full(gk[0]), full(gk[1]), rows(LANES), rows(LANES),
                full(mqn), full(wuq_p), rows(LANES), rows(LANES),
                full(mkvn), full(wukvk_p), full(wukvv), rows(LANES), rows(LANES)]
    gv_rows = GQA_KV_HEADS * GQA_HEAD_DIM
    mv_rows = MLA_HEADS * MLA_V_DIM
    out_shape = [
        jax.ShapeDtypeStruct((B, L, HY_COLS), F32),
        jax.ShapeDtypeStruct((B, L, GQA_HEADS * GQA_HEAD_DIM), BF16),
        jax.ShapeDtypeStruct((B, GQA_KV_HEADS, L, LANES), BF16),
        jax.ShapeDtypeStruct((B, gv_rows, L), BF16),
        jax.ShapeDtypeStruct((B, L, MLA_HEADS * LANES), BF16),
        jax.ShapeDtypeStruct((B, MLA_HEADS, L, LANES), BF16),
        jax.ShapeDtypeStruct((B, mv_rows, L), BF16),
    ]
    out_specs = [
        pl.BlockSpec((1, tm, HY_COLS), lambda b, i: (b, i, 0)),
        pl.BlockSpec((1, tm, GQA_HEADS * GQA_HEAD_DIM), lambda b, i: (b, i, 0)),
        pl.BlockSpec((1, GQA_KV_HEADS, tm, LANES), lambda b, i: (b, 0, i, 0)),
        pl.BlockSpec((1, gv_rows, tm), lambda b, i: (b, 0, i)),
        pl.BlockSpec((1, tm, MLA_HEADS * LANES), lambda b, i: (b, i, 0)),
        pl.BlockSpec((1, MLA_HEADS, tm, LANES), lambda b, i: (b, 0, i, 0)),
        pl.BlockSpec((1, mv_rows, tm), lambda b, i: (b, 0, i)),
    ]
    return pl.pallas_call(
        _kin_kernel, grid=(B, nt), in_specs=in_specs, out_specs=out_specs, out_shape=out_shape,
        compiler_params=_cparams(("parallel", "parallel")), name="in_proj",
    )(x, gpre, win_p, *gq, *gk, *tg, mqn, wuq_p, *tmq, mkvn, wukvk_p, wukvv, *tmk)


def _attn_kernel(q_ref, k_ref, vt_ref, o_ref, sa_ref, sb_ref, ma_ref, mb_ref, *,
                 n_kv, n_rep, pack, tq, n_chunks, tk):
    i = pl.program_id(0)
    cols = n_rep * tq
    width = n_kv * cols
    grp = tk // 8

    @pl.when(i == 0)
    def _():
        sb_ref[...] = jnp.zeros(sb_ref.shape, F32)
        mb_ref[...] = jnp.zeros(mb_ref.shape, F32)

    def step(sw_ref, mw_ref, sr_ref, mr_ref):
        def q_head(h):
            tile, half = divmod(h, pack)
            qt = q_ref[0, tile * LANES:(tile + 1) * LANES, :]
            if pack == 2:
                z = jnp.zeros((LANES // 2, qt.shape[1]), qt.dtype)
                qt = jnp.concatenate([qt[:LANES // 2], z] if half == 0 else [z, qt[LANES // 2:]],
                                     axis=0)
            return qt

        qs = [jnp.concatenate([q_head(a * n_rep + j) for j in range(n_rep)], axis=1)
              for a in range(n_kv)]
        mx = jnp.max(mr_ref[...], axis=0, keepdims=True)
        m = jnp.full((8, width), -jnp.inf, F32)
        l = jnp.zeros((8, width), F32)
        accs = [jnp.zeros((vt_ref.shape[2], cols), F32) for _ in range(n_kv)]
        for c in range(n_chunks):
            st = jnp.concatenate(
                [jnp.dot(k_ref[0, a, c * tk:(c + 1) * tk, :], qs[a], preferred_element_type=F32)
                 for a in range(n_kv)], axis=1)
            sw_ref[c] = st
            m = jnp.maximum(m, jnp.max(st.reshape(grp, 8, width), axis=0))
            p = jnp.exp2(sr_ref[c] - mx)
            l = l + jnp.sum(p.reshape(grp, 8, width), axis=0)
            pb = p.astype(BF16)
            for a in range(n_kv):
                accs[a] = accs[a] + jnp.dot(vt_ref[0, a, :, c * tk:(c + 1) * tk],
                                            pb[:, a * cols:(a + 1) * cols],
                                            preferred_element_type=F32)
        mw_ref[...] = m
        ls = jnp.sum(l, axis=0, keepdims=True)
        heads = []
        for a in range(n_kv):
            oa = accs[a] / ls[:, a * cols:(a + 1) * cols]
            heads += [oa[:, j * tq:(j + 1) * tq] for j in range(n_rep)]
        o_ref[0] = jnp.concatenate(heads, axis=0).T.astype(BF16)

    @pl.when(i % 2 == 0)
    def _():
        step(sa_ref, ma_ref, sb_ref, mb_ref)

    @pl.when(i % 2 == 1)
    def _():
        step(sb_ref, mb_ref, sa_ref, ma_ref)


def _attn_call(qt, k, vt, *, n_kv, n_rep, pack, tq, name):
    B, hq, L = qt.shape
    H = hq * pack // LANES
    hkv, dv = vt.shape[1], vt.shape[2]
    G = hkv // n_kv
    hs = n_kv * n_rep
    tk = TK_ATTN
    n_chunks = L // tk
    nq = L // tq
    width = hs * tq
    kern = functools.partial(_attn_kernel, n_kv=n_kv, n_rep=n_rep, pack=pack, tq=tq,
                             n_chunks=n_chunks, tk=tk)
    total = B * G * nq

    def blk(s):
        return s // (G * nq), (s // nq) % G, s % nq

    def q_map(s):
        b, g, i = blk(jnp.minimum(s, total - 1))
        return b, g, i

    def k_map(s):
        b, g, _ = blk(jnp.minimum(s, total - 1))
        return b, g, 0, 0

    def v_map(s):
        b, g, _ = blk(jnp.maximum(s - 1, 0))
        return b, g, 0, 0

    def o_map(s):
        b, g, i = blk(jnp.maximum(s - 1, 0))
        return b, i, g

    return pl.pallas_call(
        kern, grid=(total + 1,),
        in_specs=[pl.BlockSpec((1, hs * LANES // pack, tq), q_map),
                  pl.BlockSpec((1, n_kv, L, LANES), k_map, pipeline_mode=pl.Buffered(1)),
                  pl.BlockSpec((1, n_kv, dv, L), v_map, pipeline_mode=pl.Buffered(1))],
        out_specs=pl.BlockSpec((1, tq, hs * dv), o_map),
        out_shape=jax.ShapeDtypeStruct((B, L, H * dv), BF16),
        scratch_shapes=[pltpu.VMEM((n_chunks, tk, width), F32), pltpu.VMEM((n_chunks, tk, width), F32),
                        pltpu.VMEM((8, width), F32), pltpu.VMEM((8, width), F32)],
        compiler_params=_cparams(("arbitrary",)), name=name,
    )(qt, k, vt)


def _hconv_kernel(x_ref, xp_ref, xn_ref, w_ref, b_ref, v_ref, x1_ref, x2_ref):
    i = pl.program_id(1)
    x = x_ref[0]
    tl = x.shape[0]
    prev = jnp.where(i > 0, xp_ref[0][7:8, :], 0.0)
    nxt = jnp.where(i < pl.num_programs(1) - 1, xn_ref[0][0:1, :], 0.0)
    r = lax.broadcasted_iota(jnp.int32, x.shape, 0)
    xm = jnp.where(r == 0, prev, pltpu.roll(x, 1, 0))
    xp = jnp.where(r == tl - 1, nxt, pltpu.roll(x, tl - 1, 0))
    uc = xm * w_ref[0:1, :] + x * w_ref[1:2, :] + xp * w_ref[2:3, :] + b_ref[...]
    for r in range(tl // FFT_N2):
        blk = uc[r * FFT_N2:(r + 1) * FFT_N2]
        v_ref[:, r, :] = blk[:, :HY_D]
        x1_ref[:, r, :] = blk[:, HY_D:2 * HY_D]
        x2_ref[:, r, :] = blk[:, 2 * HY_D:]


def _hconv_call(hy_in, w, b):
    B, L, C = hy_in.shape
    tl = TL_HCONV
    nb = tl // 8
    last = L // 8 - 1
    nt = L // tl
    rows = tl // FFT_N2
    out = jax.ShapeDtypeStruct((FFT_N2, B * L // FFT_N2, HY_D), F32)
    ospec = pl.BlockSpec((FFT_N2, rows, HY_D), lambda b_, i: (0, b_ * nt + i, 0))
    return pl.pallas_call(
        _hconv_kernel, grid=(B, L // tl),
        in_specs=[pl.BlockSpec((1, tl, C), lambda b_, i: (b_, i, 0)),
                  pl.BlockSpec((1, 8, C), lambda b_, i: (b_, jnp.maximum(i * nb - 1, 0), 0)),
                  pl.BlockSpec((1, 8, C), lambda b_, i: (b_, jnp.minimum((i + 1) * nb, last), 0)),
                  pl.BlockSpec((3, C), lambda b_, i: (0, 0)),
                  pl.BlockSpec((1, C), lambda b_, i: (0, 0))],
        out_specs=[ospec, ospec, ospec], out_shape=[out, out, out],
        compiler_params=_cparams(("parallel", "parallel")), name="hy_conv3",
    )(hy_in, hy_in, hy_in, w, b)


def _split_bf16(a):
    hi = a.astype(BF16)
    return hi, (a - hi.astype(F32)).astype(BF16)


def _dot3(a, w_hi, w_lo):
    a_hi, a_lo = _split_bf16(a)
    dot = functools.partial(jnp.dot, preferred_element_type=F32)
    return dot(a_hi, w_hi) + dot(a_lo, w_hi) + dot(a_hi, w_lo)


def _filt_kernel(z_ref, w1h_ref, w1l_ref, b1_ref, f1_ref, w2h_ref, w2l_ref, b2_ref, f2_ref,
                 w3h_ref, w3l_ref, win_ref, gaf_ref, o_ref):
    h = jnp.sin(f1_ref[...] * (_dot3(z_ref[...], w1h_ref[...], w1l_ref[...]) + b1_ref[...]))
    h = jnp.sin(f2_ref[...] * (_dot3(h, w2h_ref[...], w2l_ref[...]) + b2_ref[...]))
    k = _dot3(h, w3h_ref[...], w3l_ref[...])
    half = k.shape[0]
    wcols = 4 * HY_D
    per_half = half // FFT_N1
    n1 = lax.broadcasted_iota(jnp.int32, (half, 2 * HY_D), 0) % FFT_N1
    for s in range(2):
        ks = k[:, s * wcols:(s + 1) * wcols]
        kk = jnp.where(n1 >= FFT_N1 // 2, ks[:, 2 * HY_D:], ks[:, :2 * HY_D])
        win = win_ref[s * half:(s + 1) * half, :]
        filt = (kk * jnp.concatenate([win, win], axis=-1)).astype(BF16)
        for j in range(per_half):
            o_ref[s * per_half + j] = jnp.dot(gaf_ref[s * per_half + j],
                                              filt[j * FFT_N1:(j + 1) * FFT_N1],
                                              preferred_element_type=F32)


def _filt_call(z_pack, fw, win_perm, gaf):
    tp = 2 * FILT_HALF
    n = win_perm.shape[0]
    to = tp // FFT_N1
    M = gaf.shape[1]

    def full(a):
        return pl.BlockSpec(a.shape, lambda i: (0,) * a.ndim)

    return pl.pallas_call(
        _filt_kernel, grid=(n // tp,),
        in_specs=[pl.BlockSpec((FILT_HALF, z_pack.shape[1]), lambda i: (i, 0))]
        + [full(a) for a in fw] + [pl.BlockSpec((tp, HY_D), lambda i: (i, 0)),
                                   pl.BlockSpec((to, M, FFT_N1), lambda i: (i, 0, 0))],
        out_specs=pl.BlockSpec((to, M, 2 * HY_D), lambda i: (i, 0, 0)),
        out_shape=jax.ShapeDtypeStruct((n // FFT_N1, M, 2 * HY_D), F32),
        compiler_params=_cparams(("parallel",)), name="hy_filter",
    )(z_pack, *fw, win_perm, gaf)


def _bm_kernel(g_ref, x_ref, o_ref, *, to):
    for t in range(to):
        o_ref[t] = jnp.dot(g_ref[t], x_ref[t].astype(BF16), preferred_element_type=F32)


def _bm_call(g, x, name, to=16):
    O, K, N = x.shape
    M = g.shape[1]
    return pl.pallas_call(
        functools.partial(_bm_kernel, to=to), grid=(O // to,),
        in_specs=[pl.BlockSpec((to, M, K), lambda i: (i, 0, 0)),
                  pl.BlockSpec((to, K, N), lambda i: (i, 0, 0))],
        out_specs=pl.BlockSpec((to, M, N), lambda i: (i, 0, 0)),
        out_shape=jax.ShapeDtypeStruct((O, M, N), F32),
        compiler_params=_cparams(("parallel",)), name=name,
    )(g, x)


def _gather_ri(x_ref, j):
    return jnp.concatenate([x_ref[:, 0, j, :], x_ref[:, 1, j, :]], axis=0).astype(BF16)


def _gather_spec(a, to):
    return pl.BlockSpec((a.shape[0], 2, to, a.shape[3]), lambda i: (0, 0, i, 0))


def _filtb_kernel(mb_ref, x_ref, o_ref, *, to):
    for j in range(to):
        o_ref[j] = jnp.dot(mb_ref[...], _gather_ri(x_ref, j),
                           preferred_element_type=F32).astype(o_ref.dtype)


def _filtb_call(mb, ka, to=16):
    O, _, P, N = ka.shape
    return pl.pallas_call(
        functools.partial(_filtb_kernel, to=to), grid=(P // to,),
        in_specs=[pl.BlockSpec(mb.shape, lambda i: (0, 0)), _gather_spec(ka, to)],
        out_specs=pl.BlockSpec((to, 2 * O, N), lambda i: (i, 0, 0)),
        out_shape=jax.ShapeDtypeStruct((P, 2 * O, N), BF16),
        compiler_params=_cparams(("parallel",)), name="hy_fft_filt_b",
    )(mb, ka)


def _convb_kernel(mb_ref, gc_ref, x_ref, kf_ref, o_ref, *, to):
    h = FFT_N2
    for j in range(to):
        xs = jnp.dot(mb_ref[...], _gather_ri(x_ref, j), preferred_element_type=F32)
        xr, xi = xs[:h], xs[h:]
        kr, ki = kf_ref[j, :h, :].astype(F32), kf_ref[j, h:, :].astype(F32)
        ys = jnp.concatenate([xr * kr - xi * ki, xr * ki + xi * kr], axis=0).astype(BF16)
        o_ref[j] = jnp.dot(gc_ref[j], ys, preferred_element_type=F32)


def _convb_call(mb, gc, a, kf, order, to=16):
    O, _, P, N = a.shape
    return pl.pallas_call(
        functools.partial(_convb_kernel, to=to), grid=(P // to,),
        in_specs=[pl.BlockSpec(mb.shape, lambda i: (0, 0)),
                  pl.BlockSpec((to, 2 * O, 2 * O), lambda i: (i, 0, 0)),
                  _gather_spec(a, to),
                  pl.BlockSpec((to, 2 * O, N), lambda i: (i, 0, order))],
        out_specs=pl.BlockSpec((to, 2 * O, N), lambda i: (i, 0, 0)),
        out_shape=jax.ShapeDtypeStruct((P, 2 * O, N), F32),
        compiler_params=_cparams(("parallel",)), name="hy_spec_mul",
    )(mb, gc, a, kf)


def _convd_kernel(md_ref, c_ref, g_ref, u_ref, s_ref, *rest, to, chain):
    if chain:
        ga_ref, z_ref, a_ref = rest
    else:
        (o_ref,) = rest
    for j in range(to):
        y = jnp.dot(md_ref[...], _gather_ri(c_ref, j), preferred_element_type=F32)
        z = g_ref[j] * (y + u_ref[j] * s_ref[...])
        if chain:
            z_ref[j] = z
            a_ref[j] = jnp.dot(ga_ref[j], z.astype(BF16), preferred_element_type=F32)
        else:
            o_ref[:, j, :] = z


def _convd_call(md, c, gate, u, skip, ga=None, to=16):
    O, _, P, N = c.shape
    R = md.shape[0]
    tspec = pl.BlockSpec((to, R, N), lambda i: (i, 0, 0))
    in_specs = [pl.BlockSpec(md.shape, lambda i: (0, 0)), _gather_spec(c, to), tspec, tspec,
                pl.BlockSpec((1, N), lambda i: (0, 0))]
    args = [md, c, gate, u, skip.reshape(1, N)]
    if ga is not None:
        M = ga.shape[1]
        in_specs.append(pl.BlockSpec((to, M, R), lambda i: (i, 0, 0)))
        args.append(ga)
        out_specs = [tspec, pl.BlockSpec((to, M, N), lambda i: (i, 0, 0))]
        out_shape = [jax.ShapeDtypeStruct((P, R, N), F32), jax.ShapeDtypeStruct((P, M, N), F32)]
    else:
        out_specs = pl.BlockSpec((R, to, N), lambda i: (0, i, 0))
        out_shape = jax.ShapeDtypeStruct((R, P, N), F32)
    return pl.pallas_call(
        functools.partial(_convd_kernel, to=to, chain=ga is not None), grid=(P // to,),
        in_specs=in_specs, out_specs=out_specs, out_shape=out_shape,
        compiler_params=_cparams(("parallel",)), name="hy_fft_d",
    )(*args)


def _dft_tables():
    n = FFT_N1 * FFT_N2
    k = np.arange(FFT_N1)
    f = np.exp(-2j * np.pi * np.outer(k, k) / FFT_N1)
    t = np.exp(-2j * np.pi * np.outer(k, k) / n)
    return f, t, n


def _dft_matrices():
    f, t, n = _dft_tables()
    fr, fi = jnp.asarray(f.real, F32), jnp.asarray(f.imag, F32)
    tr, ti = jnp.asarray(t.real, F32), jnp.asarray(t.imag, F32)
    half = FFT_N1 // 2
    er = fr[None] * tr[:, :, None] - fi[None] * ti[:, :, None]
    ei = fr[None] * ti[:, :, None] + fi[None] * tr[:, :, None]
    ga = jnp.concatenate([jnp.concatenate([er[:, :, :half], -ei[:, :, :half]], axis=2),
                          jnp.concatenate([ei[:, :, :half], er[:, :, :half]], axis=2)], axis=1)
    gaf = jnp.concatenate([er, ei], axis=1)
    mb = jnp.concatenate([jnp.concatenate([fr, -fi], axis=1),
                          jnp.concatenate([fi, fr], axis=1)], axis=0)
    tct = jnp.transpose(tr)[:, :, None]
    tst = -jnp.transpose(ti)[:, :, None]
    gr = tct * fr[None] - tst * (-fi[None])
    gi = tct * (-fi[None]) + tst * fr[None]
    gc = jnp.concatenate([jnp.concatenate([gr, -gi], axis=2),
                          jnp.concatenate([gi, gr], axis=2)], axis=1)
    hr, hi = fr[:half] / n, -fi[:half] / n
    md = jnp.concatenate([jnp.concatenate([hr, -hi], axis=1),
                          jnp.concatenate([hi, hr], axis=1)], axis=0)
    return (ga.astype(BF16), gaf.astype(BF16), mb.astype(BF16), gc.astype(BF16), md.astype(BF16))


def _hyena_positions(L):
    p = FFT_N2 * np.arange(FFT_N1)[None, :] + np.arange(FFT_N2)[:, None]
    pos = np.where(p < L, p, 2 * L - 1 - p).reshape(2 * L, 1).astype(np.float64)
    t = pos / (L - 1)
    w = 2.0 * math.pi * pos / L
    f = np.linspace(1e-4, HY_BANDS - 1, HY_BANDS)[None, :]
    z = np.concatenate([t, np.cos(f * w), -np.sin(f * w),
                        np.zeros((2 * L, FILT_FEAT - HY_EMB))], axis=-1)
    z_pack = (z.reshape(-1, 2, FILT_HALF, FILT_FEAT).transpose(0, 2, 1, 3)
              .reshape(-1, 2 * FILT_FEAT))
    max_decay = math.log(HY_DECAY_TARGET) / HY_FAST_DECAY_PCT
    min_decay = math.log(HY_DECAY_TARGET) / HY_SLOW_DECAY_PCT
    deltas = jnp.linspace(min_decay, max_decay, HY_D, dtype=F32)
    window = jnp.exp(-jnp.asarray(t, F32) * jnp.abs(deltas)[None, :])
    return jnp.asarray(z_pack, F32), window


def _filter_weights(w1, b1, f1, w2, b2, f2, w3):
    def bd(w):
        z = jnp.zeros_like(w)
        return jnp.concatenate([jnp.concatenate([w, z], axis=1),
                                jnp.concatenate([z, w], axis=1)], axis=0)

    def twice(v):
        return jnp.concatenate([v, v])[None]

    w1 = jnp.pad(w1, ((0, FILT_FEAT - w1.shape[0]), (0, 0)))
    w3 = w3.reshape(-1, 2, 2, HY_D).transpose(0, 2, 1, 3).reshape(-1, 4 * HY_D)
    return (*_split_bf16(bd(w1)), twice(b1), twice(f1), *_split_bf16(bd(w2)), twice(b2), twice(f2),
            *_split_bf16(bd(w3)))


def _hyena_layer(hy_in, conv_w, conv_b, fw, skip, z_perm, win_perm, mats):
    B, L, _ = hy_in.shape
    ga, gaf, mb, gc, md = mats
    v, x1, x2 = _hconv_call(hy_in, conv_w, conv_b[None])
    ka = _filt_call(z_perm, fw, win_perm, gaf)
    kf = _filtb_call(mb, ka.reshape(FFT_N2, 2, FFT_N1, 2 * HY_D))

    a = _bm_call(ga, v, "hy_fft_a")
    c = _convb_call(mb, gc, a.reshape(FFT_N2, 2, FFT_N1, HY_D), kf, 0)
    z, a = _convd_call(md, c.reshape(FFT_N1, 2, FFT_N2, HY_D), x1, v, skip[0], ga=ga)
    c = _convb_call(mb, gc, a.reshape(FFT_N2, 2, FFT_N1, HY_D), kf, 1)
    z = _convd_call(md, c.reshape(FFT_N1, 2, FFT_N2, HY_D), x2, z, skip[1])
    return z.reshape(B, L, HY_D)


def _kout_kernel(x_ref, yh_ref, yg_ref, ym_ref, gh_ref, gg_ref, gm_ref, w_ref, gpost_ref, gffn_ref,
                 xo_ref, h_ref):
    a = _rms(yh_ref[0], gh_ref[...]).astype(BF16)
    b = _rms(yg_ref[0].astype(F32), gg_ref[...]).astype(BF16)
    c = _rms(ym_ref[0].astype(F32), gm_ref[...]).astype(BF16)
    o1 = HY_D
    o2 = o1 + GQA_HEADS * GQA_HEAD_DIM
    y = (jnp.dot(a, w_ref[:o1, :], preferred_element_type=F32)
         + jnp.dot(b, w_ref[o1:o2, :], preferred_element_type=F32)
         + jnp.dot(c, w_ref[o2:, :], preferred_element_type=F32))
    xo = x_ref[0] + _rms(y, gpost_ref[...])
    xo_ref[0] = xo
    h_ref[0] = _rms(xo, gffn_ref[...]).astype(BF16)


def _kout_call(x, yh, yg, ym, gh, gg, gm, w_p, gpost, gffn):
    B, L, D = x.shape
    tm = TM_OUT

    def rows(a):
        return pl.BlockSpec((1, tm, a.shape[2]), lambda b, i: (b, i, 0))

    def full(a):
        return pl.BlockSpec(a.shape, lambda b, i: (0,) * a.ndim)

    return pl.pallas_call(
        _kout_kernel, grid=(B, L // tm),
        in_specs=[rows(x), rows(yh), rows(yg), rows(ym), full(gh), full(gg), full(gm), full(w_p),
                  full(gpost), full(gffn)],
        out_specs=[rows(x), rows(x)],
        out_shape=[jax.ShapeDtypeStruct((B, L, D), F32), jax.ShapeDtypeStruct((B, L, D), BF16)],
        compiler_params=_cparams(("parallel", "parallel")), name="out_proj",
    )(x, yh, yg, ym, gh, gg, gm, w_p, gpost, gffn)


HALO = 16


def _ffn_kernel(h_ref, hp_ref, hn_ref, x_ref, wup_ref, cw_ref, cb_ref, wd_ref, gpost_ref,
                o_ref, act_ref):
    i = pl.program_id(1)
    tm = h_ref.shape[1]
    prev = jnp.where(i > 0, hp_ref[0], jnp.zeros_like(hp_ref[0]))
    nxt = jnp.where(i < pl.num_programs(1) - 1, hn_ref[0], jnp.zeros_like(hn_ref[0]))
    he = jnp.concatenate([prev, h_ref[0], nxt], axis=0)
    ext = tm + 2 * HALO
    tf = TF_FFN

    def conv(c0):
        up = jnp.dot(he, wup_ref[:, c0:c0 + tf], preferred_element_type=F32)
        um = pltpu.roll(up, 1, 0)[HALO:HALO + tm]
        upl = pltpu.roll(up, ext - 1, 0)[HALO:HALO + tm]
        return (um * cw_ref[0:1, c0:c0 + tf] + up[HALO:HALO + tm] * cw_ref[1:2, c0:c0 + tf]
                + upl * cw_ref[2:3, c0:c0 + tf] + cb_ref[:, c0:c0 + tf])

    for j in range(D_FF // tf):
        g = conv(j * tf)
        u = conv(D_FF + j * tf)
        gelu = 0.5 * g * (1.0 + jnp.tanh(math.sqrt(2.0 / math.pi) * (g + 0.044715 * (g * g * g))))
        act_ref[:, j * tf:(j + 1) * tf] = (gelu * u).astype(BF16)
    f = jnp.dot(act_ref[...], wd_ref[...], preferred_element_type=F32)
    o_ref[0] = x_ref[0] + _rms(f, gpost_ref[...])


def _ffn_call(h, x, w_up, cw, cb, w_down, gpost):
    B, L, D = x.shape
    tm = TM_FFN
    nb = tm // HALO
    last = L // HALO - 1

    def resident(a):
        return pl.BlockSpec(a.shape, lambda b, i: (0,) * a.ndim, pipeline_mode=pl.Buffered(1))

    cb = cb[None]
    return pl.pallas_call(
        _ffn_kernel, grid=(B, L // tm),
        in_specs=[pl.BlockSpec((1, tm, D), lambda b, i: (b, i, 0)),
                  pl.BlockSpec((1, HALO, D), lambda b, i: (b, jnp.maximum(i * nb - 1, 0), 0)),
                  pl.BlockSpec((1, HALO, D), lambda b, i: (b, jnp.minimum((i + 1) * nb, last), 0)),
                  pl.BlockSpec((1, tm, D), lambda b, i: (b, i, 0)),
                  resident(w_up), resident(cw), resident(cb), resident(w_down), resident(gpost)],
        out_specs=pl.BlockSpec((1, tm, D), lambda b, i: (b, i, 0)),
        out_shape=jax.ShapeDtypeStruct((B, L, D), F32),
        scratch_shapes=[pltpu.VMEM((tm, D_FF), BF16)],
        compiler_params=_cparams(("parallel", "parallel")), name="conv_ffn",
    )(h, h, h, x, w_up, cw, cb, w_down, gpost)


def _axial_tables(L, rot_dim):
    pos = np.arange(L)
    n_axis = rot_dim // 4
    inv = ROPE_THETA ** (-np.arange(n_axis) / n_axis)
    ang = np.concatenate([(pos // GRID_W)[:, None] * inv, (pos % GRID_W)[:, None] * inv], axis=-1)
    return jnp.asarray(np.cos(ang), F32), jnp.asarray(np.sin(ang), F32)


def _rope_tables(L):
    def lanes(parts):
        used = sum(p.shape[1] for p in parts)
        return jnp.concatenate(parts + [jnp.zeros((L, LANES - used), F32)], axis=1)

    cg, sg = _axial_tables(L, GQA_HEAD_DIM)
    tg = (lanes([cg, cg, cg, cg]), lanes([sg, sg, sg, sg]))
    cm, sm = _axial_tables(L, MLA_ROPE_DIM)
    nope0 = jnp.zeros((L, MLA_NOPE_DIM), F32)
    nope1 = jnp.ones((L, MLA_NOPE_DIM), F32)
    tmk = (lanes([nope0, cm, cm]), lanes([nope0, sm, sm]))
    sc = (MLA_NOPE_DIM + MLA_ROPE_DIM) ** -0.5 * LOG2E
    tmq = (lanes([nope1, cm, cm]) * sc, tmk[1] * sc)
    return tg, tmq, tmk


def _partner(w, half, sign=-1.0):
    return jnp.concatenate([sign * w[..., half:], w[..., :half]], axis=-1)


def _gain_pair(g, half):
    return jnp.tile(g, 2)[None], jnp.tile(_partner(g, half, 1.0), 2)[None]


def _pad_heads(w, n_heads, width):
    k = w.shape[0]
    return jnp.pad(w.reshape(k, n_heads, width), ((0, 0), (0, 0), (0, LANES - width))).reshape(
        k, n_heads * LANES)


def kernel(x, mix_pre_norm, w_in, hy_conv_w, hy_conv_b, hy_filt_w1, hy_filt_b1, hy_filt_freq1,
           hy_filt_w2, hy_filt_b2, hy_filt_freq2, hy_filt_w3, hy_skip, gqa_q_norm, gqa_k_norm,
           mla_q_a_norm, mla_w_uq, mla_kv_a_norm, mla_w_ukv, hy_out_norm, gqa_out_norm,
           mla_out_norm, w_out, mix_post_norm, ffn_pre_norm, w_up, ffn_conv_w, ffn_conv_b,
           w_down, ffn_post_norm):
    B, L, D = x.shape
    assert B == 2 and 2 * L == FFT_N1 * FFT_N2 and D == D_MODEL
    depth = w_in.shape[0]
    tg, tmq, tmk = _rope_tables(L)
    z_perm, win_perm = _hyena_positions(L)
    mats = _dft_matrices()

    for l in range(depth):
        wl = w_in[l]
        hd, hh, rh = GQA_HEAD_DIM, GQA_HEAD_DIM // 2, MLA_ROPE_DIM // 2
        o1 = HY_COLS
        o2 = o1 + GQA_HEADS * hd
        o3 = o2 + GQA_KV_HEADS * hd
        o4 = o3 + GQA_KV_HEADS * hd
        o5 = o4 + MLA_Q_RANK
        o6 = o5 + MLA_KV_RANK
        wq = wl[:, o1:o2].reshape(D, GQA_HEADS, hd)
        wk = wl[:, o2:o3].reshape(D, GQA_KV_HEADS, hd)
        wkr = wl[:, o6:]
        pe_pad = ((0, 0), (MLA_NOPE_DIM, LANES - MLA_NOPE_DIM - MLA_ROPE_DIM))
        win_p = jnp.concatenate(
            [wl[:, :o1],
             wq.reshape(D, -1), _partner(wq, hh).reshape(D, -1),
             wk.reshape(D, -1), _partner(wk, hh).reshape(D, -1),
             wl[:, o3:o6], jnp.pad(wkr, pe_pad), jnp.pad(_partner(wkr, rh), pe_pad)],
            axis=1).astype(BF16)
        gq = _gain_pair(gqa_q_norm[l] * (hd ** -0.5 * LOG2E), hh)
        gk = _gain_pair(gqa_k_norm[l], hh)
        wuq = mla_w_uq[l].reshape(MLA_Q_RANK, MLA_HEADS, MLA_NOPE_DIM + MLA_ROPE_DIM)
        wuq_pe = jnp.pad(_partner(wuq[:, :, MLA_NOPE_DIM:], rh),
                         ((0, 0), (0, 0), (MLA_NOPE_DIM, 0)))
        wuq_p = jnp.concatenate(
            [_pad_heads(mla_w_uq[l], MLA_HEADS, MLA_NOPE_DIM + MLA_ROPE_DIM),
             _pad_heads(wuq_pe.reshape(MLA_Q_RANK, -1), MLA_HEADS, MLA_NOPE_DIM + MLA_ROPE_DIM)],
            axis=1).astype(BF16)
        wukv = mla_w_ukv[l].reshape(MLA_KV_RANK, MLA_HEADS, MLA_NOPE_DIM + MLA_V_DIM)
        wukvk_p = _pad_heads(wukv[:, :, :MLA_NOPE_DIM].reshape(MLA_KV_RANK, -1), MLA_HEADS,
                             MLA_NOPE_DIM).astype(BF16)
        wukvv = wukv[:, :, MLA_NOPE_DIM:].reshape(MLA_KV_RANK, MLA_HEADS * MLA_V_DIM).astype(BF16)

        hy_in, qg, kg, vg, qm, km, vm = _kin_call(
            x, mix_pre_norm[l][None], win_p, gq, gk, tg, mla_q_a_norm[l][None], wuq_p, tmq,
            mla_kv_a_norm[l][None], wukvk_p, wukvv, tmk)

        fw = _filter_weights(hy_filt_w1[l], hy_filt_b1[l], hy_filt_freq1[l], hy_filt_w2[l],
                             hy_filt_b2[l], hy_filt_freq2[l], hy_filt_w3[l])
        y_hy = _hyena_layer(hy_in, hy_conv_w[l], hy_conv_b[l], fw, hy_skip[l], z_perm, win_perm, mats)

        y_gqa = _attn_call(qg.transpose(0, 2, 1), kg, vg.reshape(B, GQA_KV_HEADS, GQA_HEAD_DIM, L),
                           n_kv=1, n_rep=GQA_HEADS // GQA_KV_HEADS, pack=2, tq=TQ_GQA,
                           name="attn_gqa")
        y_mla = _attn_call(qm.transpose(0, 2, 1), km, vm.reshape(B, MLA_HEADS, MLA_V_DIM, L),
                           n_kv=2, n_rep=1, pack=1, tq=TQ_MLA, name="attn_mla")

        x, h2 = _kout_call(x, y_hy, y_gqa, y_mla, hy_out_norm[l][None], gqa_out_norm[l][None],
                           mla_out_norm[l][None], w_out[l].astype(BF16),
                           mix_post_norm[l][None], ffn_pre_norm[l][None])
        x = _ffn_call(h2, x, w_up[l].astype(BF16), ffn_conv_w[l], ffn_conv_b[l],
                      w_down[l].astype(BF16), ffn_post_norm[l][None])
    return x
```

```python
import functools
import math

import numpy as np
import jax
import jax.numpy as jnp
from jax import lax
from jax.experimental import pallas as pl
from jax.experimental.pallas import tpu as pltpu

F32 = jnp.float32
BF16 = jnp.bfloat16

NORM_EPS = 1e-6
ROPE_THETA = 10000.0
GRID_W = 64
LOG2E = math.log2(math.e)

D_MODEL = 1024
HY_D = 256
HY_COLS = 3 * HY_D
HY_EMB = 33
HY_BANDS = 16
HY_DECAY_TARGET = 1e-2
HY_FAST_DECAY_PCT = 0.3
HY_SLOW_DECAY_PCT = 1.5
GQA_HEADS = 8
GQA_KV_HEADS = 2
GQA_HEAD_DIM = 64
MLA_HEADS = 4
MLA_Q_RANK = 256
MLA_KV_RANK = 128
MLA_NOPE_DIM = 64
MLA_ROPE_DIM = 32
MLA_V_DIM = 64
D_FF = 2816

LANES = 128
FFT_N1 = 128
FFT_N2 = 128

TM_IN = 512
TK_ATTN = 512
TQ_GQA = 128
TQ_MLA = 256
TM_FFN = 512
TF_FFN = 256
TL_HCONV = 1024
FILT_HALF = 512
FILT_FEAT = 64
VMEM_LIMIT = 56 * 1024 * 1024


def _cparams(sem):
    return pltpu.CompilerParams(dimension_semantics=sem, vmem_limit_bytes=VMEM_LIMIT)


def _rms(x, g):
    return x * lax.rsqrt(jnp.mean(x * x, axis=-1, keepdims=True) + NORM_EPS) * g


def _kin_kernel(x_ref, gpre_ref, win_ref, gqn_ref, gqs_ref, gkn_ref, gks_ref, cg_ref, sg_ref,
                mqn_ref, wuq_ref, cmq_ref, smq_ref,
                mkvn_ref, wukvk_ref, wukvv_ref, cmk_ref, smk_ref,
                hy_ref, qg_ref, kg_ref, vg_ref, qm_ref, km_ref, vm_ref):
    x = x_ref[0]
    h = _rms(x, gpre_ref[...]).astype(BF16)
    cur = [0]

    def proj(n):
        lo = cur[0]
        cur[0] = lo + n
        return jnp.dot(h, win_ref[:, lo:lo + n], preferred_element_type=F32)

    hy_ref[0] = proj(HY_COLS)

    cg, sg = cg_ref[...], sg_ref[...]
    low = lax.broadcasted_iota(jnp.int32, (x.shape[0], LANES), 1) < GQA_HEAD_DIM

    def head_pairs(n_tiles, gain, gain_sw):
        xa, xb = proj(n_tiles * LANES), proj(n_tiles * LANES)
        ca, sa = cg * gain, sg * gain_sw
        out = []
        for t in range(n_tiles):
            xc, xs = xa[:, t * LANES:(t + 1) * LANES], xb[:, t * LANES:(t + 1) * LANES]
            sq = xc * xc
            tot = jnp.sum(sq, axis=-1, keepdims=True)
            lo = jnp.sum(jnp.where(low, sq, 0.0), axis=-1, keepdims=True)
            ms = jnp.where(low, lo, tot - lo) * (1.0 / GQA_HEAD_DIM)
            out.append((xc * ca + xs * sa) * lax.rsqrt(ms + NORM_EPS))
        return out

    for t, q in enumerate(head_pairs(GQA_HEADS // 2, gqn_ref[...], gqs_ref[...])):
        qg_ref[0, :, t * LANES:(t + 1) * LANES] = q.astype(BF16)
    (kk,) = head_pairs(GQA_KV_HEADS // 2, gkn_ref[...], gks_ref[...])
    ksw = pltpu.roll(kk, GQA_HEAD_DIM, 1)
    kg_ref[0, 0] = jnp.where(low, kk, ksw).astype(BF16)
    kg_ref[0, 1] = jnp.where(low, ksw, kk).astype(BF16)
    vg_ref[0] = proj(LANES).T.astype(BF16)

    cq = _rms(proj(MLA_Q_RANK), mqn_ref[...]).astype(BF16)
    qm = jnp.dot(cq, wuq_ref[...], preferred_element_type=F32)
    cmq, smq = cmq_ref[...], smq_ref[...]
    nq = MLA_HEADS * LANES
    for j in range(MLA_HEADS):
        qm_ref[0, :, j * LANES:(j + 1) * LANES] = (
            qm[:, j * LANES:(j + 1) * LANES] * cmq
            + qm[:, nq + j * LANES:nq + (j + 1) * LANES] * smq).astype(BF16)

    ckv = _rms(proj(MLA_KV_RANK), mkvn_ref[...]).astype(BF16)
    kpe = proj(LANES) * cmk_ref[...] + proj(LANES) * smk_ref[...]
    kn = jnp.dot(ckv, wukvk_ref[...], preferred_element_type=F32)
    for j in range(MLA_HEADS):
        km_ref[0, j] = (kn[:, j * LANES:(j + 1) * LANES] + kpe).astype(BF16)
    vm = jnp.dot(ckv, wukvv_ref[...], preferred_element_type=F32)
    for j in range(vm.shape[1] // LANES):
        vm_ref[0, j * LANES:(j + 1) * LANES, :] = vm[:, j * LANES:(j + 1) * LANES].T.astype(BF16)


def _kin_call(x, gpre, win_p, gq, gk, tg, mqn, wuq_p, tmq, mkvn, wukvk_p, wukvv, tmk):
    B, L, D = x.shape
    tm = TM_IN
    nt = L // tm

    def full(a):
        return pl.BlockSpec(a.shape, lambda b, i: (0,) * a.ndim, pipeline_mode=pl.Buffered(1))

    def rows(w):
        return pl.BlockSpec((tm, w), lambda b, i: (i, 0))

    in_specs = [pl.BlockSpec((1, tm, D), lambda b, i: (b, i, 0)), full(gpre), full(win_p),
                full(gq[0]), full(gq[1]), full(gk[0]), full(gk[1]), rows(LANES), rows(LANES),
                full(mqn), full(wuq_p), rows(LANES), rows(LANES),
                full(mkvn), full(wukvk_p), full(wukvv), rows(LANES), rows(LANES)]
    gv_rows = GQA_KV_HEADS * GQA_HEAD_DIM
    mv_rows = MLA_HEADS * MLA_V_DIM
    out_shape = [
        jax.ShapeDtypeStruct((B, L, HY_COLS), F32),
        jax.ShapeDtypeStruct((B, L, GQA_HEADS * GQA_HEAD_DIM), BF16),
        jax.ShapeDtypeStruct((B, GQA_KV_HEADS, L, LANES), BF16),
        jax.ShapeDtypeStruct((B, gv_rows, L), BF16),
        jax.ShapeDtypeStruct((B, L, MLA_HEADS * LANES), BF16),
        jax.ShapeDtypeStruct((B, MLA_HEADS, L, LANES), BF16),
        jax.ShapeDtypeStruct((B, mv_rows, L), BF16),
    ]
    out_specs = [
        pl.BlockSpec((1, tm, HY_COLS), lambda b, i: (b, i, 0)),
        pl.BlockSpec((1, tm, GQA_HEADS * GQA_HEAD_DIM), lambda b, i: (b, i, 0)),
        pl.BlockSpec((1, GQA_KV_HEADS, tm, LANES), lambda b, i: (b, 0, i, 0)),
        pl.BlockSpec((1, gv_rows, tm), lambda b, i: (b, 0, i)),
        pl.BlockSpec((1, tm, MLA_HEADS * LANES), lambda b, i: (b, i, 0)),
        pl.BlockSpec((1, MLA_HEADS, tm, LANES), lambda b, i: (b, 0, i, 0)),
        pl.BlockSpec((1, mv_rows, tm), lambda b, i: (b, 0, i)),
    ]
    return pl.pallas_call(
        _kin_kernel, grid=(B, nt), in_specs=in_specs, out_specs=out_specs, out_shape=out_shape,
        compiler_params=_cparams(("parallel", "parallel")), name="in_proj",
    )(x, gpre, win_p, *gq, *gk, *tg, mqn, wuq_p, *tmq, mkvn, wukvk_p, wukvv, *tmk)


def _attn_kernel(q_ref, k_ref, vt_ref, o_ref, sa_ref, sb_ref, ma_ref, mb_ref, *,
                 n_kv, n_rep, pack, tq, n_chunks, tk):
    i = pl.program_id(0)
    cols = n_rep * tq
    width = n_kv * cols
    grp = tk // 8

    @pl.when(i == 0)
    def _():
        sb_ref[...] = jnp.zeros(sb_ref.shape, F32)
        mb_ref[...] = jnp.zeros(mb_ref.shape, F32)

    def step(sw_ref, mw_ref, sr_ref, mr_ref):
        def q_head(h):
            tile, half = divmod(h, pack)
            qt = q_ref[0, tile * LANES:(tile + 1) * LANES, :]
            if pack == 2:
                z = jnp.zeros((LANES // 2, qt.shape[1]), qt.dtype)
                qt = jnp.concatenate([qt[:LANES // 2], z] if half == 0 else [z, qt[LANES // 2:]],
                                     axis=0)
            return qt

        qs = [jnp.concatenate([q_head(a * n_rep + j) for j in range(n_rep)], axis=1)
              for a in range(n_kv)]
        mx = jnp.max(mr_ref[...], axis=0, keepdims=True)
        m = jnp.full((8, width), -jnp.inf, F32)
        l = jnp.zeros((8, width), F32)
        accs = [jnp.zeros((vt_ref.shape[2], cols), F32) for _ in range(n_kv)]
        for c in range(n_chunks):
            st = jnp.concatenate(
                [jnp.dot(k_ref[0, a, c * tk:(c + 1) * tk, :], qs[a], preferred_element_type=F32)
                 for a in range(n_kv)], axis=1)
            sw_ref[c] = st
            m = jnp.maximum(m, jnp.max(st.reshape(grp, 8, width), axis=0))
            p = jnp.exp2(sr_ref[c] - mx)
            l = l + jnp.sum(p.reshape(grp, 8, width), axis=0)
            pb = p.astype(BF16)
            for a in range(n_kv):
                accs[a] = accs[a] + jnp.dot(vt_ref[0, a, :, c * tk:(c + 1) * tk],
                                            pb[:, a * cols:(a + 1) * cols],
                                            preferred_element_type=F32)
        mw_ref[...] = m
        ls = jnp.sum(l, axis=0, keepdims=True)
        heads = []
        for a in range(n_kv):
            oa = accs[a] / ls[:, a * cols:(a + 1) * cols]
            heads += [oa[:, j * tq:(j + 1) * tq] for j in range(n_rep)]
        o_ref[0] = jnp.concatenate(heads, axis=0).T.astype(BF16)

    @pl.when(i % 2 == 0)
    def _():
        step(sa_ref, ma_ref, sb_ref, mb_ref)

    @pl.when(i % 2 == 1)
    def _():
        step(sb_ref, mb_ref, sa_ref, ma_ref)


def _attn_call(qt, k, vt, *, n_kv, n_rep, pack, tq, name):
    B, hq, L = qt.shape
    H = hq * pack // LANES
    hkv, dv = vt.shape[1], vt.shape[2]
    G = hkv // n_kv
    hs = n_kv * n_rep
    tk = TK_ATTN
    n_chunks = L // tk
    nq = L // tq
    width = hs * tq
    kern = functools.partial(_attn_kernel, n_kv=n_kv, n_rep=n_rep, pack=pack, tq=tq,
                             n_chunks=n_chunks, tk=tk)
    total = B * G * nq

    def blk(s):
        return s // (G * nq), (s // nq) % G, s % nq

    def q_map(s):
        b, g, i = blk(jnp.minimum(s, total - 1))
        return b, g, i

    def k_map(s):
        b, g, _ = blk(jnp.minimum(s, total - 1))
        return b, g, 0, 0

    def v_map(s):
        b, g, _ = blk(jnp.maximum(s - 1, 0))
        return b, g, 0, 0

    def o_map(s):
        b, g, i = blk(jnp.maximum(s - 1, 0))
        return b, i, g

    return pl.pallas_call(
        kern, grid=(total + 1,),
        in_specs=[pl.BlockSpec((1, hs * LANES // pack, tq), q_map),
                  pl.BlockSpec((1, n_kv, L, LANES), k_map, pipeline_mode=pl.Buffered(1)),
                  pl.BlockSpec((1, n_kv, dv, L), v_map, pipeline_mode=pl.Buffered(1))],
        out_specs=pl.BlockSpec((1, tq, hs * dv), o_map),
        out_shape=jax.ShapeDtypeStruct((B, L, H * dv), BF16),
        scratch_shapes=[pltpu.VMEM((n_chunks, tk, width), F32), pltpu.VMEM((n_chunks, tk, width), F32),
                        pltpu.VMEM((8, width), F32), pltpu.VMEM((8, width), F32)],
        compiler_params=_cparams(("arbitrary",)), name=name,
    )(qt, k, vt)


def _hconv_kernel(x_ref, xp_ref, xn_ref, w_ref, b_ref, v_ref, x1_ref, x2_ref):
    i = pl.program_id(1)
    x = x_ref[0]
    tl = x.shape[0]
    prev = jnp.where(i > 0, xp_ref[0][7:8, :], 0.0)
    nxt = jnp.where(i < pl.num_programs(1) - 1, xn_ref[0][0:1, :], 0.0)
    r = lax.broadcasted_iota(jnp.int32, x.shape, 0)
    xm = jnp.where(r == 0, prev, pltpu.roll(x, 1, 0))
    xp = jnp.where(r == tl - 1, nxt, pltpu.roll(x, tl - 1, 0))
    uc = xm * w_ref[0:1, :] + x * w_ref[1:2, :] + xp * w_ref[2:3, :] + b_ref[...]
    for r in range(tl // FFT_N2):
        blk = uc[r * FFT_N2:(r + 1) * FFT_N2]
        v_ref[:, r, :] = blk[:, :HY_D]
        x1_ref[:, r, :] = blk[:, HY_D:2 * HY_D]
        x2_ref[:, r, :] = blk[:, 2 * HY_D:]


def _hconv_call(hy_in, w, b):
    B, L, C = hy_in.shape
    tl = TL_HCONV
    nb = tl // 8
    last = L // 8 - 1
    nt = L // tl
    rows = tl // FFT_N2
    out = jax.ShapeDtypeStruct((FFT_N2, B * L // FFT_N2, HY_D), F32)
    ospec = pl.BlockSpec((FFT_N2, rows, HY_D), lambda b_, i: (0, b_ * nt + i, 0))
    return pl.pallas_call(
        _hconv_kernel, grid=(B, L // tl),
        in_specs=[pl.BlockSpec((1, tl, C), lambda b_, i: (b_, i, 0)),
                  pl.BlockSpec((1, 8, C), lambda b_, i: (b_, jnp.maximum(i * nb - 1, 0), 0)),
                  pl.BlockSpec((1, 8, C), lambda b_, i: (b_, jnp.minimum((i + 1) * nb, last), 0)),
                  pl.BlockSpec((3, C), lambda b_, i: (0, 0)),
                  pl.BlockSpec((1, C), lambda b_, i: (0, 0))],
        out_specs=[ospec, ospec, ospec], out_shape=[out, out, out],
        compiler_params=_cparams(("parallel", "parallel")), name="hy_conv3",
    )(hy_in, hy_in, hy_in, w, b)


def _split_bf16(a):
    hi = a.astype(BF16)
    return hi, (a - hi.astype(F32)).astype(BF16)


def _dot3(a, w_hi, w_lo):
    a_hi, a_lo = _split_bf16(a)
    dot = functools.partial(jnp.dot, preferred_element_type=F32)
    return dot(a_hi, w_hi) + dot(a_lo, w_hi) + dot(a_hi, w_lo)


def _filt_kernel(z_ref, w1h_ref, w1l_ref, b1_ref, f1_ref, w2h_ref, w2l_ref, b2_ref, f2_ref,
                 w3h_ref, w3l_ref, win_ref, gaf_ref, o_ref):
    h = jnp.sin(f1_ref[...] * (_dot3(z_ref[...], w1h_ref[...], w1l_ref[...]) + b1_ref[...]))
    h = jnp.sin(f2_ref[...] * (_dot3(h, w2h_ref[...], w2l_ref[...]) + b2_ref[...]))
    k = _dot3(h, w3h_ref[...], w3l_ref[...])
    half = k.shape[0]
    wcols = 4 * HY_D
    per_half = half // FFT_N1
    n1 = lax.broadcasted_iota(jnp.int32, (half, 2 * HY_D), 0) % FFT_N1
    for s in range(2):
        ks = k[:, s * wcols:(s + 1) * wcols]
        kk = jnp.where(n1 >= FFT_N1 // 2, ks[:, 2 * HY_D:], ks[:, :2 * HY_D])
        win = win_ref[s * half:(s + 1) * half, :]
        filt = (kk * jnp.concatenate([win, win], axis=-1)).astype(BF16)
        for j in range(per_half):
            o_ref[s * per_half + j] = jnp.dot(gaf_ref[s * per_half + j],
                                              filt[j * FFT_N1:(j + 1) * FFT_N1],
                                              preferred_element_type=F32)


def _filt_call(z_pack, fw, win_perm, gaf):
    tp = 2 * FILT_HALF
    n = win_perm.shape[0]
    to = tp // FFT_N1
    M = gaf.shape[1]

    def full(a):
        return pl.BlockSpec(a.shape, lambda i: (0,) * a.ndim)

    return pl.pallas_call(
        _filt_kernel, grid=(n // tp,),
        in_specs=[pl.BlockSpec((FILT_HALF, z_pack.shape[1]), lambda i: (i, 0))]
        + [full(a) for a in fw] + [pl.BlockSpec((tp, HY_D), lambda i: (i, 0)),
                                   pl.BlockSpec((to, M, FFT_N1), lambda i: (i, 0, 0))],
        out_specs=pl.BlockSpec((to, M, 2 * HY_D), lambda i: (i, 0, 0)),
        out_shape=jax.ShapeDtypeStruct((n // FFT_N1, M, 2 * HY_D), F32),
        compiler_params=_cparams(("parallel",)), name="hy_filter",
    )(z_pack, *fw, win_perm, gaf)


def _bm_kernel(g_ref, x_ref, o_ref, *, to):
    for t in range(to):
        o_ref[t] = jnp.dot(g_ref[t], x_ref[t].astype(BF16), preferred_element_type=F32)


def _bm_call(g, x, name, to=8):
    O, K, N = x.shape
    M = g.shape[1]
    return pl.pallas_call(
        functools.partial(_bm_kernel, to=to), grid=(O // to,),
        in_specs=[pl.BlockSpec((to, M, K), lambda i: (i, 0, 0)),
                  pl.BlockSpec((to, K, N), lambda i: (i, 0, 0))],
        out_specs=pl.BlockSpec((to, M, N), lambda i: (i, 0, 0)),
        out_shape=jax.ShapeDtypeStruct((O, M, N), F32),
        compiler_params=_cparams(("parallel",)), name=name,
    )(g, x)


def _gather_ri(x_ref, j):
    return jnp.concatenate([x_ref[:, 0, j, :], x_ref[:, 1, j, :]], axis=0).astype(BF16)


def _gather_spec(a, to):
    return pl.BlockSpec((a.shape[0], 2, to, a.shape[3]), lambda i: (0, 0, i, 0))


def _filtb_kernel(mb_ref, x_ref, o_ref, *, to):
    for j in range(to):
        o_ref[j] = jnp.dot(mb_ref[...], _gather_ri(x_ref, j),
                           preferred_element_type=F32).astype(o_ref.dtype)


def _filtb_call(mb, ka, to=8):
    O, _, P, N = ka.shape
    return pl.pallas_call(
        functools.partial(_filtb_kernel, to=to), grid=(P // to,),
        in_specs=[pl.BlockSpec(mb.shape, lambda i: (0, 0)), _gather_spec(ka, to)],
        out_specs=pl.BlockSpec((to, 2 * O, N), lambda i: (i, 0, 0)),
        out_shape=jax.ShapeDtypeStruct((P, 2 * O, N), BF16),
        compiler_params=_cparams(("parallel",)), name="hy_fft_filt_b",
    )(mb, ka)


def _convb_kernel(mb_ref, gc_ref, x_ref, kf_ref, o_ref, *, to):
    h = FFT_N2
    for j in range(to):
        xs = jnp.dot(mb_ref[...], _gather_ri(x_ref, j), preferred_element_type=F32)
        xr, xi = xs[:h], xs[h:]
        kr, ki = kf_ref[j, :h, :].astype(F32), kf_ref[j, h:, :].astype(F32)
        ys = jnp.concatenate([xr * kr - xi * ki, xr * ki + xi * kr], axis=0).astype(BF16)
        o_ref[j] = jnp.dot(gc_ref[j], ys, preferred_element_type=F32)


def _convb_call(mb, gc, a, kf, order, to=8):
    O, _, P, N = a.shape
    return pl.pallas_call(
        functools.partial(_convb_kernel, to=to), grid=(P // to,),
        in_specs=[pl.BlockSpec(mb.shape, lambda i: (0, 0)),
                  pl.BlockSpec((to, 2 * O, 2 * O), lambda i: (i, 0, 0)),
                  _gather_spec(a, to),
                  pl.BlockSpec((to, 2 * O, N), lambda i: (i, 0, order))],
        out_specs=pl.BlockSpec((to, 2 * O, N), lambda i: (i, 0, 0)),
        out_shape=jax.ShapeDtypeStruct((P, 2 * O, N), F32),
        compiler_params=_cparams(("parallel",)), name="hy_spec_mul",
    )(mb, gc, a, kf)


def _convd_kernel(md_ref, c_ref, g_ref, u_ref, s_ref, *rest, to, chain):
    if chain:
        ga_ref, z_ref, a_ref = rest
    else:
        (o_ref,) = rest
    for j in range(to):
        y = jnp.dot(md_ref[...], _gather_ri(c_ref, j), preferred_element_type=F32)
        z = g_ref[j] * (y + u_ref[j] * s_ref[...])
        if chain:
            z_ref[j] = z
            a_ref[j] = jnp.dot(ga_ref[j], z.astype(BF16), preferred_element_type=F32)
        else:
            o_ref[:, j, :] = z


def _convd_call(md, c, gate, u, skip, ga=None, to=8):
    O, _, P, N = c.shape
    R = md.shape[0]
    tspec = pl.BlockSpec((to, R, N), lambda i: (i, 0, 0))
    in_specs = [pl.BlockSpec(md.shape, lambda i: (0, 0)), _gather_spec(c, to), tspec, tspec,
                pl.BlockSpec((1, N), lambda i: (0, 0))]
    args = [md, c, gate, u, skip.reshape(1, N)]
    if ga is not None:
        M = ga.shape[1]
        in_specs.append(pl.BlockSpec((to, M, R), lambda i: (i, 0, 0)))
        args.append(ga)
        out_specs = [tspec, pl.BlockSpec((to, M, N), lambda i: (i, 0, 0))]
        out_shape = [jax.ShapeDtypeStruct((P, R, N), F32), jax.ShapeDtypeStruct((P, M, N), F32)]
    else:
        out_specs = pl.BlockSpec((R, to, N), lambda i: (0, i, 0))
        out_shape = jax.ShapeDtypeStruct((R, P, N), F32)
    return pl.pallas_call(
        functools.partial(_convd_kernel, to=to, chain=ga is not None), grid=(P // to,),
        in_specs=in_specs, out_specs=out_specs, out_shape=out_shape,
        compiler_params=_cparams(("parallel",)), name="hy_fft_d",
    )(*args)


def _dft_tables():
    n = FFT_N1 * FFT_N2
    k = np.arange(FFT_N1)
    f = np.exp(-2j * np.pi * np.outer(k, k) / FFT_N1)
    t = np.exp(-2j * np.pi * np.outer(k, k) / n)
    return f, t, n


def _dft_matrices():
    f, t, n = _dft_tables()
    fr, fi = jnp.asarray(f.real, F32), jnp.asarray(f.imag, F32)
    tr, ti = jnp.asarray(t.real, F32), jnp.asarray(t.imag, F32)
    half = FFT_N1 // 2
    er = fr[None] * tr[:, :, None] - fi[None] * ti[:, :, None]
    ei = fr[None] * ti[:, :, None] + fi[None] * tr[:, :, None]
    ga = jnp.concatenate([jnp.concatenate([er[:, :, :half], -ei[:, :, :half]], axis=2),
                          jnp.concatenate([ei[:, :, :half], er[:, :, :half]], axis=2)], axis=1)
    gaf = jnp.concatenate([er, ei], axis=1)
    mb = jnp.concatenate([jnp.concatenate([fr, -fi], axis=1),
                          jnp.concatenate([fi, fr], axis=1)], axis=0)
    tct = jnp.transpose(tr)[:, :, None]
    tst = -jnp.transpose(ti)[:, :, None]
    gr = tct * fr[None] - tst * (-fi[None])
    gi = tct * (-fi[None]) + tst * fr[None]
    gc = jnp.concatenate([jnp.concatenate([gr, -gi], axis=2),
                          jnp.concatenate([gi, gr], axis=2)], axis=1)
    hr, hi = fr[:half] / n, -fi[:half] / n
    md = jnp.concatenate([jnp.concatenate([hr, -hi], axis=1),
                          jnp.concatenate([hi, hr], axis=1)], axis=0)
    return (ga.astype(BF16), gaf.astype(BF16), mb.astype(BF16), gc.astype(BF16), md.astype(BF16))


def _hyena_positions(L):
    p = FFT_N2 * np.arange(FFT_N1)[None, :] + np.arange(FFT_N2)[:, None]
    pos = np.where(p < L, p, 2 * L - 1 - p).reshape(2 * L, 1).astype(np.float64)
    t = pos / (L - 1)
    w = 2.0 * math.pi * pos / L
    f = np.linspace(1e-4, HY_BANDS - 1, HY_BANDS)[None, :]
    z = np.concatenate([t, np.cos(f * w), -np.sin(f * w),
                        np.zeros((2 * L, FILT_FEAT - HY_EMB))], axis=-1)
    z_pack = (z.reshape(-1, 2, FILT_HALF, FILT_FEAT).transpose(0, 2, 1, 3)
              .reshape(-1, 2 * FILT_FEAT))
    max_decay = math.log(HY_DECAY_TARGET) / HY_FAST_DECAY_PCT
    min_decay = math.log(HY_DECAY_TARGET) / HY_SLOW_DECAY_PCT
    deltas = jnp.linspace(min_decay, max_decay, HY_D, dtype=F32)
    window = jnp.exp(-jnp.asarray(t, F32) * jnp.abs(deltas)[None, :])
    return jnp.asarray(z_pack, F32), window


def _filter_weights(w1, b1, f1, w2, b2, f2, w3):
    def bd(w):
        z = jnp.zeros_like(w)
        return jnp.concatenate([jnp.concatenate([w, z], axis=1),
                                jnp.concatenate([z, w], axis=1)], axis=0)

    def twice(v):
        return jnp.concatenate([v, v])[None]

    w1 = jnp.pad(w1, ((0, FILT_FEAT - w1.shape[0]), (0, 0)))
    w3 = w3.reshape(-1, 2, 2, HY_D).transpose(0, 2, 1, 3).reshape(-1, 4 * HY_D)
    return (*_split_bf16(bd(w1)), twice(b1), twice(f1), *_split_bf16(bd(w2)), twice(b2), twice(f2),
            *_split_bf16(bd(w3)))


def _hyena_layer(hy_in, conv_w, conv_b, fw, skip, z_perm, win_perm, mats):
    B, L, _ = hy_in.shape
    ga, gaf, mb, gc, md = mats
    v, x1, x2 = _hconv_call(hy_in, conv_w, conv_b[None])
    ka = _filt_call(z_perm, fw, win_perm, gaf)
    kf = _filtb_call(mb, ka.reshape(FFT_N2, 2, FFT_N1, 2 * HY_D))

    a = _bm_call(ga, v, "hy_fft_a")
    c = _convb_call(mb, gc, a.reshape(FFT_N2, 2, FFT_N1, HY_D), kf, 0)
    z, a = _convd_call(md, c.reshape(FFT_N1, 2, FFT_N2, HY_D), x1, v, skip[0], ga=ga)
    c = _convb_call(mb, gc, a.reshape(FFT_N2, 2, FFT_N1, HY_D), kf, 1)
    z = _convd_call(md, c.reshape(FFT_N1, 2, FFT_N2, HY_D), x2, z, skip[1])
    return z.reshape(B, L, HY_D)


HALO = 16


def _mix_ffn_kernel(x_ref, xp_ref, xn_ref, yh_ref, yhp_ref, yhn_ref, yg_ref, ygp_ref, ygn_ref,
                    ym_ref, ymp_ref, ymn_ref, gh_ref, gg_ref, gm_ref, wo_ref, gmix_ref, gpre_ref,
                    wup_ref, cw_ref, cb_ref, wd_ref, gpost_ref, o_ref, act_ref):
    i = pl.program_id(1)
    tm = x_ref.shape[1]
    ext = tm + 2 * HALO
    tf = TF_FFN

    def rows(p_ref, m_ref, n_ref):
        return jnp.concatenate([p_ref[0], m_ref[0], n_ref[0]], axis=0).astype(F32)

    a = _rms(rows(yhp_ref, yh_ref, yhn_ref), gh_ref[...]).astype(BF16)
    b = _rms(rows(ygp_ref, yg_ref, ygn_ref), gg_ref[...]).astype(BF16)
    c = _rms(rows(ymp_ref, ym_ref, ymn_ref), gm_ref[...]).astype(BF16)
    o1 = HY_D
    o2 = o1 + GQA_HEADS * GQA_HEAD_DIM
    y = (jnp.dot(a, wo_ref[:o1, :], preferred_element_type=F32)
         + jnp.dot(b, wo_ref[o1:o2, :], preferred_element_type=F32)
         + jnp.dot(c, wo_ref[o2:, :], preferred_element_type=F32))
    xo = rows(xp_ref, x_ref, xn_ref) + _rms(y, gmix_ref[...])
    r = lax.broadcasted_iota(jnp.int32, (ext, 1), 0)
    inside = jnp.logical_and(jnp.logical_or(i > 0, r >= HALO),
                             jnp.logical_or(i < pl.num_programs(1) - 1, r < HALO + tm))
    he = jnp.where(inside, _rms(xo, gpre_ref[...]), 0.0).astype(BF16)

    def conv(c0):
        up = jnp.dot(he, wup_ref[:, c0:c0 + tf], preferred_element_type=F32)
        um = pltpu.roll(up, 1, 0)[HALO:HALO + tm]
        upl = pltpu.roll(up, ext - 1, 0)[HALO:HALO + tm]
        return (um * cw_ref[0:1, c0:c0 + tf] + up[HALO:HALO + tm] * cw_ref[1:2, c0:c0 + tf]
                + upl * cw_ref[2:3, c0:c0 + tf] + cb_ref[:, c0:c0 + tf])

    for j in range(D_FF // tf):
        g = conv(j * tf)
        u = conv(D_FF + j * tf)
        gelu = 0.5 * g * (1.0 + jnp.tanh(math.sqrt(2.0 / math.pi) * (g + 0.044715 * (g * g * g))))
        act_ref[:, j * tf:(j + 1) * tf] = (gelu * u).astype(BF16)
    f = jnp.dot(act_ref[...], wd_ref[...], preferred_element_type=F32)
    o_ref[0] = xo[HALO:HALO + tm] + _rms(f, gpost_ref[...])


def _mix_ffn_call(x, yh, yg, ym, gh, gg, gm, w_out, gmix, gpre, w_up, cw, cb, w_down, gpost):
    B, L, D = x.shape
    tm = TM_FFN
    nb = tm // HALO
    last = L // HALO - 1

    def tiles(a):
        w = a.shape[2]
        return [pl.BlockSpec((1, tm, w), lambda b, i: (b, i, 0)),
                pl.BlockSpec((1, HALO, w), lambda b, i: (b, jnp.maximum(i * nb - 1, 0), 0)),
                pl.BlockSpec((1, HALO, w), lambda b, i: (b, jnp.minimum((i + 1) * nb, last), 0))]

    def resident(a):
        return pl.BlockSpec(a.shape, lambda b, i: (0,) * a.ndim, pipeline_mode=pl.Buffered(1))

    cb = cb[None]
    weights = [gh, gg, gm, w_out, gmix, gpre, w_up, cw, cb, w_down, gpost]
    return pl.pallas_call(
        _mix_ffn_kernel, grid=(B, L // tm),
        in_specs=tiles(x) + tiles(yh) + tiles(yg) + tiles(ym) + [resident(a) for a in weights],
        out_specs=pl.BlockSpec((1, tm, D), lambda b, i: (b, i, 0)),
        out_shape=jax.ShapeDtypeStruct((B, L, D), F32),
        scratch_shapes=[pltpu.VMEM((tm, D_FF), BF16)],
        compiler_params=_cparams(("parallel", "parallel")), name="mix_ffn",
    )(x, x, x, yh, yh, yh, yg, yg, yg, ym, ym, ym, *weights)


def _axial_tables(L, rot_dim):
    pos = np.arange(L)
    n_axis = rot_dim // 4
    inv = ROPE_THETA ** (-np.arange(n_axis) / n_axis)
    ang = np.concatenate([(pos // GRID_W)[:, None] * inv, (pos % GRID_W)[:, None] * inv], axis=-1)
    return jnp.asarray(np.cos(ang), F32), jnp.asarray(np.sin(ang), F32)


def _rope_tables(L):
    def lanes(parts):
        used = sum(p.shape[1] for p in parts)
        return jnp.concatenate(parts + [jnp.zeros((L, LANES - used), F32)], axis=1)

    cg, sg = _axial_tables(L, GQA_HEAD_DIM)
    tg = (lanes([cg, cg, cg, cg]), lanes([sg, sg, sg, sg]))
    cm, sm = _axial_tables(L, MLA_ROPE_DIM)
    nope0 = jnp.zeros((L, MLA_NOPE_DIM), F32)
    nope1 = jnp.ones((L, MLA_NOPE_DIM), F32)
    tmk = (lanes([nope0, cm, cm]), lanes([nope0, sm, sm]))
    sc = (MLA_NOPE_DIM + MLA_ROPE_DIM) ** -0.5 * LOG2E
    tmq = (lanes([nope1, cm, cm]) * sc, tmk[1] * sc)
    return tg, tmq, tmk


def _partner(w, half, sign=-1.0):
    return jnp.concatenate([sign * w[..., half:], w[..., :half]], axis=-1)


def _gain_pair(g, half):
    return jnp.tile(g, 2)[None], jnp.tile(_partner(g, half, 1.0), 2)[None]


def _pad_heads(w, n_heads, width):
    k = w.shape[0]
    return jnp.pad(w.reshape(k, n_heads, width), ((0, 0), (0, 0), (0, LANES - width))).reshape(
        k, n_heads * LANES)


def kernel(x, mix_pre_norm, w_in, hy_conv_w, hy_conv_b, hy_filt_w1, hy_filt_b1, hy_filt_freq1,
           hy_filt_w2, hy_filt_b2, hy_filt_freq2, hy_filt_w3, hy_skip, gqa_q_norm, gqa_k_norm,
           mla_q_a_norm, mla_w_uq, mla_kv_a_norm, mla_w_ukv, hy_out_norm, gqa_out_norm,
           mla_out_norm, w_out, mix_post_norm, ffn_pre_norm, w_up, ffn_conv_w, ffn_conv_b,
           w_down, ffn_post_norm):
    B, L, D = x.shape
    assert B == 2 and 2 * L == FFT_N1 * FFT_N2 and D == D_MODEL
    depth = w_in.shape[0]
    tg, tmq, tmk = _rope_tables(L)
    z_perm, win_perm = _hyena_positions(L)
    mats = _dft_matrices()

    for l in range(depth):
        wl = w_in[l]
        hd, hh, rh = GQA_HEAD_DIM, GQA_HEAD_DIM // 2, MLA_ROPE_DIM // 2
        o1 = HY_COLS
        o2 = o1 + GQA_HEADS * hd
        o3 = o2 + GQA_KV_HEADS * hd
        o4 = o3 + GQA_KV_HEADS * hd
        o5 = o4 + MLA_Q_RANK
        o6 = o5 + MLA_KV_RANK
        wq = wl[:, o1:o2].reshape(D, GQA_HEADS, hd)
        wk = wl[:, o2:o3].reshape(D, GQA_KV_HEADS, hd)
        wkr = wl[:, o6:]
        pe_pad = ((0, 0), (MLA_NOPE_DIM, LANES - MLA_NOPE_DIM - MLA_ROPE_DIM))
        win_p = jnp.concatenate(
            [wl[:, :o1],
             wq.reshape(D, -1), _partner(wq, hh).reshape(D, -1),
             wk.reshape(D, -1), _partner(wk, hh).reshape(D, -1),
             wl[:, o3:o6], jnp.pad(wkr, pe_pad), jnp.pad(_partner(wkr, rh), pe_pad)],
            axis=1).astype(BF16)
        gq = _gain_pair(gqa_q_norm[l] * (hd ** -0.5 * LOG2E), hh)
        gk = _gain_pair(gqa_k_norm[l], hh)
        wuq = mla_w_uq[l].reshape(MLA_Q_RANK, MLA_HEADS, MLA_NOPE_DIM + MLA_ROPE_DIM)
        wuq_pe = jnp.pad(_partner(wuq[:, :, MLA_NOPE_DIM:], rh),
                         ((0, 0), (0, 0), (MLA_NOPE_DIM, 0)))
        wuq_p = jnp.concatenate(
            [_pad_heads(mla_w_uq[l], MLA_HEADS, MLA_NOPE_DIM + MLA_ROPE_DIM),
             _pad_heads(wuq_pe.reshape(MLA_Q_RANK, -1), MLA_HEADS, MLA_NOPE_DIM + MLA_ROPE_DIM)],
            axis=1).astype(BF16)
        wukv = mla_w_ukv[l].reshape(MLA_KV_RANK, MLA_HEADS, MLA_NOPE_DIM + MLA_V_DIM)
        wukvk_p = _pad_heads(wukv[:, :, :MLA_NOPE_DIM].reshape(MLA_KV_RANK, -1), MLA_HEADS,
                             MLA_NOPE_DIM).astype(BF16)
        wukvv = wukv[:, :, MLA_NOPE_DIM:].reshape(MLA_KV_RANK, MLA_HEADS * MLA_V_DIM).astype(BF16)

        hy_in, qg, kg, vg, qm, km, vm = _kin_call(
            x, mix_pre_norm[l][None], win_p, gq, gk, tg, mla_q_a_norm[l][None], wuq_p, tmq,
            mla_kv_a_norm[l][None], wukvk_p, wukvv, tmk)

        fw = _filter_weights(hy_filt_w1[l], hy_filt_b1[l], hy_filt_freq1[l], hy_filt_w2[l],
                             hy_filt_b2[l], hy_filt_freq2[l], hy_filt_w3[l])
        y_hy = _hyena_layer(hy_in, hy_conv_w[l], hy_conv_b[l], fw, hy_skip[l], z_perm, win_perm, mats)

        y_gqa = _attn_call(qg.transpose(0, 2, 1), kg, vg.reshape(B, GQA_KV_HEADS, GQA_HEAD_DIM, L),
                           n_kv=1, n_rep=GQA_HEADS // GQA_KV_HEADS, pack=2, tq=TQ_GQA,
                           name="attn_gqa")
        y_mla = _attn_call(qm.transpose(0, 2, 1), km, vm.reshape(B, MLA_HEADS, MLA_V_DIM, L),
                           n_kv=2, n_rep=1, pack=1, tq=TQ_MLA, name="attn_mla")

        x = _mix_ffn_call(x, y_hy, y_gqa, y_mla, hy_out_norm[l][None], gqa_out_norm[l][None],
                          mla_out_norm[l][None], w_out[l].astype(BF16), mix_post_norm[l][None],
                          ffn_pre_norm[l][None], w_up[l].astype(BF16), ffn_conv_w[l],
                          ffn_conv_b[l], w_down[l].astype(BF16), ffn_post_norm[l][None])
    return x
```

```python
import functools
import math

import numpy as np
import jax
import jax.numpy as jnp
from jax import lax
from jax.experimental import pallas as pl
from jax.experimental.pallas import tpu as pltpu

F32 = jnp.float32
BF16 = jnp.bfloat16

NORM_EPS = 1e-6
ROPE_THETA = 10000.0
GRID_W = 64
LOG2E = math.log2(math.e)

D_MODEL = 1024
HY_D = 256
HY_COLS = 3 * HY_D
HY_EMB = 33
HY_BANDS = 16
HY_DECAY_TARGET = 1e-2
HY_FAST_DECAY_PCT = 0.3
HY_SLOW_DECAY_PCT = 1.5
GQA_HEADS = 8
GQA_KV_HEADS = 2
GQA_HEAD_DIM = 64
MLA_HEADS = 4
MLA_Q_RANK = 256
MLA_KV_RANK = 128
MLA_NOPE_DIM = 64
MLA_ROPE_DIM = 32
MLA_V_DIM = 64
D_FF = 2816

LANES = 128
FFT_N1 = 128
FFT_N2 = 128

TM_IN = 1024
TK_ATTN = 512
TQ_GQA = 128
TQ_MLA = 256
TM_OUT = 1024
TM_FFN = 512
TF_FFN = 256
TL_HCONV = 1024
FILT_HALF = 512
FILT_FEAT = 64
VMEM_LIMIT = 56 * 1024 * 1024


def _cparams(sem):
    return pltpu.CompilerParams(dimension_semantics=sem, vmem_limit_bytes=VMEM_LIMIT)


def _rms(x, g):
    return x * lax.rsqrt(jnp.mean(x * x, axis=-1, keepdims=True) + NORM_EPS) * g


def _kin_kernel(x_ref, gpre_ref, win_ref, gqn_ref, gqs_ref, gkn_ref, gks_ref, cg_ref, sg_ref,
                mqn_ref, wuq_ref, cmq_ref, smq_ref,
                mkvn_ref, wukvk_ref, wukvv_ref, cmk_ref, smk_ref,
                hy_ref, qg_ref, kg_ref, vg_ref, qm_ref, km_ref, vm_ref):
    x = x_ref[0]
    h = _rms(x, gpre_ref[...]).astype(BF16)
    cur = [0]

    def proj(n):
        lo = cur[0]
        cur[0] = lo + n
        return jnp.dot(h, win_ref[:, lo:lo + n], preferred_element_type=F32)

    hy_ref[0] = proj(HY_COLS)

    cg, sg = cg_ref[...], sg_ref[...]
    low = lax.broadcasted_iota(jnp.int32, (x.shape[0], LANES), 1) < GQA_HEAD_DIM

    def head_pairs(n_tiles, gain, gain_sw):
        xa, xb = proj(n_tiles * LANES), proj(n_tiles * LANES)
        ca, sa = cg * gain, sg * gain_sw
        out = []
        for t in range(n_tiles):
            xc, xs = xa[:, t * LANES:(t + 1) * LANES], xb[:, t * LANES:(t + 1) * LANES]
            sq = xc * xc
            tot = jnp.sum(sq, axis=-1, keepdims=True)
            lo = jnp.sum(jnp.where(low, sq, 0.0), axis=-1, keepdims=True)
            ms = jnp.where(low, lo, tot - lo) * (1.0 / GQA_HEAD_DIM)
            out.append((xc * ca + xs * sa) * lax.rsqrt(ms + NORM_EPS))
        return out

    for t, q in enumerate(head_pairs(GQA_HEADS // 2, gqn_ref[...], gqs_ref[...])):
        qg_ref[0, :, t * LANES:(t + 1) * LANES] = q.astype(BF16)
    (kk,) = head_pairs(GQA_KV_HEADS // 2, gkn_ref[...], gks_ref[...])
    ksw = pltpu.roll(kk, GQA_HEAD_DIM, 1)
    kg_ref[0, 0] = jnp.where(low, kk, ksw).astype(BF16)
    kg_ref[0, 1] = jnp.where(low, ksw, kk).astype(BF16)
    vg_ref[0] = proj(LANES).T.astype(BF16)

    cq = _rms(proj(MLA_Q_RANK), mqn_ref[...]).astype(BF16)
    qm = jnp.dot(cq, wuq_ref[...], preferred_element_type=F32)
    cmq, smq = cmq_ref[...], smq_ref[...]
    nq = MLA_HEADS * LANES
    for j in range(MLA_HEADS):
        qm_ref[0, :, j * LANES:(j + 1) * LANES] = (
            qm[:, j * LANES:(j + 1) * LANES] * cmq
            + qm[:, nq + j * LANES:nq + (j + 1) * LANES] * smq).astype(BF16)

    ckv = _rms(proj(MLA_KV_RANK), mkvn_ref[...]).astype(BF16)
    kpe = proj(LANES) * cmk_ref[...] + proj(LANES) * smk_ref[...]
    kn = jnp.dot(ckv, wukvk_ref[...], preferred_element_type=F32)
    for j in range(MLA_HEADS):
        km_ref[0, j] = (kn[:, j * LANES:(j + 1) * LANES] + kpe).astype(BF16)
    vm = jnp.dot(ckv, wukvv_ref[...], preferred_element_type=F32)
    for j in range(vm.shape[1] // LANES):
        vm_ref[0, j * LANES:(j + 1) * LANES, :] = vm[:, j * LANES:(j + 1) * LANES].T.astype(BF16)


def _kin_call(x, gpre, win_p, gq, gk, tg, mqn, wuq_p, tmq, mkvn, wukvk_p, wukvv, tmk):
    B, L, D = x.shape
    tm = TM_IN
    nt = L // tm

    def full(a):
        return pl.BlockSpec(a.shape, lambda b, i: (0,) * a.ndim, pipeline_mode=pl.Buffered(1))

    def rows(w):
        return pl.BlockSpec((tm, w), lambda b, i: (i, 0))

    in_specs = [pl.BlockSpec((1, tm, D), lambda b, i: (b, i, 0)), full(gpre), full(win_p),
                full(gq[0]), full(gq[1]), full(gk[0]), full(gk[1]), rows(LANES), rows(LANES),
                full(mqn), full(wuq_p), rows(LANES), rows(LANES),
                full(mkvn), full(wukvk_p), full(wukvv), rows(LANES), rows(LANES)]
    gv_rows = GQA_KV_HEADS * GQA_HEAD_DIM
    mv_rows = MLA_HEADS * MLA_V_DIM
    out_shape = [
        jax.ShapeDtypeStruct((B, L, HY_COLS), F32),
        jax.ShapeDtypeStruct((B, L, GQA_HEADS * GQA_HEAD_DIM), BF16),
        jax.ShapeDtypeStruct((B, GQA_KV_HEADS, L, LANES), BF16),
        jax.ShapeDtypeStruct((B, gv_rows, L), BF16),
        jax.ShapeDtypeStruct((B, L, MLA_HEADS * LANES), BF16),
        jax.ShapeDtypeStruct((B, MLA_HEADS, L, LANES), BF16),
        jax.ShapeDtypeStruct((B, mv_rows, L), BF16),
    ]
    out_specs = [
        pl.BlockSpec((1, tm, HY_COLS), lambda b, i: (b, i, 0)),
        pl.BlockSpec((1, tm, GQA_HEADS * GQA_HEAD_DIM), lambda b, i: (b, i, 0)),
        pl.BlockSpec((1, GQA_KV_HEADS, tm, LANES), lambda b, i: (b, 0, i, 0)),
        pl.BlockSpec((1, gv_rows, tm), lambda b, i: (b, 0, i)),
        pl.BlockSpec((1, tm, MLA_HEADS * LANES), lambda b, i: (b, i, 0)),
        pl.BlockSpec((1, MLA_HEADS, tm, LANES), lambda b, i: (b, 0, i, 0)),
        pl.BlockSpec((1, mv_rows, tm), lambda b, i: (b, 0, i)),
    ]
    return pl.pallas_call(
        _kin_kernel, grid=(B, nt), in_specs=in_specs, out_specs=out_specs, out_shape=out_shape,
        compiler_params=_cparams(("parallel", "parallel")), name="in_proj",
    )(x, gpre, win_p, *gq, *gk, *tg, mqn, wuq_p, *tmq, mkvn, wukvk_p, wukvv, *tmk)


def _attn_kernel(q_ref, k_ref, vt_ref, o_ref, sa_ref, sb_ref, ma_ref, mb_ref, *,
                 n_kv, n_rep, pack, tq, n_chunks, tk):
    i = pl.program_id(0)
    cols = n_rep * tq
    width = n_kv * cols
    grp = tk // 8

    @pl.when(i == 0)
    def _():
        sb_ref[...] = jnp.zeros(sb_ref.shape, F32)
        mb_ref[...] = jnp.zeros(mb_ref.shape, F32)

    def step(sw_ref, mw_ref, sr_ref, mr_ref):
        def q_head(h):
            tile, half = divmod(h, pack)
            qt = q_ref[0, tile * LANES:(tile + 1) * LANES, :]
            if pack == 2:
                z = jnp.zeros((LANES // 2, qt.shape[1]), qt.dtype)
                qt = jnp.concatenate([qt[:LANES // 2], z] if half == 0 else [z, qt[LANES // 2:]],
                                     axis=0)
            return qt

        qs = [jnp.concatenate([q_head(a * n_rep + j) for j in range(n_rep)], axis=1)
              for a in range(n_kv)]
        mx = jnp.max(mr_ref[...], axis=0, keepdims=True)
        m = jnp.full((8, width), -jnp.inf, F32)
        l = jnp.zeros((8, width), F32)
        accs = [jnp.zeros((vt_ref.shape[2], cols), F32) for _ in range(n_kv)]
        for c in range(n_chunks):
            st = jnp.concatenate(
                [jnp.dot(k_ref[0, a, c * tk:(c + 1) * tk, :], qs[a], preferred_element_type=F32)
                 for a in range(n_kv)], axis=1)
            sw_ref[c] = st
            m = jnp.maximum(m, jnp.max(st.reshape(grp, 8, width), axis=0))
            p = jnp.exp2(sr_ref[c] - mx)
            l = l + jnp.sum(p.reshape(grp, 8, width), axis=0)
            pb = p.astype(BF16)
            for a in range(n_kv):
                accs[a] = accs[a] + jnp.dot(vt_ref[0, a, :, c * tk:(c + 1) * tk],
                                            pb[:, a * cols:(a + 1) * cols],
                                            preferred_element_type=F32)
        mw_ref[...] = m
        ls = jnp.sum(l, axis=0, keepdims=True)
        heads = []
        for a in range(n_kv):
            oa = accs[a] / ls[:, a * cols:(a + 1) * cols]
            heads += [oa[:, j * tq:(j + 1) * tq] for j in range(n_rep)]
        o_ref[0] = jnp.concatenate(heads, axis=0).T.astype(BF16)

    @pl.when(i % 2 == 0)
    def _():
        step(sa_ref, ma_ref, sb_ref, mb_ref)

    @pl.when(i % 2 == 1)
    def _():
        step(sb_ref, mb_ref, sa_ref, ma_ref)


def _attn_call(qt, k, vt, *, n_kv, n_rep, pack, tq, name):
    B, hq, L = qt.shape
    H = hq * pack // LANES
    hkv, dv = vt.shape[1], vt.shape[2]
    G = hkv // n_kv
    hs = n_kv * n_rep
    tk = TK_ATTN
    n_chunks = L // tk
    nq = L // tq
    width = hs * tq
    kern = functools.partial(_attn_kernel, n_kv=n_kv, n_rep=n_rep, pack=pack, tq=tq,
                             n_chunks=n_chunks, tk=tk)
    total = B * G * nq

    def blk(s):
        return s // (G * nq), (s // nq) % G, s % nq

    def q_map(s):
        b, g, i = blk(jnp.minimum(s, total - 1))
        return b, g, i

    def k_map(s):
        b, g, _ = blk(jnp.minimum(s, total - 1))
        return b, g, 0, 0

    def v_map(s):
        b, g, _ = blk(jnp.maximum(s - 1, 0))
        return b, g, 0, 0

    def o_map(s):
        b, g, i = blk(jnp.maximum(s - 1, 0))
        return b, i, g

    return pl.pallas_call(
        kern, grid=(total + 1,),
        in_specs=[pl.BlockSpec((1, hs * LANES // pack, tq), q_map),
                  pl.BlockSpec((1, n_kv, L, LANES), k_map, pipeline_mode=pl.Buffered(1)),
                  pl.BlockSpec((1, n_kv, dv, L), v_map, pipeline_mode=pl.Buffered(1))],
        out_specs=pl.BlockSpec((1, tq, hs * dv), o_map),
        out_shape=jax.ShapeDtypeStruct((B, L, H * dv), BF16),
        scratch_shapes=[pltpu.VMEM((n_chunks, tk, width), F32), pltpu.VMEM((n_chunks, tk, width), F32),
                        pltpu.VMEM((8, width), F32), pltpu.VMEM((8, width), F32)],
        compiler_params=_cparams(("arbitrary",)), name=name,
    )(qt, k, vt)


def _hconv_kernel(x_ref, xp_ref, xn_ref, w_ref, b_ref, v_ref, x1_ref, x2_ref):
    i = pl.program_id(1)
    x = x_ref[0]
    tl = x.shape[0]
    prev = jnp.where(i > 0, xp_ref[0][7:8, :], 0.0)
    nxt = jnp.where(i < pl.num_programs(1) - 1, xn_ref[0][0:1, :], 0.0)
    r = lax.broadcasted_iota(jnp.int32, x.shape, 0)
    xm = jnp.where(r == 0, prev, pltpu.roll(x, 1, 0))
    xp = jnp.where(r == tl - 1, nxt, pltpu.roll(x, tl - 1, 0))
    uc = xm * w_ref[0:1, :] + x * w_ref[1:2, :] + xp * w_ref[2:3, :] + b_ref[...]
    for r in range(tl // FFT_N2):
        blk = uc[r * FFT_N2:(r + 1) * FFT_N2]
        v_ref[:, r, :] = blk[:, :HY_D]
        x1_ref[:, r, :] = blk[:, HY_D:2 * HY_D]
        x2_ref[:, r, :] = blk[:, 2 * HY_D:]


def _hconv_call(hy_in, w, b):
    B, L, C = hy_in.shape
    tl = TL_HCONV
    nb = tl // 8
    last = L // 8 - 1
    nt = L // tl
    rows = tl // FFT_N2
    out = jax.ShapeDtypeStruct((FFT_N2, B * L // FFT_N2, HY_D), F32)
    ospec = pl.BlockSpec((FFT_N2, rows, HY_D), lambda b_, i: (0, b_ * nt + i, 0))
    return pl.pallas_call(
        _hconv_kernel, grid=(B, L // tl),
        in_specs=[pl.BlockSpec((1, tl, C), lambda b_, i: (b_, i, 0)),
                  pl.BlockSpec((1, 8, C), lambda b_, i: (b_, jnp.maximum(i * nb - 1, 0), 0)),
                  pl.BlockSpec((1, 8, C), lambda b_, i: (b_, jnp.minimum((i + 1) * nb, last), 0)),
                  pl.BlockSpec((3, C), lambda b_, i: (0, 0)),
                  pl.BlockSpec((1, C), lambda b_, i: (0, 0))],
        out_specs=[ospec, ospec, ospec], out_shape=[out, out, out],
        compiler_params=_cparams(("parallel", "parallel")), name="hy_conv3",
    )(hy_in, hy_in, hy_in, w, b)


def _split_bf16(a):
    hi = a.astype(BF16)
    return hi, (a - hi.astype(F32)).astype(BF16)


def _dot3(a, w_hi, w_lo):
    a_hi, a_lo = _split_bf16(a)
    dot = functools.partial(jnp.dot, preferred_element_type=F32)
    return dot(a_hi, w_hi) + dot(a_lo, w_hi) + dot(a_hi, w_lo)


def _filt_kernel(z_ref, w1h_ref, w1l_ref, b1_ref, f1_ref, w2h_ref, w2l_ref, b2_ref, f2_ref,
                 w3h_ref, w3l_ref, win_ref, gaf_ref, o_ref):
    h = jnp.sin(f1_ref[...] * (_dot3(z_ref[...], w1h_ref[...], w1l_ref[...]) + b1_ref[...]))
    h = jnp.sin(f2_ref[...] * (_dot3(h, w2h_ref[...], w2l_ref[...]) + b2_ref[...]))
    k = _dot3(h, w3h_ref[...], w3l_ref[...])
    half = k.shape[0]
    wcols = 4 * HY_D
    per_half = half // FFT_N1
    n1 = lax.broadcasted_iota(jnp.int32, (half, 2 * HY_D), 0) % FFT_N1
    for s in range(2):
        ks = k[:, s * wcols:(s + 1) * wcols]
        kk = jnp.where(n1 >= FFT_N1 // 2, ks[:, 2 * HY_D:], ks[:, :2 * HY_D])
        win = win_ref[s * half:(s + 1) * half, :]
        filt = (kk * jnp.concatenate([win, win], axis=-1)).astype(BF16)
        for j in range(per_half):
            o_ref[s * per_half + j] = jnp.dot(gaf_ref[s * per_half + j],
                                              filt[j * FFT_N1:(j + 1) * FFT_N1],
                                              preferred_element_type=F32)


def _filt_call(z_pack, fw, win_perm, gaf):
    tp = 2 * FILT_HALF
    n = win_perm.shape[0]
    to = tp // FFT_N1
    M = gaf.shape[1]

    def full(a):
        return pl.BlockSpec(a.shape, lambda i: (0,) * a.ndim)

    return pl.pallas_call(
        _filt_kernel, grid=(n // tp,),
        in_specs=[pl.BlockSpec((FILT_HALF, z_pack.shape[1]), lambda i: (i, 0))]
        + [full(a) for a in fw] + [pl.BlockSpec((tp, HY_D), lambda i: (i, 0)),
                                   pl.BlockSpec((to, M, FFT_N1), lambda i: (i, 0, 0))],
        out_specs=pl.BlockSpec((to, M, 2 * HY_D), lambda i: (i, 0, 0)),
        out_shape=jax.ShapeDtypeStruct((n // FFT_N1, M, 2 * HY_D), F32),
        compiler_params=_cparams(("parallel",)), name="hy_filter",
    )(z_pack, *fw, win_perm, gaf)


def _bm_kernel(g_ref, x_ref, o_ref, *, to):
    for t in range(to):
        o_ref[t] = jnp.dot(g_ref[t], x_ref[t].astype(BF16), preferred_element_type=F32)


def _bm_call(g, x, name, to=16):
    O, K, N = x.shape
    M = g.shape[1]
    return pl.pallas_call(
        functools.partial(_bm_kernel, to=to), grid=(O // to,),
        in_specs=[pl.BlockSpec((to, M, K), lambda i: (i, 0, 0)),
                  pl.BlockSpec((to, K, N), lambda i: (i, 0, 0))],
        out_specs=pl.BlockSpec((to, M, N), lambda i: (i, 0, 0)),
        out_shape=jax.ShapeDtypeStruct((O, M, N), F32),
        compiler_params=_cparams(("parallel",)), name=name,
    )(g, x)


def _gather_ri(x_ref, j):
    return jnp.concatenate([x_ref[:, 0, j, :], x_ref[:, 1, j, :]], axis=0).astype(BF16)


def _gather_spec(a, to):
    return pl.BlockSpec((a.shape[0], 2, to, a.shape[3]), lambda i: (0, 0, i, 0))


def _filtb_kernel(mb_ref, x_ref, o_ref, *, to):
    for j in range(to):
        o_ref[j] = jnp.dot(mb_ref[...], _gather_ri(x_ref, j),
                           preferred_element_type=F32).astype(o_ref.dtype)


def _filtb_call(mb, ka, to=16):
    O, _, P, N = ka.shape
    return pl.pallas_call(
        functools.partial(_filtb_kernel, to=to), grid=(P // to,),
        in_specs=[pl.BlockSpec(mb.shape, lambda i: (0, 0)), _gather_spec(ka, to)],
        out_specs=pl.BlockSpec((to, 2 * O, N), lambda i: (i, 0, 0)),
        out_shape=jax.ShapeDtypeStruct((P, 2 * O, N), BF16),
        compiler_params=_cparams(("parallel",)), name="hy_fft_filt_b",
    )(mb, ka)


def _convb_kernel(mb_ref, gc_ref, x_ref, kf_ref, o_ref, *, to):
    h = FFT_N2
    for j in range(to):
        xs = jnp.dot(mb_ref[...], _gather_ri(x_ref, j), preferred_element_type=F32)
        xr, xi = xs[:h], xs[h:]
        kr, ki = kf_ref[j, :h, :].astype(F32), kf_ref[j, h:, :].astype(F32)
        ys = jnp.concatenate([xr * kr - xi * ki, xr * ki + xi * kr], axis=0).astype(BF16)
        o_ref[j] = jnp.dot(gc_ref[j], ys, preferred_element_type=F32)


def _convb_call(mb, gc, a, kf, order, to=16):
    O, _, P, N = a.shape
    return pl.pallas_call(
        functools.partial(_convb_kernel, to=to), grid=(P // to,),
        in_specs=[pl.BlockSpec(mb.shape, lambda i: (0, 0)),
                  pl.BlockSpec((to, 2 * O, 2 * O), lambda i: (i, 0, 0)),
                  _gather_spec(a, to),
                  pl.BlockSpec((to, 2 * O, N), lambda i: (i, 0, order))],
        out_specs=pl.BlockSpec((to, 2 * O, N), lambda i: (i, 0, 0)),
        out_shape=jax.ShapeDtypeStruct((P, 2 * O, N), F32),
        compiler_params=_cparams(("parallel",)), name="hy_spec_mul",
    )(mb, gc, a, kf)


def _convd_kernel(md_ref, c_ref, g_ref, u_ref, s_ref, *rest, to, chain):
    if chain:
        ga_ref, z_ref, a_ref = rest
    else:
        (o_ref,) = rest
    for j in range(to):
        y = jnp.dot(md_ref[...], _gather_ri(c_ref, j), preferred_element_type=F32)
        z = g_ref[j] * (y + u_ref[j] * s_ref[...])
        if chain:
            z_ref[j] = z
            a_ref[j] = jnp.dot(ga_ref[j], z.astype(BF16), preferred_element_type=F32)
        else:
            o_ref[:, j, :] = z


def _convd_call(md, c, gate, u, skip, ga=None, to=16):
    O, _, P, N = c.shape
    R = md.shape[0]
    tspec = pl.BlockSpec((to, R, N), lambda i: (i, 0, 0))
    in_specs = [pl.BlockSpec(md.shape, lambda i: (0, 0)), _gather_spec(c, to), tspec, tspec,
                pl.BlockSpec((1, N), lambda i: (0, 0))]
    args = [md, c, gate, u, skip.reshape(1, N)]
    if ga is not None:
        M = ga.shape[1]
        in_specs.append(pl.BlockSpec((to, M, R), lambda i: (i, 0, 0)))
        args.append(ga)
        out_specs = [tspec, pl.BlockSpec((to, M, N), lambda i: (i, 0, 0))]
        out_shape = [jax.ShapeDtypeStruct((P, R, N), F32), jax.ShapeDtypeStruct((P, M, N), F32)]
    else:
        out_specs = pl.BlockSpec((R, to, N), lambda i: (0, i, 0))
        out_shape = jax.ShapeDtypeStruct((R, P, N), F32)
    return pl.pallas_call(
        functools.partial(_convd_kernel, to=to, chain=ga is not None), grid=(P // to,),
        in_specs=in_specs, out_specs=out_specs, out_shape=out_shape,
        compiler_params=_cparams(("parallel",)), name="hy_fft_d",
    )(*args)


def _dft_tables():
    n = FFT_N1 * FFT_N2
    k = np.arange(FFT_N1)
    f = np.exp(-2j * np.pi * np.outer(k, k) / FFT_N1)
    t = np.exp(-2j * np.pi * np.outer(k, k) / n)
    return f, t, n


def _dft_matrices():
    f, t, n = _dft_tables()
    fr, fi = jnp.asarray(f.real, F32), jnp.asarray(f.imag, F32)
    tr, ti = jnp.asarray(t.real, F32), jnp.asarray(t.imag, F32)
    half = FFT_N1 // 2
    er = fr[None] * tr[:, :, None] - fi[None] * ti[:, :, None]
    ei = fr[None] * ti[:, :, None] + fi[None] * tr[:, :, None]
    ga = jnp.concatenate([jnp.concatenate([er[:, :, :half], -ei[:, :, :half]], axis=2),
                          jnp.concatenate([ei[:, :, :half], er[:, :, :half]], axis=2)], axis=1)
    gaf = jnp.concatenate([er, ei], axis=1)
    mb = jnp.concatenate([jnp.concatenate([fr, -fi], axis=1),
                          jnp.concatenate([fi, fr], axis=1)], axis=0)
    tct = jnp.transpose(tr)[:, :, None]
    tst = -jnp.transpose(ti)[:, :, None]
    gr = tct * fr[None] - tst * (-fi[None])
    gi = tct * (-fi[None]) + tst * fr[None]
    gc = jnp.concatenate([jnp.concatenate([gr, -gi], axis=2),
                          jnp.concatenate([gi, gr], axis=2)], axis=1)
    hr, hi = fr[:half] / n, -fi[:half] / n
    md = jnp.concatenate([jnp.concatenate([hr, -hi], axis=1),
                          jnp.concatenate([hi, hr], axis=1)], axis=0)
    return (ga.astype(BF16), gaf.astype(BF16), mb.astype(BF16), gc.astype(BF16), md.astype(BF16))


def _hyena_positions(L):
    p = FFT_N2 * np.arange(FFT_N1)[None, :] + np.arange(FFT_N2)[:, None]
    pos = np.where(p < L, p, 2 * L - 1 - p).reshape(2 * L, 1).astype(np.float64)
    t = pos / (L - 1)
    w = 2.0 * math.pi * pos / L
    f = np.linspace(1e-4, HY_BANDS - 1, HY_BANDS)[None, :]
    z = np.concatenate([t, np.cos(f * w), -np.sin(f * w),
                        np.zeros((2 * L, FILT_FEAT - HY_EMB))], axis=-1)
    z_pack = (z.reshape(-1, 2, FILT_HALF, FILT_FEAT).transpose(0, 2, 1, 3)
              .reshape(-1, 2 * FILT_FEAT))
    max_decay = math.log(HY_DECAY_TARGET) / HY_FAST_DECAY_PCT
    min_decay = math.log(HY_DECAY_TARGET) / HY_SLOW_DECAY_PCT
    deltas = jnp.linspace(min_decay, max_decay, HY_D, dtype=F32)
    window = jnp.exp(-jnp.asarray(t, F32) * jnp.abs(deltas)[None, :])
    return jnp.asarray(z_pack, F32), window


def _filter_weights(w1, b1, f1, w2, b2, f2, w3):
    def bd(w):
        z = jnp.zeros_like(w)
        return jnp.concatenate([jnp.concatenate([w, z], axis=1),
                                jnp.concatenate([z, w], axis=1)], axis=0)

    def twice(v):
        return jnp.concatenate([v, v])[None]

    w1 = jnp.pad(w1, ((0, FILT_FEAT - w1.shape[0]), (0, 0)))
    w3 = w3.reshape(-1, 2, 2, HY_D).transpose(0, 2, 1, 3).reshape(-1, 4 * HY_D)
    return (*_split_bf16(bd(w1)), twice(b1), twice(f1), *_split_bf16(bd(w2)), twice(b2), twice(f2),
            *_split_bf16(bd(w3)))


def _hyena_layer(hy_in, conv_w, conv_b, fw, skip, z_perm, win_perm, mats):
    B, L, _ = hy_in.shape
    ga, gaf, mb, gc, md = mats
    v, x1, x2 = _hconv_call(hy_in, conv_w, conv_b[None])
    ka = _filt_call(z_perm, fw, win_perm, gaf)
    kf = _filtb_call(mb, ka.reshape(FFT_N2, 2, FFT_N1, 2 * HY_D))

    a = _bm_call(ga, v, "hy_fft_a")
    c = _convb_call(mb, gc, a.reshape(FFT_N2, 2, FFT_N1, HY_D), kf, 0)
    z, a = _convd_call(md, c.reshape(FFT_N1, 2, FFT_N2, HY_D), x1, v, skip[0], ga=ga)
    c = _convb_call(mb, gc, a.reshape(FFT_N2, 2, FFT_N1, HY_D), kf, 1)
    z = _convd_call(md, c.reshape(FFT_N1, 2, FFT_N2, HY_D), x2, z, skip[1])
    return z.reshape(B, L, HY_D)


def _kout_kernel(x_ref, yh_ref, yg_ref, ym_ref, gh_ref, gg_ref, gm_ref, w_ref, gpost_ref, gffn_ref,
                 xo_ref, h_ref):
    a = _rms(yh_ref[0], gh_ref[...]).astype(BF16)
    b = _rms(yg_ref[0].astype(F32), gg_ref[...]).astype(BF16)
    c = _rms(ym_ref[0].astype(F32), gm_ref[...]).astype(BF16)
    o1 = HY_D
    o2 = o1 + GQA_HEADS * GQA_HEAD_DIM
    y = (jnp.dot(a, w_ref[:o1, :], preferred_element_type=F32)
         + jnp.dot(b, w_ref[o1:o2, :], preferred_element_type=F32)
         + jnp.dot(c, w_ref[o2:, :], preferred_element_type=F32))
    xo = x_ref[0] + _rms(y, gpost_ref[...])
    xo_ref[0] = xo
    h_ref[0] = _rms(xo, gffn_ref[...]).astype(BF16)


def _kout_call(x, yh, yg, ym, gh, gg, gm, w_p, gpost, gffn):
    B, L, D = x.shape
    tm = TM_OUT

    def rows(a):
        return pl.BlockSpec((1, tm, a.shape[2]), lambda b, i: (b, i, 0))

    def full(a):
        return pl.BlockSpec(a.shape, lambda b, i: (0,) * a.ndim)

    return pl.pallas_call(
        _kout_kernel, grid=(B, L // tm),
        in_specs=[rows(x), rows(yh), rows(yg), rows(ym), full(gh), full(gg), full(gm), full(w_p),
                  full(gpost), full(gffn)],
        out_specs=[rows(x), rows(x)],
        out_shape=[jax.ShapeDtypeStruct((B, L, D), F32), jax.ShapeDtypeStruct((B, L, D), BF16)],
        compiler_params=_cparams(("parallel", "parallel")), name="out_proj",
    )(x, yh, yg, ym, gh, gg, gm, w_p, gpost, gffn)


HALO = 16


def _ffn_kernel(h_ref, hp_ref, hn_ref, x_ref, wup_ref, cw_ref, cb_ref, wd_ref, gpost_ref,
                o_ref, act_ref):
    i = pl.program_id(1)
    tm = h_ref.shape[1]
    prev = jnp.where(i > 0, hp_ref[0], jnp.zeros_like(hp_ref[0]))
    nxt = jnp.where(i < pl.num_programs(1) - 1, hn_ref[0], jnp.zeros_like(hn_ref[0]))
    he = jnp.concatenate([prev, h_ref[0], nxt], axis=0)
    ext = tm + 2 * HALO
    tf = TF_FFN

    def conv(c0):
        up = jnp.dot(he, wup_ref[:, c0:c0 + tf], preferred_element_type=F32)
        um = pltpu.roll(up, 1, 0)[HALO:HALO + tm]
        upl = pltpu.roll(up, ext - 1, 0)[HALO:HALO + tm]
        return (um * cw_ref[0:1, c0:c0 + tf] + up[HALO:HALO + tm] * cw_ref[1:2, c0:c0 + tf]
                + upl * cw_ref[2:3, c0:c0 + tf] + cb_ref[:, c0:c0 + tf])

    for j in range(D_FF // tf):
        g = conv(j * tf)
        u = conv(D_FF + j * tf)
        gelu = 0.5 * g * (1.0 + jnp.tanh(math.sqrt(2.0 / math.pi) * (g + 0.044715 * (g * g * g))))
        act_ref[:, j * tf:(j + 1) * tf] = (gelu * u).astype(BF16)
    f = jnp.dot(act_ref[...], wd_ref[...], preferred_element_type=F32)
    o_ref[0] = x_ref[0] + _rms(f, gpost_ref[...])


def _ffn_call(h, x, w_up, cw, cb, w_down, gpost):
    B, L, D = x.shape
    tm = TM_FFN
    nb = tm // HALO
    last = L // HALO - 1

    def resident(a):
        return pl.BlockSpec(a.shape, lambda b, i: (0,) * a.ndim, pipeline_mode=pl.Buffered(1))

    cb = cb[None]
    return pl.pallas_call(
        _ffn_kernel, grid=(B, L // tm),
        in_specs=[pl.BlockSpec((1, tm, D), lambda b, i: (b, i, 0)),
                  pl.BlockSpec((1, HALO, D), lambda b, i: (b, jnp.maximum(i * nb - 1, 0), 0)),
                  pl.BlockSpec((1, HALO, D), lambda b, i: (b, jnp.minimum((i + 1) * nb, last), 0)),
                  pl.BlockSpec((1, tm, D), lambda b, i: (b, i, 0)),
                  resident(w_up), resident(cw), resident(cb), resident(w_down), resident(gpost)],
        out_specs=pl.BlockSpec((1, tm, D), lambda b, i: (b, i, 0)),
        out_shape=jax.ShapeDtypeStruct((B, L, D), F32),
        scratch_shapes=[pltpu.VMEM((tm, D_FF), BF16)],
        compiler_params=_cparams(("parallel", "parallel")), name="conv_ffn",
    )(h, h, h, x, w_up, cw, cb, w_down, gpost)


def _axial_tables(L, rot_dim):
    pos = np.arange(L)
    n_axis = rot_dim // 4
    inv = ROPE_THETA ** (-np.arange(n_axis) / n_axis)
    ang = np.concatenate([(pos // GRID_W)[:, None] * inv, (pos % GRID_W)[:, None] * inv], axis=-1)
    return jnp.asarray(np.cos(ang), F32), jnp.asarray(np.sin(ang), F32)


def _rope_tables(L):
    def lanes(parts):
        used = sum(p.shape[1] for p in parts)
        return jnp.concatenate(parts + [jnp.zeros((L, LANES - used), F32)], axis=1)

    cg, sg = _axial_tables(L, GQA_HEAD_DIM)
    tg = (lanes([cg, cg, cg, cg]), lanes([sg, sg, sg, sg]))
    cm, sm = _axial_tables(L, MLA_ROPE_DIM)
    nope0 = jnp.zeros((L, MLA_NOPE_DIM), F32)
    nope1 = jnp.ones((L, MLA_NOPE_DIM), F32)
    tmk = (lanes([nope0, cm, cm]), lanes([nope0, sm, sm]))
    sc = (MLA_NOPE_DIM + MLA_ROPE_DIM) ** -0.5 * LOG2E
    tmq = (lanes([nope1, cm, cm]) * sc, tmk[1] * sc)
    return tg, tmq, tmk


def _partner(w, half, sign=-1.0):
    return jnp.concatenate([sign * w[..., half:], w[..., :half]], axis=-1)


def _gain_pair(g, half):
    return jnp.tile(g, 2)[None], jnp.tile(_partner(g, half, 1.0), 2)[None]


def _pad_heads(w, n_heads, width):
    k = w.shape[0]
    return jnp.pad(w.reshape(k, n_heads, width), ((0, 0), (0, 0), (0, LANES - width))).reshape(
        k, n_heads * LANES)


def kernel(x, mix_pre_norm, w_in, hy_conv_w, hy_conv_b, hy_filt_w1, hy_filt_b1, hy_filt_freq1,
           hy_filt_w2, hy_filt_b2, hy_filt_freq2, hy_filt_w3, hy_skip, gqa_q_norm, gqa_k_norm,
           mla_q_a_norm, mla_w_uq, mla_kv_a_norm, mla_w_ukv, hy_out_norm, gqa_out_norm,
           mla_out_norm, w_out, mix_post_norm, ffn_pre_norm, w_up, ffn_conv_w, ffn_conv_b,
           w_down, ffn_post_norm):
    B, L, D = x.shape
    assert B == 2 and 2 * L == FFT_N1 * FFT_N2 and D == D_MODEL
    depth = w_in.shape[0]
    tg, tmq, tmk = _rope_tables(L)
    z_perm, win_perm = _hyena_positions(L)
    mats = _dft_matrices()

    for l in range(depth):
        wl = w_in[l]
        hd, hh, rh = GQA_HEAD_DIM, GQA_HEAD_DIM // 2, MLA_ROPE_DIM // 2
        o1 = HY_COLS
        o2 = o1 + GQA_HEADS * hd
        o3 = o2 + GQA_KV_HEADS * hd
        o4 = o3 + GQA_KV_HEADS * hd
        o5 = o4 + MLA_Q_RANK
        o6 = o5 + MLA_KV_RANK
        wq = wl[:, o1:o2].reshape(D, GQA_HEADS, hd)
        wk = wl[:, o2:o3].reshape(D, GQA_KV_HEADS, hd)
        wkr = wl[:, o6:]
        pe_pad = ((0, 0), (MLA_NOPE_DIM, LANES - MLA_NOPE_DIM - MLA_ROPE_DIM))
        win_p = jnp.concatenate(
            [wl[:, :o1],
             wq.reshape(D, -1), _partner(wq, hh).reshape(D, -1),
             wk.reshape(D, -1), _partner(wk, hh).reshape(D, -1),
             wl[:, o3:o6], jnp.pad(wkr, pe_pad), jnp.pad(_partner(wkr, rh), pe_pad)],
            axis=1).astype(BF16)
        gq = _gain_pair(gqa_q_norm[l] * (hd ** -0.5 * LOG2E), hh)
        gk = _gain_pair(gqa_k_norm[l], hh)
        wuq = mla_w_uq[l].reshape(MLA_Q_RANK, MLA_HEADS, MLA_NOPE_DIM + MLA_ROPE_DIM)
        wuq_pe = jnp.pad(_partner(wuq[:, :, MLA_NOPE_DIM:], rh),
                         ((0, 0), (0, 0), (MLA_NOPE_DIM, 0)))
        wuq_p = jnp.concatenate(
            [_pad_heads(mla_w_uq[l], MLA_HEADS, MLA_NOPE_DIM + MLA_ROPE_DIM),
             _pad_heads(wuq_pe.reshape(MLA_Q_RANK, -1), MLA_HEADS, MLA_NOPE_DIM + MLA_ROPE_DIM)],
            axis=1).astype(BF16)
        wukv = mla_w_ukv[l].reshape(MLA_KV_RANK, MLA_HEADS, MLA_NOPE_DIM + MLA_V_DIM)
        wukvk_p = _pad_heads(wukv[:, :, :MLA_NOPE_DIM].reshape(MLA_KV_RANK, -1), MLA_HEADS,
                             MLA_NOPE_DIM).astype(BF16)
        wukvv = wukv[:, :, MLA_NOPE_DIM:].reshape(MLA_KV_RANK, MLA_HEADS * MLA_V_DIM).astype(BF16)

        hy_in, qg, kg, vg, qm, km, vm = _kin_call(
            x, mix_pre_norm[l][None], win_p, gq, gk, tg, mla_q_a_norm[l][None], wuq_p, tmq,
            mla_kv_a_norm[l][None], wukvk_p, wukvv, tmk)

        fw = _filter_weights(hy_filt_w1[l], hy_filt_b1[l], hy_filt_freq1[l], hy_filt_w2[l],
                             hy_filt_b2[l], hy_filt_freq2[l], hy_filt_w3[l])
        y_hy = _hyena_layer(hy_in, hy_conv_w[l], hy_conv_b[l], fw, hy_skip[l], z_perm, win_perm, mats)

        y_gqa = _attn_call(qg.transpose(0, 2, 1), kg, vg.reshape(B, GQA_KV_HEADS, GQA_HEAD_DIM, L),
                           n_kv=1, n_rep=GQA_HEADS // GQA_KV_HEADS, pack=2, tq=TQ_GQA,
                           name="attn_gqa")
        y_mla = _attn_call(qm.transpose(0, 2, 1), km, vm.reshape(B, MLA_HEADS, MLA_V_DIM, L),
                           n_kv=2, n_rep=1, pack=1, tq=TQ_MLA, name="attn_mla")

        x, h2 = _kout_call(x, y_hy, y_gqa, y_mla, hy_out_norm[l][None], gqa_out_norm[l][None],
                           mla_out_norm[l][None], w_out[l].astype(BF16),
                           mix_post_norm[l][None], ffn_pre_norm[l][None])
        x = _ffn_call(h2, x, w_up[l].astype(BF16), ffn_conv_w[l], ffn_conv_b[l],
                      w_down[l].astype(BF16), ffn_post_norm[l][None])
    return x
```

```python
import functools
import math

import numpy as np
import jax
import jax.numpy as jnp
from jax import lax
from jax.experimental import pallas as pl
from jax.experimental.pallas import tpu as pltpu

F32 = jnp.float32
BF16 = jnp.bfloat16

NORM_EPS = 1e-6
ROPE_THETA = 10000.0
GRID_W = 64
LOG2E = math.log2(math.e)

D_MODEL = 1024
HY_D = 256
HY_COLS = 3 * HY_D
HY_EMB = 33
HY_BANDS = 16
HY_DECAY_TARGET = 1e-2
HY_FAST_DECAY_PCT = 0.3
HY_SLOW_DECAY_PCT = 1.5
GQA_HEADS = 8
GQA_KV_HEADS = 2
GQA_HEAD_DIM = 64
MLA_HEADS = 4
MLA_Q_RANK = 256
MLA_KV_RANK = 128
MLA_NOPE_DIM = 64
MLA_ROPE_DIM = 32
MLA_V_DIM = 64
D_FF = 2816

LANES = 128
FFT_N1 = 128
FFT_N2 = 128

TM_IN = 1024
TK_ATTN = 512
TQ_GQA = 128
TQ_MLA = 256
TM_OUT = 1024
TM_FFN = 512
TF_FFN = 256
FILT_HALF = 512
FILT_FEAT = 64
VMEM_LIMIT = 56 * 1024 * 1024


def _cparams(sem):
    return pltpu.CompilerParams(dimension_semantics=sem, vmem_limit_bytes=VMEM_LIMIT)


def _rms(x, g):
    return x * lax.rsqrt(jnp.mean(x * x, axis=-1, keepdims=True) + NORM_EPS) * g


def _kin_kernel(x_ref, gpre_ref, win_ref, gqn_ref, gqs_ref, gkn_ref, gks_ref, cg_ref, sg_ref,
                mqn_ref, wuq_ref, cmq_ref, smq_ref,
                mkvn_ref, wukvk_ref, wukvv_ref, cmk_ref, smk_ref,
                hy_ref, qg_ref, kg_ref, vg_ref, qm_ref, km_ref, vm_ref):
    x = x_ref[0]
    h = _rms(x, gpre_ref[...]).astype(BF16)
    cur = [0]

    def proj(n):
        lo = cur[0]
        cur[0] = lo + n
        return jnp.dot(h, win_ref[:, lo:lo + n], preferred_element_type=F32)

    hy = proj(HY_COLS)
    for r in range(x.shape[0] // FFT_N2):
        hy_ref[:, r, :] = hy[r * FFT_N2:(r + 1) * FFT_N2]

    cg, sg = cg_ref[...], sg_ref[...]
    low = lax.broadcasted_iota(jnp.int32, (x.shape[0], LANES), 1) < GQA_HEAD_DIM

    def head_pairs(n_tiles, gain, gain_sw):
        xa, xb = proj(n_tiles * LANES), proj(n_tiles * LANES)
        ca, sa = cg * gain, sg * gain_sw
        out = []
        for t in range(n_tiles):
            xc, xs = xa[:, t * LANES:(t + 1) * LANES], xb[:, t * LANES:(t + 1) * LANES]
            sq = xc * xc
            tot = jnp.sum(sq, axis=-1, keepdims=True)
            lo = jnp.sum(jnp.where(low, sq, 0.0), axis=-1, keepdims=True)
            ms = jnp.where(low, lo, tot - lo) * (1.0 / GQA_HEAD_DIM)
            out.append((xc * ca + xs * sa) * lax.rsqrt(ms + NORM_EPS))
        return out

    for t, q in enumerate(head_pairs(GQA_HEADS // 2, gqn_ref[...], gqs_ref[...])):
        qg_ref[0, :, t * LANES:(t + 1) * LANES] = q.astype(BF16)
    (kk,) = head_pairs(GQA_KV_HEADS // 2, gkn_ref[...], gks_ref[...])
    ksw = pltpu.roll(kk, GQA_HEAD_DIM, 1)
    kg_ref[0, 0] = jnp.where(low, kk, ksw).astype(BF16)
    kg_ref[0, 1] = jnp.where(low, ksw, kk).astype(BF16)
    vg_ref[0] = proj(LANES).T.astype(BF16)

    cq = _rms(proj(MLA_Q_RANK), mqn_ref[...]).astype(BF16)
    qm = jnp.dot(cq, wuq_ref[...], preferred_element_type=F32)
    cmq, smq = cmq_ref[...], smq_ref[...]
    nq = MLA_HEADS * LANES
    for j in range(MLA_HEADS):
        qm_ref[0, :, j * LANES:(j + 1) * LANES] = (
            qm[:, j * LANES:(j + 1) * LANES] * cmq
            + qm[:, nq + j * LANES:nq + (j + 1) * LANES] * smq).astype(BF16)

    ckv = _rms(proj(MLA_KV_RANK), mkvn_ref[...]).astype(BF16)
    kpe = proj(LANES) * cmk_ref[...] + proj(LANES) * smk_ref[...]
    kn = jnp.dot(ckv, wukvk_ref[...], preferred_element_type=F32)
    for j in range(MLA_HEADS):
        km_ref[0, j] = (kn[:, j * LANES:(j + 1) * LANES] + kpe).astype(BF16)
    vm = jnp.dot(ckv, wukvv_ref[...], preferred_element_type=F32)
    for j in range(vm.shape[1] // LANES):
        vm_ref[0, j * LANES:(j + 1) * LANES, :] = vm[:, j * LANES:(j + 1) * LANES].T.astype(BF16)


def _kin_call(x, gpre, win_p, gq, gk, tg, mqn, wuq_p, tmq, mkvn, wukvk_p, wukvv, tmk):
    B, L, D = x.shape
    tm = TM_IN
    nt = L // tm

    def full(a):
        return pl.BlockSpec(a.shape, lambda b, i: (0,) * a.ndim, pipeline_mode=pl.Buffered(1))

    def rows(w):
        return pl.BlockSpec((tm, w), lambda b, i: (i, 0))

    in_specs = [pl.BlockSpec((1, tm, D), lambda b, i: (b, i, 0)), full(gpre), full(win_p),
                full(gq[0]), full(gq[1]), full(gk[0]), full(gk[1]), rows(LANES), rows(LANES),
                full(mqn), full(wuq_p), rows(LANES), rows(LANES),
                full(mkvn), full(wukvk_p), full(wukvv), rows(LANES), rows(LANES)]
    gv_rows = GQA_KV_HEADS * GQA_HEAD_DIM
    mv_rows = MLA_HEADS * MLA_V_DIM
    out_shape = [
        jax.ShapeDtypeStruct((FFT_N2, B * L // FFT_N2, HY_COLS), F32),
        jax.ShapeDtypeStruct((B, L, GQA_HEADS * GQA_HEAD_DIM), BF16),
        jax.ShapeDtypeStruct((B, GQA_KV_HEADS, L, LANES), BF16),
        jax.ShapeDtypeStruct((B, gv_rows, L), BF16),
        jax.ShapeDtypeStruct((B, L, MLA_HEADS * LANES), BF16),
        jax.ShapeDtypeStruct((B, MLA_HEADS, L, LANES), BF16),
        jax.ShapeDtypeStruct((B, mv_rows, L), BF16),
    ]
    out_specs = [
        pl.BlockSpec((FFT_N2, tm // FFT_N2, HY_COLS), lambda b, i: (0, b * nt + i, 0)),
        pl.BlockSpec((1, tm, GQA_HEADS * GQA_HEAD_DIM), lambda b, i: (b, i, 0)),
        pl.BlockSpec((1, GQA_KV_HEADS, tm, LANES), lambda b, i: (b, 0, i, 0)),
        pl.BlockSpec((1, gv_rows, tm), lambda b, i: (b, 0, i)),
        pl.BlockSpec((1, tm, MLA_HEADS * LANES), lambda b, i: (b, i, 0)),
        pl.BlockSpec((1, MLA_HEADS, tm, LANES), lambda b, i: (b, 0, i, 0)),
        pl.BlockSpec((1, mv_rows, tm), lambda b, i: (b, 0, i)),
    ]
    return pl.pallas_call(
        _kin_kernel, grid=(B, nt), in_specs=in_specs, out_specs=out_specs, out_shape=out_shape,
        compiler_params=_cparams(("parallel", "parallel")), name="in_proj",
    )(x, gpre, win_p, *gq, *gk, *tg, mqn, wuq_p, *tmq, mkvn, wukvk_p, wukvv, *tmk)


def _attn_kernel(q_ref, k_ref, vt_ref, o_ref, sa_ref, sb_ref, ma_ref, mb_ref, *,
                 n_kv, n_rep, pack, tq, n_chunks, tk):
    i = pl.program_id(0)
    cols = n_rep * tq
    width = n_kv * cols
    grp = tk // 8

    @pl.when(i == 0)
    def _():
        sb_ref[...] = jnp.zeros(sb_ref.shape, F32)
        mb_ref[...] = jnp.zeros(mb_ref.shape, F32)

    def step(sw_ref, mw_ref, sr_ref, mr_ref):
        def q_head(h):
            tile, half = divmod(h, pack)
            qt = q_ref[0, tile * LANES:(tile + 1) * LANES, :]
            if pack == 2:
                z = jnp.zeros((LANES // 2, qt.shape[1]), qt.dtype)
                qt = jnp.concatenate([qt[:LANES // 2], z] if half == 0 else [z, qt[LANES // 2:]],
                                     axis=0)
            return qt

        qs = [jnp.concatenate([q_head(a * n_rep + j) for j in range(n_rep)], axis=1)
              for a in range(n_kv)]
        mx = jnp.max(mr_ref[...], axis=0, keepdims=True)
        m = jnp.full((8, width), -jnp.inf, F32)
        l = jnp.zeros((8, width), F32)
        accs = [jnp.zeros((vt_ref.shape[2], cols), F32) for _ in range(n_kv)]
        for c in range(n_chunks):
            st = jnp.concatenate(
                [jnp.dot(k_ref[0, a, c * tk:(c + 1) * tk, :], qs[a], preferred_element_type=F32)
                 for a in range(n_kv)], axis=1)
            sw_ref[c] = st
            m = jnp.maximum(m, jnp.max(st.reshape(grp, 8, width), axis=0))
            p = jnp.exp2(sr_ref[c] - mx)
            l = l + jnp.sum(p.reshape(grp, 8, width), axis=0)
            pb = p.astype(BF16)
            for a in range(n_kv):
                accs[a] = accs[a] + jnp.dot(vt_ref[0, a, :, c * tk:(c + 1) * tk],
                                            pb[:, a * cols:(a + 1) * cols],
                                            preferred_element_type=F32)
        mw_ref[...] = m
        ls = jnp.sum(l, axis=0, keepdims=True)
        heads = []
        for a in range(n_kv):
            oa = accs[a] / ls[:, a * cols:(a + 1) * cols]
            heads += [oa[:, j * tq:(j + 1) * tq] for j in range(n_rep)]
        o_ref[0] = jnp.concatenate(heads, axis=0).T.astype(BF16)

    @pl.when(i % 2 == 0)
    def _():
        step(sa_ref, ma_ref, sb_ref, mb_ref)

    @pl.when(i % 2 == 1)
    def _():
        step(sb_ref, mb_ref, sa_ref, ma_ref)


def _attn_call(qt, k, vt, *, n_kv, n_rep, pack, tq, name):
    B, hq, L = qt.shape
    H = hq * pack // LANES
    hkv, dv = vt.shape[1], vt.shape[2]
    G = hkv // n_kv
    hs = n_kv * n_rep
    tk = TK_ATTN
    n_chunks = L // tk
    nq = L // tq
    width = hs * tq
    kern = functools.partial(_attn_kernel, n_kv=n_kv, n_rep=n_rep, pack=pack, tq=tq,
                             n_chunks=n_chunks, tk=tk)
    total = B * G * nq

    def blk(s):
        return s // (G * nq), (s // nq) % G, s % nq

    def q_map(s):
        b, g, i = blk(jnp.minimum(s, total - 1))
        return b, g, i

    def k_map(s):
        b, g, _ = blk(jnp.minimum(s, total - 1))
        return b, g, 0, 0

    def v_map(s):
        b, g, _ = blk(jnp.maximum(s - 1, 0))
        return b, g, 0, 0

    def o_map(s):
        b, g, i = blk(jnp.maximum(s - 1, 0))
        return b, i, g

    return pl.pallas_call(
        kern, grid=(total + 1,),
        in_specs=[pl.BlockSpec((1, hs * LANES // pack, tq), q_map),
                  pl.BlockSpec((1, n_kv, L, LANES), k_map, pipeline_mode=pl.Buffered(1)),
                  pl.BlockSpec((1, n_kv, dv, L), v_map, pipeline_mode=pl.Buffered(1))],
        out_specs=pl.BlockSpec((1, tq, hs * dv), o_map),
        out_shape=jax.ShapeDtypeStruct((B, L, H * dv), BF16),
        scratch_shapes=[pltpu.VMEM((n_chunks, tk, width), F32), pltpu.VMEM((n_chunks, tk, width), F32),
                        pltpu.VMEM((8, width), F32), pltpu.VMEM((8, width), F32)],
        compiler_params=_cparams(("arbitrary",)), name=name,
    )(qt, k, vt)


def _hconv_kernel(x_ref, xp_ref, xn_ref, w_ref, b_ref, ga_ref, v_ref, x1_ref, x2_ref, a_ref, *, to):
    i = pl.program_id(0)
    n1 = lax.broadcasted_iota(jnp.int32, xp_ref.shape[1:], 0) % (FFT_N1 // 2)
    hp, hn = xp_ref[0], xn_ref[0]
    rows = hp.shape[0]
    hp_wrap = jnp.where(n1 == 0, 0.0, pltpu.roll(hp, 1, 0))
    hn_wrap = jnp.where(n1 == FFT_N1 // 2 - 1, 0.0, pltpu.roll(hn, rows - 1, 0))
    hp = jnp.where(i == 0, hp_wrap, hp)
    hn = jnp.where(i == pl.num_programs(0) - 1, hn_wrap, hn)
    w0, w1, w2, bias = w_ref[0:1, :], w_ref[1:2, :], w_ref[2:3, :], b_ref[...]
    for t in range(to):
        prev = x_ref[t - 1] if t > 0 else hp
        nxt = x_ref[t + 1] if t < to - 1 else hn
        uc = prev * w0 + x_ref[t] * w1 + nxt * w2 + bias
        v = uc[:, :HY_D]
        v_ref[t] = v
        x1_ref[t] = uc[:, HY_D:2 * HY_D]
        x2_ref[t] = uc[:, 2 * HY_D:]
        a_ref[t] = jnp.dot(ga_ref[t], v.astype(BF16), preferred_element_type=F32)


def _hconv_call(hy_t, w, b, ga, to=8):
    P, R, C = hy_t.shape
    M = ga.shape[1]
    nsteps = P // to
    slab = pl.BlockSpec((to, R, C), lambda i: (i, 0, 0))
    out = jax.ShapeDtypeStruct((P, R, HY_D), F32)
    ospec = pl.BlockSpec((to, R, HY_D), lambda i: (i, 0, 0))
    return pl.pallas_call(
        functools.partial(_hconv_kernel, to=to), grid=(nsteps,),
        in_specs=[slab,
                  pl.BlockSpec((1, R, C), lambda i: ((i * to + P - 1) % P, 0, 0)),
                  pl.BlockSpec((1, R, C), lambda i: (((i + 1) * to) % P, 0, 0)),
                  pl.BlockSpec((3, C), lambda i: (0, 0)),
                  pl.BlockSpec((1, C), lambda i: (0, 0)),
                  pl.BlockSpec((to, M, R), lambda i: (i, 0, 0))],
        out_specs=[ospec, ospec, ospec, pl.BlockSpec((to, M, HY_D), lambda i: (i, 0, 0))],
        out_shape=[out, out, out, jax.ShapeDtypeStruct((P, M, HY_D), F32)],
        compiler_params=_cparams(("parallel",)), name="hy_conv3",
    )(hy_t, hy_t, hy_t, w, b, ga)


def _split_bf16(a):
    hi = a.astype(BF16)
    return hi, (a - hi.astype(F32)).astype(BF16)


def _dot3(a, w_hi, w_lo):
    a_hi, a_lo = _split_bf16(a)
    dot = functools.partial(jnp.dot, preferred_element_type=F32)
    return dot(a_hi, w_hi) + dot(a_lo, w_hi) + dot(a_hi, w_lo)


def _filt_kernel(z_ref, w1h_ref, w1l_ref, b1_ref, f1_ref, w2h_ref, w2l_ref, b2_ref, f2_ref,
                 w3h_ref, w3l_ref, win_ref, gaf_ref, o_ref):
    h = jnp.sin(f1_ref[...] * (_dot3(z_ref[...], w1h_ref[...], w1l_ref[...]) + b1_ref[...]))
    h = jnp.sin(f2_ref[...] * (_dot3(h, w2h_ref[...], w2l_ref[...]) + b2_ref[...]))
    k = _dot3(h, w3h_ref[...], w3l_ref[...])
    half = k.shape[0]
    wcols = 4 * HY_D
    per_half = half // FFT_N1
    n1 = lax.broadcasted_iota(jnp.int32, (half, 2 * HY_D), 0) % FFT_N1
    for s in range(2):
        ks = k[:, s * wcols:(s + 1) * wcols]
        kk = jnp.where(n1 >= FFT_N1 // 2, ks[:, 2 * HY_D:], ks[:, :2 * HY_D])
        win = win_ref[s * half:(s + 1) * half, :]
        filt = (kk * jnp.concatenate([win, win], axis=-1)).astype(BF16)
        for j in range(per_half):
            o_ref[s * per_half + j] = jnp.dot(gaf_ref[s * per_half + j],
                                              filt[j * FFT_N1:(j + 1) * FFT_N1],
                                              preferred_element_type=F32)


def _filt_call(z_pack, fw, win_perm, gaf):
    tp = 2 * FILT_HALF
    n = win_perm.shape[0]
    to = tp // FFT_N1
    M = gaf.shape[1]

    def full(a):
        return pl.BlockSpec(a.shape, lambda i: (0,) * a.ndim)

    return pl.pallas_call(
        _filt_kernel, grid=(n // tp,),
        in_specs=[pl.BlockSpec((FILT_HALF, z_pack.shape[1]), lambda i: (i, 0))]
        + [full(a) for a in fw] + [pl.BlockSpec((tp, HY_D), lambda i: (i, 0)),
                                   pl.BlockSpec((to, M, FFT_N1), lambda i: (i, 0, 0))],
        out_specs=pl.BlockSpec((to, M, 2 * HY_D), lambda i: (i, 0, 0)),
        out_shape=jax.ShapeDtypeStruct((n // FFT_N1, M, 2 * HY_D), F32),
        compiler_params=_cparams(("parallel",)), name="hy_filter",
    )(z_pack, *fw, win_perm, gaf)


def _gather_ri(x_ref, j):
    return jnp.concatenate([x_ref[:, 0, j, :], x_ref[:, 1, j, :]], axis=0).astype(BF16)


def _gather_spec(a, to):
    return pl.BlockSpec((a.shape[0], 2, to, a.shape[3]), lambda i: (0, 0, i, 0))


def _filtb_kernel(mb_ref, x_ref, o_ref, *, to):
    for j in range(to):
        o_ref[j] = jnp.dot(mb_ref[...], _gather_ri(x_ref, j),
                           preferred_element_type=F32).astype(o_ref.dtype)


def _filtb_call(mb, ka, to=16):
    O, _, P, N = ka.shape
    return pl.pallas_call(
        functools.partial(_filtb_kernel, to=to), grid=(P // to,),
        in_specs=[pl.BlockSpec(mb.shape, lambda i: (0, 0)), _gather_spec(ka, to)],
        out_specs=pl.BlockSpec((to, 2 * O, N), lambda i: (i, 0, 0)),
        out_shape=jax.ShapeDtypeStruct((P, 2 * O, N), BF16),
        compiler_params=_cparams(("parallel",)), name="hy_fft_filt_b",
    )(mb, ka)


def _convb_kernel(mb_ref, gc_ref, x_ref, kf_ref, o_ref, *, to):
    h = FFT_N2
    for j in range(to):
        xs = jnp.dot(mb_ref[...], _gather_ri(x_ref, j), preferred_element_type=F32)
        xr, xi = xs[:h], xs[h:]
        kr, ki = kf_ref[j, :h, :].astype(F32), kf_ref[j, h:, :].astype(F32)
        ys = jnp.concatenate([xr * kr - xi * ki, xr * ki + xi * kr], axis=0).astype(BF16)
        o_ref[j] = jnp.dot(gc_ref[j], ys, preferred_element_type=F32)


def _convb_call(mb, gc, a, kf, order, to=16):
    O, _, P, N = a.shape
    return pl.pallas_call(
        functools.partial(_convb_kernel, to=to), grid=(P // to,),
        in_specs=[pl.BlockSpec(mb.shape, lambda i: (0, 0)),
                  pl.BlockSpec((to, 2 * O, 2 * O), lambda i: (i, 0, 0)),
                  _gather_spec(a, to),
                  pl.BlockSpec((to, 2 * O, N), lambda i: (i, 0, order))],
        out_specs=pl.BlockSpec((to, 2 * O, N), lambda i: (i, 0, 0)),
        out_shape=jax.ShapeDtypeStruct((P, 2 * O, N), F32),
        compiler_params=_cparams(("parallel",)), name="hy_spec_mul",
    )(mb, gc, a, kf)


def _convd_kernel(md_ref, c_ref, g_ref, u_ref, s_ref, *rest, to, chain):
    if chain:
        ga_ref, z_ref, a_ref = rest
    else:
        (o_ref,) = rest
    for j in range(to):
        y = jnp.dot(md_ref[...], _gather_ri(c_ref, j), preferred_element_type=F32)
        z = g_ref[j] * (y + u_ref[j] * s_ref[...])
        if chain:
            z_ref[j] = z
            a_ref[j] = jnp.dot(ga_ref[j], z.astype(BF16), preferred_element_type=F32)
        else:
            o_ref[:, j, :] = z


def _convd_call(md, c, gate, u, skip, ga=None, to=16):
    O, _, P, N = c.shape
    R = md.shape[0]
    tspec = pl.BlockSpec((to, R, N), lambda i: (i, 0, 0))
    in_specs = [pl.BlockSpec(md.shape, lambda i: (0, 0)), _gather_spec(c, to), tspec, tspec,
                pl.BlockSpec((1, N), lambda i: (0, 0))]
    args = [md, c, gate, u, skip.reshape(1, N)]
    if ga is not None:
        M = ga.shape[1]
        in_specs.append(pl.BlockSpec((to, M, R), lambda i: (i, 0, 0)))
        args.append(ga)
        out_specs = [tspec, pl.BlockSpec((to, M, N), lambda i: (i, 0, 0))]
        out_shape = [jax.ShapeDtypeStruct((P, R, N), F32), jax.ShapeDtypeStruct((P, M, N), F32)]
    else:
        out_specs = pl.BlockSpec((R, to, N), lambda i: (0, i, 0))
        out_shape = jax.ShapeDtypeStruct((R, P, N), F32)
    return pl.pallas_call(
        functools.partial(_convd_kernel, to=to, chain=ga is not None), grid=(P // to,),
        in_specs=in_specs, out_specs=out_specs, out_shape=out_shape,
        compiler_params=_cparams(("parallel",)), name="hy_fft_d",
    )(*args)


def _dft_tables():
    n = FFT_N1 * FFT_N2
    k = np.arange(FFT_N1)
    f = np.exp(-2j * np.pi * np.outer(k, k) / FFT_N1)
    t = np.exp(-2j * np.pi * np.outer(k, k) / n)
    return f, t, n


def _dft_matrices():
    f, t, n = _dft_tables()
    fr, fi = jnp.asarray(f.real, F32), jnp.asarray(f.imag, F32)
    tr, ti = jnp.asarray(t.real, F32), jnp.asarray(t.imag, F32)
    half = FFT_N1 // 2
    er = fr[None] * tr[:, :, None] - fi[None] * ti[:, :, None]
    ei = fr[None] * ti[:, :, None] + fi[None] * tr[:, :, None]
    ga = jnp.concatenate([jnp.concatenate([er[:, :, :half], -ei[:, :, :half]], axis=2),
                          jnp.concatenate([ei[:, :, :half], er[:, :, :half]], axis=2)], axis=1)
    gaf = jnp.concatenate([er, ei], axis=1)
    mb = jnp.concatenate([jnp.concatenate([fr, -fi], axis=1),
                          jnp.concatenate([fi, fr], axis=1)], axis=0)
    tct = jnp.transpose(tr)[:, :, None]
    tst = -jnp.transpose(ti)[:, :, None]
    gr = tct * fr[None] - tst * (-fi[None])
    gi = tct * (-fi[None]) + tst * fr[None]
    gc = jnp.concatenate([jnp.concatenate([gr, -gi], axis=2),
                          jnp.concatenate([gi, gr], axis=2)], axis=1)
    hr, hi = fr[:half] / n, -fi[:half] / n
    md = jnp.concatenate([jnp.concatenate([hr, -hi], axis=1),
                          jnp.concatenate([hi, hr], axis=1)], axis=0)
    return (ga.astype(BF16), gaf.astype(BF16), mb.astype(BF16), gc.astype(BF16), md.astype(BF16))


def _hyena_positions(L):
    p = FFT_N2 * np.arange(FFT_N1)[None, :] + np.arange(FFT_N2)[:, None]
    pos = np.where(p < L, p, 2 * L - 1 - p).reshape(2 * L, 1).astype(np.float64)
    t = pos / (L - 1)
    w = 2.0 * math.pi * pos / L
    f = np.linspace(1e-4, HY_BANDS - 1, HY_BANDS)[None, :]
    z = np.concatenate([t, np.cos(f * w), -np.sin(f * w),
                        np.zeros((2 * L, FILT_FEAT - HY_EMB))], axis=-1)
    z_pack = (z.reshape(-1, 2, FILT_HALF, FILT_FEAT).transpose(0, 2, 1, 3)
              .reshape(-1, 2 * FILT_FEAT))
    max_decay = math.log(HY_DECAY_TARGET) / HY_FAST_DECAY_PCT
    min_decay = math.log(HY_DECAY_TARGET) / HY_SLOW_DECAY_PCT
    deltas = jnp.linspace(min_decay, max_decay, HY_D, dtype=F32)
    window = jnp.exp(-jnp.asarray(t, F32) * jnp.abs(deltas)[None, :])
    return jnp.asarray(z_pack, F32), window


def _filter_weights(w1, b1, f1, w2, b2, f2, w3):
    def bd(w):
        z = jnp.zeros_like(w)
        return jnp.concatenate([jnp.concatenate([w, z], axis=1),
                                jnp.concatenate([z, w], axis=1)], axis=0)

    def twice(v):
        return jnp.concatenate([v, v])[None]

    w1 = jnp.pad(w1, ((0, FILT_FEAT - w1.shape[0]), (0, 0)))
    w3 = w3.reshape(-1, 2, 2, HY_D).transpose(0, 2, 1, 3).reshape(-1, 4 * HY_D)
    return (*_split_bf16(bd(w1)), twice(b1), twice(f1), *_split_bf16(bd(w2)), twice(b2), twice(f2),
            *_split_bf16(bd(w3)))


def _hyena_layer(hy_in, B, L, conv_w, conv_b, fw, skip, z_perm, win_perm, mats):
    ga, gaf, mb, gc, md = mats
    v, x1, x2, a = _hconv_call(hy_in, conv_w, conv_b[None], ga)
    ka = _filt_call(z_perm, fw, win_perm, gaf)
    kf = _filtb_call(mb, ka.reshape(FFT_N2, 2, FFT_N1, 2 * HY_D))

    c = _convb_call(mb, gc, a.reshape(FFT_N2, 2, FFT_N1, HY_D), kf, 0)
    z, a = _convd_call(md, c.reshape(FFT_N1, 2, FFT_N2, HY_D), x1, v, skip[0], ga=ga)
    c = _convb_call(mb, gc, a.reshape(FFT_N2, 2, FFT_N1, HY_D), kf, 1)
    z = _convd_call(md, c.reshape(FFT_N1, 2, FFT_N2, HY_D), x2, z, skip[1])
    return z.reshape(B, L, HY_D)


def _kout_kernel(x_ref, yh_ref, yg_ref, ym_ref, gh_ref, gg_ref, gm_ref, w_ref, gpost_ref, gffn_ref,
                 xo_ref, h_ref):
    a = _rms(yh_ref[0], gh_ref[...]).astype(BF16)
    b = _rms(yg_ref[0].astype(F32), gg_ref[...]).astype(BF16)
    c = _rms(ym_ref[0].astype(F32), gm_ref[...]).astype(BF16)
    o1 = HY_D
    o2 = o1 + GQA_HEADS * GQA_HEAD_DIM
    y = (jnp.dot(a, w_ref[:o1, :], preferred_element_type=F32)
         + jnp.dot(b, w_ref[o1:o2, :], preferred_element_type=F32)
         + jnp.dot(c, w_ref[o2:, :], preferred_element_type=F32))
    xo = x_ref[0] + _rms(y, gpost_ref[...])
    xo_ref[0] = xo
    h_ref[0] = _rms(xo, gffn_ref[...]).astype(BF16)


def _kout_call(x, yh, yg, ym, gh, gg, gm, w_p, gpost, gffn):
    B, L, D = x.shape
    tm = TM_OUT

    def rows(a):
        return pl.BlockSpec((1, tm, a.shape[2]), lambda b, i: (b, i, 0))

    def full(a):
        return pl.BlockSpec(a.shape, lambda b, i: (0,) * a.ndim)

    return pl.pallas_call(
        _kout_kernel, grid=(B, L // tm),
        in_specs=[rows(x), rows(yh), rows(yg), rows(ym), full(gh), full(gg), full(gm), full(w_p),
                  full(gpost), full(gffn)],
        out_specs=[rows(x), rows(x)],
        out_shape=[jax.ShapeDtypeStruct((B, L, D), F32), jax.ShapeDtypeStruct((B, L, D), BF16)],
        compiler_params=_cparams(("parallel", "parallel")), name="out_proj",
    )(x, yh, yg, ym, gh, gg, gm, w_p, gpost, gffn)


HALO = 16


def _ffn_kernel(h_ref, hp_ref, hn_ref, x_ref, wup_ref, cw_ref, cb_ref, wd_ref, gpost_ref,
                o_ref, act_ref):
    i = pl.program_id(1)
    tm = h_ref.shape[1]
    prev = jnp.where(i > 0, hp_ref[0], jnp.zeros_like(hp_ref[0]))
    nxt = jnp.where(i < pl.num_programs(1) - 1, hn_ref[0], jnp.zeros_like(hn_ref[0]))
    he = jnp.concatenate([prev, h_ref[0], nxt], axis=0)
    ext = tm + 2 * HALO
    tf = TF_FFN

    def conv(c0):
        up = jnp.dot(he, wup_ref[:, c0:c0 + tf], preferred_element_type=F32)
        um = pltpu.roll(up, 1, 0)[HALO:HALO + tm]
        upl = pltpu.roll(up, ext - 1, 0)[HALO:HALO + tm]
        return (um * cw_ref[0:1, c0:c0 + tf] + up[HALO:HALO + tm] * cw_ref[1:2, c0:c0 + tf]
                + upl * cw_ref[2:3, c0:c0 + tf] + cb_ref[:, c0:c0 + tf])

    for j in range(D_FF // tf):
        g = conv(j * tf)
        u = conv(D_FF + j * tf)
        gelu = 0.5 * g * (1.0 + jnp.tanh(math.sqrt(2.0 / math.pi) * (g + 0.044715 * (g * g * g))))
        act_ref[:, j * tf:(j + 1) * tf] = (gelu * u).astype(BF16)
    f = jnp.dot(act_ref[...], wd_ref[...], preferred_element_type=F32)
    o_ref[0] = x_ref[0] + _rms(f, gpost_ref[...])


def _ffn_call(h, x, w_up, cw, cb, w_down, gpost):
    B, L, D = x.shape
    tm = TM_FFN
    nb = tm // HALO
    last = L // HALO - 1

    def resident(a):
        return pl.BlockSpec(a.shape, lambda b, i: (0,) * a.ndim, pipeline_mode=pl.Buffered(1))

    cb = cb[None]
    return pl.pallas_call(
        _ffn_kernel, grid=(B, L // tm),
        in_specs=[pl.BlockSpec((1, tm, D), lambda b, i: (b, i, 0)),
                  pl.BlockSpec((1, HALO, D), lambda b, i: (b, jnp.maximum(i * nb - 1, 0), 0)),
                  pl.BlockSpec((1, HALO, D), lambda b, i: (b, jnp.minimum((i + 1) * nb, last), 0)),
                  pl.BlockSpec((1, tm, D), lambda b, i: (b, i, 0)),
                  resident(w_up), resident(cw), resident(cb), resident(w_down), resident(gpost)],
        out_specs=pl.BlockSpec((1, tm, D), lambda b, i: (b, i, 0)),
        out_shape=jax.ShapeDtypeStruct((B, L, D), F32),
        scratch_shapes=[pltpu.VMEM((tm, D_FF), BF16)],
        compiler_params=_cparams(("parallel", "parallel")), name="conv_ffn",
    )(h, h, h, x, w_up, cw, cb, w_down, gpost)


def _axial_tables(L, rot_dim):
    pos = np.arange(L)
    n_axis = rot_dim // 4
    inv = ROPE_THETA ** (-np.arange(n_axis) / n_axis)
    ang = np.concatenate([(pos // GRID_W)[:, None] * inv, (pos % GRID_W)[:, None] * inv], axis=-1)
    return jnp.asarray(np.cos(ang), F32), jnp.asarray(np.sin(ang), F32)


def _rope_tables(L):
    def lanes(parts):
        used = sum(p.shape[1] for p in parts)
        return jnp.concatenate(parts + [jnp.zeros((L, LANES - used), F32)], axis=1)

    cg, sg = _axial_tables(L, GQA_HEAD_DIM)
    tg = (lanes([cg, cg, cg, cg]), lanes([sg, sg, sg, sg]))
    cm, sm = _axial_tables(L, MLA_ROPE_DIM)
    nope0 = jnp.zeros((L, MLA_NOPE_DIM), F32)
    nope1 = jnp.ones((L, MLA_NOPE_DIM), F32)
    tmk = (lanes([nope0, cm, cm]), lanes([nope0, sm, sm]))
    sc = (MLA_NOPE_DIM + MLA_ROPE_DIM) ** -0.5 * LOG2E
    tmq = (lanes([nope1, cm, cm]) * sc, tmk[1] * sc)
    return tg, tmq, tmk


def _partner(w, half, sign=-1.0):
    return jnp.concatenate([sign * w[..., half:], w[..., :half]], axis=-1)


def _gain_pair(g, half):
    return jnp.tile(g, 2)[None], jnp.tile(_partner(g, half, 1.0), 2)[None]


def _pad_heads(w, n_heads, width):
    k = w.shape[0]
    return jnp.pad(w.reshape(k, n_heads, width), ((0, 0), (0, 0), (0, LANES - width))).reshape(
        k, n_heads * LANES)


def kernel(x, mix_pre_norm, w_in, hy_conv_w, hy_conv_b, hy_filt_w1, hy_filt_b1, hy_filt_freq1,
           hy_filt_w2, hy_filt_b2, hy_filt_freq2, hy_filt_w3, hy_skip, gqa_q_norm, gqa_k_norm,
           mla_q_a_norm, mla_w_uq, mla_kv_a_norm, mla_w_ukv, hy_out_norm, gqa_out_norm,
           mla_out_norm, w_out, mix_post_norm, ffn_pre_norm, w_up, ffn_conv_w, ffn_conv_b,
           w_down, ffn_post_norm):
    B, L, D = x.shape
    assert B == 2 and 2 * L == FFT_N1 * FFT_N2 and D == D_MODEL
    depth = w_in.shape[0]
    tg, tmq, tmk = _rope_tables(L)
    z_perm, win_perm = _hyena_positions(L)
    mats = _dft_matrices()

    for l in range(depth):
        wl = w_in[l]
        hd, hh, rh = GQA_HEAD_DIM, GQA_HEAD_DIM // 2, MLA_ROPE_DIM // 2
        o1 = HY_COLS
        o2 = o1 + GQA_HEADS * hd
        o3 = o2 + GQA_KV_HEADS * hd
        o4 = o3 + GQA_KV_HEADS * hd
        o5 = o4 + MLA_Q_RANK
        o6 = o5 + MLA_KV_RANK
        wq = wl[:, o1:o2].reshape(D, GQA_HEADS, hd)
        wk = wl[:, o2:o3].reshape(D, GQA_KV_HEADS, hd)
        wkr = wl[:, o6:]
        pe_pad = ((0, 0), (MLA_NOPE_DIM, LANES - MLA_NOPE_DIM - MLA_ROPE_DIM))
        win_p = jnp.concatenate(
            [wl[:, :o1],
             wq.reshape(D, -1), _partner(wq, hh).reshape(D, -1),
             wk.reshape(D, -1), _partner(wk, hh).reshape(D, -1),
             wl[:, o3:o6], jnp.pad(wkr, pe_pad), jnp.pad(_partner(wkr, rh), pe_pad)],
            axis=1).astype(BF16)
        gq = _gain_pair(gqa_q_norm[l] * (hd ** -0.5 * LOG2E), hh)
        gk = _gain_pair(gqa_k_norm[l], hh)
        wuq = mla_w_uq[l].reshape(MLA_Q_RANK, MLA_HEADS, MLA_NOPE_DIM + MLA_ROPE_DIM)
        wuq_pe = jnp.pad(_partner(wuq[:, :, MLA_NOPE_DIM:], rh),
                         ((0, 0), (0, 0), (MLA_NOPE_DIM, 0)))
        wuq_p = jnp.concatenate(
            [_pad_heads(mla_w_uq[l], MLA_HEADS, MLA_NOPE_DIM + MLA_ROPE_DIM),
             _pad_heads(wuq_pe.reshape(MLA_Q_RANK, -1), MLA_HEADS, MLA_NOPE_DIM + MLA_ROPE_DIM)],
            axis=1).astype(BF16)
        wukv = mla_w_ukv[l].reshape(MLA_KV_RANK, MLA_HEADS, MLA_NOPE_DIM + MLA_V_DIM)
        wukvk_p = _pad_heads(wukv[:, :, :MLA_NOPE_DIM].reshape(MLA_KV_RANK, -1), MLA_HEADS,
                             MLA_NOPE_DIM).astype(BF16)
        wukvv = wukv[:, :, MLA_NOPE_DIM:].reshape(MLA_KV_RANK, MLA_HEADS * MLA_V_DIM).astype(BF16)

        hy_in, qg, kg, vg, qm, km, vm = _kin_call(
            x, mix_pre_norm[l][None], win_p, gq, gk, tg, mla_q_a_norm[l][None], wuq_p, tmq,
            mla_kv_a_norm[l][None], wukvk_p, wukvv, tmk)

        fw = _filter_weights(hy_filt_w1[l], hy_filt_b1[l], hy_filt_freq1[l], hy_filt_w2[l],
                             hy_filt_b2[l], hy_filt_freq2[l], hy_filt_w3[l])
        y_hy = _hyena_layer(hy_in, B, L, hy_conv_w[l], hy_conv_b[l], fw, hy_skip[l], z_perm,
                            win_perm, mats)

        y_gqa = _attn_call(qg.transpose(0, 2, 1), kg, vg.reshape(B, GQA_KV_HEADS, GQA_HEAD_DIM, L),
                           n_kv=1, n_rep=GQA_HEADS // GQA_KV_HEADS, pack=2, tq=TQ_GQA,
                           name="attn_gqa")
        y_mla = _attn_call(qm.transpose(0, 2, 1), km, vm.reshape(B, MLA_HEADS, MLA_V_DIM, L),
                           n_kv=2, n_rep=1, pack=1, tq=TQ_MLA, name="attn_mla")

        x, h2 = _kout_call(x, y_hy, y_gqa, y_mla, hy_out_norm[l][None], gqa_out_norm[l][None],
                           mla_out_norm[l][None], w_out[l].astype(BF16),
                           mix_post_norm[l][None], ffn_pre_norm[l][None])
        x = _ffn_call(h2, x, w_up[l].astype(BF16), ffn_conv_w[l], ffn_conv_b[l],
                      w_down[l].astype(BF16), ffn_post_norm[l][None])
    return x
```

```python
import functools
import math

import numpy as np
import jax
import jax.numpy as jnp
from jax import lax
from jax.experimental import pallas as pl
from jax.experimental.pallas import tpu as pltpu

F32 = jnp.float32
BF16 = jnp.bfloat16

NORM_EPS = 1e-6
ROPE_THETA = 10000.0
GRID_W = 64
LOG2E = math.log2(math.e)

D_MODEL = 1024
HY_D = 256
HY_COLS = 3 * HY_D
HY_EMB = 33
HY_BANDS = 16
HY_DECAY_TARGET = 1e-2
HY_FAST_DECAY_PCT = 0.3
HY_SLOW_DECAY_PCT = 1.5
GQA_HEADS = 8
GQA_KV_HEADS = 2
GQA_HEAD_DIM = 64
MLA_HEADS = 4
MLA_Q_RANK = 256
MLA_KV_RANK = 128
MLA_NOPE_DIM = 64
MLA_ROPE_DIM = 32
MLA_V_DIM = 64
D_FF = 2816

LANES = 128
FFT_N1 = 128
FFT_N2 = 128

TM_IN = 1024
TK_ATTN = 256
TQ_GQA = 128
TQ_MLA = 256
TM_OUT = 1024
TM_FFN = 512
TF_FFN = 256
TL_HCONV = 1024
FILT_HALF = 512
FILT_FEAT = 64
VMEM_LIMIT = 56 * 1024 * 1024


def _cparams(sem):
    return pltpu.CompilerParams(dimension_semantics=sem, vmem_limit_bytes=VMEM_LIMIT)


def _rms(x, g):
    return x * lax.rsqrt(jnp.mean(x * x, axis=-1, keepdims=True) + NORM_EPS) * g


def _kin_kernel(x_ref, gpre_ref, win_ref, gqn_ref, gqs_ref, gkn_ref, gks_ref, cg_ref, sg_ref,
                mqn_ref, wuq_ref, cmq_ref, smq_ref,
                mkvn_ref, wukvk_ref, wukvv_ref, cmk_ref, smk_ref,
                hy_ref, qg_ref, kg_ref, vg_ref, qm_ref, km_ref, vm_ref):
    x = x_ref[0]
    h = _rms(x, gpre_ref[...]).astype(BF16)
    cur = [0]

    def proj(n):
        lo = cur[0]
        cur[0] = lo + n
        return jnp.dot(h, win_ref[:, lo:lo + n], preferred_element_type=F32)

    hy_ref[0] = proj(HY_COLS)

    cg, sg = cg_ref[...], sg_ref[...]
    low = lax.broadcasted_iota(jnp.int32, (x.shape[0], LANES), 1) < GQA_HEAD_DIM

    def head_pairs(n_tiles, gain, gain_sw):
        xa, xb = proj(n_tiles * LANES), proj(n_tiles * LANES)
        ca, sa = cg * gain, sg * gain_sw
        out = []
        for t in range(n_tiles):
            xc, xs = xa[:, t * LANES:(t + 1) * LANES], xb[:, t * LANES:(t + 1) * LANES]
            sq = xc * xc
            tot = jnp.sum(sq, axis=-1, keepdims=True)
            lo = jnp.sum(jnp.where(low, sq, 0.0), axis=-1, keepdims=True)
            ms = jnp.where(low, lo, tot - lo) * (1.0 / GQA_HEAD_DIM)
            out.append((xc * ca + xs * sa) * lax.rsqrt(ms + NORM_EPS))
        return out

    for t, q in enumerate(head_pairs(GQA_HEADS // 2, gqn_ref[...], gqs_ref[...])):
        qg_ref[0, :, t * LANES:(t + 1) * LANES] = q.astype(BF16)
    (kk,) = head_pairs(GQA_KV_HEADS // 2, gkn_ref[...], gks_ref[...])
    ksw = pltpu.roll(kk, GQA_HEAD_DIM, 1)
    kg_ref[0, 0] = jnp.where(low, kk, ksw).astype(BF16)
    kg_ref[0, 1] = jnp.where(low, ksw, kk).astype(BF16)
    vg_ref[0] = proj(LANES).T.astype(BF16)

    cq = _rms(proj(MLA_Q_RANK), mqn_ref[...]).astype(BF16)
    qm = jnp.dot(cq, wuq_ref[...], preferred_element_type=F32)
    cmq, smq = cmq_ref[...], smq_ref[...]
    nq = MLA_HEADS * LANES
    for j in range(MLA_HEADS):
        qm_ref[0, :, j * LANES:(j + 1) * LANES] = (
            qm[:, j * LANES:(j + 1) * LANES] * cmq
            + qm[:, nq + j * LANES:nq + (j + 1) * LANES] * smq).astype(BF16)

    ckv = _rms(proj(MLA_KV_RANK), mkvn_ref[...]).astype(BF16)
    kpe = proj(LANES) * cmk_ref[...] + proj(LANES) * smk_ref[...]
    kn = jnp.dot(ckv, wukvk_ref[...], preferred_element_type=F32)
    for j in range(MLA_HEADS):
        km_ref[0, j] = (kn[:, j * LANES:(j + 1) * LANES] + kpe).astype(BF16)
    vm = jnp.dot(ckv, wukvv_ref[...], preferred_element_type=F32)
    for j in range(vm.shape[1] // LANES):
        vm_ref[0, j * LANES:(j + 1) * LANES, :] = vm[:, j * LANES:(j + 1) * LANES].T.astype(BF16)


def _kin_call(x, gpre, win_p, gq, gk, tg, mqn, wuq_p, tmq, mkvn, wukvk_p, wukvv, tmk):
    B, L, D = x.shape
    tm = TM_IN
    nt = L // tm

    def full(a):
        return pl.BlockSpec(a.shape, lambda b, i: (0,) * a.ndim, pipeline_mode=pl.Buffered(1))

    def rows(w):
        return pl.BlockSpec((tm, w), lambda b, i: (i, 0))

    in_specs = [pl.BlockSpec((1, tm, D), lambda b, i: (b, i, 0)), full(gpre), full(win_p),
                full(gq[0]), full(gq[1]), full(gk[0]), full(gk[1]), rows(LANES), rows(LANES),
                full(mqn), full(wuq_p), rows(LANES), rows(LANES),
                full(mkvn), full(wukvk_p), full(wukvv), rows(LANES), rows(LANES)]
    gv_rows = GQA_KV_HEADS * GQA_HEAD_DIM
    mv_rows = MLA_HEADS * MLA_V_DIM
    out_shape = [
        jax.ShapeDtypeStruct((B, L, HY_COLS), F32),
        jax.ShapeDtypeStruct((B, L, GQA_HEADS * GQA_HEAD_DIM), BF16),
        jax.ShapeDtypeStruct((B, GQA_KV_HEADS, L, LANES), BF16),
        jax.ShapeDtypeStruct((B, gv_rows, L), BF16),
        jax.ShapeDtypeStruct((B, L, MLA_HEADS * LANES), BF16),
        jax.ShapeDtypeStruct((B, MLA_HEADS, L, LANES), BF16),
        jax.ShapeDtypeStruct((B, mv_rows, L), BF16),
    ]
    out_specs = [
        pl.BlockSpec((1, tm, HY_COLS), lambda b, i: (b, i, 0)),
        pl.BlockSpec((1, tm, GQA_HEADS * GQA_HEAD_DIM), lambda b, i: (b, i, 0)),
        pl.BlockSpec((1, GQA_KV_HEADS, tm, LANES), lambda b, i: (b, 0, i, 0)),
        pl.BlockSpec((1, gv_rows, tm), lambda b, i: (b, 0, i)),
        pl.BlockSpec((1, tm, MLA_HEADS * LANES), lambda b, i: (b, i, 0)),
        pl.BlockSpec((1, MLA_HEADS, tm, LANES), lambda b, i: (b, 0, i, 0)),
        pl.BlockSpec((1, mv_rows, tm), lambda b, i: (b, 0, i)),
    ]
    return pl.pallas_call(
        _kin_kernel, grid=(B, nt), in_specs=in_specs, out_specs=out_specs, out_shape=out_shape,
        compiler_params=_cparams(("parallel", "parallel")), name="in_proj",
    )(x, gpre, win_p, *gq, *gk, *tg, mqn, wuq_p, *tmq, mkvn, wukvk_p, wukvv, *tmk)


def _attn_kernel(q_ref, k_ref, vt_ref, o_ref, sa_ref, sb_ref, ma_ref, mb_ref, *,
                 n_kv, n_rep, pack, tq, n_chunks, tk):
    i = pl.program_id(0)
    cols = n_rep * tq
    width = n_kv * cols
    grp = tk // 8

    @pl.when(i == 0)
    def _():
        sb_ref[...] = jnp.zeros(sb_ref.shape, F32)
        mb_ref[...] = jnp.zeros(mb_ref.shape, F32)

    def step(sw_ref, mw_ref, sr_ref, mr_ref):
        def q_head(h):
            tile, half = divmod(h, pack)
            qt = q_ref[0, tile * LANES:(tile + 1) * LANES, :]
            if pack == 2:
                z = jnp.zeros((LANES // 2, qt.shape[1]), qt.dtype)
                qt = jnp.concatenate([qt[:LANES // 2], z] if half == 0 else [z, qt[LANES // 2:]],
                                     axis=0)
            return qt

        qs = [jnp.concatenate([q_head(a * n_rep + j) for j in range(n_rep)], axis=1)
              for a in range(n_kv)]
        mx = jnp.max(mr_ref[...], axis=0, keepdims=True)
        m = jnp.full((8, width), -jnp.inf, F32)
        l = jnp.zeros((8, width), F32)
        accs = [jnp.zeros((vt_ref.shape[2], cols), F32) for _ in range(n_kv)]
        for c in range(n_chunks):
            st = jnp.concatenate(
                [jnp.dot(k_ref[0, a, c * tk:(c + 1) * tk, :], qs[a], preferred_element_type=F32)
                 for a in range(n_kv)], axis=1)
            sw_ref[c] = st
            m = jnp.maximum(m, jnp.max(st.reshape(grp, 8, width), axis=0))
            p = jnp.exp2(sr_ref[c] - mx)
            l = l + jnp.sum(p.reshape(grp, 8, width), axis=0)
            pb = p.astype(BF16)
            for a in range(n_kv):
                accs[a] = accs[a] + jnp.dot(vt_ref[0, a, :, c * tk:(c + 1) * tk],
                                            pb[:, a * cols:(a + 1) * cols],
                                            preferred_element_type=F32)
        mw_ref[...] = m
        ls = jnp.sum(l, axis=0, keepdims=True)
        heads = []
        for a in range(n_kv):
            oa = accs[a] / ls[:, a * cols:(a + 1) * cols]
            heads += [oa[:, j * tq:(j + 1) * tq] for j in range(n_rep)]
        o_ref[0] = jnp.concatenate(heads, axis=0).T.astype(BF16)

    @pl.when(i % 2 == 0)
    def _():
        step(sa_ref, ma_ref, sb_ref, mb_ref)

    @pl.when(i % 2 == 1)
    def _():
        step(sb_ref, mb_ref, sa_ref, ma_ref)


def _attn_call(qt, k, vt, *, n_kv, n_rep, pack, tq, name):
    B, hq, L = qt.shape
    H = hq * pack // LANES
    hkv, dv = vt.shape[1], vt.shape[2]
    G = hkv // n_kv
    hs = n_kv * n_rep
    tk = TK_ATTN
    n_chunks = L // tk
    nq = L // tq
    width = hs * tq
    kern = functools.partial(_attn_kernel, n_kv=n_kv, n_rep=n_rep, pack=pack, tq=tq,
                             n_chunks=n_chunks, tk=tk)
    total = B * G * nq

    def blk(s):
        return s // (G * nq), (s // nq) % G, s % nq

    def q_map(s):
        b, g, i = blk(jnp.minimum(s, total - 1))
        return b, g, i

    def k_map(s):
        b, g, _ = blk(jnp.minimum(s, total - 1))
        return b, g, 0, 0

    def v_map(s):
        b, g, _ = blk(jnp.maximum(s - 1, 0))
        return b, g, 0, 0

    def o_map(s):
        b, g, i = blk(jnp.maximum(s - 1, 0))
        return b, i, g

    return pl.pallas_call(
        kern, grid=(total + 1,),
        in_specs=[pl.BlockSpec((1, hs * LANES // pack, tq), q_map),
                  pl.BlockSpec((1, n_kv, L, LANES), k_map, pipeline_mode=pl.Buffered(1)),
                  pl.BlockSpec((1, n_kv, dv, L), v_map, pipeline_mode=pl.Buffered(1))],
        out_specs=pl.BlockSpec((1, tq, hs * dv), o_map),
        out_shape=jax.ShapeDtypeStruct((B, L, H * dv), BF16),
        scratch_shapes=[pltpu.VMEM((n_chunks, tk, width), F32), pltpu.VMEM((n_chunks, tk, width), F32),
                        pltpu.VMEM((8, width), F32), pltpu.VMEM((8, width), F32)],
        compiler_params=_cparams(("arbitrary",)), name=name,
    )(qt, k, vt)


def _hconv_kernel(x_ref, xp_ref, xn_ref, w_ref, b_ref, v_ref, x1_ref, x2_ref):
    i = pl.program_id(1)
    x = x_ref[0]
    tl = x.shape[0]
    prev = jnp.where(i > 0, xp_ref[0][7:8, :], 0.0)
    nxt = jnp.where(i < pl.num_programs(1) - 1, xn_ref[0][0:1, :], 0.0)
    r = lax.broadcasted_iota(jnp.int32, x.shape, 0)
    xm = jnp.where(r == 0, prev, pltpu.roll(x, 1, 0))
    xp = jnp.where(r == tl - 1, nxt, pltpu.roll(x, tl - 1, 0))
    uc = xm * w_ref[0:1, :] + x * w_ref[1:2, :] + xp * w_ref[2:3, :] + b_ref[...]
    for r in range(tl // FFT_N2):
        blk = uc[r * FFT_N2:(r + 1) * FFT_N2]
        v_ref[:, r, :] = blk[:, :HY_D]
        x1_ref[:, r, :] = blk[:, HY_D:2 * HY_D]
        x2_ref[:, r, :] = blk[:, 2 * HY_D:]


def _hconv_call(hy_in, w, b):
    B, L, C = hy_in.shape
    tl = TL_HCONV
    nb = tl // 8
    last = L // 8 - 1
    nt = L // tl
    rows = tl // FFT_N2
    out = jax.ShapeDtypeStruct((FFT_N2, B * L // FFT_N2, HY_D), F32)
    ospec = pl.BlockSpec((FFT_N2, rows, HY_D), lambda b_, i: (0, b_ * nt + i, 0))
    return pl.pallas_call(
        _hconv_kernel, grid=(B, L // tl),
        in_specs=[pl.BlockSpec((1, tl, C), lambda b_, i: (b_, i, 0)),
                  pl.BlockSpec((1, 8, C), lambda b_, i: (b_, jnp.maximum(i * nb - 1, 0), 0)),
                  pl.BlockSpec((1, 8, C), lambda b_, i: (b_, jnp.minimum((i + 1) * nb, last), 0)),
                  pl.BlockSpec((3, C), lambda b_, i: (0, 0)),
                  pl.BlockSpec((1, C), lambda b_, i: (0, 0))],
        out_specs=[ospec, ospec, ospec], out_shape=[out, out, out],
        compiler_params=_cparams(("parallel", "parallel")), name="hy_conv3",
    )(hy_in, hy_in, hy_in, w, b)


def _split_bf16(a):
    hi = a.astype(BF16)
    return hi, (a - hi.astype(F32)).astype(BF16)


def _dot3(a, w_hi, w_lo):
    a_hi, a_lo = _split_bf16(a)
    dot = functools.partial(jnp.dot, preferred_element_type=F32)
    return dot(a_hi, w_hi) + dot(a_lo, w_hi) + dot(a_hi, w_lo)


def _filt_kernel(z_ref, w1h_ref, w1l_ref, b1_ref, f1_ref, w2h_ref, w2l_ref, b2_ref, f2_ref,
                 w3h_ref, w3l_ref, win_ref, gaf_ref, o_ref):
    h = jnp.sin(f1_ref[...] * (_dot3(z_ref[...], w1h_ref[...], w1l_ref[...]) + b1_ref[...]))
    h = jnp.sin(f2_ref[...] * (_dot3(h, w2h_ref[...], w2l_ref[...]) + b2_ref[...]))
    k = _dot3(h, w3h_ref[...], w3l_ref[...])
    half = k.shape[0]
    wcols = 4 * HY_D
    per_half = half // FFT_N1
    n1 = lax.broadcasted_iota(jnp.int32, (half, 2 * HY_D), 0) % FFT_N1
    for s in range(2):
        ks = k[:, s * wcols:(s + 1) * wcols]
        kk = jnp.where(n1 >= FFT_N1 // 2, ks[:, 2 * HY_D:], ks[:, :2 * HY_D])
        win = win_ref[s * half:(s + 1) * half, :]
        filt = (kk * jnp.concatenate([win, win], axis=-1)).astype(BF16)
        for j in range(per_half):
            o_ref[s * per_half + j] = jnp.dot(gaf_ref[s * per_half + j],
                                              filt[j * FFT_N1:(j + 1) * FFT_N1],
                                              preferred_element_type=F32)


def _filt_call(z_pack, fw, win_perm, gaf):
    tp = 2 * FILT_HALF
    n = win_perm.shape[0]
    to = tp // FFT_N1
    M = gaf.shape[1]

    def full(a):
        return pl.BlockSpec(a.shape, lambda i: (0,) * a.ndim)

    return pl.pallas_call(
        _filt_kernel, grid=(n // tp,),
        in_specs=[pl.BlockSpec((FILT_HALF, z_pack.shape[1]), lambda i: (i, 0))]
        + [full(a) for a in fw] + [pl.BlockSpec((tp, HY_D), lambda i: (i, 0)),
                                   pl.BlockSpec((to, M, FFT_N1), lambda i: (i, 0, 0))],
        out_specs=pl.BlockSpec((to, M, 2 * HY_D), lambda i: (i, 0, 0)),
        out_shape=jax.ShapeDtypeStruct((n // FFT_N1, M, 2 * HY_D), F32),
        compiler_params=_cparams(("parallel",)), name="hy_filter",
    )(z_pack, *fw, win_perm, gaf)


def _bm_kernel(g_ref, x_ref, o_ref, *, to):
    for t in range(to):
        o_ref[t] = jnp.dot(g_ref[t], x_ref[t].astype(BF16), preferred_element_type=F32)


def _bm_call(g, x, name, to=16):
    O, K, N = x.shape
    M = g.shape[1]
    return pl.pallas_call(
        functools.partial(_bm_kernel, to=to), grid=(O // to,),
        in_specs=[pl.BlockSpec((to, M, K), lambda i: (i, 0, 0)),
                  pl.BlockSpec((to, K, N), lambda i: (i, 0, 0))],
        out_specs=pl.BlockSpec((to, M, N), lambda i: (i, 0, 0)),
        out_shape=jax.ShapeDtypeStruct((O, M, N), F32),
        compiler_params=_cparams(("parallel",)), name=name,
    )(g, x)


def _gather_ri(x_ref, j):
    return jnp.concatenate([x_ref[:, 0, j, :], x_ref[:, 1, j, :]], axis=0).astype(BF16)


def _gather_spec(a, to):
    return pl.BlockSpec((a.shape[0], 2, to, a.shape[3]), lambda i: (0, 0, i, 0))


def _filtb_kernel(mb_ref, x_ref, o_ref, *, to):
    for j in range(to):
        o_ref[j] = jnp.dot(mb_ref[...], _gather_ri(x_ref, j),
                           preferred_element_type=F32).astype(o_ref.dtype)


def _filtb_call(mb, ka, to=16):
    O, _, P, N = ka.shape
    return pl.pallas_call(
        functools.partial(_filtb_kernel, to=to), grid=(P // to,),
        in_specs=[pl.BlockSpec(mb.shape, lambda i: (0, 0)), _gather_spec(ka, to)],
        out_specs=pl.BlockSpec((to, 2 * O, N), lambda i: (i, 0, 0)),
        out_shape=jax.ShapeDtypeStruct((P, 2 * O, N), BF16),
        compiler_params=_cparams(("parallel",)), name="hy_fft_filt_b",
    )(mb, ka)


def _convb_kernel(mb_ref, gc_ref, x_ref, kf_ref, o_ref, *, to):
    h = FFT_N2
    for j in range(to):
        xs = jnp.dot(mb_ref[...], _gather_ri(x_ref, j), preferred_element_type=F32)
        xr, xi = xs[:h], xs[h:]
        kr, ki = kf_ref[j, :h, :].astype(F32), kf_ref[j, h:, :].astype(F32)
        ys = jnp.concatenate([xr * kr - xi * ki, xr * ki + xi * kr], axis=0).astype(BF16)
        o_ref[j] = jnp.dot(gc_ref[j], ys, preferred_element_type=F32)


def _convb_call(mb, gc, a, kf, order, to=16):
    O, _, P, N = a.shape
    return pl.pallas_call(
        functools.partial(_convb_kernel, to=to), grid=(P // to,),
        in_specs=[pl.BlockSpec(mb.shape, lambda i: (0, 0)),
                  pl.BlockSpec((to, 2 * O, 2 * O), lambda i: (i, 0, 0)),
                  _gather_spec(a, to),
                  pl.BlockSpec((to, 2 * O, N), lambda i: (i, 0, order))],
        out_specs=pl.BlockSpec((to, 2 * O, N), lambda i: (i, 0, 0)),
        out_shape=jax.ShapeDtypeStruct((P, 2 * O, N), F32),
        compiler_params=_cparams(("parallel",)), name="hy_spec_mul",
    )(mb, gc, a, kf)


def _convd_kernel(md_ref, c_ref, g_ref, u_ref, s_ref, *rest, to, chain):
    if chain:
        ga_ref, z_ref, a_ref = rest
    else:
        (o_ref,) = rest
    for j in range(to):
        y = jnp.dot(md_ref[...], _gather_ri(c_ref, j), preferred_element_type=F32)
        z = g_ref[j] * (y + u_ref[j] * s_ref[...])
        if chain:
            z_ref[j] = z
            a_ref[j] = jnp.dot(ga_ref[j], z.astype(BF16), preferred_element_type=F32)
        else:
            o_ref[:, j, :] = z


def _convd_call(md, c, gate, u, skip, ga=None, to=16):
    O, _, P, N = c.shape
    R = md.shape[0]
    tspec = pl.BlockSpec((to, R, N), lambda i: (i, 0, 0))
    in_specs = [pl.BlockSpec(md.shape, lambda i: (0, 0)), _gather_spec(c, to), tspec, tspec,
                pl.BlockSpec((1, N), lambda i: (0, 0))]
    args = [md, c, gate, u, skip.reshape(1, N)]
    if ga is not None:
        M = ga.shape[1]
        in_specs.append(pl.BlockSpec((to, M, R), lambda i: (i, 0, 0)))
        args.append(ga)
        out_specs = [tspec, pl.BlockSpec((to, M, N), lambda i: (i, 0, 0))]
        out_shape = [jax.ShapeDtypeStruct((P, R, N), F32), jax.ShapeDtypeStruct((P, M, N), F32)]
    else:
        out_specs = pl.BlockSpec((R, to, N), lambda i: (0, i, 0))
        out_shape = jax.ShapeDtypeStruct((R, P, N), F32)
    return pl.pallas_call(
        functools.partial(_convd_kernel, to=to, chain=ga is not None), grid=(P // to,),
        in_specs=in_specs, out_specs=out_specs, out_shape=out_shape,
        compiler_params=_cparams(("parallel",)), name="hy_fft_d",
    )(*args)


def _dft_tables():
    n = FFT_N1 * FFT_N2
    k = np.arange(FFT_N1)
    f = np.exp(-2j * np.pi * np.outer(k, k) / FFT_N1)
    t = np.exp(-2j * np.pi * np.outer(k, k) / n)
    return f, t, n


def _dft_matrices():
    f, t, n = _dft_tables()
    fr, fi = jnp.asarray(f.real, F32), jnp.asarray(f.imag, F32)
    tr, ti = jnp.asarray(t.real, F32), jnp.asarray(t.imag, F32)
    half = FFT_N1 // 2
    er = fr[None] * tr[:, :, None] - fi[None] * ti[:, :, None]
    ei = fr[None] * ti[:, :, None] + fi[None] * tr[:, :, None]
    ga = jnp.concatenate([jnp.concatenate([er[:, :, :half], -ei[:, :, :half]], axis=2),
                          jnp.concatenate([ei[:, :, :half], er[:, :, :half]], axis=2)], axis=1)
    gaf = jnp.concatenate([er, ei], axis=1)
    mb = jnp.concatenate([jnp.concatenate([fr, -fi], axis=1),
                          jnp.concatenate([fi, fr], axis=1)], axis=0)
    tct = jnp.transpose(tr)[:, :, None]
    tst = -jnp.transpose(ti)[:, :, None]
    gr = tct * fr[None] - tst * (-fi[None])
    gi = tct * (-fi[None]) + tst * fr[None]
    gc = jnp.concatenate([jnp.concatenate([gr, -gi], axis=2),
                          jnp.concatenate([gi, gr], axis=2)], axis=1)
    hr, hi = fr[:half] / n, -fi[:half] / n
    md = jnp.concatenate([jnp.concatenate([hr, -hi], axis=1),
                          jnp.concatenate([hi, hr], axis=1)], axis=0)
    return (ga.astype(BF16), gaf.astype(BF16), mb.astype(BF16), gc.astype(BF16), md.astype(BF16))


def _hyena_positions(L):
    p = FFT_N2 * np.arange(FFT_N1)[None, :] + np.arange(FFT_N2)[:, None]
    pos = np.where(p < L, p, 2 * L - 1 - p).reshape(2 * L, 1).astype(np.float64)
    t = pos / (L - 1)
    w = 2.0 * math.pi * pos / L
    f = np.linspace(1e-4, HY_BANDS - 1, HY_BANDS)[None, :]
    z = np.concatenate([t, np.cos(f * w), -np.sin(f * w),
                        np.zeros((2 * L, FILT_FEAT - HY_EMB))], axis=-1)
    z_pack = (z.reshape(-1, 2, FILT_HALF, FILT_FEAT).transpose(0, 2, 1, 3)
              .reshape(-1, 2 * FILT_FEAT))
    max_decay = math.log(HY_DECAY_TARGET) / HY_FAST_DECAY_PCT
    min_decay = math.log(HY_DECAY_TARGET) / HY_SLOW_DECAY_PCT
    deltas = jnp.linspace(min_decay, max_decay, HY_D, dtype=F32)
    window = jnp.exp(-jnp.asarray(t, F32) * jnp.abs(deltas)[None, :])
    return jnp.asarray(z_pack, F32), window


def _filter_weights(w1, b1, f1, w2, b2, f2, w3):
    def bd(w):
        z = jnp.zeros_like(w)
        return jnp.concatenate([jnp.concatenate([w, z], axis=1),
                                jnp.concatenate([z, w], axis=1)], axis=0)

    def twice(v):
        return jnp.concatenate([v, v])[None]

    w1 = jnp.pad(w1, ((0, FILT_FEAT - w1.shape[0]), (0, 0)))
    w3 = w3.reshape(-1, 2, 2, HY_D).transpose(0, 2, 1, 3).reshape(-1, 4 * HY_D)
    return (*_split_bf16(bd(w1)), twice(b1), twice(f1), *_split_bf16(bd(w2)), twice(b2), twice(f2),
            *_split_bf16(bd(w3)))


def _hyena_layer(hy_in, conv_w, conv_b, fw, skip, z_perm, win_perm, mats):
    B, L, _ = hy_in.shape
    ga, gaf, mb, gc, md = mats
    v, x1, x2 = _hconv_call(hy_in, conv_w, conv_b[None])
    ka = _filt_call(z_perm, fw, win_perm, gaf)
    kf = _filtb_call(mb, ka.reshape(FFT_N2, 2, FFT_N1, 2 * HY_D))

    a = _bm_call(ga, v, "hy_fft_a")
    c = _convb_call(mb, gc, a.reshape(FFT_N2, 2, FFT_N1, HY_D), kf, 0)
    z, a = _convd_call(md, c.reshape(FFT_N1, 2, FFT_N2, HY_D), x1, v, skip[0], ga=ga)
    c = _convb_call(mb, gc, a.reshape(FFT_N2, 2, FFT_N1, HY_D), kf, 1)
    z = _convd_call(md, c.reshape(FFT_N1, 2, FFT_N2, HY_D), x2, z, skip[1])
    return z.reshape(B, L, HY_D)


def _kout_kernel(x_ref, yh_ref, yg_ref, ym_ref, gh_ref, gg_ref, gm_ref, w_ref, gpost_ref, gffn_ref,
                 xo_ref, h_ref):
    a = _rms(yh_ref[0], gh_ref[...]).astype(BF16)
    b = _rms(yg_ref[0].astype(F32), gg_ref[...]).astype(BF16)
    c = _rms(ym_ref[0].astype(F32), gm_ref[...]).astype(BF16)
    o1 = HY_D
    o2 = o1 + GQA_HEADS * GQA_HEAD_DIM
    y = (jnp.dot(a, w_ref[:o1, :], preferred_element_type=F32)
         + jnp.dot(b, w_ref[o1:o2, :], preferred_element_type=F32)
         + jnp.dot(c, w_ref[o2:, :], preferred_element_type=F32))
    xo = x_ref[0] + _rms(y, gpost_ref[...])
    xo_ref[0] = xo
    h_ref[0] = _rms(xo, gffn_ref[...]).astype(BF16)


def _kout_call(x, yh, yg, ym, gh, gg, gm, w_p, gpost, gffn):
    B, L, D = x.shape
    tm = TM_OUT

    def rows(a):
        return pl.BlockSpec((1, tm, a.shape[2]), lambda b, i: (b, i, 0))

    def full(a):
        return pl.BlockSpec(a.shape, lambda b, i: (0,) * a.ndim)

    return pl.pallas_call(
        _kout_kernel, grid=(B, L // tm),
        in_specs=[rows(x), rows(yh), rows(yg), rows(ym), full(gh), full(gg), full(gm), full(w_p),
                  full(gpost), full(gffn)],
        out_specs=[rows(x), rows(x)],
        out_shape=[jax.ShapeDtypeStruct((B, L, D), F32), jax.ShapeDtypeStruct((B, L, D), BF16)],
        compiler_params=_cparams(("parallel", "parallel")), name="out_proj",
    )(x, yh, yg, ym, gh, gg, gm, w_p, gpost, gffn)


HALO = 16


def _ffn_kernel(h_ref, hp_ref, hn_ref, x_ref, wup_ref, cw_ref, cb_ref, wd_ref, gpost_ref,
                o_ref, act_ref):
    i = pl.program_id(1)
    tm = h_ref.shape[1]
    prev = jnp.where(i > 0, hp_ref[0], jnp.zeros_like(hp_ref[0]))
    nxt = jnp.where(i < pl.num_programs(1) - 1, hn_ref[0], jnp.zeros_like(hn_ref[0]))
    he = jnp.concatenate([prev, h_ref[0], nxt], axis=0)
    ext = tm + 2 * HALO
    tf = TF_FFN

    def conv(c0):
        up = jnp.dot(he, wup_ref[:, c0:c0 + tf], preferred_element_type=F32)
        um = pltpu.roll(up, 1, 0)[HALO:HALO + tm]
        upl = pltpu.roll(up, ext - 1, 0)[HALO:HALO + tm]
        return (um * cw_ref[0:1, c0:c0 + tf] + up[HALO:HALO + tm] * cw_ref[1:2, c0:c0 + tf]
                + upl * cw_ref[2:3, c0:c0 + tf] + cb_ref[:, c0:c0 + tf])

    for j in range(D_FF // tf):
        g = conv(j * tf)
        u = conv(D_FF + j * tf)
        gelu = 0.5 * g * (1.0 + jnp.tanh(math.sqrt(2.0 / math.pi) * (g + 0.044715 * (g * g * g))))
        act_ref[:, j * tf:(j + 1) * tf] = (gelu * u).astype(BF16)
    f = jnp.dot(act_ref[...], wd_ref[...], preferred_element_type=F32)
    o_ref[0] = x_ref[0] + _rms(f, gpost_ref[...])


def _ffn_call(h, x, w_up, cw, cb, w_down, gpost):
    B, L, D = x.shape
    tm = TM_FFN
    nb = tm // HALO
    last = L // HALO - 1

    def resident(a):
        return pl.BlockSpec(a.shape, lambda b, i: (0,) * a.ndim, pipeline_mode=pl.Buffered(1))

    cb = cb[None]
    return pl.pallas_call(
        _ffn_kernel, grid=(B, L // tm),
        in_specs=[pl.BlockSpec((1, tm, D), lambda b, i: (b, i, 0)),
                  pl.BlockSpec((1, HALO, D), lambda b, i: (b, jnp.maximum(i * nb - 1, 0), 0)),
                  pl.BlockSpec((1, HALO, D), lambda b, i: (b, jnp.minimum((i + 1) * nb, last), 0)),
                  pl.BlockSpec((1, tm, D), lambda b, i: (b, i, 0)),
                  resident(w_up), resident(cw), resident(cb), resident(w_down), resident(gpost)],
        out_specs=pl.BlockSpec((1, tm, D), lambda b, i: (b, i, 0)),
        out_shape=jax.ShapeDtypeStruct((B, L, D), F32),
        scratch_shapes=[pltpu.VMEM((tm, D_FF), BF16)],
        compiler_params=_cparams(("parallel", "parallel")), name="conv_ffn",
    )(h, h, h, x, w_up, cw, cb, w_down, gpost)


def _axial_tables(L, rot_dim):
    pos = np.arange(L)
    n_axis = rot_dim // 4
    inv = ROPE_THETA ** (-np.arange(n_axis) / n_axis)
    ang = np.concatenate([(pos // GRID_W)[:, None] * inv, (pos % GRID_W)[:, None] * inv], axis=-1)
    return jnp.asarray(np.cos(ang), F32), jnp.asarray(np.sin(ang), F32)


def _rope_tables(L):
    def lanes(parts):
        used = sum(p.shape[1] for p in parts)
        return jnp.concatenate(parts + [jnp.zeros((L, LANES - used), F32)], axis=1)

    cg, sg = _axial_tables(L, GQA_HEAD_DIM)
    tg = (lanes([cg, cg, cg, cg]), lanes([sg, sg, sg, sg]))
    cm, sm = _axial_tables(L, MLA_ROPE_DIM)
    nope0 = jnp.zeros((L, MLA_NOPE_DIM), F32)
    nope1 = jnp.ones((L, MLA_NOPE_DIM), F32)
    tmk = (lanes([nope0, cm, cm]), lanes([nope0, sm, sm]))
    sc = (MLA_NOPE_DIM + MLA_ROPE_DIM) ** -0.5 * LOG2E
    tmq = (lanes([nope1, cm, cm]) * sc, tmk[1] * sc)
    return tg, tmq, tmk


def _partner(w, half, sign=-1.0):
    return jnp.concatenate([sign * w[..., half:], w[..., :half]], axis=-1)


def _gain_pair(g, half):
    return jnp.tile(g, 2)[None], jnp.tile(_partner(g, half, 1.0), 2)[None]


def _pad_heads(w, n_heads, width):
    k = w.shape[0]
    return jnp.pad(w.reshape(k, n_heads, width), ((0, 0), (0, 0), (0, LANES - width))).reshape(
        k, n_heads * LANES)


def kernel(x, mix_pre_norm, w_in, hy_conv_w, hy_conv_b, hy_filt_w1, hy_filt_b1, hy_filt_freq1,
           hy_filt_w2, hy_filt_b2, hy_filt_freq2, hy_filt_w3, hy_skip, gqa_q_norm, gqa_k_norm,
           mla_q_a_norm, mla_w_uq, mla_kv_a_norm, mla_w_ukv, hy_out_norm, gqa_out_norm,
           mla_out_norm, w_out, mix_post_norm, ffn_pre_norm, w_up, ffn_conv_w, ffn_conv_b,
           w_down, ffn_post_norm):
    B, L, D = x.shape
    assert B == 2 and 2 * L == FFT_N1 * FFT_N2 and D == D_MODEL
    depth = w_in.shape[0]
    tg, tmq, tmk = _rope_tables(L)
    z_perm, win_perm = _hyena_positions(L)
    mats = _dft_matrices()

    for l in range(depth):
        wl = w_in[l]
        hd, hh, rh = GQA_HEAD_DIM, GQA_HEAD_DIM // 2, MLA_ROPE_DIM // 2
        o1 = HY_COLS
        o2 = o1 + GQA_HEADS * hd
        o3 = o2 + GQA_KV_HEADS * hd
        o4 = o3 + GQA_KV_HEADS * hd
        o5 = o4 + MLA_Q_RANK
        o6 = o5 + MLA_KV_RANK
        wq = wl[:, o1:o2].reshape(D, GQA_HEADS, hd)
        wk = wl[:, o2:o3].reshape(D, GQA_KV_HEADS, hd)
        wkr = wl[:, o6:]
        pe_pad = ((0, 0), (MLA_NOPE_DIM, LANES - MLA_NOPE_DIM - MLA_ROPE_DIM))
        win_p = jnp.concatenate(
            [wl[:, :o1],
             wq.reshape(D, -1), _partner(wq, hh).reshape(D, -1),
             wk.reshape(D, -1), _partner(wk, hh).reshape(D, -1),
             wl[:, o3:o6], jnp.pad(wkr, pe_pad), jnp.pad(_partner(wkr, rh), pe_pad)],
            axis=1).astype(BF16)
        gq = _gain_pair(gqa_q_norm[l] * (hd ** -0.5 * LOG2E), hh)
        gk = _gain_pair(gqa_k_norm[l], hh)
        wuq = mla_w_uq[l].reshape(MLA_Q_RANK, MLA_HEADS, MLA_NOPE_DIM + MLA_ROPE_DIM)
        wuq_pe = jnp.pad(_partner(wuq[:, :, MLA_NOPE_DIM:], rh),
                         ((0, 0), (0, 0), (MLA_NOPE_DIM, 0)))
        wuq_p = jnp.concatenate(
            [_pad_heads(mla_w_uq[l], MLA_HEADS, MLA_NOPE_DIM + MLA_ROPE_DIM),
             _pad_heads(wuq_pe.reshape(MLA_Q_RANK, -1), MLA_HEADS, MLA_NOPE_DIM + MLA_ROPE_DIM)],
            axis=1).astype(BF16)
        wukv = mla_w_ukv[l].reshape(MLA_KV_RANK, MLA_HEADS, MLA_NOPE_DIM + MLA_V_DIM)
        wukvk_p = _pad_heads(wukv[:, :, :MLA_NOPE_DIM].reshape(MLA_KV_RANK, -1), MLA_HEADS,
                             MLA_NOPE_DIM).astype(BF16)
        wukvv = wukv[:, :, MLA_NOPE_DIM:].reshape(MLA_KV_RANK, MLA_HEADS * MLA_V_DIM).astype(BF16)

        hy_in, qg, kg, vg, qm, km, vm = _kin_call(
            x, mix_pre_norm[l][None], win_p, gq, gk, tg, mla_q_a_norm[l][None], wuq_p, tmq,
            mla_kv_a_norm[l][None], wukvk_p, wukvv, tmk)

        fw = _filter_weights(hy_filt_w1[l], hy_filt_b1[l], hy_filt_freq1[l], hy_filt_w2[l],
                             hy_filt_b2[l], hy_filt_freq2[l], hy_filt_w3[l])
        y_hy = _hyena_layer(hy_in, hy_conv_w[l], hy_conv_b[l], fw, hy_skip[l], z_perm, win_perm, mats)

        y_gqa = _attn_call(qg.transpose(0, 2, 1), kg, vg.reshape(B, GQA_KV_HEADS, GQA_HEAD_DIM, L),
                           n_kv=1, n_rep=GQA_HEADS // GQA_KV_HEADS, pack=2, tq=TQ_GQA,
                           name="attn_gqa")
        y_mla = _attn_call(qm.transpose(0, 2, 1), km, vm.reshape(B, MLA_HEADS, MLA_V_DIM, L),
                           n_kv=2, n_rep=1, pack=1, tq=TQ_MLA, name="attn_mla")

        x, h2 = _kout_call(x, y_hy, y_gqa, y_mla, hy_out_norm[l][None], gqa_out_norm[l][None],
                           mla_out_norm[l][None], w_out[l].astype(BF16),
                           mix_post_norm[l][None], ffn_pre_norm[l][None])
        x = _ffn_call(h2, x, w_up[l].astype(BF16), ffn_conv_w[l], ffn_conv_b[l],
                      w_down[l].astype(BF16), ffn_post_norm[l][None])
    return x
```

```python
import functools
import math

import numpy as np
import jax
import jax.numpy as jnp
from jax import lax
from jax.experimental import pallas as pl
from jax.experimental.pallas import tpu as pltpu

F32 = jnp.float32
BF16 = jnp.bfloat16

NORM_EPS = 1e-6
ROPE_THETA = 10000.0
GRID_W = 64
LOG2E = math.log2(math.e)

D_MODEL = 1024
HY_D = 256
HY_COLS = 3 * HY_D
HY_EMB = 33
HY_BANDS = 16
HY_DECAY_TARGET = 1e-2
HY_FAST_DECAY_PCT = 0.3
HY_SLOW_DECAY_PCT = 1.5
GQA_HEADS = 8
GQA_KV_HEADS = 2
GQA_HEAD_DIM = 64
MLA_HEADS = 4
MLA_Q_RANK = 256
MLA_KV_RANK = 128
MLA_NOPE_DIM = 64
MLA_ROPE_DIM = 32
MLA_V_DIM = 64
D_FF = 2816

LANES = 128
FFT_N1 = 128
FFT_N2 = 128

TM_IN = 1024
TK_ATTN = 128
TQ_GQA = 128
TQ_MLA = 256
TM_OUT = 1024
TM_FFN = 512
TF_FFN = 256
TL_HCONV = 1024
FILT_HALF = 512
FILT_FEAT = 64
VMEM_LIMIT = 56 * 1024 * 1024


def _cparams(sem):
    return pltpu.CompilerParams(dimension_semantics=sem, vmem_limit_bytes=VMEM_LIMIT)


def _rms(x, g):
    return x * lax.rsqrt(jnp.mean(x * x, axis=-1, keepdims=True) + NORM_EPS) * g


def _kin_kernel(x_ref, gpre_ref, win_ref, gqn_ref, gqs_ref, gkn_ref, gks_ref, cg_ref, sg_ref,
                mqn_ref, wuq_ref, cmq_ref, smq_ref,
                mkvn_ref, wukvk_ref, wukvv_ref, cmk_ref, smk_ref,
                hy_ref, qg_ref, kg_ref, vg_ref, qm_ref, km_ref, vm_ref):
    x = x_ref[0]
    h = _rms(x, gpre_ref[...]).astype(BF16)
    cur = [0]

    def proj(n):
        lo = cur[0]
        cur[0] = lo + n
        return jnp.dot(h, win_ref[:, lo:lo + n], preferred_element_type=F32)

    hy_ref[0] = proj(HY_COLS)

    cg, sg = cg_ref[...], sg_ref[...]
    low = lax.broadcasted_iota(jnp.int32, (x.shape[0], LANES), 1) < GQA_HEAD_DIM

    def head_pairs(n_tiles, gain, gain_sw):
        xa, xb = proj(n_tiles * LANES), proj(n_tiles * LANES)
        ca, sa = cg * gain, sg * gain_sw
        out = []
        for t in range(n_tiles):
            xc, xs = xa[:, t * LANES:(t + 1) * LANES], xb[:, t * LANES:(t + 1) * LANES]
            sq = xc * xc
            tot = jnp.sum(sq, axis=-1, keepdims=True)
            lo = jnp.sum(jnp.where(low, sq, 0.0), axis=-1, keepdims=True)
            ms = jnp.where(low, lo, tot - lo) * (1.0 / GQA_HEAD_DIM)
            out.append((xc * ca + xs * sa) * lax.rsqrt(ms + NORM_EPS))
        return out

    for t, q in enumerate(head_pairs(GQA_HEADS // 2, gqn_ref[...], gqs_ref[...])):
        qg_ref[0, :, t * LANES:(t + 1) * LANES] = q.astype(BF16)
    (kk,) = head_pairs(GQA_KV_HEADS // 2, gkn_ref[...], gks_ref[...])
    ksw = pltpu.roll(kk, GQA_HEAD_DIM, 1)
    kg_ref[0, 0] = jnp.where(low, kk, ksw).astype(BF16)
    kg_ref[0, 1] = jnp.where(low, ksw, kk).astype(BF16)
    vg_ref[0] = proj(LANES).T.astype(BF16)

    cq = _rms(proj(MLA_Q_RANK), mqn_ref[...]).astype(BF16)
    qm = jnp.dot(cq, wuq_ref[...], preferred_element_type=F32)
    cmq, smq = cmq_ref[...], smq_ref[...]
    nq = MLA_HEADS * LANES
    for j in range(MLA_HEADS):
        qm_ref[0, :, j * LANES:(j + 1) * LANES] = (
            qm[:, j * LANES:(j + 1) * LANES] * cmq
            + qm[:, nq + j * LANES:nq + (j + 1) * LANES] * smq).astype(BF16)

    ckv = _rms(proj(MLA_KV_RANK), mkvn_ref[...]).astype(BF16)
    kpe = proj(LANES) * cmk_ref[...] + proj(LANES) * smk_ref[...]
    kn = jnp.dot(ckv, wukvk_ref[...], preferred_element_type=F32)
    for j in range(MLA_HEADS):
        km_ref[0, j] = (kn[:, j * LANES:(j + 1) * LANES] + kpe).astype(BF16)
    vm = jnp.dot(ckv, wukvv_ref[...], preferred_element_type=F32)
    for j in range(vm.shape[1] // LANES):
        vm_ref[0, j * LANES:(j + 1) * LANES, :] = vm[:, j * LANES:(j + 1) * LANES].T.astype(BF16)


def _kin_call(x, gpre, win_p, gq, gk, tg, mqn, wuq_p, tmq, mkvn, wukvk_p, wukvv, tmk):
    B, L, D = x.shape
    tm = TM_IN
    nt = L // tm

    def full(a):
        return pl.BlockSpec(a.shape, lambda b, i: (0,) * a.ndim, pipeline_mode=pl.Buffered(1))

    def rows(w):
        return pl.BlockSpec((tm, w), lambda b, i: (i, 0))

    in_specs = [pl.BlockSpec((1, tm, D), lambda b, i: (b, i, 0)), full(gpre), full(win_p),
                full(gq[0]), full(gq[1]), full(gk[0]), full(gk[1]), rows(LANES), rows(LANES),
                full(mqn), full(wuq_p), rows(LANES), rows(LANES),
                full(mkvn), full(wukvk_p), full(wukvv), rows(LANES), rows(LANES)]
    gv_rows = GQA_KV_HEADS * GQA_HEAD_DIM
    mv_rows = MLA_HEADS * MLA_V_DIM
    out_shape = [
        jax.ShapeDtypeStruct((B, L, HY_COLS), F32),
        jax.ShapeDtypeStruct((B, L, GQA_HEADS * GQA_HEAD_DIM), BF16),
        jax.ShapeDtypeStruct((B, GQA_KV_HEADS, L, LANES), BF16),
        jax.ShapeDtypeStruct((B, gv_rows, L), BF16),
        jax.ShapeDtypeStruct((B, L, MLA_HEADS * LANES), BF16),
        jax.ShapeDtypeStruct((B, MLA_HEADS, L, LANES), BF16),
        jax.ShapeDtypeStruct((B, mv_rows, L), BF16),
    ]
    out_specs = [
        pl.BlockSpec((1, tm, HY_COLS), lambda b, i: (b, i, 0)),
        pl.BlockSpec((1, tm, GQA_HEADS * GQA_HEAD_DIM), lambda b, i: (b, i, 0)),
        pl.BlockSpec((1, GQA_KV_HEADS, tm, LANES), lambda b, i: (b, 0, i, 0)),
        pl.BlockSpec((1, gv_rows, tm), lambda b, i: (b, 0, i)),
        pl.BlockSpec((1, tm, MLA_HEADS * LANES), lambda b, i: (b, i, 0)),
        pl.BlockSpec((1, MLA_HEADS, tm, LANES), lambda b, i: (b, 0, i, 0)),
        pl.BlockSpec((1, mv_rows, tm), lambda b, i: (b, 0, i)),
    ]
    return pl.pallas_call(
        _kin_kernel, grid=(B, nt), in_specs=in_specs, out_specs=out_specs, out_shape=out_shape,
        compiler_params=_cparams(("parallel", "parallel")), name="in_proj",
    )(x, gpre, win_p, *gq, *gk, *tg, mqn, wuq_p, *tmq, mkvn, wukvk_p, wukvv, *tmk)


def _attn_kernel(q_ref, k_ref, vt_ref, o_ref, sa_ref, sb_ref, ma_ref, mb_ref, *,
                 n_kv, n_rep, pack, tq, n_chunks, tk):
    i = pl.program_id(0)
    cols = n_rep * tq
    width = n_kv * cols
    grp = tk // 8

    @pl.when(i == 0)
    def _():
        sb_ref[...] = jnp.zeros(sb_ref.shape, F32)
        mb_ref[...] = jnp.zeros(mb_ref.shape, F32)

    def step(sw_ref, mw_ref, sr_ref, mr_ref):
        def q_head(h):
            tile, half = divmod(h, pack)
            qt = q_ref[0, tile * LANES:(tile + 1) * LANES, :]
            if pack == 2:
                z = jnp.zeros((LANES // 2, qt.shape[1]), qt.dtype)
                qt = jnp.concatenate([qt[:LANES // 2], z] if half == 0 else [z, qt[LANES // 2:]],
                                     axis=0)
            return qt

        qs = [jnp.concatenate([q_head(a * n_rep + j) for j in range(n_rep)], axis=1)
              for a in range(n_kv)]
        mx = jnp.max(mr_ref[...], axis=0, keepdims=True)
        m = jnp.full((8, width), -jnp.inf, F32)
        l = jnp.zeros((8, width), F32)
        accs = [jnp.zeros((vt_ref.shape[2], cols), F32) for _ in range(n_kv)]
        for c in range(n_chunks):
            st = jnp.concatenate(
                [jnp.dot(k_ref[0, a, c * tk:(c + 1) * tk, :], qs[a], preferred_element_type=F32)
                 for a in range(n_kv)], axis=1)
            sw_ref[c] = st
            m = jnp.maximum(m, jnp.max(st.reshape(grp, 8, width), axis=0))
            p = jnp.exp2(sr_ref[c] - mx)
            l = l + jnp.sum(p.reshape(grp, 8, width), axis=0)
            pb = p.astype(BF16)
            for a in range(n_kv):
                accs[a] = accs[a] + jnp.dot(vt_ref[0, a, :, c * tk:(c + 1) * tk],
                                            pb[:, a * cols:(a + 1) * cols],
                                            preferred_element_type=F32)
        mw_ref[...] = m
        ls = jnp.sum(l, axis=0, keepdims=True)
        heads = []
        for a in range(n_kv):
            oa = accs[a] / ls[:, a * cols:(a + 1) * cols]
            heads += [oa[:, j * tq:(j + 1) * tq] for j in range(n_rep)]
        o_ref[0] = jnp.concatenate(heads, axis=0).T.astype(BF16)

    @pl.when(i % 2 == 0)
    def _():
        step(sa_ref, ma_ref, sb_ref, mb_ref)

    @pl.when(i % 2 == 1)
    def _():
        step(sb_ref, mb_ref, sa_ref, ma_ref)


def _attn_call(qt, k, vt, *, n_kv, n_rep, pack, tq, name):
    B, hq, L = qt.shape
    H = hq * pack // LANES
    hkv, dv = vt.shape[1], vt.shape[2]
    G = hkv // n_kv
    hs = n_kv * n_rep
    tk = TK_ATTN
    n_chunks = L // tk
    nq = L // tq
    width = hs * tq
    kern = functools.partial(_attn_kernel, n_kv=n_kv, n_rep=n_rep, pack=pack, tq=tq,
                             n_chunks=n_chunks, tk=tk)
    total = B * G * nq

    def blk(s):
        return s // (G * nq), (s // nq) % G, s % nq

    def q_map(s):
        b, g, i = blk(jnp.minimum(s, total - 1))
        return b, g, i

    def k_map(s):
        b, g, _ = blk(jnp.minimum(s, total - 1))
        return b, g, 0, 0

    def v_map(s):
        b, g, _ = blk(jnp.maximum(s - 1, 0))
        return b, g, 0, 0

    def o_map(s):
        b, g, i = blk(jnp.maximum(s - 1, 0))
        return b, i, g

    return pl.pallas_call(
        kern, grid=(total + 1,),
        in_specs=[pl.BlockSpec((1, hs * LANES // pack, tq), q_map),
                  pl.BlockSpec((1, n_kv, L, LANES), k_map, pipeline_mode=pl.Buffered(1)),
                  pl.BlockSpec((1, n_kv, dv, L), v_map, pipeline_mode=pl.Buffered(1))],
        out_specs=pl.BlockSpec((1, tq, hs * dv), o_map),
        out_shape=jax.ShapeDtypeStruct((B, L, H * dv), BF16),
        scratch_shapes=[pltpu.VMEM((n_chunks, tk, width), F32), pltpu.VMEM((n_chunks, tk, width), F32),
                        pltpu.VMEM((8, width), F32), pltpu.VMEM((8, width), F32)],
        compiler_params=_cparams(("arbitrary",)), name=name,
    )(qt, k, vt)


def _hconv_kernel(x_ref, xp_ref, xn_ref, w_ref, b_ref, v_ref, x1_ref, x2_ref):
    i = pl.program_id(1)
    x = x_ref[0]
    tl = x.shape[0]
    prev = jnp.where(i > 0, xp_ref[0][7:8, :], 0.0)
    nxt = jnp.where(i < pl.num_programs(1) - 1, xn_ref[0][0:1, :], 0.0)
    r = lax.broadcasted_iota(jnp.int32, x.shape, 0)
    xm = jnp.where(r == 0, prev, pltpu.roll(x, 1, 0))
    xp = jnp.where(r == tl - 1, nxt, pltpu.roll(x, tl - 1, 0))
    uc = xm * w_ref[0:1, :] + x * w_ref[1:2, :] + xp * w_ref[2:3, :] + b_ref[...]
    for r in range(tl // FFT_N2):
        blk = uc[r * FFT_N2:(r + 1) * FFT_N2]
        v_ref[:, r, :] = blk[:, :HY_D]
        x1_ref[:, r, :] = blk[:, HY_D:2 * HY_D]
        x2_ref[:, r, :] = blk[:, 2 * HY_D:]


def _hconv_call(hy_in, w, b):
    B, L, C = hy_in.shape
    tl = TL_HCONV
    nb = tl // 8
    last = L // 8 - 1
    nt = L // tl
    rows = tl // FFT_N2
    out = jax.ShapeDtypeStruct((FFT_N2, B * L // FFT_N2, HY_D), F32)
    ospec = pl.BlockSpec((FFT_N2, rows, HY_D), lambda b_, i: (0, b_ * nt + i, 0))
    return pl.pallas_call(
        _hconv_kernel, grid=(B, L // tl),
        in_specs=[pl.BlockSpec((1, tl, C), lambda b_, i: (b_, i, 0)),
                  pl.BlockSpec((1, 8, C), lambda b_, i: (b_, jnp.maximum(i * nb - 1, 0), 0)),
                  pl.BlockSpec((1, 8, C), lambda b_, i: (b_, jnp.minimum((i + 1) * nb, last), 0)),
                  pl.BlockSpec((3, C), lambda b_, i: (0, 0)),
                  pl.BlockSpec((1, C), lambda b_, i: (0, 0))],
        out_specs=[ospec, ospec, ospec], out_shape=[out, out, out],
        compiler_params=_cparams(("parallel", "parallel")), name="hy_conv3",
    )(hy_in, hy_in, hy_in, w, b)


def _split_bf16(a):
    hi = a.astype(BF16)
    return hi, (a - hi.astype(F32)).astype(BF16)


def _dot3(a, w_hi, w_lo):
    a_hi, a_lo = _split_bf16(a)
    dot = functools.partial(jnp.dot, preferred_element_type=F32)
    return dot(a_hi, w_hi) + dot(a_lo, w_hi) + dot(a_hi, w_lo)


def _filt_kernel(z_ref, w1h_ref, w1l_ref, b1_ref, f1_ref, w2h_ref, w2l_ref, b2_ref, f2_ref,
                 w3h_ref, w3l_ref, win_ref, gaf_ref, o_ref):
    h = jnp.sin(f1_ref[...] * (_dot3(z_ref[...], w1h_ref[...], w1l_ref[...]) + b1_ref[...]))
    h = jnp.sin(f2_ref[...] * (_dot3(h, w2h_ref[...], w2l_ref[...]) + b2_ref[...]))
    k = _dot3(h, w3h_ref[...], w3l_ref[...])
    half = k.shape[0]
    wcols = 4 * HY_D
    per_half = half // FFT_N1
    n1 = lax.broadcasted_iota(jnp.int32, (half, 2 * HY_D), 0) % FFT_N1
    for s in range(2):
        ks = k[:, s * wcols:(s + 1) * wcols]
        kk = jnp.where(n1 >= FFT_N1 // 2, ks[:, 2 * HY_D:], ks[:, :2 * HY_D])
        win = win_ref[s * half:(s + 1) * half, :]
        filt = (kk * jnp.concatenate([win, win], axis=-1)).astype(BF16)
        for j in range(per_half):
            o_ref[s * per_half + j] = jnp.dot(gaf_ref[s * per_half + j],
                                              filt[j * FFT_N1:(j + 1) * FFT_N1],
                                              preferred_element_type=F32)


def _filt_call(z_pack, fw, win_perm, gaf):
    tp = 2 * FILT_HALF
    n = win_perm.shape[0]
    to = tp // FFT_N1
    M = gaf.shape[1]

    def full(a):
        return pl.BlockSpec(a.shape, lambda i: (0,) * a.ndim)

    return pl.pallas_call(
        _filt_kernel, grid=(n // tp,),
        in_specs=[pl.BlockSpec((FILT_HALF, z_pack.shape[1]), lambda i: (i, 0))]
        + [full(a) for a in fw] + [pl.BlockSpec((tp, HY_D), lambda i: (i, 0)),
                                   pl.BlockSpec((to, M, FFT_N1), lambda i: (i, 0, 0))],
        out_specs=pl.BlockSpec((to, M, 2 * HY_D), lambda i: (i, 0, 0)),
        out_shape=jax.ShapeDtypeStruct((n // FFT_N1, M, 2 * HY_D), F32),
        compiler_params=_cparams(("parallel",)), name="hy_filter",
    )(z_pack, *fw, win_perm, gaf)


def _bm_kernel(g_ref, x_ref, o_ref, *, to):
    for t in range(to):
        o_ref[t] = jnp.dot(g_ref[t], x_ref[t].astype(BF16), preferred_element_type=F32)


def _bm_call(g, x, name, to=16):
    O, K, N = x.shape
    M = g.shape[1]
    return pl.pallas_call(
        functools.partial(_bm_kernel, to=to), grid=(O // to,),
        in_specs=[pl.BlockSpec((to, M, K), lambda i: (i, 0, 0)),
                  pl.BlockSpec((to, K, N), lambda i: (i, 0, 0))],
        out_specs=pl.BlockSpec((to, M, N), lambda i: (i, 0, 0)),
        out_shape=jax.ShapeDtypeStruct((O, M, N), F32),
        compiler_params=_cparams(("parallel",)), name=name,
    )(g, x)


def _gather_ri(x_ref, j):
    return jnp.concatenate([x_ref[:, 0, j, :], x_ref[:, 1, j, :]], axis=0).astype(BF16)


def _gather_spec(a, to):
    return pl.BlockSpec((a.shape[0], 2, to, a.shape[3]), lambda i: (0, 0, i, 0))


def _filtb_kernel(mb_ref, x_ref, o_ref, *, to):
    for j in range(to):
        o_ref[j] = jnp.dot(mb_ref[...], _gather_ri(x_ref, j),
                           preferred_element_type=F32).astype(o_ref.dtype)


def _filtb_call(mb, ka, to=16):
    O, _, P, N = ka.shape
    return pl.pallas_call(
        functools.partial(_filtb_kernel, to=to), grid=(P // to,),
        in_specs=[pl.BlockSpec(mb.shape, lambda i: (0, 0)), _gather_spec(ka, to)],
        out_specs=pl.BlockSpec((to, 2 * O, N), lambda i: (i, 0, 0)),
        out_shape=jax.ShapeDtypeStruct((P, 2 * O, N), BF16),
        compiler_params=_cparams(("parallel",)), name="hy_fft_filt_b",
    )(mb, ka)


def _convb_kernel(mb_ref, gc_ref, x_ref, kf_ref, o_ref, *, to):
    h = FFT_N2
    for j in range(to):
        xs = jnp.dot(mb_ref[...], _gather_ri(x_ref, j), preferred_element_type=F32)
        xr, xi = xs[:h], xs[h:]
        kr, ki = kf_ref[j, :h, :].astype(F32), kf_ref[j, h:, :].astype(F32)
        ys = jnp.concatenate([xr * kr - xi * ki, xr * ki + xi * kr], axis=0).astype(BF16)
        o_ref[j] = jnp.dot(gc_ref[j], ys, preferred_element_type=F32)


def _convb_call(mb, gc, a, kf, order, to=16):
    O, _, P, N = a.shape
    return pl.pallas_call(
        functools.partial(_convb_kernel, to=to), grid=(P // to,),
        in_specs=[pl.BlockSpec(mb.shape, lambda i: (0, 0)),
                  pl.BlockSpec((to, 2 * O, 2 * O), lambda i: (i, 0, 0)),
                  _gather_spec(a, to),
                  pl.BlockSpec((to, 2 * O, N), lambda i: (i, 0, order))],
        out_specs=pl.BlockSpec((to, 2 * O, N), lambda i: (i, 0, 0)),
        out_shape=jax.ShapeDtypeStruct((P, 2 * O, N), F32),
        compiler_params=_cparams(("parallel",)), name="hy_spec_mul",
    )(mb, gc, a, kf)


def _convd_kernel(md_ref, c_ref, g_ref, u_ref, s_ref, *rest, to, chain):
    if chain:
        ga_ref, z_ref, a_ref = rest
    else:
        (o_ref,) = rest
    for j in range(to):
        y = jnp.dot(md_ref[...], _gather_ri(c_ref, j), preferred_element_type=F32)
        z = g_ref[j] * (y + u_ref[j] * s_ref[...])
        if chain:
            z_ref[j] = z
            a_ref[j] = jnp.dot(ga_ref[j], z.astype(BF16), preferred_element_type=F32)
        else:
            o_ref[:, j, :] = z


def _convd_call(md, c, gate, u, skip, ga=None, to=16):
    O, _, P, N = c.shape
    R = md.shape[0]
    tspec = pl.BlockSpec((to, R, N), lambda i: (i, 0, 0))
    in_specs = [pl.BlockSpec(md.shape, lambda i: (0, 0)), _gather_spec(c, to), tspec, tspec,
                pl.BlockSpec((1, N), lambda i: (0, 0))]
    args = [md, c, gate, u, skip.reshape(1, N)]
    if ga is not None:
        M = ga.shape[1]
        in_specs.append(pl.BlockSpec((to, M, R), lambda i: (i, 0, 0)))
        args.append(ga)
        out_specs = [tspec, pl.BlockSpec((to, M, N), lambda i: (i, 0, 0))]
        out_shape = [jax.ShapeDtypeStruct((P, R, N), F32), jax.ShapeDtypeStruct((P, M, N), F32)]
    else:
        out_specs = pl.BlockSpec((R, to, N), lambda i: (0, i, 0))
        out_shape = jax.ShapeDtypeStruct((R, P, N), F32)
    return pl.pallas_call(
        functools.partial(_convd_kernel, to=to, chain=ga is not None), grid=(P // to,),
        in_specs=in_specs, out_specs=out_specs, out_shape=out_shape,
        compiler_params=_cparams(("parallel",)), name="hy_fft_d",
    )(*args)


def _dft_tables():
    n = FFT_N1 * FFT_N2
    k = np.arange(FFT_N1)
    f = np.exp(-2j * np.pi * np.outer(k, k) / FFT_N1)
    t = np.exp(-2j * np.pi * np.outer(k, k) / n)
    return f, t, n


def _dft_matrices():
    f, t, n = _dft_tables()
    fr, fi = jnp.asarray(f.real, F32), jnp.asarray(f.imag, F32)
    tr, ti = jnp.asarray(t.real, F32), jnp.asarray(t.imag, F32)
    half = FFT_N1 // 2
    er = fr[None] * tr[:, :, None] - fi[None] * ti[:, :, None]
    ei = fr[None] * ti[:, :, None] + fi[None] * tr[:, :, None]
    ga = jnp.concatenate([jnp.concatenate([er[:, :, :half], -ei[:, :, :half]], axis=2),
                          jnp.concatenate([ei[:, :, :half], er[:, :, :half]], axis=2)], axis=1)
    gaf = jnp.concatenate([er, ei], axis=1)
    mb = jnp.concatenate([jnp.concatenate([fr, -fi], axis=1),
                          jnp.concatenate([fi, fr], axis=1)], axis=0)
    tct = jnp.transpose(tr)[:, :, None]
    tst = -jnp.transpose(ti)[:, :, None]
    gr = tct * fr[None] - tst * (-fi[None])
    gi = tct * (-fi[None]) + tst * fr[None]
    gc = jnp.concatenate([jnp.concatenate([gr, -gi], axis=2),
                          jnp.concatenate([gi, gr], axis=2)], axis=1)
    hr, hi = fr[:half] / n, -fi[:half] / n
    md = jnp.concatenate([jnp.concatenate([hr, -hi], axis=1),
                          jnp.concatenate([hi, hr], axis=1)], axis=0)
    return (ga.astype(BF16), gaf.astype(BF16), mb.astype(BF16), gc.astype(BF16), md.astype(BF16))


def _hyena_positions(L):
    p = FFT_N2 * np.arange(FFT_N1)[None, :] + np.arange(FFT_N2)[:, None]
    pos = np.where(p < L, p, 2 * L - 1 - p).reshape(2 * L, 1).astype(np.float64)
    t = pos / (L - 1)
    w = 2.0 * math.pi * pos / L
    f = np.linspace(1e-4, HY_BANDS - 1, HY_BANDS)[None, :]
    z = np.concatenate([t, np.cos(f * w), -np.sin(f * w),
                        np.zeros((2 * L, FILT_FEAT - HY_EMB))], axis=-1)
    z_pack = (z.reshape(-1, 2, FILT_HALF, FILT_FEAT).transpose(0, 2, 1, 3)
              .reshape(-1, 2 * FILT_FEAT))
    max_decay = math.log(HY_DECAY_TARGET) / HY_FAST_DECAY_PCT
    min_decay = math.log(HY_DECAY_TARGET) / HY_SLOW_DECAY_PCT
    deltas = jnp.linspace(min_decay, max_decay, HY_D, dtype=F32)
    window = jnp.exp(-jnp.asarray(t, F32) * jnp.abs(deltas)[None, :])
    return jnp.asarray(z_pack, F32), window


def _filter_weights(w1, b1, f1, w2, b2, f2, w3):
    def bd(w):
        z = jnp.zeros_like(w)
        return jnp.concatenate([jnp.concatenate([w, z], axis=1),
                                jnp.concatenate([z, w], axis=1)], axis=0)

    def twice(v):
        return jnp.concatenate([v, v])[None]

    w1 = jnp.pad(w1, ((0, FILT_FEAT - w1.shape[0]), (0, 0)))
    w3 = w3.reshape(-1, 2, 2, HY_D).transpose(0, 2, 1, 3).reshape(-1, 4 * HY_D)
    return (*_split_bf16(bd(w1)), twice(b1), twice(f1), *_split_bf16(bd(w2)), twice(b2), twice(f2),
            *_split_bf16(bd(w3)))


def _hyena_layer(hy_in, conv_w, conv_b, fw, skip, z_perm, win_perm, mats):
    B, L, _ = hy_in.shape
    ga, gaf, mb, gc, md = mats
    v, x1, x2 = _hconv_call(hy_in, conv_w, conv_b[None])
    ka = _filt_call(z_perm, fw, win_perm, gaf)
    kf = _filtb_call(mb, ka.reshape(FFT_N2, 2, FFT_N1, 2 * HY_D))

    a = _bm_call(ga, v, "hy_fft_a")
    c = _convb_call(mb, gc, a.reshape(FFT_N2, 2, FFT_N1, HY_D), kf, 0)
    z, a = _convd_call(md, c.reshape(FFT_N1, 2, FFT_N2, HY_D), x1, v, skip[0], ga=ga)
    c = _convb_call(mb, gc, a.reshape(FFT_N2, 2, FFT_N1, HY_D), kf, 1)
    z = _convd_call(md, c.reshape(FFT_N1, 2, FFT_N2, HY_D), x2, z, skip[1])
    return z.reshape(B, L, HY_D)


def _kout_kernel(x_ref, yh_ref, yg_ref, ym_ref, gh_ref, gg_ref, gm_ref, w_ref, gpost_ref, gffn_ref,
                 xo_ref, h_ref):
    a = _rms(yh_ref[0], gh_ref[...]).astype(BF16)
    b = _rms(yg_ref[0].astype(F32), gg_ref[...]).astype(BF16)
    c = _rms(ym_ref[0].astype(F32), gm_ref[...]).astype(BF16)
    o1 = HY_D
    o2 = o1 + GQA_HEADS * GQA_HEAD_DIM
    y = (jnp.dot(a, w_ref[:o1, :], preferred_element_type=F32)
         + jnp.dot(b, w_ref[o1:o2, :], preferred_element_type=F32)
         + jnp.dot(c, w_ref[o2:, :], preferred_element_type=F32))
    xo = x_ref[0] + _rms(y, gpost_ref[...])
    xo_ref[0] = xo
    h_ref[0] = _rms(xo, gffn_ref[...]).astype(BF16)


def _kout_call(x, yh, yg, ym, gh, gg, gm, w_p, gpost, gffn):
    B, L, D = x.shape
    tm = TM_OUT

    def rows(a):
        return pl.BlockSpec((1, tm, a.shape[2]), lambda b, i: (b, i, 0))

    def full(a):
        return pl.BlockSpec(a.shape, lambda b, i: (0,) * a.ndim)

    return pl.pallas_call(
        _kout_kernel, grid=(B, L // tm),
        in_specs=[rows(x), rows(yh), rows(yg), rows(ym), full(gh), full(gg), full(gm), full(w_p),
                  full(gpost), full(gffn)],
        out_specs=[rows(x), rows(x)],
        out_shape=[jax.ShapeDtypeStruct((B, L, D), F32), jax.ShapeDtypeStruct((B, L, D), BF16)],
        compiler_params=_cparams(("parallel", "parallel")), name="out_proj",
    )(x, yh, yg, ym, gh, gg, gm, w_p, gpost, gffn)


HALO = 16


def _ffn_kernel(h_ref, hp_ref, hn_ref, x_ref, wup_ref, cw_ref, cb_ref, wd_ref, gpost_ref,
                o_ref, act_ref):
    i = pl.program_id(1)
    tm = h_ref.shape[1]
    prev = jnp.where(i > 0, hp_ref[0], jnp.zeros_like(hp_ref[0]))
    nxt = jnp.where(i < pl.num_programs(1) - 1, hn_ref[0], jnp.zeros_like(hn_ref[0]))
    he = jnp.concatenate([prev, h_ref[0], nxt], axis=0)
    ext = tm + 2 * HALO
    tf = TF_FFN

    def conv(c0):
        up = jnp.dot(he, wup_ref[:, c0:c0 + tf], preferred_element_type=F32)
        um = pltpu.roll(up, 1, 0)[HALO:HALO + tm]
        upl = pltpu.roll(up, ext - 1, 0)[HALO:HALO + tm]
        return (um * cw_ref[0:1, c0:c0 + tf] + up[HALO:HALO + tm] * cw_ref[1:2, c0:c0 + tf]
                + upl * cw_ref[2:3, c0:c0 + tf] + cb_ref[:, c0:c0 + tf])

    for j in range(D_FF // tf):
        g = conv(j * tf)
        u = conv(D_FF + j * tf)
        gelu = 0.5 * g * (1.0 + jnp.tanh(math.sqrt(2.0 / math.pi) * (g + 0.044715 * (g * g * g))))
        act_ref[:, j * tf:(j + 1) * tf] = (gelu * u).astype(BF16)
    f = jnp.dot(act_ref[...], wd_ref[...], preferred_element_type=F32)
    o_ref[0] = x_ref[0] + _rms(f, gpost_ref[...])


def _ffn_call(h, x, w_up, cw, cb, w_down, gpost):
    B, L, D = x.shape
    tm = TM_FFN
    nb = tm // HALO
    last = L // HALO - 1

    def resident(a):
        return pl.BlockSpec(a.shape, lambda b, i: (0,) * a.ndim, pipeline_mode=pl.Buffered(1))

    cb = cb[None]
    return pl.pallas_call(
        _ffn_kernel, grid=(B, L // tm),
        in_specs=[pl.BlockSpec((1, tm, D), lambda b, i: (b, i, 0)),
                  pl.BlockSpec((1, HALO, D), lambda b, i: (b, jnp.maximum(i * nb - 1, 0), 0)),
                  pl.BlockSpec((1, HALO, D), lambda b, i: (b, jnp.minimum((i + 1) * nb, last), 0)),
                  pl.BlockSpec((1, tm, D), lambda b, i: (b, i, 0)),
                  resident(w_up), resident(cw), resident(cb), resident(w_down), resident(gpost)],
        out_specs=pl.BlockSpec((1, tm, D), lambda b, i: (b, i, 0)),
        out_shape=jax.ShapeDtypeStruct((B, L, D), F32),
        scratch_shapes=[pltpu.VMEM((tm, D_FF), BF16)],
        compiler_params=_cparams(("parallel", "parallel")), name="conv_ffn",
    )(h, h, h, x, w_up, cw, cb, w_down, gpost)


def _axial_tables(L, rot_dim):
    pos = np.arange(L)
    n_axis = rot_dim // 4
    inv = ROPE_THETA ** (-np.arange(n_axis) / n_axis)
    ang = np.concatenate([(pos // GRID_W)[:, None] * inv, (pos % GRID_W)[:, None] * inv], axis=-1)
    return jnp.asarray(np.cos(ang), F32), jnp.asarray(np.sin(ang), F32)


def _rope_tables(L):
    def lanes(parts):
        used = sum(p.shape[1] for p in parts)
        return jnp.concatenate(parts + [jnp.zeros((L, LANES - used), F32)], axis=1)

    cg, sg = _axial_tables(L, GQA_HEAD_DIM)
    tg = (lanes([cg, cg, cg, cg]), lanes([sg, sg, sg, sg]))
    cm, sm = _axial_tables(L, MLA_ROPE_DIM)
    nope0 = jnp.zeros((L, MLA_NOPE_DIM), F32)
    nope1 = jnp.ones((L, MLA_NOPE_DIM), F32)
    tmk = (lanes([nope0, cm, cm]), lanes([nope0, sm, sm]))
    sc = (MLA_NOPE_DIM + MLA_ROPE_DIM) ** -0.5 * LOG2E
    tmq = (lanes([nope1, cm, cm]) * sc, tmk[1] * sc)
    return tg, tmq, tmk


def _partner(w, half, sign=-1.0):
    return jnp.concatenate([sign * w[..., half:], w[..., :half]], axis=-1)


def _gain_pair(g, half):
    return jnp.tile(g, 2)[None], jnp.tile(_partner(g, half, 1.0), 2)[None]


def _pad_heads(w, n_heads, width):
    k = w.shape[0]
    return jnp.pad(w.reshape(k, n_heads, width), ((0, 0), (0, 0), (0, LANES - width))).reshape(
        k, n_heads * LANES)


def kernel(x, mix_pre_norm, w_in, hy_conv_w, hy_conv_b, hy_filt_w1, hy_filt_b1, hy_filt_freq1,
           hy_filt_w2, hy_filt_b2, hy_filt_freq2, hy_filt_w3, hy_skip, gqa_q_norm, gqa_k_norm,
           mla_q_a_norm, mla_w_uq, mla_kv_a_norm, mla_w_ukv, hy_out_norm, gqa_out_norm,
           mla_out_norm, w_out, mix_post_norm, ffn_pre_norm, w_up, ffn_conv_w, ffn_conv_b,
           w_down, ffn_post_norm):
    B, L, D = x.shape
    assert B == 2 and 2 * L == FFT_N1 * FFT_N2 and D == D_MODEL
    depth = w_in.shape[0]
    tg, tmq, tmk = _rope_tables(L)
    z_perm, win_perm = _hyena_positions(L)
    mats = _dft_matrices()

    for l in range(depth):
        wl = w_in[l]
        hd, hh, rh = GQA_HEAD_DIM, GQA_HEAD_DIM // 2, MLA_ROPE_DIM // 2
        o1 = HY_COLS
        o2 = o1 + GQA_HEADS * hd
        o3 = o2 + GQA_KV_HEADS * hd
        o4 = o3 + GQA_KV_HEADS * hd
        o5 = o4 + MLA_Q_RANK
        o6 = o5 + MLA_KV_RANK
        wq = wl[:, o1:o2].reshape(D, GQA_HEADS, hd)
        wk = wl[:, o2:o3].reshape(D, GQA_KV_HEADS, hd)
        wkr = wl[:, o6:]
        pe_pad = ((0, 0), (MLA_NOPE_DIM, LANES - MLA_NOPE_DIM - MLA_ROPE_DIM))
        win_p = jnp.concatenate(
            [wl[:, :o1],
             wq.reshape(D, -1), _partner(wq, hh).reshape(D, -1),
             wk.reshape(D, -1), _partner(wk, hh).reshape(D, -1),
             wl[:, o3:o6], jnp.pad(wkr, pe_pad), jnp.pad(_partner(wkr, rh), pe_pad)],
            axis=1).astype(BF16)
        gq = _gain_pair(gqa_q_norm[l] * (hd ** -0.5 * LOG2E), hh)
        gk = _gain_pair(gqa_k_norm[l], hh)
        wuq = mla_w_uq[l].reshape(MLA_Q_RANK, MLA_HEADS, MLA_NOPE_DIM + MLA_ROPE_DIM)
        wuq_pe = jnp.pad(_partner(wuq[:, :, MLA_NOPE_DIM:], rh),
                         ((0, 0), (0, 0), (MLA_NOPE_DIM, 0)))
        wuq_p = jnp.concatenate(
            [_pad_heads(mla_w_uq[l], MLA_HEADS, MLA_NOPE_DIM + MLA_ROPE_DIM),
             _pad_heads(wuq_pe.reshape(MLA_Q_RANK, -1), MLA_HEADS, MLA_NOPE_DIM + MLA_ROPE_DIM)],
            axis=1).astype(BF16)
        wukv = mla_w_ukv[l].reshape(MLA_KV_RANK, MLA_HEADS, MLA_NOPE_DIM + MLA_V_DIM)
        wukvk_p = _pad_heads(wukv[:, :, :MLA_NOPE_DIM].reshape(MLA_KV_RANK, -1), MLA_HEADS,
                             MLA_NOPE_DIM).astype(BF16)
        wukvv = wukv[:, :, MLA_NOPE_DIM:].reshape(MLA_KV_RANK, MLA_HEADS * MLA_V_DIM).astype(BF16)

        hy_in, qg, kg, vg, qm, km, vm = _kin_call(
            x, mix_pre_norm[l][None], win_p, gq, gk, tg, mla_q_a_norm[l][None], wuq_p, tmq,
            mla_kv_a_norm[l][None], wukvk_p, wukvv, tmk)

        fw = _filter_weights(hy_filt_w1[l], hy_filt_b1[l], hy_filt_freq1[l], hy_filt_w2[l],
                             hy_filt_b2[l], hy_filt_freq2[l], hy_filt_w3[l])
        y_hy = _hyena_layer(hy_in, hy_conv_w[l], hy_conv_b[l], fw, hy_skip[l], z_perm, win_perm, mats)

        y_gqa = _attn_call(qg.transpose(0, 2, 1), kg, vg.reshape(B, GQA_KV_HEADS, GQA_HEAD_DIM, L),
                           n_kv=1, n_rep=GQA_HEADS // GQA_KV_HEADS, pack=2, tq=TQ_GQA,
                           name="attn_gqa")
        y_mla = _attn_call(qm.transpose(0, 2, 1), km, vm.reshape(B, MLA_HEADS, MLA_V_DIM, L),
                           n_kv=2, n_rep=1, pack=1, tq=TQ_MLA, name="attn_mla")

        x, h2 = _kout_call(x, y_hy, y_gqa, y_mla, hy_out_norm[l][None], gqa_out_norm[l][None],
                           mla_out_norm[l][None], w_out[l].astype(BF16),
                           mix_post_norm[l][None], ffn_pre_norm[l][None])
        x = _ffn_call(h2, x, w_up[l].astype(BF16), ffn_conv_w[l], ffn_conv_b[l],
                      w_down[l].astype(BF16), ffn_post_norm[l][None])
    return x
```

```python
import functools
import math

import numpy as np
import jax
import jax.numpy as jnp
from jax import lax
from jax.experimental import pallas as pl
from jax.experimental.pallas import tpu as pltpu

F32 = jnp.float32
BF16 = jnp.bfloat16

NORM_EPS = 1e-6
ROPE_THETA = 10000.0
GRID_W = 64
LOG2E = math.log2(math.e)

D_MODEL = 1024
HY_D = 256
HY_COLS = 3 * HY_D
HY_EMB = 33
HY_BANDS = 16
HY_DECAY_TARGET = 1e-2
HY_FAST_DECAY_PCT = 0.3
HY_SLOW_DECAY_PCT = 1.5
GQA_HEADS = 8
GQA_KV_HEADS = 2
GQA_HEAD_DIM = 64
MLA_HEADS = 4
MLA_Q_RANK = 256
MLA_KV_RANK = 128
MLA_NOPE_DIM = 64
MLA_ROPE_DIM = 32
MLA_V_DIM = 64
D_FF = 2816

LANES = 128
FFT_N1 = 128
FFT_N2 = 128

TM_IN = 1024
TK_ATTN = 256
TQ_GQA = 128
TQ_MLA = 256
TM_OUT = 1024
TM_FFN = 512
TF_FFN = 256
TL_HCONV = 1024
FILT_HALF = 512
FILT_FEAT = 64
VMEM_LIMIT = 56 * 1024 * 1024


def _cparams(sem):
    return pltpu.CompilerParams(dimension_semantics=sem, vmem_limit_bytes=VMEM_LIMIT)


def _rms(x, g):
    return x * lax.rsqrt(jnp.mean(x * x, axis=-1, keepdims=True) + NORM_EPS) * g


def _kin_kernel(x_ref, gpre_ref, win_ref, gqn_ref, gqs_ref, gkn_ref, gks_ref, cg_ref, sg_ref,
                mqn_ref, wuq_ref, cmq_ref, smq_ref,
                mkvn_ref, wukvk_ref, wukvv_ref, cmk_ref, smk_ref,
                hy_ref, qg_ref, kg_ref, vg_ref, qm_ref, km_ref, vm_ref):
    x = x_ref[0]
    h = _rms(x, gpre_ref[...]).astype(BF16)
    cur = [0]

    def proj(n):
        lo = cur[0]
        cur[0] = lo + n
        return jnp.dot(h, win_ref[:, lo:lo + n], preferred_element_type=F32)

    hy_ref[0] = proj(HY_COLS)

    cg, sg = cg_ref[...], sg_ref[...]
    low = lax.broadcasted_iota(jnp.int32, (x.shape[0], LANES), 1) < GQA_HEAD_DIM

    def head_pairs(n_tiles, gain, gain_sw):
        xa, xb = proj(n_tiles * LANES), proj(n_tiles * LANES)
        ca, sa = cg * gain, sg * gain_sw
        out = []
        for t in range(n_tiles):
            xc, xs = xa[:, t * LANES:(t + 1) * LANES], xb[:, t * LANES:(t + 1) * LANES]
            sq = xc * xc
            tot = jnp.sum(sq, axis=-1, keepdims=True)
            lo = jnp.sum(jnp.where(low, sq, 0.0), axis=-1, keepdims=True)
            ms = jnp.where(low, lo, tot - lo) * (1.0 / GQA_HEAD_DIM)
            out.append((xc * ca + xs * sa) * lax.rsqrt(ms + NORM_EPS))
        return out

    for t, q in enumerate(head_pairs(GQA_HEADS // 2, gqn_ref[...], gqs_ref[...])):
        qg_ref[0, :, t * LANES:(t + 1) * LANES] = q.astype(BF16)
    (kk,) = head_pairs(GQA_KV_HEADS // 2, gkn_ref[...], gks_ref[...])
    ksw = pltpu.roll(kk, GQA_HEAD_DIM, 1)
    kg_ref[0, 0] = jnp.where(low, kk, ksw).astype(BF16)
    kg_ref[0, 1] = jnp.where(low, ksw, kk).astype(BF16)
    vg_ref[0] = proj(LANES).T.astype(BF16)

    cq = _rms(proj(MLA_Q_RANK), mqn_ref[...]).astype(BF16)
    qm = jnp.dot(cq, wuq_ref[...], preferred_element_type=F32)
    cmq, smq = cmq_ref[...], smq_ref[...]
    nq = MLA_HEADS * LANES
    for j in range(MLA_HEADS):
        qm_ref[0, :, j * LANES:(j + 1) * LANES] = (
            qm[:, j * LANES:(j + 1) * LANES] * cmq
            + qm[:, nq + j * LANES:nq + (j + 1) * LANES] * smq).astype(BF16)

    ckv = _rms(proj(MLA_KV_RANK), mkvn_ref[...]).astype(BF16)
    kpe = proj(LANES) * cmk_ref[...] + proj(LANES) * smk_ref[...]
    kn = jnp.dot(ckv, wukvk_ref[...], preferred_element_type=F32)
    for j in range(MLA_HEADS):
        km_ref[0, j] = (kn[:, j * LANES:(j + 1) * LANES] + kpe).astype(BF16)
    vm = jnp.dot(ckv, wukvv_ref[...], preferred_element_type=F32)
    for j in range(vm.shape[1] // LANES):
        vm_ref[0, j * LANES:(j + 1) * LANES, :] = vm[:, j * LANES:(j + 1) * LANES].T.astype(BF16)


def _kin_call(x, gpre, win_p, gq, gk, tg, mqn, wuq_p, tmq, mkvn, wukvk_p, wukvv, tmk):
    B, L, D = x.shape
    tm = TM_IN
    nt = L // tm

    def full(a):
        return pl.BlockSpec(a.shape, lambda b, i: (0,) * a.ndim, pipeline_mode=pl.Buffered(1))

    def rows(w):
        return pl.BlockSpec((tm, w), lambda b, i: (i, 0))

    in_specs = [pl.BlockSpec((1, tm, D), lambda b, i: (b, i, 0)), full(gpre), full(win_p),
                full(gq[0]), full(gq[1]), full(gk[0]), full(gk[1]), rows(LANES), rows(LANES),
                full(mqn), full(wuq_p), rows(LANES), rows(LANES),
                full(mkvn), full(wukvk_p), full(wukvv), rows(LANES), rows(LANES)]
    gv_rows = GQA_KV_HEADS * GQA_HEAD_DIM
    mv_rows = MLA_HEADS * MLA_V_DIM
    out_shape = [
        jax.ShapeDtypeStruct((B, L, HY_COLS), F32),
        jax.ShapeDtypeStruct((B, L, GQA_HEADS * GQA_HEAD_DIM), BF16),
        jax.ShapeDtypeStruct((B, GQA_KV_HEADS, L, LANES), BF16),
        jax.ShapeDtypeStruct((B, gv_rows, L), BF16),
        jax.ShapeDtypeStruct((B, L, MLA_HEADS * LANES), BF16),
        jax.ShapeDtypeStruct((B, MLA_HEADS, L, LANES), BF16),
        jax.ShapeDtypeStruct((B, mv_rows, L), BF16),
    ]
    out_specs = [
        pl.BlockSpec((1, tm, HY_COLS), lambda b, i: (b, i, 0)),
        pl.BlockSpec((1, tm, GQA_HEADS * GQA_HEAD_DIM), lambda b, i: (b, i, 0)),
        pl.BlockSpec((1, GQA_KV_HEADS, tm, LANES), lambda b, i: (b, 0, i, 0)),
        pl.BlockSpec((1, gv_rows, tm), lambda b, i: (b, 0, i)),
        pl.BlockSpec((1, tm, MLA_HEADS * LANES), lambda b, i: (b, i, 0)),
        pl.BlockSpec((1, MLA_HEADS, tm, LANES), lambda b, i: (b, 0, i, 0)),
        pl.BlockSpec((1, mv_rows, tm), lambda b, i: (b, 0, i)),
    ]
    return pl.pallas_call(
        _kin_kernel, grid=(B, nt), in_specs=in_specs, out_specs=out_specs, out_shape=out_shape,
        compiler_params=_cparams(("parallel", "parallel")), name="in_proj",
    )(x, gpre, win_p, *gq, *gk, *tg, mqn, wuq_p, *tmq, mkvn, wukvk_p, wukvv, *tmk)


def _attn_kernel(q_ref, k_ref, vt_ref, o_ref, sa_ref, sb_ref, ma_ref, mb_ref, *,
                 n_kv, n_rep, pack, tq, n_chunks, tk):
    i = pl.program_id(0)
    cols = n_rep * tq
    width = n_kv * cols
    grp = tk // 8

    @pl.when(i == 0)
    def _():
        sb_ref[...] = jnp.zeros(sb_ref.shape, F32)
        mb_ref[...] = jnp.zeros(mb_ref.shape, F32)

    def step(sw_ref, mw_ref, sr_ref, mr_ref):
        def q_head(h):
            rows_h = LANES // pack
            return q_ref[0, h * rows_h:(h + 1) * rows_h, :]

        qs = [jnp.concatenate([q_head(a * n_rep + j) for j in range(n_rep)], axis=1)
              for a in range(n_kv)]
        mx = jnp.max(mr_ref[...], axis=0, keepdims=True)
        m = jnp.full((8, width), -jnp.inf, F32)
        l = jnp.zeros((8, width), F32)
        accs = [jnp.zeros((vt_ref.shape[2], cols), F32) for _ in range(n_kv)]
        for c in range(n_chunks):
            st = jnp.concatenate(
                [jnp.dot(k_ref[0, a, c * tk:(c + 1) * tk, :LANES // pack], qs[a],
                         preferred_element_type=F32)
                 for a in range(n_kv)], axis=1)
            sw_ref[c] = st
            m = jnp.maximum(m, jnp.max(st.reshape(grp, 8, width), axis=0))
            p = jnp.exp2(sr_ref[c] - mx)
            l = l + jnp.sum(p.reshape(grp, 8, width), axis=0)
            pb = p.astype(BF16)
            for a in range(n_kv):
                accs[a] = accs[a] + jnp.dot(vt_ref[0, a, :, c * tk:(c + 1) * tk],
                                            pb[:, a * cols:(a + 1) * cols],
                                            preferred_element_type=F32)
        mw_ref[...] = m
        ls = jnp.sum(l, axis=0, keepdims=True)
        heads = []
        for a in range(n_kv):
            oa = accs[a] / ls[:, a * cols:(a + 1) * cols]
            heads += [oa[:, j * tq:(j + 1) * tq] for j in range(n_rep)]
        o_ref[0] = jnp.concatenate(heads, axis=0).T.astype(BF16)

    @pl.when(i % 2 == 0)
    def _():
        step(sa_ref, ma_ref, sb_ref, mb_ref)

    @pl.when(i % 2 == 1)
    def _():
        step(sb_ref, mb_ref, sa_ref, ma_ref)


def _attn_call(qt, k, vt, *, n_kv, n_rep, pack, tq, name):
    B, hq, L = qt.shape
    H = hq * pack // LANES
    hkv, dv = vt.shape[1], vt.shape[2]
    G = hkv // n_kv
    hs = n_kv * n_rep
    tk = TK_ATTN
    n_chunks = L // tk
    nq = L // tq
    width = hs * tq
    kern = functools.partial(_attn_kernel, n_kv=n_kv, n_rep=n_rep, pack=pack, tq=tq,
                             n_chunks=n_chunks, tk=tk)
    total = B * G * nq

    def blk(s):
        return s // (G * nq), (s // nq) % G, s % nq

    def q_map(s):
        b, g, i = blk(jnp.minimum(s, total - 1))
        return b, g, i

    def k_map(s):
        b, g, _ = blk(jnp.minimum(s, total - 1))
        return b, g, 0, 0

    def v_map(s):
        b, g, _ = blk(jnp.maximum(s - 1, 0))
        return b, g, 0, 0

    def o_map(s):
        b, g, i = blk(jnp.maximum(s - 1, 0))
        return b, i, g

    return pl.pallas_call(
        kern, grid=(total + 1,),
        in_specs=[pl.BlockSpec((1, hs * LANES // pack, tq), q_map),
                  pl.BlockSpec((1, n_kv, L, LANES), k_map, pipeline_mode=pl.Buffered(1)),
                  pl.BlockSpec((1, n_kv, dv, L), v_map, pipeline_mode=pl.Buffered(1))],
        out_specs=pl.BlockSpec((1, tq, hs * dv), o_map),
        out_shape=jax.ShapeDtypeStruct((B, L, H * dv), BF16),
        scratch_shapes=[pltpu.VMEM((n_chunks, tk, width), F32), pltpu.VMEM((n_chunks, tk, width), F32),
                        pltpu.VMEM((8, width), F32), pltpu.VMEM((8, width), F32)],
        compiler_params=_cparams(("arbitrary",)), name=name,
    )(qt, k, vt)


def _hconv_kernel(x_ref, xp_ref, xn_ref, w_ref, b_ref, v_ref, x1_ref, x2_ref):
    i = pl.program_id(1)
    x = x_ref[0]
    tl = x.shape[0]
    prev = jnp.where(i > 0, xp_ref[0][7:8, :], 0.0)
    nxt = jnp.where(i < pl.num_programs(1) - 1, xn_ref[0][0:1, :], 0.0)
    r = lax.broadcasted_iota(jnp.int32, x.shape, 0)
    xm = jnp.where(r == 0, prev, pltpu.roll(x, 1, 0))
    xp = jnp.where(r == tl - 1, nxt, pltpu.roll(x, tl - 1, 0))
    uc = xm * w_ref[0:1, :] + x * w_ref[1:2, :] + xp * w_ref[2:3, :] + b_ref[...]
    for r in range(tl // FFT_N2):
        blk = uc[r * FFT_N2:(r + 1) * FFT_N2]
        v_ref[:, r, :] = blk[:, :HY_D]
        x1_ref[:, r, :] = blk[:, HY_D:2 * HY_D]
        x2_ref[:, r, :] = blk[:, 2 * HY_D:]


def _hconv_call(hy_in, w, b):
    B, L, C = hy_in.shape
    tl = TL_HCONV
    nb = tl // 8
    last = L // 8 - 1
    nt = L // tl
    rows = tl // FFT_N2
    out = jax.ShapeDtypeStruct((FFT_N2, B * L // FFT_N2, HY_D), F32)
    ospec = pl.BlockSpec((FFT_N2, rows, HY_D), lambda b_, i: (0, b_ * nt + i, 0))
    return pl.pallas_call(
        _hconv_kernel, grid=(B, L // tl),
        in_specs=[pl.BlockSpec((1, tl, C), lambda b_, i: (b_, i, 0)),
                  pl.BlockSpec((1, 8, C), lambda b_, i: (b_, jnp.maximum(i * nb - 1, 0), 0)),
                  pl.BlockSpec((1, 8, C), lambda b_, i: (b_, jnp.minimum((i + 1) * nb, last), 0)),
                  pl.BlockSpec((3, C), lambda b_, i: (0, 0)),
                  pl.BlockSpec((1, C), lambda b_, i: (0, 0))],
        out_specs=[ospec, ospec, ospec], out_shape=[out, out, out],
        compiler_params=_cparams(("parallel", "parallel")), name="hy_conv3",
    )(hy_in, hy_in, hy_in, w, b)


def _split_bf16(a):
    hi = a.astype(BF16)
    return hi, (a - hi.astype(F32)).astype(BF16)


def _dot3(a, w_hi, w_lo):
    a_hi, a_lo = _split_bf16(a)
    dot = functools.partial(jnp.dot, preferred_element_type=F32)
    return dot(a_hi, w_hi) + dot(a_lo, w_hi) + dot(a_hi, w_lo)


def _filt_kernel(z_ref, w1h_ref, w1l_ref, b1_ref, f1_ref, w2h_ref, w2l_ref, b2_ref, f2_ref,
                 w3h_ref, w3l_ref, win_ref, gaf_ref, o_ref):
    h = jnp.sin(f1_ref[...] * (_dot3(z_ref[...], w1h_ref[...], w1l_ref[...]) + b1_ref[...]))
    h = jnp.sin(f2_ref[...] * (_dot3(h, w2h_ref[...], w2l_ref[...]) + b2_ref[...]))
    k = _dot3(h, w3h_ref[...], w3l_ref[...])
    half = k.shape[0]
    wcols = 4 * HY_D
    per_half = half // FFT_N1
    n1 = lax.broadcasted_iota(jnp.int32, (half, 2 * HY_D), 0) % FFT_N1
    for s in range(2):
        ks = k[:, s * wcols:(s + 1) * wcols]
        kk = jnp.where(n1 >= FFT_N1 // 2, ks[:, 2 * HY_D:], ks[:, :2 * HY_D])
        win = win_ref[s * half:(s + 1) * half, :]
        filt = (kk * jnp.concatenate([win, win], axis=-1)).astype(BF16)
        for j in range(per_half):
            o_ref[s * per_half + j] = jnp.dot(gaf_ref[s * per_half + j],
                                              filt[j * FFT_N1:(j + 1) * FFT_N1],
                                              preferred_element_type=F32)


def _filt_call(z_pack, fw, win_perm, gaf):
    tp = 2 * FILT_HALF
    n = win_perm.shape[0]
    to = tp // FFT_N1
    M = gaf.shape[1]

    def full(a):
        return pl.BlockSpec(a.shape, lambda i: (0,) * a.ndim)

    return pl.pallas_call(
        _filt_kernel, grid=(n // tp,),
        in_specs=[pl.BlockSpec((FILT_HALF, z_pack.shape[1]), lambda i: (i, 0))]
        + [full(a) for a in fw] + [pl.BlockSpec((tp, HY_D), lambda i: (i, 0)),
                                   pl.BlockSpec((to, M, FFT_N1), lambda i: (i, 0, 0))],
        out_specs=pl.BlockSpec((to, M, 2 * HY_D), lambda i: (i, 0, 0)),
        out_shape=jax.ShapeDtypeStruct((n // FFT_N1, M, 2 * HY_D), F32),
        compiler_params=_cparams(("parallel",)), name="hy_filter",
    )(z_pack, *fw, win_perm, gaf)


def _bm_kernel(g_ref, x_ref, o_ref, *, to):
    for t in range(to):
        o_ref[t] = jnp.dot(g_ref[t], x_ref[t].astype(BF16), preferred_element_type=F32)


def _bm_call(g, x, name, to=16):
    O, K, N = x.shape
    M = g.shape[1]
    return pl.pallas_call(
        functools.partial(_bm_kernel, to=to), grid=(O // to,),
        in_specs=[pl.BlockSpec((to, M, K), lambda i: (i, 0, 0)),
                  pl.BlockSpec((to, K, N), lambda i: (i, 0, 0))],
        out_specs=pl.BlockSpec((to, M, N), lambda i: (i, 0, 0)),
        out_shape=jax.ShapeDtypeStruct((O, M, N), F32),
        compiler_params=_cparams(("parallel",)), name=name,
    )(g, x)


def _gather_ri(x_ref, j):
    return jnp.concatenate([x_ref[:, 0, j, :], x_ref[:, 1, j, :]], axis=0).astype(BF16)


def _gather_spec(a, to):
    return pl.BlockSpec((a.shape[0], 2, to, a.shape[3]), lambda i: (0, 0, i, 0))


def _filtb_kernel(mb_ref, x_ref, o_ref, *, to):
    for j in range(to):
        o_ref[j] = jnp.dot(mb_ref[...], _gather_ri(x_ref, j),
                           preferred_element_type=F32).astype(o_ref.dtype)


def _filtb_call(mb, ka, to=16):
    O, _, P, N = ka.shape
    return pl.pallas_call(
        functools.partial(_filtb_kernel, to=to), grid=(P // to,),
        in_specs=[pl.BlockSpec(mb.shape, lambda i: (0, 0)), _gather_spec(ka, to)],
        out_specs=pl.BlockSpec((to, 2 * O, N), lambda i: (i, 0, 0)),
        out_shape=jax.ShapeDtypeStruct((P, 2 * O, N), BF16),
        compiler_params=_cparams(("parallel",)), name="hy_fft_filt_b",
    )(mb, ka)


def _convb_kernel(mb_ref, gc_ref, x_ref, kf_ref, o_ref, *, to):
    h = FFT_N2
    for j in range(to):
        xs = jnp.dot(mb_ref[...], _gather_ri(x_ref, j), preferred_element_type=F32)
        xr, xi = xs[:h], xs[h:]
        kr, ki = kf_ref[j, :h, :].astype(F32), kf_ref[j, h:, :].astype(F32)
        ys = jnp.concatenate([xr * kr - xi * ki, xr * ki + xi * kr], axis=0).astype(BF16)
        o_ref[j] = jnp.dot(gc_ref[j], ys, preferred_element_type=F32)


def _convb_call(mb, gc, a, kf, order, to=16):
    O, _, P, N = a.shape
    return pl.pallas_call(
        functools.partial(_convb_kernel, to=to), grid=(P // to,),
        in_specs=[pl.BlockSpec(mb.shape, lambda i: (0, 0)),
                  pl.BlockSpec((to, 2 * O, 2 * O), lambda i: (i, 0, 0)),
                  _gather_spec(a, to),
                  pl.BlockSpec((to, 2 * O, N), lambda i: (i, 0, order))],
        out_specs=pl.BlockSpec((to, 2 * O, N), lambda i: (i, 0, 0)),
        out_shape=jax.ShapeDtypeStruct((P, 2 * O, N), F32),
        compiler_params=_cparams(("parallel",)), name="hy_spec_mul",
    )(mb, gc, a, kf)


def _convd_kernel(md_ref, c_ref, g_ref, u_ref, s_ref, *rest, to, chain):
    if chain:
        ga_ref, z_ref, a_ref = rest
    else:
        (o_ref,) = rest
    for j in range(to):
        y = jnp.dot(md_ref[...], _gather_ri(c_ref, j), preferred_element_type=F32)
        z = g_ref[j] * (y + u_ref[j] * s_ref[...])
        if chain:
            z_ref[j] = z
            a_ref[j] = jnp.dot(ga_ref[j], z.astype(BF16), preferred_element_type=F32)
        else:
            o_ref[:, j, :] = z


def _convd_call(md, c, gate, u, skip, ga=None, to=16):
    O, _, P, N = c.shape
    R = md.shape[0]
    tspec = pl.BlockSpec((to, R, N), lambda i: (i, 0, 0))
    in_specs = [pl.BlockSpec(md.shape, lambda i: (0, 0)), _gather_spec(c, to), tspec, tspec,
                pl.BlockSpec((1, N), lambda i: (0, 0))]
    args = [md, c, gate, u, skip.reshape(1, N)]
    if ga is not None:
        M = ga.shape[1]
        in_specs.append(pl.BlockSpec((to, M, R), lambda i: (i, 0, 0)))
        args.append(ga)
        out_specs = [tspec, pl.BlockSpec((to, M, N), lambda i: (i, 0, 0))]
        out_shape = [jax.ShapeDtypeStruct((P, R, N), F32), jax.ShapeDtypeStruct((P, M, N), F32)]
    else:
        out_specs = pl.BlockSpec((R, to, N), lambda i: (0, i, 0))
        out_shape = jax.ShapeDtypeStruct((R, P, N), F32)
    return pl.pallas_call(
        functools.partial(_convd_kernel, to=to, chain=ga is not None), grid=(P // to,),
        in_specs=in_specs, out_specs=out_specs, out_shape=out_shape,
        compiler_params=_cparams(("parallel",)), name="hy_fft_d",
    )(*args)


def _dft_tables():
    n = FFT_N1 * FFT_N2
    k = np.arange(FFT_N1)
    f = np.exp(-2j * np.pi * np.outer(k, k) / FFT_N1)
    t = np.exp(-2j * np.pi * np.outer(k, k) / n)
    return f, t, n


def _dft_matrices():
    f, t, n = _dft_tables()
    fr, fi = jnp.asarray(f.real, F32), jnp.asarray(f.imag, F32)
    tr, ti = jnp.asarray(t.real, F32), jnp.asarray(t.imag, F32)
    half = FFT_N1 // 2
    er = fr[None] * tr[:, :, None] - fi[None] * ti[:, :, None]
    ei = fr[None] * ti[:, :, None] + fi[None] * tr[:, :, None]
    ga = jnp.concatenate([jnp.concatenate([er[:, :, :half], -ei[:, :, :half]], axis=2),
                          jnp.concatenate([ei[:, :, :half], er[:, :, :half]], axis=2)], axis=1)
    gaf = jnp.concatenate([er, ei], axis=1)
    mb = jnp.concatenate([jnp.concatenate([fr, -fi], axis=1),
                          jnp.concatenate([fi, fr], axis=1)], axis=0)
    tct = jnp.transpose(tr)[:, :, None]
    tst = -jnp.transpose(ti)[:, :, None]
    gr = tct * fr[None] - tst * (-fi[None])
    gi = tct * (-fi[None]) + tst * fr[None]
    gc = jnp.concatenate([jnp.concatenate([gr, -gi], axis=2),
                          jnp.concatenate([gi, gr], axis=2)], axis=1)
    hr, hi = fr[:half] / n, -fi[:half] / n
    md = jnp.concatenate([jnp.concatenate([hr, -hi], axis=1),
                          jnp.concatenate([hi, hr], axis=1)], axis=0)
    return (ga.astype(BF16), gaf.astype(BF16), mb.astype(BF16), gc.astype(BF16), md.astype(BF16))


def _hyena_positions(L):
    p = FFT_N2 * np.arange(FFT_N1)[None, :] + np.arange(FFT_N2)[:, None]
    pos = np.where(p < L, p, 2 * L - 1 - p).reshape(2 * L, 1).astype(np.float64)
    t = pos / (L - 1)
    w = 2.0 * math.pi * pos / L
    f = np.linspace(1e-4, HY_BANDS - 1, HY_BANDS)[None, :]
    z = np.concatenate([t, np.cos(f * w), -np.sin(f * w),
                        np.zeros((2 * L, FILT_FEAT - HY_EMB))], axis=-1)
    z_pack = (z.reshape(-1, 2, FILT_HALF, FILT_FEAT).transpose(0, 2, 1, 3)
              .reshape(-1, 2 * FILT_FEAT))
    max_decay = math.log(HY_DECAY_TARGET) / HY_FAST_DECAY_PCT
    min_decay = math.log(HY_DECAY_TARGET) / HY_SLOW_DECAY_PCT
    deltas = jnp.linspace(min_decay, max_decay, HY_D, dtype=F32)
    window = jnp.exp(-jnp.asarray(t, F32) * jnp.abs(deltas)[None, :])
    return jnp.asarray(z_pack, F32), window


def _filter_weights(w1, b1, f1, w2, b2, f2, w3):
    def bd(w):
        z = jnp.zeros_like(w)
        return jnp.concatenate([jnp.concatenate([w, z], axis=1),
                                jnp.concatenate([z, w], axis=1)], axis=0)

    def twice(v):
        return jnp.concatenate([v, v])[None]

    w1 = jnp.pad(w1, ((0, FILT_FEAT - w1.shape[0]), (0, 0)))
    w3 = w3.reshape(-1, 2, 2, HY_D).transpose(0, 2, 1, 3).reshape(-1, 4 * HY_D)
    return (*_split_bf16(bd(w1)), twice(b1), twice(f1), *_split_bf16(bd(w2)), twice(b2), twice(f2),
            *_split_bf16(bd(w3)))


def _hyena_layer(hy_in, conv_w, conv_b, fw, skip, z_perm, win_perm, mats):
    B, L, _ = hy_in.shape
    ga, gaf, mb, gc, md = mats
    v, x1, x2 = _hconv_call(hy_in, conv_w, conv_b[None])
    ka = _filt_call(z_perm, fw, win_perm, gaf)
    kf = _filtb_call(mb, ka.reshape(FFT_N2, 2, FFT_N1, 2 * HY_D))

    a = _bm_call(ga, v, "hy_fft_a")
    c = _convb_call(mb, gc, a.reshape(FFT_N2, 2, FFT_N1, HY_D), kf, 0)
    z, a = _convd_call(md, c.reshape(FFT_N1, 2, FFT_N2, HY_D), x1, v, skip[0], ga=ga)
    c = _convb_call(mb, gc, a.reshape(FFT_N2, 2, FFT_N1, HY_D), kf, 1)
    z = _convd_call(md, c.reshape(FFT_N1, 2, FFT_N2, HY_D), x2, z, skip[1])
    return z.reshape(B, L, HY_D)


def _kout_kernel(x_ref, yh_ref, yg_ref, ym_ref, gh_ref, gg_ref, gm_ref, w_ref, gpost_ref, gffn_ref,
                 xo_ref, h_ref):
    a = _rms(yh_ref[0], gh_ref[...]).astype(BF16)
    b = _rms(yg_ref[0].astype(F32), gg_ref[...]).astype(BF16)
    c = _rms(ym_ref[0].astype(F32), gm_ref[...]).astype(BF16)
    o1 = HY_D
    o2 = o1 + GQA_HEADS * GQA_HEAD_DIM
    y = (jnp.dot(a, w_ref[:o1, :], preferred_element_type=F32)
         + jnp.dot(b, w_ref[o1:o2, :], preferred_element_type=F32)
         + jnp.dot(c, w_ref[o2:, :], preferred_element_type=F32))
    xo = x_ref[0] + _rms(y, gpost_ref[...])
    xo_ref[0] = xo
    h_ref[0] = _rms(xo, gffn_ref[...]).astype(BF16)


def _kout_call(x, yh, yg, ym, gh, gg, gm, w_p, gpost, gffn):
    B, L, D = x.shape
    tm = TM_OUT

    def rows(a):
        return pl.BlockSpec((1, tm, a.shape[2]), lambda b, i: (b, i, 0))

    def full(a):
        return pl.BlockSpec(a.shape, lambda b, i: (0,) * a.ndim)

    return pl.pallas_call(
        _kout_kernel, grid=(B, L // tm),
        in_specs=[rows(x), rows(yh), rows(yg), rows(ym), full(gh), full(gg), full(gm), full(w_p),
                  full(gpost), full(gffn)],
        out_specs=[rows(x), rows(x)],
        out_shape=[jax.ShapeDtypeStruct((B, L, D), F32), jax.ShapeDtypeStruct((B, L, D), BF16)],
        compiler_params=_cparams(("parallel", "parallel")), name="out_proj",
    )(x, yh, yg, ym, gh, gg, gm, w_p, gpost, gffn)


HALO = 16


def _ffn_kernel(h_ref, hp_ref, hn_ref, x_ref, wup_ref, cw_ref, cb_ref, wd_ref, gpost_ref,
                o_ref, act_ref):
    i = pl.program_id(1)
    tm = h_ref.shape[1]
    prev = jnp.where(i > 0, hp_ref[0], jnp.zeros_like(hp_ref[0]))
    nxt = jnp.where(i < pl.num_programs(1) - 1, hn_ref[0], jnp.zeros_like(hn_ref[0]))
    he = jnp.concatenate([prev, h_ref[0], nxt], axis=0)
    ext = tm + 2 * HALO
    tf = TF_FFN

    def conv(c0):
        up = jnp.dot(he, wup_ref[:, c0:c0 + tf], preferred_element_type=F32)
        um = pltpu.roll(up, 1, 0)[HALO:HALO + tm]
        upl = pltpu.roll(up, ext - 1, 0)[HALO:HALO + tm]
        return (um * cw_ref[0:1, c0:c0 + tf] + up[HALO:HALO + tm] * cw_ref[1:2, c0:c0 + tf]
                + upl * cw_ref[2:3, c0:c0 + tf] + cb_ref[:, c0:c0 + tf])

    for j in range(D_FF // tf):
        g = conv(j * tf)
        u = conv(D_FF + j * tf)
        gelu = 0.5 * g * (1.0 + jnp.tanh(math.sqrt(2.0 / math.pi) * (g + 0.044715 * (g * g * g))))
        act_ref[:, j * tf:(j + 1) * tf] = (gelu * u).astype(BF16)
    f = jnp.dot(act_ref[...], wd_ref[...], preferred_element_type=F32)
    o_ref[0] = x_ref[0] + _rms(f, gpost_ref[...])


def _ffn_call(h, x, w_up, cw, cb, w_down, gpost):
    B, L, D = x.shape
    tm = TM_FFN
    nb = tm // HALO
    last = L // HALO - 1

    def resident(a):
        return pl.BlockSpec(a.shape, lambda b, i: (0,) * a.ndim, pipeline_mode=pl.Buffered(1))

    cb = cb[None]
    return pl.pallas_call(
        _ffn_kernel, grid=(B, L // tm),
        in_specs=[pl.BlockSpec((1, tm, D), lambda b, i: (b, i, 0)),
                  pl.BlockSpec((1, HALO, D), lambda b, i: (b, jnp.maximum(i * nb - 1, 0), 0)),
                  pl.BlockSpec((1, HALO, D), lambda b, i: (b, jnp.minimum((i + 1) * nb, last), 0)),
                  pl.BlockSpec((1, tm, D), lambda b, i: (b, i, 0)),
                  resident(w_up), resident(cw), resident(cb), resident(w_down), resident(gpost)],
        out_specs=pl.BlockSpec((1, tm, D), lambda b, i: (b, i, 0)),
        out_shape=jax.ShapeDtypeStruct((B, L, D), F32),
        scratch_shapes=[pltpu.VMEM((tm, D_FF), BF16)],
        compiler_params=_cparams(("parallel", "parallel")), name="conv_ffn",
    )(h, h, h, x, w_up, cw, cb, w_down, gpost)


def _axial_tables(L, rot_dim):
    pos = np.arange(L)
    n_axis = rot_dim // 4
    inv = ROPE_THETA ** (-np.arange(n_axis) / n_axis)
    ang = np.concatenate([(pos // GRID_W)[:, None] * inv, (pos % GRID_W)[:, None] * inv], axis=-1)
    return jnp.asarray(np.cos(ang), F32), jnp.asarray(np.sin(ang), F32)


def _rope_tables(L):
    def lanes(parts):
        used = sum(p.shape[1] for p in parts)
        return jnp.concatenate(parts + [jnp.zeros((L, LANES - used), F32)], axis=1)

    cg, sg = _axial_tables(L, GQA_HEAD_DIM)
    tg = (lanes([cg, cg, cg, cg]), lanes([sg, sg, sg, sg]))
    cm, sm = _axial_tables(L, MLA_ROPE_DIM)
    nope0 = jnp.zeros((L, MLA_NOPE_DIM), F32)
    nope1 = jnp.ones((L, MLA_NOPE_DIM), F32)
    tmk = (lanes([nope0, cm, cm]), lanes([nope0, sm, sm]))
    sc = (MLA_NOPE_DIM + MLA_ROPE_DIM) ** -0.5 * LOG2E
    tmq = (lanes([nope1, cm, cm]) * sc, tmk[1] * sc)
    return tg, tmq, tmk


def _partner(w, half, sign=-1.0):
    return jnp.concatenate([sign * w[..., half:], w[..., :half]], axis=-1)


def _gain_pair(g, half):
    return jnp.tile(g, 2)[None], jnp.tile(_partner(g, half, 1.0), 2)[None]


def _pad_heads(w, n_heads, width):
    k = w.shape[0]
    return jnp.pad(w.reshape(k, n_heads, width), ((0, 0), (0, 0), (0, LANES - width))).reshape(
        k, n_heads * LANES)


def kernel(x, mix_pre_norm, w_in, hy_conv_w, hy_conv_b, hy_filt_w1, hy_filt_b1, hy_filt_freq1,
           hy_filt_w2, hy_filt_b2, hy_filt_freq2, hy_filt_w3, hy_skip, gqa_q_norm, gqa_k_norm,
           mla_q_a_norm, mla_w_uq, mla_kv_a_norm, mla_w_ukv, hy_out_norm, gqa_out_norm,
           mla_out_norm, w_out, mix_post_norm, ffn_pre_norm, w_up, ffn_conv_w, ffn_conv_b,
           w_down, ffn_post_norm):
    B, L, D = x.shape
    assert B == 2 and 2 * L == FFT_N1 * FFT_N2 and D == D_MODEL
    depth = w_in.shape[0]
    tg, tmq, tmk = _rope_tables(L)
    z_perm, win_perm = _hyena_positions(L)
    mats = _dft_matrices()

    for l in range(depth):
        wl = w_in[l]
        hd, hh, rh = GQA_HEAD_DIM, GQA_HEAD_DIM // 2, MLA_ROPE_DIM // 2
        o1 = HY_COLS
        o2 = o1 + GQA_HEADS * hd
        o3 = o2 + GQA_KV_HEADS * hd
        o4 = o3 + GQA_KV_HEADS * hd
        o5 = o4 + MLA_Q_RANK
        o6 = o5 + MLA_KV_RANK
        wq = wl[:, o1:o2].reshape(D, GQA_HEADS, hd)
        wk = wl[:, o2:o3].reshape(D, GQA_KV_HEADS, hd)
        wkr = wl[:, o6:]
        pe_pad = ((0, 0), (MLA_NOPE_DIM, LANES - MLA_NOPE_DIM - MLA_ROPE_DIM))
        win_p = jnp.concatenate(
            [wl[:, :o1],
             wq.reshape(D, -1), _partner(wq, hh).reshape(D, -1),
             wk.reshape(D, -1), _partner(wk, hh).reshape(D, -1),
             wl[:, o3:o6], jnp.pad(wkr, pe_pad), jnp.pad(_partner(wkr, rh), pe_pad)],
            axis=1).astype(BF16)
        gq = _gain_pair(gqa_q_norm[l] * (hd ** -0.5 * LOG2E), hh)
        gk = _gain_pair(gqa_k_norm[l], hh)
        wuq = mla_w_uq[l].reshape(MLA_Q_RANK, MLA_HEADS, MLA_NOPE_DIM + MLA_ROPE_DIM)
        wuq_pe = jnp.pad(_partner(wuq[:, :, MLA_NOPE_DIM:], rh),
                         ((0, 0), (0, 0), (MLA_NOPE_DIM, 0)))
        wuq_p = jnp.concatenate(
            [_pad_heads(mla_w_uq[l], MLA_HEADS, MLA_NOPE_DIM + MLA_ROPE_DIM),
             _pad_heads(wuq_pe.reshape(MLA_Q_RANK, -1), MLA_HEADS, MLA_NOPE_DIM + MLA_ROPE_DIM)],
            axis=1).astype(BF16)
        wukv = mla_w_ukv[l].reshape(MLA_KV_RANK, MLA_HEADS, MLA_NOPE_DIM + MLA_V_DIM)
        wukvk_p = _pad_heads(wukv[:, :, :MLA_NOPE_DIM].reshape(MLA_KV_RANK, -1), MLA_HEADS,
                             MLA_NOPE_DIM).astype(BF16)
        wukvv = wukv[:, :, MLA_NOPE_DIM:].reshape(MLA_KV_RANK, MLA_HEADS * MLA_V_DIM).astype(BF16)

        hy_in, qg, kg, vg, qm, km, vm = _kin_call(
            x, mix_pre_norm[l][None], win_p, gq, gk, tg, mla_q_a_norm[l][None], wuq_p, tmq,
            mla_kv_a_norm[l][None], wukvk_p, wukvv, tmk)

        fw = _filter_weights(hy_filt_w1[l], hy_filt_b1[l], hy_filt_freq1[l], hy_filt_w2[l],
                             hy_filt_b2[l], hy_filt_freq2[l], hy_filt_w3[l])
        y_hy = _hyena_layer(hy_in, hy_conv_w[l], hy_conv_b[l], fw, hy_skip[l], z_perm, win_perm, mats)

        y_gqa = _attn_call(qg.transpose(0, 2, 1), kg, vg.reshape(B, GQA_KV_HEADS, GQA_HEAD_DIM, L),
                           n_kv=1, n_rep=GQA_HEADS // GQA_KV_HEADS, pack=2, tq=TQ_GQA,
                           name="attn_gqa")
        y_mla = _attn_call(qm.transpose(0, 2, 1), km, vm.reshape(B, MLA_HEADS, MLA_V_DIM, L),
                           n_kv=2, n_rep=1, pack=1, tq=TQ_MLA, name="attn_mla")

        x, h2 = _kout_call(x, y_hy, y_gqa, y_mla, hy_out_norm[l][None], gqa_out_norm[l][None],
                           mla_out_norm[l][None], w_out[l].astype(BF16),
                           mix_post_norm[l][None], ffn_pre_norm[l][None])
        x = _ffn_call(h2, x, w_up[l].astype(BF16), ffn_conv_w[l], ffn_conv_b[l],
                      w_down[l].astype(BF16), ffn_post_norm[l][None])
    return x
```

```python
import functools
import math

import numpy as np
import jax
import jax.numpy as jnp
from jax import lax
from jax.experimental import pallas as pl
from jax.experimental.pallas import tpu as pltpu

F32 = jnp.float32
BF16 = jnp.bfloat16

NORM_EPS = 1e-6
ROPE_THETA = 10000.0
GRID_W = 64
LOG2E = math.log2(math.e)

D_MODEL = 1024
HY_D = 256
HY_COLS = 3 * HY_D
HY_EMB = 33
HY_BANDS = 16
HY_DECAY_TARGET = 1e-2
HY_FAST_DECAY_PCT = 0.3
HY_SLOW_DECAY_PCT = 1.5
GQA_HEADS = 8
GQA_KV_HEADS = 2
GQA_HEAD_DIM = 64
MLA_HEADS = 4
MLA_Q_RANK = 256
MLA_KV_RANK = 128
MLA_NOPE_DIM = 64
MLA_ROPE_DIM = 32
MLA_V_DIM = 64
D_FF = 2816

LANES = 128
FFT_N1 = 128
FFT_N2 = 128

TM_IN = 1024
TK_ATTN = 256
TQ_GQA = 128
TQ_MLA = 256
TM_OUT = 1024
TM_FFN = 512
TF_FFN = 256
TL_HCONV = 1024
FILT_HALF = 512
FILT_FEAT = 64
VMEM_LIMIT = 56 * 1024 * 1024


def _cparams(sem):
    return pltpu.CompilerParams(dimension_semantics=sem, vmem_limit_bytes=VMEM_LIMIT)


def _rms(x, g):
    return x * lax.rsqrt(jnp.mean(x * x, axis=-1, keepdims=True) + NORM_EPS) * g


def _kin_kernel(x_ref, gpre_ref, win_ref, gqn_ref, gqs_ref, gkn_ref, gks_ref, cg_ref, sg_ref,
                mqn_ref, wuq_ref, cmq_ref, smq_ref,
                mkvn_ref, wukvk_ref, wukvv_ref, cmk_ref, smk_ref,
                hy_ref, qg_ref, kg_ref, vg_ref, qm_ref, km_ref, vm_ref):
    x = x_ref[0]
    h = _rms(x, gpre_ref[...]).astype(BF16)
    cur = [0]

    def proj(n):
        lo = cur[0]
        cur[0] = lo + n
        return jnp.dot(h, win_ref[:, lo:lo + n], preferred_element_type=F32)

    hy_ref[0] = proj(HY_COLS)

    cg, sg = cg_ref[...], sg_ref[...]
    low = lax.broadcasted_iota(jnp.int32, (x.shape[0], LANES), 1) < GQA_HEAD_DIM

    def head_pairs(n_tiles, gain, gain_sw):
        xa, xb = proj(n_tiles * LANES), proj(n_tiles * LANES)
        ca, sa = cg * gain, sg * gain_sw
        out = []
        for t in range(n_tiles):
            xc, xs = xa[:, t * LANES:(t + 1) * LANES], xb[:, t * LANES:(t + 1) * LANES]
            sq = xc * xc
            tot = jnp.sum(sq, axis=-1, keepdims=True)
            lo = jnp.sum(jnp.where(low, sq, 0.0), axis=-1, keepdims=True)
            ms = jnp.where(low, lo, tot - lo) * (1.0 / GQA_HEAD_DIM)
            out.append((xc * ca + xs * sa) * lax.rsqrt(ms + NORM_EPS))
        return out

    for t, q in enumerate(head_pairs(GQA_HEADS // 2, gqn_ref[...], gqs_ref[...])):
        qg_ref[0, :, t * LANES:(t + 1) * LANES] = q.astype(BF16)
    (kk,) = head_pairs(GQA_KV_HEADS // 2, gkn_ref[...], gks_ref[...])
    ksw = pltpu.roll(kk, GQA_HEAD_DIM, 1)
    kg_ref[0, 0] = jnp.where(low, kk, ksw).astype(BF16)
    kg_ref[0, 1] = jnp.where(low, ksw, kk).astype(BF16)
    vg_ref[0] = proj(LANES).T.astype(BF16)

    cq = _rms(proj(MLA_Q_RANK), mqn_ref[...]).astype(BF16)
    qm = jnp.dot(cq, wuq_ref[...], preferred_element_type=F32)
    cmq, smq = cmq_ref[...], smq_ref[...]
    nq = MLA_HEADS * LANES
    for j in range(MLA_HEADS):
        qm_ref[0, :, j * LANES:(j + 1) * LANES] = (
            qm[:, j * LANES:(j + 1) * LANES] * cmq
            + qm[:, nq + j * LANES:nq + (j + 1) * LANES] * smq).astype(BF16)

    ckv = _rms(proj(MLA_KV_RANK), mkvn_ref[...]).astype(BF16)
    kpe = proj(LANES) * cmk_ref[...] + proj(LANES) * smk_ref[...]
    kn = jnp.dot(ckv, wukvk_ref[...], preferred_element_type=F32)
    for j in range(MLA_HEADS):
        km_ref[0, j] = (kn[:, j * LANES:(j + 1) * LANES] + kpe).astype(BF16)
    vm = jnp.dot(ckv, wukvv_ref[...], preferred_element_type=F32)
    for j in range(vm.shape[1] // LANES):
        vm_ref[0, j * LANES:(j + 1) * LANES, :] = vm[:, j * LANES:(j + 1) * LANES].T.astype(BF16)


def _kin_call(x, gpre, win_p, gq, gk, tg, mqn, wuq_p, tmq, mkvn, wukvk_p, wukvv, tmk):
    B, L, D = x.shape
    tm = TM_IN
    nt = L // tm

    def full(a):
        return pl.BlockSpec(a.shape, lambda b, i: (0,) * a.ndim, pipeline_mode=pl.Buffered(1))

    def rows(w):
        return pl.BlockSpec((tm, w), lambda b, i: (i, 0))

    in_specs = [pl.BlockSpec((1, tm, D), lambda b, i: (b, i, 0)), full(gpre), full(win_p),
                full(gq[0]), full(gq[1]), full(gk[0]), full(gk[1]), rows(LANES), rows(LANES),
                full(mqn), full(wuq_p), rows(LANES), rows(LANES),
                full(mkvn), full(wukvk_p), full(wukvv), rows(LANES), rows(LANES)]
    gv_rows = GQA_KV_HEADS * GQA_HEAD_DIM
    mv_rows = MLA_HEADS * MLA_V_DIM
    out_shape = [
        jax.ShapeDtypeStruct((B, L, HY_COLS), F32),
        jax.ShapeDtypeStruct((B, L, GQA_HEADS * GQA_HEAD_DIM), BF16),
        jax.ShapeDtypeStruct((B, GQA_KV_HEADS, L, LANES), BF16),
        jax.ShapeDtypeStruct((B, gv_rows, L), BF16),
        jax.ShapeDtypeStruct((B, L, MLA_HEADS * LANES), BF16),
        jax.ShapeDtypeStruct((B, MLA_HEADS, L, LANES), BF16),
        jax.ShapeDtypeStruct((B, mv_rows, L), BF16),
    ]
    out_specs = [
        pl.BlockSpec((1, tm, HY_COLS), lambda b, i: (b, i, 0)),
        pl.BlockSpec((1, tm, GQA_HEADS * GQA_HEAD_DIM), lambda b, i: (b, i, 0)),
        pl.BlockSpec((1, GQA_KV_HEADS, tm, LANES), lambda b, i: (b, 0, i, 0)),
        pl.BlockSpec((1, gv_rows, tm), lambda b, i: (b, 0, i)),
        pl.BlockSpec((1, tm, MLA_HEADS * LANES), lambda b, i: (b, i, 0)),
        pl.BlockSpec((1, MLA_HEADS, tm, LANES), lambda b, i: (b, 0, i, 0)),
        pl.BlockSpec((1, mv_rows, tm), lambda b, i: (b, 0, i)),
    ]
    return pl.pallas_call(
        _kin_kernel, grid=(B, nt), in_specs=in_specs, out_specs=out_specs, out_shape=out_shape,
        compiler_params=_cparams(("parallel", "parallel")), name="in_proj",
    )(x, gpre, win_p, *gq, *gk, *tg, mqn, wuq_p, *tmq, mkvn, wukvk_p, wukvv, *tmk)


def _attn_kernel(q_ref, k_ref, vt_ref, o_ref, sa_ref, sb_ref, ma_ref, mb_ref, *,
                 n_kv, n_rep, pack, depth, tq, n_chunks, tk):
    i = pl.program_id(0)
    cols = n_rep * tq
    width = n_kv * cols
    grp = tk // 8

    @pl.when(i == 0)
    def _():
        sb_ref[...] = jnp.zeros(sb_ref.shape, F32)
        mb_ref[...] = jnp.zeros(mb_ref.shape, F32)

    def step(sw_ref, mw_ref, sr_ref, mr_ref):
        def q_head(h):
            rows_h = LANES // pack
            return q_ref[0, h * rows_h:h * rows_h + depth, :]

        qs = [jnp.concatenate([q_head(a * n_rep + j) for j in range(n_rep)], axis=1)
              for a in range(n_kv)]
        mx = jnp.max(mr_ref[...], axis=0, keepdims=True)
        m = jnp.full((8, width), -jnp.inf, F32)
        l = jnp.zeros((8, width), F32)
        accs = [jnp.zeros((vt_ref.shape[2], cols), F32) for _ in range(n_kv)]
        for c in range(n_chunks):
            st = jnp.concatenate(
                [jnp.dot(k_ref[0, a, c * tk:(c + 1) * tk, :depth], qs[a],
                         preferred_element_type=F32)
                 for a in range(n_kv)], axis=1)
            sw_ref[c] = st
            m = jnp.maximum(m, jnp.max(st.reshape(grp, 8, width), axis=0))
            p = jnp.exp2(sr_ref[c] - mx)
            l = l + jnp.sum(p.reshape(grp, 8, width), axis=0)
            pb = p.astype(BF16)
            for a in range(n_kv):
                accs[a] = accs[a] + jnp.dot(vt_ref[0, a, :, c * tk:(c + 1) * tk],
                                            pb[:, a * cols:(a + 1) * cols],
                                            preferred_element_type=F32)
        mw_ref[...] = m
        ls = jnp.sum(l, axis=0, keepdims=True)
        heads = []
        for a in range(n_kv):
            oa = accs[a] / ls[:, a * cols:(a + 1) * cols]
            heads += [oa[:, j * tq:(j + 1) * tq] for j in range(n_rep)]
        o_ref[0] = jnp.concatenate(heads, axis=0).T.astype(BF16)

    @pl.when(i % 2 == 0)
    def _():
        step(sa_ref, ma_ref, sb_ref, mb_ref)

    @pl.when(i % 2 == 1)
    def _():
        step(sb_ref, mb_ref, sa_ref, ma_ref)


def _attn_call(qt, k, vt, *, n_kv, n_rep, pack, depth, tq, name):
    B, hq, L = qt.shape
    H = hq * pack // LANES
    hkv, dv = vt.shape[1], vt.shape[2]
    G = hkv // n_kv
    hs = n_kv * n_rep
    tk = TK_ATTN
    n_chunks = L // tk
    nq = L // tq
    width = hs * tq
    kern = functools.partial(_attn_kernel, n_kv=n_kv, n_rep=n_rep, pack=pack, depth=depth, tq=tq,
                             n_chunks=n_chunks, tk=tk)
    total = B * G * nq

    def blk(s):
        return s // (G * nq), (s // nq) % G, s % nq

    def q_map(s):
        b, g, i = blk(jnp.minimum(s, total - 1))
        return b, g, i

    def k_map(s):
        b, g, _ = blk(jnp.minimum(s, total - 1))
        return b, g, 0, 0

    def v_map(s):
        b, g, _ = blk(jnp.maximum(s - 1, 0))
        return b, g, 0, 0

    def o_map(s):
        b, g, i = blk(jnp.maximum(s - 1, 0))
        return b, i, g

    return pl.pallas_call(
        kern, grid=(total + 1,),
        in_specs=[pl.BlockSpec((1, hs * LANES // pack, tq), q_map),
                  pl.BlockSpec((1, n_kv, L, LANES), k_map, pipeline_mode=pl.Buffered(1)),
                  pl.BlockSpec((1, n_kv, dv, L), v_map, pipeline_mode=pl.Buffered(1))],
        out_specs=pl.BlockSpec((1, tq, hs * dv), o_map),
        out_shape=jax.ShapeDtypeStruct((B, L, H * dv), BF16),
        scratch_shapes=[pltpu.VMEM((n_chunks, tk, width), F32), pltpu.VMEM((n_chunks, tk, width), F32),
                        pltpu.VMEM((8, width), F32), pltpu.VMEM((8, width), F32)],
        compiler_params=_cparams(("arbitrary",)), name=name,
    )(qt, k, vt)


def _hconv_kernel(x_ref, xp_ref, xn_ref, w_ref, b_ref, v_ref, x1_ref, x2_ref):
    i = pl.program_id(1)
    x = x_ref[0]
    tl = x.shape[0]
    prev = jnp.where(i > 0, xp_ref[0][7:8, :], 0.0)
    nxt = jnp.where(i < pl.num_programs(1) - 1, xn_ref[0][0:1, :], 0.0)
    r = lax.broadcasted_iota(jnp.int32, x.shape, 0)
    xm = jnp.where(r == 0, prev, pltpu.roll(x, 1, 0))
    xp = jnp.where(r == tl - 1, nxt, pltpu.roll(x, tl - 1, 0))
    uc = xm * w_ref[0:1, :] + x * w_ref[1:2, :] + xp * w_ref[2:3, :] + b_ref[...]
    for r in range(tl // FFT_N2):
        blk = uc[r * FFT_N2:(r + 1) * FFT_N2]
        v_ref[:, r, :] = blk[:, :HY_D]
        x1_ref[:, r, :] = blk[:, HY_D:2 * HY_D]
        x2_ref[:, r, :] = blk[:, 2 * HY_D:]


def _hconv_call(hy_in, w, b):
    B, L, C = hy_in.shape
    tl = TL_HCONV
    nb = tl // 8
    last = L // 8 - 1
    nt = L // tl
    rows = tl // FFT_N2
    out = jax.ShapeDtypeStruct((FFT_N2, B * L // FFT_N2, HY_D), F32)
    ospec = pl.BlockSpec((FFT_N2, rows, HY_D), lambda b_, i: (0, b_ * nt + i, 0))
    return pl.pallas_call(
        _hconv_kernel, grid=(B, L // tl),
        in_specs=[pl.BlockSpec((1, tl, C), lambda b_, i: (b_, i, 0)),
                  pl.BlockSpec((1, 8, C), lambda b_, i: (b_, jnp.maximum(i * nb - 1, 0), 0)),
                  pl.BlockSpec((1, 8, C), lambda b_, i: (b_, jnp.minimum((i + 1) * nb, last), 0)),
                  pl.BlockSpec((3, C), lambda b_, i: (0, 0)),
                  pl.BlockSpec((1, C), lambda b_, i: (0, 0))],
        out_specs=[ospec, ospec, ospec], out_shape=[out, out, out],
        compiler_params=_cparams(("parallel", "parallel")), name="hy_conv3",
    )(hy_in, hy_in, hy_in, w, b)


def _split_bf16(a):
    hi = a.astype(BF16)
    return hi, (a - hi.astype(F32)).astype(BF16)


def _dot3(a, w_hi, w_lo):
    a_hi, a_lo = _split_bf16(a)
    dot = functools.partial(jnp.dot, preferred_element_type=F32)
    return dot(a_hi, w_hi) + dot(a_lo, w_hi) + dot(a_hi, w_lo)


def _filt_kernel(z_ref, w1h_ref, w1l_ref, b1_ref, f1_ref, w2h_ref, w2l_ref, b2_ref, f2_ref,
                 w3h_ref, w3l_ref, win_ref, gaf_ref, o_ref):
    h = jnp.sin(f1_ref[...] * (_dot3(z_ref[...], w1h_ref[...], w1l_ref[...]) + b1_ref[...]))
    h = jnp.sin(f2_ref[...] * (_dot3(h, w2h_ref[...], w2l_ref[...]) + b2_ref[...]))
    k = _dot3(h, w3h_ref[...], w3l_ref[...])
    half = k.shape[0]
    wcols = 4 * HY_D
    per_half = half // FFT_N1
    n1 = lax.broadcasted_iota(jnp.int32, (half, 2 * HY_D), 0) % FFT_N1
    for s in range(2):
        ks = k[:, s * wcols:(s + 1) * wcols]
        kk = jnp.where(n1 >= FFT_N1 // 2, ks[:, 2 * HY_D:], ks[:, :2 * HY_D])
        win = win_ref[s * half:(s + 1) * half, :]
        filt = (kk * jnp.concatenate([win, win], axis=-1)).astype(BF16)
        for j in range(per_half):
            o_ref[s * per_half + j] = jnp.dot(gaf_ref[s * per_half + j],
                                              filt[j * FFT_N1:(j + 1) * FFT_N1],
                                              preferred_element_type=F32)


def _filt_call(z_pack, fw, win_perm, gaf):
    tp = 2 * FILT_HALF
    n = win_perm.shape[0]
    to = tp // FFT_N1
    M = gaf.shape[1]

    def full(a):
        return pl.BlockSpec(a.shape, lambda i: (0,) * a.ndim)

    return pl.pallas_call(
        _filt_kernel, grid=(n // tp,),
        in_specs=[pl.BlockSpec((FILT_HALF, z_pack.shape[1]), lambda i: (i, 0))]
        + [full(a) for a in fw] + [pl.BlockSpec((tp, HY_D), lambda i: (i, 0)),
                                   pl.BlockSpec((to, M, FFT_N1), lambda i: (i, 0, 0))],
        out_specs=pl.BlockSpec((to, M, 2 * HY_D), lambda i: (i, 0, 0)),
        out_shape=jax.ShapeDtypeStruct((n // FFT_N1, M, 2 * HY_D), F32),
        compiler_params=_cparams(("parallel",)), name="hy_filter",
    )(z_pack, *fw, win_perm, gaf)


def _bm_kernel(g_ref, x_ref, o_ref, *, to):
    for t in range(to):
        o_ref[t] = jnp.dot(g_ref[t], x_ref[t].astype(BF16), preferred_element_type=F32)


def _bm_call(g, x, name, to=16):
    O, K, N = x.shape
    M = g.shape[1]
    return pl.pallas_call(
        functools.partial(_bm_kernel, to=to), grid=(O // to,),
        in_specs=[pl.BlockSpec((to, M, K), lambda i: (i, 0, 0)),
                  pl.BlockSpec((to, K, N), lambda i: (i, 0, 0))],
        out_specs=pl.BlockSpec((to, M, N), lambda i: (i, 0, 0)),
        out_shape=jax.ShapeDtypeStruct((O, M, N), F32),
        compiler_params=_cparams(("parallel",)), name=name,
    )(g, x)


def _gather_ri(x_ref, j):
    return jnp.concatenate([x_ref[:, 0, j, :], x_ref[:, 1, j, :]], axis=0).astype(BF16)


def _gather_spec(a, to):
    return pl.BlockSpec((a.shape[0], 2, to, a.shape[3]), lambda i: (0, 0, i, 0))


def _filtb_kernel(mb_ref, x_ref, o_ref, *, to):
    for j in range(to):
        o_ref[j] = jnp.dot(mb_ref[...], _gather_ri(x_ref, j),
                           preferred_element_type=F32).astype(o_ref.dtype)


def _filtb_call(mb, ka, to=16):
    O, _, P, N = ka.shape
    return pl.pallas_call(
        functools.partial(_filtb_kernel, to=to), grid=(P // to,),
        in_specs=[pl.BlockSpec(mb.shape, lambda i: (0, 0)), _gather_spec(ka, to)],
        out_specs=pl.BlockSpec((to, 2 * O, N), lambda i: (i, 0, 0)),
        out_shape=jax.ShapeDtypeStruct((P, 2 * O, N), BF16),
        compiler_params=_cparams(("parallel",)), name="hy_fft_filt_b",
    )(mb, ka)


def _convb_kernel(mb_ref, gc_ref, x_ref, kf_ref, o_ref, *, to):
    h = FFT_N2
    for j in range(to):
        xs = jnp.dot(mb_ref[...], _gather_ri(x_ref, j), preferred_element_type=F32)
        xr, xi = xs[:h], xs[h:]
        kr, ki = kf_ref[j, :h, :].astype(F32), kf_ref[j, h:, :].astype(F32)
        ys = jnp.concatenate([xr * kr - xi * ki, xr * ki + xi * kr], axis=0).astype(BF16)
        o_ref[j] = jnp.dot(gc_ref[j], ys, preferred_element_type=F32)


def _convb_call(mb, gc, a, kf, order, to=16):
    O, _, P, N = a.shape
    return pl.pallas_call(
        functools.partial(_convb_kernel, to=to), grid=(P // to,),
        in_specs=[pl.BlockSpec(mb.shape, lambda i: (0, 0)),
                  pl.BlockSpec((to, 2 * O, 2 * O), lambda i: (i, 0, 0)),
                  _gather_spec(a, to),
                  pl.BlockSpec((to, 2 * O, N), lambda i: (i, 0, order))],
        out_specs=pl.BlockSpec((to, 2 * O, N), lambda i: (i, 0, 0)),
        out_shape=jax.ShapeDtypeStruct((P, 2 * O, N), F32),
        compiler_params=_cparams(("parallel",)), name="hy_spec_mul",
    )(mb, gc, a, kf)


def _convd_kernel(md_ref, c_ref, g_ref, u_ref, s_ref, *rest, to, chain):
    if chain:
        ga_ref, z_ref, a_ref = rest
    else:
        (o_ref,) = rest
    for j in range(to):
        y = jnp.dot(md_ref[...], _gather_ri(c_ref, j), preferred_element_type=F32)
        z = g_ref[j] * (y + u_ref[j] * s_ref[...])
        if chain:
            z_ref[j] = z
            a_ref[j] = jnp.dot(ga_ref[j], z.astype(BF16), preferred_element_type=F32)
        else:
            o_ref[:, j, :] = z


def _convd_call(md, c, gate, u, skip, ga=None, to=16):
    O, _, P, N = c.shape
    R = md.shape[0]
    tspec = pl.BlockSpec((to, R, N), lambda i: (i, 0, 0))
    in_specs = [pl.BlockSpec(md.shape, lambda i: (0, 0)), _gather_spec(c, to), tspec, tspec,
                pl.BlockSpec((1, N), lambda i: (0, 0))]
    args = [md, c, gate, u, skip.reshape(1, N)]
    if ga is not None:
        M = ga.shape[1]
        in_specs.append(pl.BlockSpec((to, M, R), lambda i: (i, 0, 0)))
        args.append(ga)
        out_specs = [tspec, pl.BlockSpec((to, M, N), lambda i: (i, 0, 0))]
        out_shape = [jax.ShapeDtypeStruct((P, R, N), F32), jax.ShapeDtypeStruct((P, M, N), F32)]
    else:
        out_specs = pl.BlockSpec((R, to, N), lambda i: (0, i, 0))
        out_shape = jax.ShapeDtypeStruct((R, P, N), F32)
    return pl.pallas_call(
        functools.partial(_convd_kernel, to=to, chain=ga is not None), grid=(P // to,),
        in_specs=in_specs, out_specs=out_specs, out_shape=out_shape,
        compiler_params=_cparams(("parallel",)), name="hy_fft_d",
    )(*args)


def _dft_tables():
    n = FFT_N1 * FFT_N2
    k = np.arange(FFT_N1)
    f = np.exp(-2j * np.pi * np.outer(k, k) / FFT_N1)
    t = np.exp(-2j * np.pi * np.outer(k, k) / n)
    return f, t, n


def _dft_matrices():
    f, t, n = _dft_tables()
    fr, fi = jnp.asarray(f.real, F32), jnp.asarray(f.imag, F32)
    tr, ti = jnp.asarray(t.real, F32), jnp.asarray(t.imag, F32)
    half = FFT_N1 // 2
    er = fr[None] * tr[:, :, None] - fi[None] * ti[:, :, None]
    ei = fr[None] * ti[:, :, None] + fi[None] * tr[:, :, None]
    ga = jnp.concatenate([jnp.concatenate([er[:, :, :half], -ei[:, :, :half]], axis=2),
                          jnp.concatenate([ei[:, :, :half], er[:, :, :half]], axis=2)], axis=1)
    gaf = jnp.concatenate([er, ei], axis=1)
    mb = jnp.concatenate([jnp.concatenate([fr, -fi], axis=1),
                          jnp.concatenate([fi, fr], axis=1)], axis=0)
    tct = jnp.transpose(tr)[:, :, None]
    tst = -jnp.transpose(ti)[:, :, None]
    gr = tct * fr[None] - tst * (-fi[None])
    gi = tct * (-fi[None]) + tst * fr[None]
    gc = jnp.concatenate([jnp.concatenate([gr, -gi], axis=2),
                          jnp.concatenate([gi, gr], axis=2)], axis=1)
    hr, hi = fr[:half] / n, -fi[:half] / n
    md = jnp.concatenate([jnp.concatenate([hr, -hi], axis=1),
                          jnp.concatenate([hi, hr], axis=1)], axis=0)
    return (ga.astype(BF16), gaf.astype(BF16), mb.astype(BF16), gc.astype(BF16), md.astype(BF16))


def _hyena_positions(L):
    p = FFT_N2 * np.arange(FFT_N1)[None, :] + np.arange(FFT_N2)[:, None]
    pos = np.where(p < L, p, 2 * L - 1 - p).reshape(2 * L, 1).astype(np.float64)
    t = pos / (L - 1)
    w = 2.0 * math.pi * pos / L
    f = np.linspace(1e-4, HY_BANDS - 1, HY_BANDS)[None, :]
    z = np.concatenate([t, np.cos(f * w), -np.sin(f * w),
                        np.zeros((2 * L, FILT_FEAT - HY_EMB))], axis=-1)
    z_pack = (z.reshape(-1, 2, FILT_HALF, FILT_FEAT).transpose(0, 2, 1, 3)
              .reshape(-1, 2 * FILT_FEAT))
    max_decay = math.log(HY_DECAY_TARGET) / HY_FAST_DECAY_PCT
    min_decay = math.log(HY_DECAY_TARGET) / HY_SLOW_DECAY_PCT
    deltas = jnp.linspace(min_decay, max_decay, HY_D, dtype=F32)
    window = jnp.exp(-jnp.asarray(t, F32) * jnp.abs(deltas)[None, :])
    return jnp.asarray(z_pack, F32), window


def _filter_weights(w1, b1, f1, w2, b2, f2, w3):
    def bd(w):
        z = jnp.zeros_like(w)
        return jnp.concatenate([jnp.concatenate([w, z], axis=1),
                                jnp.concatenate([z, w], axis=1)], axis=0)

    def twice(v):
        return jnp.concatenate([v, v])[None]

    w1 = jnp.pad(w1, ((0, FILT_FEAT - w1.shape[0]), (0, 0)))
    w3 = w3.reshape(-1, 2, 2, HY_D).transpose(0, 2, 1, 3).reshape(-1, 4 * HY_D)
    return (*_split_bf16(bd(w1)), twice(b1), twice(f1), *_split_bf16(bd(w2)), twice(b2), twice(f2),
            *_split_bf16(bd(w3)))


def _hyena_layer(hy_in, conv_w, conv_b, fw, skip, z_perm, win_perm, mats):
    B, L, _ = hy_in.shape
    ga, gaf, mb, gc, md = mats
    v, x1, x2 = _hconv_call(hy_in, conv_w, conv_b[None])
    ka = _filt_call(z_perm, fw, win_perm, gaf)
    kf = _filtb_call(mb, ka.reshape(FFT_N2, 2, FFT_N1, 2 * HY_D))

    a = _bm_call(ga, v, "hy_fft_a")
    c = _convb_call(mb, gc, a.reshape(FFT_N2, 2, FFT_N1, HY_D), kf, 0)
    z, a = _convd_call(md, c.reshape(FFT_N1, 2, FFT_N2, HY_D), x1, v, skip[0], ga=ga)
    c = _convb_call(mb, gc, a.reshape(FFT_N2, 2, FFT_N1, HY_D), kf, 1)
    z = _convd_call(md, c.reshape(FFT_N1, 2, FFT_N2, HY_D), x2, z, skip[1])
    return z.reshape(B, L, HY_D)


def _kout_kernel(x_ref, yh_ref, yg_ref, ym_ref, gh_ref, gg_ref, gm_ref, w_ref, gpost_ref, gffn_ref,
                 xo_ref, h_ref):
    a = _rms(yh_ref[0], gh_ref[...]).astype(BF16)
    b = _rms(yg_ref[0].astype(F32), gg_ref[...]).astype(BF16)
    c = _rms(ym_ref[0].astype(F32), gm_ref[...]).astype(BF16)
    o1 = HY_D
    o2 = o1 + GQA_HEADS * GQA_HEAD_DIM
    y = (jnp.dot(a, w_ref[:o1, :], preferred_element_type=F32)
         + jnp.dot(b, w_ref[o1:o2, :], preferred_element_type=F32)
         + jnp.dot(c, w_ref[o2:, :], preferred_element_type=F32))
    xo = x_ref[0] + _rms(y, gpost_ref[...])
    xo_ref[0] = xo
    h_ref[0] = _rms(xo, gffn_ref[...]).astype(BF16)


def _kout_call(x, yh, yg, ym, gh, gg, gm, w_p, gpost, gffn):
    B, L, D = x.shape
    tm = TM_OUT

    def rows(a):
        return pl.BlockSpec((1, tm, a.shape[2]), lambda b, i: (b, i, 0))

    def full(a):
        return pl.BlockSpec(a.shape, lambda b, i: (0,) * a.ndim)

    return pl.pallas_call(
        _kout_kernel, grid=(B, L // tm),
        in_specs=[rows(x), rows(yh), rows(yg), rows(ym), full(gh), full(gg), full(gm), full(w_p),
                  full(gpost), full(gffn)],
        out_specs=[rows(x), rows(x)],
        out_shape=[jax.ShapeDtypeStruct((B, L, D), F32), jax.ShapeDtypeStruct((B, L, D), BF16)],
        compiler_params=_cparams(("parallel", "parallel")), name="out_proj",
    )(x, yh, yg, ym, gh, gg, gm, w_p, gpost, gffn)


HALO = 16


def _ffn_kernel(h_ref, hp_ref, hn_ref, x_ref, wup_ref, cw_ref, cb_ref, wd_ref, gpost_ref,
                o_ref, act_ref):
    i = pl.program_id(1)
    tm = h_ref.shape[1]
    prev = jnp.where(i > 0, hp_ref[0], jnp.zeros_like(hp_ref[0]))
    nxt = jnp.where(i < pl.num_programs(1) - 1, hn_ref[0], jnp.zeros_like(hn_ref[0]))
    he = jnp.concatenate([prev, h_ref[0], nxt], axis=0)
    ext = tm + 2 * HALO
    tf = TF_FFN

    def conv(c0):
        up = jnp.dot(he, wup_ref[:, c0:c0 + tf], preferred_element_type=F32)
        um = pltpu.roll(up, 1, 0)[HALO:HALO + tm]
        upl = pltpu.roll(up, ext - 1, 0)[HALO:HALO + tm]
        return (um * cw_ref[0:1, c0:c0 + tf] + up[HALO:HALO + tm] * cw_ref[1:2, c0:c0 + tf]
                + upl * cw_ref[2:3, c0:c0 + tf] + cb_ref[:, c0:c0 + tf])

    for j in range(D_FF // tf):
        g = conv(j * tf)
        u = conv(D_FF + j * tf)
        gelu = 0.5 * g * (1.0 + jnp.tanh(math.sqrt(2.0 / math.pi) * (g + 0.044715 * (g * g * g))))
        act_ref[:, j * tf:(j + 1) * tf] = (gelu * u).astype(BF16)
    f = jnp.dot(act_ref[...], wd_ref[...], preferred_element_type=F32)
    o_ref[0] = x_ref[0] + _rms(f, gpost_ref[...])


def _ffn_call(h, x, w_up, cw, cb, w_down, gpost):
    B, L, D = x.shape
    tm = TM_FFN
    nb = tm // HALO
    last = L // HALO - 1

    def resident(a):
        return pl.BlockSpec(a.shape, lambda b, i: (0,) * a.ndim, pipeline_mode=pl.Buffered(1))

    cb = cb[None]
    return pl.pallas_call(
        _ffn_kernel, grid=(B, L // tm),
        in_specs=[pl.BlockSpec((1, tm, D), lambda b, i: (b, i, 0)),
                  pl.BlockSpec((1, HALO, D), lambda b, i: (b, jnp.maximum(i * nb - 1, 0), 0)),
                  pl.BlockSpec((1, HALO, D), lambda b, i: (b, jnp.minimum((i + 1) * nb, last), 0)),
                  pl.BlockSpec((1, tm, D), lambda b, i: (b, i, 0)),
                  resident(w_up), resident(cw), resident(cb), resident(w_down), resident(gpost)],
        out_specs=pl.BlockSpec((1, tm, D), lambda b, i: (b, i, 0)),
        out_shape=jax.ShapeDtypeStruct((B, L, D), F32),
        scratch_shapes=[pltpu.VMEM((tm, D_FF), BF16)],
        compiler_params=_cparams(("parallel", "parallel")), name="conv_ffn",
    )(h, h, h, x, w_up, cw, cb, w_down, gpost)


def _axial_tables(L, rot_dim):
    pos = np.arange(L)
    n_axis = rot_dim // 4
    inv = ROPE_THETA ** (-np.arange(n_axis) / n_axis)
    ang = np.concatenate([(pos // GRID_W)[:, None] * inv, (pos % GRID_W)[:, None] * inv], axis=-1)
    return jnp.asarray(np.cos(ang), F32), jnp.asarray(np.sin(ang), F32)


def _rope_tables(L):
    def lanes(parts):
        used = sum(p.shape[1] for p in parts)
        return jnp.concatenate(parts + [jnp.zeros((L, LANES - used), F32)], axis=1)

    cg, sg = _axial_tables(L, GQA_HEAD_DIM)
    tg = (lanes([cg, cg, cg, cg]), lanes([sg, sg, sg, sg]))
    cm, sm = _axial_tables(L, MLA_ROPE_DIM)
    nope0 = jnp.zeros((L, MLA_NOPE_DIM), F32)
    nope1 = jnp.ones((L, MLA_NOPE_DIM), F32)
    tmk = (lanes([nope0, cm, cm]), lanes([nope0, sm, sm]))
    sc = (MLA_NOPE_DIM + MLA_ROPE_DIM) ** -0.5 * LOG2E
    tmq = (lanes([nope1, cm, cm]) * sc, tmk[1] * sc)
    return tg, tmq, tmk


def _partner(w, half, sign=-1.0):
    return jnp.concatenate([sign * w[..., half:], w[..., :half]], axis=-1)


def _gain_pair(g, half):
    return jnp.tile(g, 2)[None], jnp.tile(_partner(g, half, 1.0), 2)[None]


def _pad_heads(w, n_heads, width):
    k = w.shape[0]
    return jnp.pad(w.reshape(k, n_heads, width), ((0, 0), (0, 0), (0, LANES - width))).reshape(
        k, n_heads * LANES)


def kernel(x, mix_pre_norm, w_in, hy_conv_w, hy_conv_b, hy_filt_w1, hy_filt_b1, hy_filt_freq1,
           hy_filt_w2, hy_filt_b2, hy_filt_freq2, hy_filt_w3, hy_skip, gqa_q_norm, gqa_k_norm,
           mla_q_a_norm, mla_w_uq, mla_kv_a_norm, mla_w_ukv, hy_out_norm, gqa_out_norm,
           mla_out_norm, w_out, mix_post_norm, ffn_pre_norm, w_up, ffn_conv_w, ffn_conv_b,
           w_down, ffn_post_norm):
    B, L, D = x.shape
    assert B == 2 and 2 * L == FFT_N1 * FFT_N2 and D == D_MODEL
    depth = w_in.shape[0]
    tg, tmq, tmk = _rope_tables(L)
    z_perm, win_perm = _hyena_positions(L)
    mats = _dft_matrices()

    for l in range(depth):
        wl = w_in[l]
        hd, hh, rh = GQA_HEAD_DIM, GQA_HEAD_DIM // 2, MLA_ROPE_DIM // 2
        o1 = HY_COLS
        o2 = o1 + GQA_HEADS * hd
        o3 = o2 + GQA_KV_HEADS * hd
        o4 = o3 + GQA_KV_HEADS * hd
        o5 = o4 + MLA_Q_RANK
        o6 = o5 + MLA_KV_RANK
        wq = wl[:, o1:o2].reshape(D, GQA_HEADS, hd)
        wk = wl[:, o2:o3].reshape(D, GQA_KV_HEADS, hd)
        wkr = wl[:, o6:]
        pe_pad = ((0, 0), (MLA_NOPE_DIM, LANES - MLA_NOPE_DIM - MLA_ROPE_DIM))
        win_p = jnp.concatenate(
            [wl[:, :o1],
             wq.reshape(D, -1), _partner(wq, hh).reshape(D, -1),
             wk.reshape(D, -1), _partner(wk, hh).reshape(D, -1),
             wl[:, o3:o6], jnp.pad(wkr, pe_pad), jnp.pad(_partner(wkr, rh), pe_pad)],
            axis=1).astype(BF16)
        gq = _gain_pair(gqa_q_norm[l] * (hd ** -0.5 * LOG2E), hh)
        gk = _gain_pair(gqa_k_norm[l], hh)
        wuq = mla_w_uq[l].reshape(MLA_Q_RANK, MLA_HEADS, MLA_NOPE_DIM + MLA_ROPE_DIM)
        wuq_pe = jnp.pad(_partner(wuq[:, :, MLA_NOPE_DIM:], rh),
                         ((0, 0), (0, 0), (MLA_NOPE_DIM, 0)))
        wuq_p = jnp.concatenate(
            [_pad_heads(mla_w_uq[l], MLA_HEADS, MLA_NOPE_DIM + MLA_ROPE_DIM),
             _pad_heads(wuq_pe.reshape(MLA_Q_RANK, -1), MLA_HEADS, MLA_NOPE_DIM + MLA_ROPE_DIM)],
            axis=1).astype(BF16)
        wukv = mla_w_ukv[l].reshape(MLA_KV_RANK, MLA_HEADS, MLA_NOPE_DIM + MLA_V_DIM)
        wukvk_p = _pad_heads(wukv[:, :, :MLA_NOPE_DIM].reshape(MLA_KV_RANK, -1), MLA_HEADS,
                             MLA_NOPE_DIM).astype(BF16)
        wukvv = wukv[:, :, MLA_NOPE_DIM:].reshape(MLA_KV_RANK, MLA_HEADS * MLA_V_DIM).astype(BF16)

        hy_in, qg, kg, vg, qm, km, vm = _kin_call(
            x, mix_pre_norm[l][None], win_p, gq, gk, tg, mla_q_a_norm[l][None], wuq_p, tmq,
            mla_kv_a_norm[l][None], wukvk_p, wukvv, tmk)

        fw = _filter_weights(hy_filt_w1[l], hy_filt_b1[l], hy_filt_freq1[l], hy_filt_w2[l],
                             hy_filt_b2[l], hy_filt_freq2[l], hy_filt_w3[l])
        y_hy = _hyena_layer(hy_in, hy_conv_w[l], hy_conv_b[l], fw, hy_skip[l], z_perm, win_perm, mats)

        y_gqa = _attn_call(qg.transpose(0, 2, 1), kg, vg.reshape(B, GQA_KV_HEADS, GQA_HEAD_DIM, L),
                           n_kv=1, n_rep=GQA_HEADS // GQA_KV_HEADS, pack=2, depth=GQA_HEAD_DIM,
                           tq=TQ_GQA,
                           name="attn_gqa")
        y_mla = _attn_call(qm.transpose(0, 2, 1), km, vm.reshape(B, MLA_HEADS, MLA_V_DIM, L),
                           n_kv=2, n_rep=1, pack=1, depth=MLA_NOPE_DIM + MLA_ROPE_DIM,
                           tq=TQ_MLA, name="attn_mla")

        x, h2 = _kout_call(x, y_hy, y_gqa, y_mla, hy_out_norm[l][None], gqa_out_norm[l][None],
                           mla_out_norm[l][None], w_out[l].astype(BF16),
                           mix_post_norm[l][None], ffn_pre_norm[l][None])
        x = _ffn_call(h2, x, w_up[l].astype(BF16), ffn_conv_w[l], ffn_conv_b[l],
                      w_down[l].astype(BF16), ffn_post_norm[l][None])
    return x
```
